```python
import jax
import jax.numpy as jnp
from jax import lax
import numpy as np

D_MODEL = 1024
BATCH = 4
SEQ = 8192
DEPTH = 1
DEC_BATCH = 16
DEC_SEQ = 32
PAST_LEN = 2048

CHUNK = 64
HEAD_DIM = 64
RWKV_HEADS = 8
RWKV_WIDTH = RWKV_HEADS * HEAD_DIM
W_LORA = 64
A_LORA = 64
G_LORA = 128
RWKV_COLS = 3 * RWKV_WIDTH + W_LORA + A_LORA + G_LORA
GN_EPS = 64e-5
ATT_HEADS = 8
ATT_WIDTH = ATT_HEADS * HEAD_DIM
KV_HEADS = 2
KV_WIDTH = KV_HEADS * HEAD_DIM
IDX_HEADS = 4
IDX_DIM = 64
TOPK_MAX = 256
ATT_COLS = ATT_WIDTH + 2 * KV_WIDTH + IDX_HEADS * IDX_DIM + IDX_DIM + IDX_HEADS
Q_BLOCK = 128
ROPE_THETA = 500000.0
ROT_DIM = HEAD_DIM // 4
IN_COLS = RWKV_COLS + ATT_COLS
MIX_WIDTH = RWKV_WIDTH + ATT_WIDTH
N_EXPERTS = 32
TOP_K = 4
D_FF = 1024
SWIGLU_ALPHA = 1.702
SWIGLU_LIMIT = 7.0
MOE_BLOCK = 128
NORM_EPS = 1e-5

kernel_name = 'hymba_rwkv7_dsa_moe_stream_step'


def _split(t, sizes):
    offs = np.cumsum((0,) + tuple(sizes))
    return tuple(t[..., int(offs[i]):int(offs[i + 1])] for i in range(len(sizes)))


def rms_norm(x, g):
    xf = x.astype(jnp.float32)
    y = xf * lax.rsqrt(jnp.mean(xf * xf, axis=-1, keepdims=True) + NORM_EPS)
    return (y * g.astype(jnp.float32)).astype(x.dtype)


def ada_mod(c, w, b):
    m = jax.nn.silu(c) @ w + b
    return jnp.split(m, 6, axis=-1)


def modulate(h, shift, scale):
    return h * (1 + scale[:, None, :]) + shift[:, None, :]


def partial_rope(x, pos):
    inv = ROPE_THETA ** (-jnp.arange(0, ROT_DIM, 2, dtype=jnp.float32) / ROT_DIM)
    ang = pos.astype(jnp.float32)[:, None] * inv[None, :]
    cos = jnp.cos(ang)[None, :, None, :]
    sin = jnp.sin(ang)[None, :, None, :]
    xr = x[..., :ROT_DIM].astype(jnp.float32)
    x1, x2 = xr[..., :ROT_DIM // 2], xr[..., ROT_DIM // 2:]
    rot = jnp.concatenate([x1 * cos - x2 * sin, x2 * cos + x1 * sin], axis=-1)
    return jnp.concatenate([rot.astype(x.dtype), x[..., ROT_DIM:]], axis=-1)


def rwkv7_mix(p, prev, s0, mu, w0, w2, a0, a2, g2, k_k, k_a, r_k, lnx_w, lnx_b):
    B, T, _ = p.shape
    f = lambda t: t.astype(jnp.float32)
    pf = f(p)
    p_prev = jnp.concatenate([f(prev), pf[:, :-1]], axis=1)
    xs = pf + (p_prev - pf) * f(mu)
    r, k, v, dw, da, dg = _split(xs, (RWKV_WIDTH, RWKV_WIDTH, RWKV_WIDTH, W_LORA, A_LORA, G_LORA))
    w_log = -jax.nn.softplus(-(f(w0) + jnp.tanh(dw) @ f(w2))) - 0.5
    decay = jnp.exp(-jnp.exp(w_log))
    a = jax.nn.sigmoid(f(a0) + da @ f(a2))
    g = jax.nn.sigmoid(dg) @ f(g2)
    heads = lambda t: t.reshape(B, T, RWKV_HEADS, HEAD_DIM)
    kk = heads(k * f(k_k))
    kk = kk / jnp.maximum(jnp.sqrt(jnp.sum(kk * kk, axis=-1, keepdims=True)), 1e-12)
    k = k * (1 + (a - 1) * f(k_a))
    r_h, w_h, k_h, v_h, a_h = heads(r), heads(decay), heads(k), heads(v), heads(a)
    b_h = kk * a_h
    seq = tuple(jnp.moveaxis(t, 1, 0) for t in (r_h, w_h, k_h, v_h, kk, b_h))

    def step(S, inp):
        r_t, w_t, k_t, v_t, kk_t, b_t = inp
        sa = jnp.einsum('bhvk,bhk->bhv', S, -kk_t)
        S = S * w_t[:, :, None, :] + sa[..., None] * b_t[:, :, None, :] + v_t[..., None] * k_t[:, :, None, :]
        return S, jnp.einsum('bhvk,bhk->bhv', S, r_t)

    s_T, o = lax.scan(step, f(s0), seq)
    o = jnp.moveaxis(o, 0, 1)
    mean = jnp.mean(o, axis=-1, keepdims=True)
    var = jnp.mean(jnp.square(o - mean), axis=-1, keepdims=True)
    o = ((o - mean) * lax.rsqrt(var + GN_EPS)).reshape(B, T, RWKV_WIDTH) * f(lnx_w) + f(lnx_b)
    bonus = jnp.sum(r_h * k_h * f(r_k), axis=-1, keepdims=True) * v_h
    o = (o + bonus.reshape(B, T, RWKV_WIDTH)) * g
    return o.astype(p.dtype), s_T, p[:, -1:]


def dsa_project(p, pos):
    B, T, _ = p.shape
    q, k, v, qi, ki, wi = _split(p, (ATT_WIDTH, KV_WIDTH, KV_WIDTH, IDX_HEADS * IDX_DIM, IDX_DIM, IDX_HEADS))
    q = partial_rope(q.reshape(B, T, ATT_HEADS, HEAD_DIM), pos)
    k = partial_rope(k.reshape(B, T, KV_HEADS, HEAD_DIM), pos)
    v = v.reshape(B, T, KV_HEADS, HEAD_DIM)
    qi = partial_rope(qi.reshape(B, T, IDX_HEADS, IDX_DIM), pos)
    ki = partial_rope(ki.reshape(B, T, 1, IDX_DIM), pos)[:, :, 0]
    wi = wi * IDX_HEADS ** -0.5
    return q, k, v, qi, ki, wi


def dsa_attend(q, qi, wi, qpos, k_all, v_all, ki_all, kpos, n_sel):
    B, Q = q.shape[:2]
    logits = jnp.einsum('bqhd,bld->bqhl', qi, ki_all, preferred_element_type=jnp.float32) * IDX_DIM ** -0.5
    score = jnp.einsum('bqh,bqhl->bql', wi.astype(jnp.float32), jax.nn.relu(logits))
    admissible = (kpos[None, :] // CHUNK) <= (qpos[:, None] // CHUNK)
    score = jnp.where(admissible[None], score, -jnp.inf)
    _, sel = lax.top_k(score, n_sel)
    valid = (kpos[sel] // CHUNK) <= (qpos[None, :, None] // CHUNK)
    gather = jax.vmap(lambda t, i: t[i])
    k_sel = gather(k_all, sel)
    v_sel = gather(v_all, sel)
    qg = q.reshape(B, Q, KV_HEADS, ATT_HEADS // KV_HEADS, HEAD_DIM)
    s = jnp.einsum('bqgrd,bqkgd->bqgrk', qg, k_sel, preferred_element_type=jnp.float32) * HEAD_DIM ** -0.5
    s = jnp.where(valid[:, :, None, None, :], s, -jnp.inf)
    pr = jax.nn.softmax(s, axis=-1)
    o = jnp.einsum('bqgrk,bqkgd->bqgrd', pr.astype(v_sel.dtype), v_sel)
    return o.reshape(B, Q, ATT_WIDTH)


def dsa_prompt(q, qi, wi, pos, k, v, ki, n_sel):
    B, T = q.shape[:2]
    nb = T // Q_BLOCK

    def to_blocks(t):
        return jnp.moveaxis(t.reshape((B, nb, Q_BLOCK) + t.shape[2:]), 1, 0)

    def one_block(args):
        qb, qib, wib, pb = args
        return dsa_attend(qb, qib, wib, pb, k, v, ki, pos, n_sel)

    o = lax.map(one_block, (to_blocks(q), to_blocks(qi), to_blocks(wi), pos.reshape(nb, Q_BLOCK)))
    return jnp.moveaxis(o, 0, 1).reshape(B, T, ATT_WIDTH)


def moe_ffn(h, router_w, router_b, w_up, b_up, w_down, b_down):
    N, D = h.shape
    logits = (h @ router_w).astype(jnp.float32) + router_b.astype(jnp.float32)
    top_val, top_idx = lax.top_k(logits, TOP_K)
    gates = jax.nn.softmax(top_val, axis=-1)
    NK = N * TOP_K
    flat_e = top_idx.reshape(NK)
    order = jnp.argsort(flat_e)
    sorted_e = flat_e[order]
    sorted_tok = (order // TOP_K).astype(jnp.int32)
    counts = jnp.zeros((N_EXPERTS,), jnp.int32).at[flat_e].add(1)
    padded = (counts + MOE_BLOCK - 1) // MOE_BLOCK * MOE_BLOCK
    padded_end = jnp.cumsum(padded)
    group_start = jnp.cumsum(counts) - counts
    dest = (padded_end - padded)[sorted_e] + jnp.arange(NK, dtype=jnp.int32) - group_start[sorted_e]
    n_blocks = NK // MOE_BLOCK + N_EXPERTS
    rows = jnp.full((n_blocks * MOE_BLOCK,), N, jnp.int32).at[dest].set(sorted_tok)
    block_e = jnp.minimum(jnp.searchsorted(padded_end, jnp.arange(n_blocks, dtype=jnp.int32) * MOE_BLOCK, side='right'), N_EXPERTS - 1)
    h_pad = jnp.concatenate([h, jnp.zeros((1, D), h.dtype)], axis=0)
    xb = h_pad[rows].reshape(n_blocks, MOE_BLOCK, D)

    def expert_block(args):
        xe, e = args
        u = xe @ w_up[e] + b_up[e]
        glu = jnp.minimum(u[..., 0::2], SWIGLU_LIMIT)
        lin = jnp.clip(u[..., 1::2], -SWIGLU_LIMIT, SWIGLU_LIMIT)
        act = glu * jax.nn.sigmoid(SWIGLU_ALPHA * glu) * (lin + 1)
        return act @ w_down[e] + b_down[e]

    yb = lax.map(expert_block, (xb, block_e)).reshape(n_blocks * MOE_BLOCK, D)
    y = yb[dest] * gates.reshape(NK)[order][:, None].astype(yb.dtype)
    return jnp.zeros((N, D), yb.dtype).at[sorted_tok].add(y)


def setup_inputs(seed: int = 0) -> dict:
    key = jax.random.key(seed)
    ks = iter(jax.random.split(key, 48))
    D = D_MODEL

    def nrm(shape, s):
        return jax.random.normal(next(ks), shape, jnp.float32) * s

    return {
        'x_prompt': nrm((BATCH, SEQ, D), 1.0),
        'x_sample': nrm((DEC_BATCH, DEC_SEQ, D), 1.0),
        'c_prompt': nrm((BATCH, D), 1.0),
        'c_sample': nrm((DEC_BATCH, D), 1.0),
        'cache_k': nrm((DEPTH, DEC_BATCH, PAST_LEN, KV_HEADS, HEAD_DIM), 1.0),
        'cache_v': nrm((DEPTH, DEC_BATCH, PAST_LEN, KV_HEADS, HEAD_DIM), 1.0),
        'cache_kidx': nrm((DEPTH, DEC_BATCH, PAST_LEN, IDX_DIM), 1.0),
        'state_wkv': nrm((DEPTH, DEC_BATCH, RWKV_HEADS, HEAD_DIM, HEAD_DIM), 0.1),
        'state_shift': nrm((DEPTH, DEC_BATCH, 1, RWKV_COLS), 1.0),
        'w_ada': nrm((DEPTH, D, 6 * D), 0.5 * D ** -0.5),
        'b_ada': nrm((DEPTH, 6 * D), 0.02),
        'g_mix': 1.0 + nrm((DEPTH, D), 0.02),
        'g_ffn': 1.0 + nrm((DEPTH, D), 0.02),
        'g_final': 1.0 + nrm((D,), 0.02),
        'w_in': nrm((DEPTH, D, IN_COLS), D ** -0.5),
        'mu_shift': jax.random.uniform(next(ks), (DEPTH, RWKV_COLS), jnp.float32),
        'w0': jax.random.uniform(next(ks), (DEPTH, RWKV_WIDTH), jnp.float32, minval=-3.0, maxval=1.0),
        'w_lora_w': nrm((DEPTH, W_LORA, RWKV_WIDTH), 0.3 * W_LORA ** -0.5),
        'a0': nrm((DEPTH, RWKV_WIDTH), 0.5),
        'w_lora_a': nrm((DEPTH, A_LORA, RWKV_WIDTH), 0.3 * A_LORA ** -0.5),
        'w_lora_g': nrm((DEPTH, G_LORA, RWKV_WIDTH), G_LORA ** -0.5),
        'k_k': 0.85 + nrm((DEPTH, RWKV_WIDTH), 0.02),
        'k_a': 1.0 + nrm((DEPTH, RWKV_WIDTH), 0.02),
        'r_k': nrm((DEPTH, RWKV_HEADS, HEAD_DIM), 0.1),
        'lnx_w': 1.0 + nrm((DEPTH, RWKV_WIDTH), 0.02),
        'lnx_b': nrm((DEPTH, RWKV_WIDTH), 0.02),
        'w_out': nrm((DEPTH, MIX_WIDTH, D), MIX_WIDTH ** -0.5),
        'router_w': nrm((DEPTH, D, N_EXPERTS), D ** -0.5),
        'router_b': nrm((DEPTH, N_EXPERTS), 0.01),
        'w_up': nrm((DEPTH, N_EXPERTS, D, 2 * D_FF), D ** -0.5),
        'b_up': nrm((DEPTH, N_EXPERTS, 2 * D_FF), 0.01),
        'w_down': nrm((DEPTH, N_EXPERTS, D_FF, D), D_FF ** -0.5),
        'b_down': nrm((DEPTH, N_EXPERTS, D), 0.01),
    }


def reference(x_prompt, x_sample, c_prompt, c_sample, cache_k, cache_v, cache_kidx, state_wkv,
              state_shift, w_ada, b_ada, g_mix, g_ffn, g_final, w_in, mu_shift, w0, w_lora_w, a0,
              w_lora_a, w_lora_g, k_k, k_a, r_k, lnx_w, lnx_b, w_out, router_w, router_b, w_up,
              b_up, w_down, b_down):
    B, T, D = x_prompt.shape
    DB, DT, _ = x_sample.shape
    P = cache_k.shape[2]
    pos_p = jnp.arange(T, dtype=jnp.int32)
    pos_s = P + jnp.arange(DT, dtype=jnp.int32)
    pos_all_s = jnp.arange(P + DT, dtype=jnp.int32)
    n_sel_p = min(TOPK_MAX, T // 4)
    n_sel_s = min(TOPK_MAX, (P + DT) // 4)
    xp, xs = x_prompt, x_sample
    kp_l, vp_l, kip_l, wkvp_l, shp_l = [], [], [], [], []
    ks_l, vs_l, kis_l, wkvs_l, shs_l = [], [], [], [], []
    for l in range(DEPTH):
        sh1p, sc1p, gt1p, sh2p, sc2p, gt2p = ada_mod(c_prompt, w_ada[l], b_ada[l])
        sh1s, sc1s, gt1s, sh2s, sc2s, gt2s = ada_mod(c_sample, w_ada[l], b_ada[l])
        pp = modulate(rms_norm(xp, g_mix[l]), sh1p, sc1p) @ w_in[l]
        ps = modulate(rms_norm(xs, g_mix[l]), sh1s, sc1s) @ w_in[l]
        rw = (mu_shift[l], w0[l], w_lora_w[l], a0[l], w_lora_a[l], w_lora_g[l], k_k[l], k_a[l],
              r_k[l], lnx_w[l], lnx_b[l])
        o_rp, wkv_p, sh_p = rwkv7_mix(pp[..., :RWKV_COLS], jnp.zeros((B, 1, RWKV_COLS), pp.dtype),
                                      jnp.zeros((B, RWKV_HEADS, HEAD_DIM, HEAD_DIM), jnp.float32), *rw)
        o_rs, wkv_s, sh_s = rwkv7_mix(ps[..., :RWKV_COLS], state_shift[l], state_wkv[l], *rw)
        qp, kp, vp, qip, kip, wip = dsa_project(pp[..., RWKV_COLS:], pos_p)
        qs, kn, vn, qis, kin, wis = dsa_project(ps[..., RWKV_COLS:], pos_s)
        o_ap = dsa_prompt(qp, qip, wip, pos_p, kp, vp, kip, n_sel_p)
        k_all = jnp.concatenate([cache_k[l].astype(kn.dtype), kn], axis=1)
        v_all = jnp.concatenate([cache_v[l].astype(vn.dtype), vn], axis=1)
        ki_all = jnp.concatenate([cache_kidx[l].astype(kin.dtype), kin], axis=1)
        o_as = dsa_attend(qs, qis, wis, pos_s, k_all, v_all, ki_all, pos_all_s, n_sel_s)
        xp = xp + gt1p[:, None, :] * (jnp.concatenate([o_rp, o_ap], axis=-1) @ w_out[l])
        xs = xs + gt1s[:, None, :] * (jnp.concatenate([o_rs, o_as], axis=-1) @ w_out[l])
        moe_w = (router_w[l], router_b[l], w_up[l], b_up[l], w_down[l], b_down[l])
        hp2 = modulate(rms_norm(xp, g_ffn[l]), sh2p, sc2p)
        hs2 = modulate(rms_norm(xs, g_ffn[l]), sh2s, sc2s)
        xp = xp + gt2p[:, None, :] * moe_ffn(hp2.reshape(B * T, D), *moe_w).reshape(B, T, D)
        xs = xs + gt2s[:, None, :] * moe_ffn(hs2.reshape(DB * DT, D), *moe_w).reshape(DB, DT, D)
        kp_l.append(kp); vp_l.append(vp); kip_l.append(kip); wkvp_l.append(wkv_p); shp_l.append(sh_p)
        ks_l.append(kn); vs_l.append(vn); kis_l.append(kin); wkvs_l.append(wkv_s); shs_l.append(sh_s)
    y_prompt = rms_norm(xp, g_final)
    y_sample = rms_norm(xs, g_final)
    k_prompt, v_prompt, kidx_prompt = jnp.stack(kp_l), jnp.stack(vp_l), jnp.stack(kip_l)
    wkv_prompt, shift_prompt = jnp.stack(wkvp_l), jnp.stack(shp_l)
    k_sample, v_sample, kidx_sample = jnp.stack(ks_l), jnp.stack(vs_l), jnp.stack(kis_l)
    wkv_sample, shift_sample = jnp.stack(wkvs_l), jnp.stack(shs_l)
    return (y_prompt, y_sample, k_prompt, v_prompt, kidx_prompt, wkv_prompt, shift_prompt,
            k_sample, v_sample, kidx_sample, wkv_sample, shift_sample)
```

```python
import functools

import numpy as np
import jax
import jax.numpy as jnp
from jax import lax
from jax.experimental import pallas as pl
from jax.experimental.pallas import tpu as pltpu

F32 = jnp.float32
BF16 = jnp.bfloat16
HIGHEST = lax.Precision.HIGHEST

D_MODEL = 1024
CHUNK = 64
HEAD_DIM = 64
RWKV_HEADS = 8
RWKV_WIDTH = RWKV_HEADS * HEAD_DIM
W_LORA = 64
A_LORA = 64
G_LORA = 128
RWKV_COLS = 3 * RWKV_WIDTH + W_LORA + A_LORA + G_LORA
GN_EPS = 64e-5
ATT_HEADS = 8
ATT_WIDTH = ATT_HEADS * HEAD_DIM
KV_HEADS = 2
KV_WIDTH = KV_HEADS * HEAD_DIM
IDX_HEADS = 4
IDX_DIM = 64
TOPK_MAX = 256
Q_BLOCK = 128
ROPE_THETA = 500000.0
ROT_DIM = HEAD_DIM // 4
N_EXPERTS = 32
TOP_K = 4
D_FF = 1024
SWIGLU_ALPHA = 1.702
SWIGLU_LIMIT = 7.0
NORM_EPS = 1e-5

LANES = 128
MOE_ROWS = 128
VMEM_LIMIT = 56 * 1024 * 1024

QI_WIDTH = IDX_HEADS * IDX_DIM
KW_WIDTH = LANES


def _cparams(sem):
    return pltpu.CompilerParams(dimension_semantics=sem, vmem_limit_bytes=VMEM_LIMIT)


def _ada_kernel(c_ref, w_ref, b_ref, o_ref):
    c = c_ref[...]
    s = c * jax.nn.sigmoid(c)
    o_ref[...] = jnp.dot(s, w_ref[...], precision=HIGHEST, preferred_element_type=F32) + b_ref[...]


def _ada(c, w, b):
    rows, d = c.shape
    n = w.shape[1]
    tn = 1536
    return pl.pallas_call(
        _ada_kernel,
        grid=(n // tn,),
        in_specs=[pl.BlockSpec((rows, d), lambda j: (0, 0)),
                  pl.BlockSpec((d, tn), lambda j: (0, j)),
                  pl.BlockSpec((1, tn), lambda j: (0, j))],
        out_specs=pl.BlockSpec((rows, tn), lambda j: (0, j)),
        out_shape=jax.ShapeDtypeStruct((rows, n), F32),
        compiler_params=_cparams(("arbitrary",)),
        name="ada",
    )(c, w, b.reshape(1, n))


def _rope_slab(y, tab_ref):
    return (y * tab_ref[0] + pltpu.roll(y, LANES - ROT_DIM // 2, 1) * tab_ref[1]
            + pltpu.roll(y, ROT_DIM // 2, 1) * tab_ref[2])


def _norm_mod(x, g, sh, sc):
    var = jnp.mean(x * x, axis=-1, keepdims=True)
    return (x * lax.rsqrt(var + NORM_EPS) * g) * (1.0 + sc) + sh


def _inproj_kernel(x_ref, g_ref, sh_ref, sc_ref, wr_ref, wa_ref, tab_ref, tabk_ref,
                   rw_ref, q_ref, k_ref, v_ref, qi_ref, kw_ref):
    h = _norm_mod(x_ref[...], g_ref[...], sh_ref[0], sc_ref[0]).astype(BF16)
    rw_ref[...] = jnp.dot(h, wr_ref[...], preferred_element_type=F32)
    pa = jnp.dot(h, wa_ref[...], preferred_element_type=F32)
    off = 0
    for ref, width, rot in ((q_ref, ATT_WIDTH, True), (k_ref, KV_WIDTH, True), (v_ref, KV_WIDTH, False),
                            (qi_ref, QI_WIDTH, True)):
        for s in range(width // LANES):
            slab = pa[:, off + s * LANES: off + (s + 1) * LANES]
            ref[:, s * LANES:(s + 1) * LANES] = _rope_slab(slab, tab_ref) if rot else slab
        off += width
    kw_ref[...] = _rope_slab(pa[:, off:off + LANES], tabk_ref)


def _rope_tables(pos):
    half = ROT_DIM // 2
    inv = ROPE_THETA ** (-jnp.arange(0, ROT_DIM, 2, dtype=F32) / ROT_DIM)
    ang = pos.astype(F32)[:, None] * inv[None, :]
    cos, sin = jnp.cos(ang), jnp.sin(ang)
    t = pos.shape[0]
    one = jnp.ones((t, HEAD_DIM - ROT_DIM), F32)
    zero_r = jnp.zeros((t, HEAD_DIM - ROT_DIM), F32)
    zero_h = jnp.zeros((t, half), F32)
    c_head = jnp.concatenate([cos, cos, one], axis=1)
    up_head = jnp.concatenate([-sin, zero_h, zero_r], axis=1)
    dn_head = jnp.concatenate([zero_h, sin, zero_r], axis=1)
    tab = jnp.stack([jnp.tile(c_head, (1, 2)), jnp.tile(up_head, (1, 2)), jnp.tile(dn_head, (1, 2))])
    wscale = jnp.concatenate([jnp.full((t, IDX_HEADS), IDX_HEADS ** -0.5, F32),
                              jnp.ones((t, HEAD_DIM - IDX_HEADS), F32)], axis=1)
    zero64 = jnp.zeros((t, HEAD_DIM), F32)
    tabk = jnp.stack([jnp.concatenate([c_head, wscale], axis=1),
                      jnp.concatenate([up_head, zero64], axis=1),
                      jnp.concatenate([dn_head, zero64], axis=1)])
    return tab, tabk


def _inproj(x, g, sh, sc, w_r, w_a, tab, tabk, tm):
    b, t, d = x.shape
    nt = t // tm
    n = b * t
    x2 = x.reshape(n, d)
    widths = (RWKV_COLS, ATT_WIDTH, KV_WIDTH, KV_WIDTH, QI_WIDTH, KW_WIDTH)
    row = lambda w: pl.BlockSpec((tm, w), lambda i: (i, 0))
    mod = pl.BlockSpec((1, 1, d), lambda i: (i // nt, 0, 0))
    tabspec = pl.BlockSpec((3, tm, LANES), lambda i: (0, i % nt, 0))
    return pl.pallas_call(
        _inproj_kernel,
        grid=(n // tm,),
        in_specs=[row(d), pl.BlockSpec((1, d), lambda i: (0, 0)), mod, mod,
                  pl.BlockSpec(w_r.shape, lambda i: (0, 0)), pl.BlockSpec(w_a.shape, lambda i: (0, 0)),
                  tabspec, tabspec],
        out_specs=[row(w) for w in widths],
        out_shape=[jax.ShapeDtypeStruct((n, w), F32) for w in widths],
        compiler_params=_cparams(("arbitrary",)),
        name="inproj",
    )(x2, g.reshape(1, d), sh, sc, w_r, w_a, tab, tabk)


def _outproj_kernel(or_ref, oa_ref, x_ref, w_ref, gt_ref, g_ref, sh_ref, sc_ref, x1_ref, h2_ref):
    m = jnp.dot(or_ref[...].astype(BF16), w_ref[:RWKV_WIDTH, :], preferred_element_type=F32)
    m = m + jnp.dot(oa_ref[...].astype(BF16), w_ref[RWKV_WIDTH:, :], preferred_element_type=F32)
    x1 = x_ref[...] + gt_ref[0] * m
    x1_ref[...] = x1
    h2_ref[...] = _norm_mod(x1, g_ref[...], sh_ref[0], sc_ref[0])


def _outproj(o_r, o_a, x, w_out, gt, g, sh, sc, tm):
    b, t, d = x.shape
    nt = t // tm
    n = b * t
    row = lambda w: pl.BlockSpec((tm, w), lambda i: (i, 0))
    mod = pl.BlockSpec((1, 1, d), lambda i: (i // nt, 0, 0))
    return pl.pallas_call(
        _outproj_kernel,
        grid=(n // tm,),
        in_specs=[row(RWKV_WIDTH), row(ATT_WIDTH), row(d), pl.BlockSpec(w_out.shape, lambda i: (0, 0)),
                  mod, pl.BlockSpec((1, d), lambda i: (0, 0)), mod, mod],
        out_specs=[row(d), row(d)],
        out_shape=[jax.ShapeDtypeStruct((n, d), F32)] * 2,
        compiler_params=_cparams(("arbitrary",)),
        name="outproj",
    )(o_r.reshape(n, RWKV_WIDTH), o_a.reshape(n, ATT_WIDTH), x.reshape(n, d), w_out, gt, g.reshape(1, d), sh, sc)


def _moe_kernel(h_ref, rwt_ref, rb_ref, tri_ref, wup_ref, bup_ref, wdn_ref, bdn_ref, y_ref,
                hb_s, rank_s, gate_s):
    e = pl.program_id(1)
    tm = h_ref.shape[0]

    @pl.when(e == 0)
    def _route():
        h = h_ref[...]
        hb_s[...] = h.astype(BF16)
        logits = lax.dot_general(rwt_ref[...], h, (((1,), (1,)), ((), ())), precision=HIGHEST,
                                 preferred_element_type=F32) + rb_ref[...]
        eidx = lax.broadcasted_iota(jnp.int32, logits.shape, 0)
        work = logits
        top = None
        for _ in range(TOP_K):
            m = jnp.max(work, axis=0, keepdims=True)
            if top is None:
                top = m
            first = jnp.min(jnp.where(work == m, eidx, N_EXPERTS), axis=0, keepdims=True)
            work = jnp.where(eidx == first, -jnp.inf, work)
        sel = work != logits
        ex = jnp.where(sel, jnp.exp(logits - top), 0.0)
        gate_s[...] = ex / jnp.sum(ex, axis=0, keepdims=True)
        before = jnp.dot(jnp.where(sel, 1.0, 0.0).astype(BF16), tri_ref[...], preferred_element_type=F32)
        rank_s[...] = jnp.where(sel, before, -1.0)
        y_ref[...] = jnp.zeros(y_ref.shape, F32)

    r_row = rank_s[pl.ds(e, 1), :]
    g_row = gate_s[pl.ds(e, 1), :]
    count = jnp.sum(jnp.where(r_row >= 0.0, 1, 0).astype(jnp.int32))
    n_blocks = (count + MOE_ROWS - 1) // MOE_ROWS

    def block(j, carry):
        rows = (lax.broadcasted_iota(jnp.int32, (MOE_ROWS, tm), 0) + j * MOE_ROWS).astype(F32)
        hit = r_row == rows
        p = jnp.where(hit, 1.0, 0.0).astype(BF16)
        xe = jnp.dot(p, hb_s[...], preferred_element_type=F32).astype(BF16)
        u = jnp.dot(xe, wup_ref[0], preferred_element_type=F32) + bup_ref[0]
        glu = jnp.minimum(u[:, :D_FF], SWIGLU_LIMIT)
        lin = jnp.clip(u[:, D_FF:], -SWIGLU_LIMIT, SWIGLU_LIMIT)
        act = glu * jax.nn.sigmoid(SWIGLU_ALPHA * glu) * (lin + 1.0)
        yb = jnp.dot(act.astype(BF16), wdn_ref[0], preferred_element_type=F32) + bdn_ref[0]
        g_rows = jnp.sum(jnp.where(hit, g_row, 0.0), axis=1, keepdims=True)
        ys = (yb * g_rows).astype(BF16)
        y_ref[...] += lax.dot_general(p, ys, (((0,), (0,)), ((), ())), preferred_element_type=F32)
        return carry

    lax.fori_loop(0, n_blocks, block, 0)


def _moe(h, rwt, rb, wup, bup, wdn, bdn, tm):
    n, d = h.shape
    tri = (lax.broadcasted_iota(jnp.int32, (tm, tm), 0) < lax.broadcasted_iota(jnp.int32, (tm, tm), 1)).astype(BF16)
    return pl.pallas_call(
        _moe_kernel,
        grid=(n // tm, N_EXPERTS),
        in_specs=[pl.BlockSpec((tm, d), lambda i, e: (i, 0)),
                  pl.BlockSpec((N_EXPERTS, d), lambda i, e: (0, 0)),
                  pl.BlockSpec((N_EXPERTS, 1), lambda i, e: (0, 0)),
                  pl.BlockSpec((tm, tm), lambda i, e: (0, 0)),
                  pl.BlockSpec((1, d, 2 * D_FF), lambda i, e: (e, 0, 0)),
                  pl.BlockSpec((1, 1, 2 * D_FF), lambda i, e: (e, 0, 0)),
                  pl.BlockSpec((1, D_FF, d), lambda i, e: (e, 0, 0)),
                  pl.BlockSpec((1, 1, d), lambda i, e: (e, 0, 0))],
        out_specs=pl.BlockSpec((tm, d), lambda i, e: (i, 0)),
        out_shape=jax.ShapeDtypeStruct((n, d), F32),
        scratch_shapes=[pltpu.VMEM((tm, d), BF16), pltpu.VMEM((N_EXPERTS, tm), F32),
                        pltpu.VMEM((N_EXPERTS, tm), F32)],
        compiler_params=_cparams(("arbitrary", "arbitrary")),
        name="moe",
    )(h, rwt, rb, tri, wup, bup, wdn, bdn)


def _final_kernel(x1_ref, y_ref, gt_ref, g_ref, o_ref):
    x = x1_ref[...] + gt_ref[0] * y_ref[...]
    var = jnp.mean(x * x, axis=-1, keepdims=True)
    o_ref[...] = x * lax.rsqrt(var + NORM_EPS) * g_ref[...]


def _final(x1, y, gt, g, b, t, tm):
    n, d = x1.shape
    nt = t // tm
    row = pl.BlockSpec((tm, d), lambda i: (i, 0))
    return pl.pallas_call(
        _final_kernel,
        grid=(n // tm,),
        in_specs=[row, row, pl.BlockSpec((1, 1, d), lambda i: (i // nt, 0, 0)),
                  pl.BlockSpec((1, d), lambda i: (0, 0))],
        out_specs=row,
        out_shape=jax.ShapeDtypeStruct((n, d), F32),
        compiler_params=_cparams(("arbitrary",)),
        name="final",
    )(x1, y, gt, g.reshape(1, d)).reshape(b, t, d)


def _split(t, sizes):
    offs = np.cumsum((0,) + tuple(sizes))
    return tuple(t[..., int(offs[i]):int(offs[i + 1])] for i in range(len(sizes)))


def _rwkv7_mix(p, prev, s0, mu, w0, w2, a0, a2, g2, k_k, k_a, r_k, lnx_w, lnx_b):
    B, T, _ = p.shape
    pf = p
    p_prev = jnp.concatenate([prev, pf[:, :-1]], axis=1)
    xs = pf + (p_prev - pf) * mu
    r, k, v, dw, da, dg = _split(xs, (RWKV_WIDTH, RWKV_WIDTH, RWKV_WIDTH, W_LORA, A_LORA, G_LORA))
    w_log = -jax.nn.softplus(-(w0 + jnp.tanh(dw) @ w2)) - 0.5
    decay = jnp.exp(-jnp.exp(w_log))
    a = jax.nn.sigmoid(a0 + da @ a2)
    g = jax.nn.sigmoid(dg) @ g2
    heads = lambda t: t.reshape(B, T, RWKV_HEADS, HEAD_DIM)
    kk = heads(k * k_k)
    kk = kk / jnp.maximum(jnp.sqrt(jnp.sum(kk * kk, axis=-1, keepdims=True)), 1e-12)
    k = k * (1 + (a - 1) * k_a)
    r_h, w_h, k_h, v_h, a_h = heads(r), heads(decay), heads(k), heads(v), heads(a)
    b_h = kk * a_h
    seq = tuple(jnp.moveaxis(t, 1, 0) for t in (r_h, w_h, k_h, v_h, kk, b_h))

    def step(S, inp):
        r_t, w_t, k_t, v_t, kk_t, b_t = inp
        sa = jnp.einsum('bhvk,bhk->bhv', S, -kk_t)
        S = S * w_t[:, :, None, :] + sa[..., None] * b_t[:, :, None, :] + v_t[..., None] * k_t[:, :, None, :]
        return S, jnp.einsum('bhvk,bhk->bhv', S, r_t)

    s_T, o = lax.scan(step, s0, seq)
    o = jnp.moveaxis(o, 0, 1)
    mean = jnp.mean(o, axis=-1, keepdims=True)
    var = jnp.mean(jnp.square(o - mean), axis=-1, keepdims=True)
    o = ((o - mean) * lax.rsqrt(var + GN_EPS)).reshape(B, T, RWKV_WIDTH) * lnx_w + lnx_b
    bonus = jnp.sum(r_h * k_h * r_k, axis=-1, keepdims=True) * v_h
    o = (o + bonus.reshape(B, T, RWKV_WIDTH)) * g
    return o, s_T, p[:, -1:]


def _dsa_attend(q, qi, wi, qpos, k_all, v_all, ki_all, kpos, n_sel):
    B, Q = q.shape[:2]
    logits = jnp.einsum('bqhd,bld->bqhl', qi, ki_all, preferred_element_type=F32) * IDX_DIM ** -0.5
    score = jnp.einsum('bqh,bqhl->bql', wi, jax.nn.relu(logits))
    admissible = (kpos[None, :] // CHUNK) <= (qpos[:, None] // CHUNK)
    score = jnp.where(admissible[None], score, -jnp.inf)
    _, sel = lax.top_k(score, n_sel)
    valid = (kpos[sel] // CHUNK) <= (qpos[None, :, None] // CHUNK)
    gather = jax.vmap(lambda t, i: t[i])
    k_sel = gather(k_all, sel)
    v_sel = gather(v_all, sel)
    qg = q.reshape(B, Q, KV_HEADS, ATT_HEADS // KV_HEADS, HEAD_DIM)
    s = jnp.einsum('bqgrd,bqkgd->bqgrk', qg, k_sel, preferred_element_type=F32) * HEAD_DIM ** -0.5
    s = jnp.where(valid[:, :, None, None, :], s, -jnp.inf)
    pr = jax.nn.softmax(s, axis=-1)
    o = jnp.einsum('bqgrk,bqkgd->bqgrd', pr, v_sel)
    return o.reshape(B, Q, ATT_WIDTH)


def _dsa_prompt(q, qi, wi, pos, k, v, ki, n_sel):
    B, T = q.shape[:2]
    nb = T // Q_BLOCK

    def to_blocks(t):
        return jnp.moveaxis(t.reshape((B, nb, Q_BLOCK) + t.shape[2:]), 1, 0)

    def one_block(args):
        qb, qib, wib, pb = args
        return _dsa_attend(qb, qib, wib, pb, k, v, ki, pos, n_sel)

    o = lax.map(one_block, (to_blocks(q), to_blocks(qi), to_blocks(wi), pos.reshape(nb, Q_BLOCK)))
    return jnp.moveaxis(o, 0, 1).reshape(B, T, ATT_WIDTH)


def _branch(x, mods, pos, prev, s0, caches, weights, tm, tm_moe):
    (g_mix, g_ffn, g_final, w_r, w_a, rw, w_out, rwt, rb, wup, bup, wdn, bdn) = weights
    sh1, sc1, gt1, sh2, sc2, gt2 = mods
    b, t, d = x.shape
    tab, tabk = _rope_tables(pos)
    p_r, q, k, v, qi, kw = _inproj(x, g_mix, sh1, sc1, w_r, w_a, tab, tabk, tm)
    o_r, wkv, shift = _rwkv7_mix(p_r.reshape(b, t, RWKV_COLS), prev, s0, *rw)
    q4 = q.reshape(b, t, ATT_HEADS, HEAD_DIM)
    k4 = k.reshape(b, t, KV_HEADS, HEAD_DIM)
    v4 = v.reshape(b, t, KV_HEADS, HEAD_DIM)
    qi4 = qi.reshape(b, t, IDX_HEADS, IDX_DIM)
    kw3 = kw.reshape(b, t, KW_WIDTH)
    ki = kw3[..., :IDX_DIM]
    wi = kw3[..., IDX_DIM:IDX_DIM + IDX_HEADS]
    if caches is None:
        o_a = _dsa_prompt(q4, qi4, wi, pos, k4, v4, ki, min(TOPK_MAX, t // 4))
    else:
        ck, cv, cki = caches
        past = ck.shape[1]
        k_all = jnp.concatenate([ck, k4], axis=1)
        v_all = jnp.concatenate([cv, v4], axis=1)
        ki_all = jnp.concatenate([cki, ki], axis=1)
        pos_all = jnp.arange(past + t, dtype=jnp.int32)
        o_a = _dsa_attend(q4, qi4, wi, pos, k_all, v_all, ki_all, pos_all, min(TOPK_MAX, (past + t) // 4))
    x1, h2 = _outproj(o_r, o_a, x, w_out, gt1, g_ffn, sh2, sc2, tm)
    y_moe = _moe(h2, rwt, rb, wup, bup, wdn, bdn, tm_moe)
    y = _final(x1, y_moe, gt2, g_final, b, t, tm)
    return y, k4, v4, ki, wkv, shift


def kernel(x_prompt, x_sample, c_prompt, c_sample, cache_k, cache_v, cache_kidx, state_wkv, state_shift,
           w_ada, b_ada, g_mix, g_ffn, g_final, w_in, mu_shift, w0, w_lora_w, a0, w_lora_a, w_lora_g,
           k_k, k_a, r_k, lnx_w, lnx_b, w_out, router_w, router_b, w_up, b_up, w_down, b_down):
    B, T, D = x_prompt.shape
    DB, DT, _ = x_sample.shape
    P = cache_k.shape[2]
    l = 0
    assert w_ada.shape[0] == 1

    rows = B + DB
    pad = (-rows) % 8
    c_all = jnp.concatenate([c_prompt, c_sample, jnp.zeros((pad, D), F32)], axis=0)
    m = _ada(c_all, w_ada[l], b_ada[l])
    mods_p = tuple(t.reshape(B, 1, D) for t in jnp.split(m[:B], 6, axis=-1))
    mods_s = tuple(t.reshape(DB, 1, D) for t in jnp.split(m[B:rows], 6, axis=-1))

    w_r = w_in[l][:, :RWKV_COLS].astype(BF16)
    w_att = w_in[l][:, RWKV_COLS:]
    att_pad = KW_WIDTH - IDX_DIM - IDX_HEADS
    w_a = jnp.concatenate([w_att, jnp.zeros((D, att_pad), F32)], axis=1).astype(BF16)
    rw = (mu_shift[l], w0[l], w_lora_w[l], a0[l], w_lora_a[l], w_lora_g[l], k_k[l], k_a[l], r_k[l],
          lnx_w[l], lnx_b[l])
    wup = jnp.concatenate([w_up[l][..., 0::2], w_up[l][..., 1::2]], axis=-1).astype(BF16)
    bup = jnp.concatenate([b_up[l][..., 0::2], b_up[l][..., 1::2]], axis=-1).reshape(N_EXPERTS, 1, 2 * D_FF)
    wdn = w_down[l].astype(BF16)
    bdn = b_down[l].reshape(N_EXPERTS, 1, D)
    weights = (g_mix[l], g_ffn[l], g_final, w_r, w_a, rw, w_out[l].astype(BF16), router_w[l].T,
               router_b[l].reshape(N_EXPERTS, 1), wup, bup, wdn, bdn)

    pos_p = jnp.arange(T, dtype=jnp.int32)
    pos_s = P + jnp.arange(DT, dtype=jnp.int32)
    yp, kp, vp, kip, wkvp, shp = _branch(
        x_prompt, mods_p, pos_p, jnp.zeros((B, 1, RWKV_COLS), F32),
        jnp.zeros((B, RWKV_HEADS, HEAD_DIM, HEAD_DIM), F32), None, weights, 512, 1024)
    ys, ks, vs, kis, wkvs, shs = _branch(
        x_sample, mods_s, pos_s, state_shift[l], state_wkv[l],
        (cache_k[l], cache_v[l], cache_kidx[l]), weights, DT, DB * DT)
    return (yp, ys, kp[None], vp[None], kip[None], wkvp[None], shp[None],
            ks[None], vs[None], kis[None], wkvs[None], shs[None])
```

```python
import functools

import numpy as np
import jax
import jax.numpy as jnp
from jax import lax
from jax.experimental import pallas as pl
from jax.experimental.pallas import tpu as pltpu

F32 = jnp.float32
BF16 = jnp.bfloat16
HIGHEST = lax.Precision.HIGHEST

D_MODEL = 1024
CHUNK = 64
HEAD_DIM = 64
RWKV_HEADS = 8
RWKV_WIDTH = RWKV_HEADS * HEAD_DIM
W_LORA = 64
A_LORA = 64
G_LORA = 128
RWKV_COLS = 3 * RWKV_WIDTH + W_LORA + A_LORA + G_LORA
GN_EPS = 64e-5
ATT_HEADS = 8
ATT_WIDTH = ATT_HEADS * HEAD_DIM
KV_HEADS = 2
KV_WIDTH = KV_HEADS * HEAD_DIM
IDX_HEADS = 4
IDX_DIM = 64
TOPK_MAX = 256
Q_BLOCK = 128
ROPE_THETA = 500000.0
ROT_DIM = HEAD_DIM // 4
N_EXPERTS = 32
TOP_K = 4
D_FF = 1024
SWIGLU_ALPHA = 1.702
SWIGLU_LIMIT = 7.0
NORM_EPS = 1e-5

LANES = 128
MOE_ROWS = 128
VMEM_LIMIT = 56 * 1024 * 1024

QI_WIDTH = IDX_HEADS * IDX_DIM
KW_WIDTH = LANES


def _cparams(sem):
    return pltpu.CompilerParams(dimension_semantics=sem, vmem_limit_bytes=VMEM_LIMIT)


def _ada_kernel(c_ref, w_ref, b_ref, o_ref):
    c = c_ref[...]
    s = c * jax.nn.sigmoid(c)
    o_ref[...] = jnp.dot(s, w_ref[...], precision=HIGHEST, preferred_element_type=F32) + b_ref[...]


def _ada(c, w, b):
    rows, d = c.shape
    n = w.shape[1]
    tn = 1536
    return pl.pallas_call(
        _ada_kernel,
        grid=(n // tn,),
        in_specs=[pl.BlockSpec((rows, d), lambda j: (0, 0)),
                  pl.BlockSpec((d, tn), lambda j: (0, j)),
                  pl.BlockSpec((1, tn), lambda j: (0, j))],
        out_specs=pl.BlockSpec((rows, tn), lambda j: (0, j)),
        out_shape=jax.ShapeDtypeStruct((rows, n), F32),
        compiler_params=_cparams(("arbitrary",)),
        name="ada",
    )(c, w, b.reshape(1, n))


def _rope_slab(y, tab_ref):
    return (y * tab_ref[0] + pltpu.roll(y, LANES - ROT_DIM // 2, 1) * tab_ref[1]
            + pltpu.roll(y, ROT_DIM // 2, 1) * tab_ref[2])


def _norm_mod(x, g, sh, sc):
    var = jnp.mean(x * x, axis=-1, keepdims=True)
    return (x * lax.rsqrt(var + NORM_EPS) * g) * (1.0 + sc) + sh


def _inproj_kernel(x_ref, g_ref, sh_ref, sc_ref, wr_ref, wa_ref, tab_ref, tabk_ref,
                   rw_ref, q_ref, k_ref, v_ref, qi_ref, kw_ref):
    h = _norm_mod(x_ref[...], g_ref[...], sh_ref[0], sc_ref[0]).astype(BF16)
    rw_ref[...] = jnp.dot(h, wr_ref[...], preferred_element_type=F32)
    pa = jnp.dot(h, wa_ref[...], preferred_element_type=F32)
    off = 0
    for ref, width, rot in ((q_ref, ATT_WIDTH, True), (k_ref, KV_WIDTH, True), (v_ref, KV_WIDTH, False),
                            (qi_ref, QI_WIDTH, True)):
        for s in range(width // LANES):
            slab = pa[:, off + s * LANES: off + (s + 1) * LANES]
            ref[:, s * LANES:(s + 1) * LANES] = _rope_slab(slab, tab_ref) if rot else slab
        off += width
    kw_ref[...] = _rope_slab(pa[:, off:off + LANES], tabk_ref)


def _rope_tables(pos):
    half = ROT_DIM // 2
    inv = ROPE_THETA ** (-jnp.arange(0, ROT_DIM, 2, dtype=F32) / ROT_DIM)
    ang = pos.astype(F32)[:, None] * inv[None, :]
    cos, sin = jnp.cos(ang), jnp.sin(ang)
    t = pos.shape[0]
    one = jnp.ones((t, HEAD_DIM - ROT_DIM), F32)
    zero_r = jnp.zeros((t, HEAD_DIM - ROT_DIM), F32)
    zero_h = jnp.zeros((t, half), F32)
    c_head = jnp.concatenate([cos, cos, one], axis=1)
    up_head = jnp.concatenate([-sin, zero_h, zero_r], axis=1)
    dn_head = jnp.concatenate([zero_h, sin, zero_r], axis=1)
    tab = jnp.stack([jnp.tile(c_head, (1, 2)), jnp.tile(up_head, (1, 2)), jnp.tile(dn_head, (1, 2))])
    wscale = jnp.concatenate([jnp.full((t, IDX_HEADS), IDX_HEADS ** -0.5, F32),
                              jnp.ones((t, HEAD_DIM - IDX_HEADS), F32)], axis=1)
    zero64 = jnp.zeros((t, HEAD_DIM), F32)
    tabk = jnp.stack([jnp.concatenate([c_head, wscale], axis=1),
                      jnp.concatenate([up_head, zero64], axis=1),
                      jnp.concatenate([dn_head, zero64], axis=1)])
    return tab, tabk


def _inproj(x, g, sh, sc, w_r, w_a, tab, tabk, tm):
    b, t, d = x.shape
    nt = t // tm
    n = b * t
    x2 = x.reshape(n, d)
    widths = (RWKV_COLS, ATT_WIDTH, KV_WIDTH, KV_WIDTH, QI_WIDTH, KW_WIDTH)
    row = lambda w: pl.BlockSpec((tm, w), lambda i: (i, 0))
    mod = pl.BlockSpec((1, 1, d), lambda i: (i // nt, 0, 0))
    tabspec = pl.BlockSpec((3, tm, LANES), lambda i: (0, i % nt, 0))
    return pl.pallas_call(
        _inproj_kernel,
        grid=(n // tm,),
        in_specs=[row(d), pl.BlockSpec((1, d), lambda i: (0, 0)), mod, mod,
                  pl.BlockSpec(w_r.shape, lambda i: (0, 0)), pl.BlockSpec(w_a.shape, lambda i: (0, 0)),
                  tabspec, tabspec],
        out_specs=[row(w) for w in widths],
        out_shape=[jax.ShapeDtypeStruct((n, w), F32) for w in widths],
        compiler_params=_cparams(("arbitrary",)),
        name="inproj",
    )(x2, g.reshape(1, d), sh, sc, w_r, w_a, tab, tabk)


def _outproj_kernel(or_ref, oa_ref, x_ref, w_ref, gt_ref, g_ref, sh_ref, sc_ref, x1_ref, h2_ref):
    m = jnp.dot(or_ref[...].astype(BF16), w_ref[:RWKV_WIDTH, :], preferred_element_type=F32)
    m = m + jnp.dot(oa_ref[...].astype(BF16), w_ref[RWKV_WIDTH:, :], preferred_element_type=F32)
    x1 = x_ref[...] + gt_ref[0] * m
    x1_ref[...] = x1
    h2_ref[...] = _norm_mod(x1, g_ref[...], sh_ref[0], sc_ref[0])


def _outproj(o_r, o_a, x, w_out, gt, g, sh, sc, tm):
    b, t, d = x.shape
    nt = t // tm
    n = b * t
    row = lambda w: pl.BlockSpec((tm, w), lambda i: (i, 0))
    mod = pl.BlockSpec((1, 1, d), lambda i: (i // nt, 0, 0))
    return pl.pallas_call(
        _outproj_kernel,
        grid=(n // tm,),
        in_specs=[row(RWKV_WIDTH), row(ATT_WIDTH), row(d), pl.BlockSpec(w_out.shape, lambda i: (0, 0)),
                  mod, pl.BlockSpec((1, d), lambda i: (0, 0)), mod, mod],
        out_specs=[row(d), row(d)],
        out_shape=[jax.ShapeDtypeStruct((n, d), F32)] * 2,
        compiler_params=_cparams(("arbitrary",)),
        name="outproj",
    )(o_r.reshape(n, RWKV_WIDTH), o_a.reshape(n, ATT_WIDTH), x.reshape(n, d), w_out, gt, g.reshape(1, d), sh, sc)


def _moe_kernel(h_ref, rwt_ref, rb_ref, tri_ref, wup_ref, bup_ref, wdn_ref, bdn_ref, y_ref,
                hb_s, rank_s, gate_s):
    e = pl.program_id(1)
    tm = h_ref.shape[0]

    @pl.when(e == 0)
    def _route():
        h = h_ref[...]
        hb_s[...] = h.astype(BF16)
        logits = lax.dot_general(rwt_ref[...], h, (((1,), (1,)), ((), ())), precision=HIGHEST,
                                 preferred_element_type=F32) + rb_ref[...]
        eidx = lax.broadcasted_iota(jnp.int32, logits.shape, 0)
        work = logits
        top = None
        for _ in range(TOP_K):
            m = jnp.max(work, axis=0, keepdims=True)
            if top is None:
                top = m
            first = jnp.min(jnp.where(work == m, eidx, N_EXPERTS), axis=0, keepdims=True)
            work = jnp.where(eidx == first, -jnp.inf, work)
        sel = work != logits
        ex = jnp.where(sel, jnp.exp(logits - top), 0.0)
        gate_s[...] = ex / jnp.sum(ex, axis=0, keepdims=True)
        before = jnp.dot(jnp.where(sel, 1.0, 0.0).astype(BF16), tri_ref[...], preferred_element_type=F32)
        rank_s[...] = jnp.where(sel, before, -1.0)
        y_ref[...] = jnp.zeros(y_ref.shape, F32)

    r_row = rank_s[pl.ds(e, 1), :]
    g_row = gate_s[pl.ds(e, 1), :]
    count = jnp.sum(jnp.where(r_row >= 0.0, 1, 0).astype(jnp.int32))
    n_blocks = (count + MOE_ROWS - 1) // MOE_ROWS

    def block(j, carry):
        rows = (lax.broadcasted_iota(jnp.int32, (MOE_ROWS, tm), 0) + j * MOE_ROWS).astype(F32)
        hit = r_row == rows
        p = jnp.where(hit, 1.0, 0.0).astype(BF16)
        xe = jnp.dot(p, hb_s[...], preferred_element_type=F32).astype(BF16)
        u = jnp.dot(xe, wup_ref[0], preferred_element_type=F32) + bup_ref[0]
        glu = jnp.minimum(u[:, :D_FF], SWIGLU_LIMIT)
        lin = jnp.clip(u[:, D_FF:], -SWIGLU_LIMIT, SWIGLU_LIMIT)
        act = glu * jax.nn.sigmoid(SWIGLU_ALPHA * glu) * (lin + 1.0)
        yb = jnp.dot(act.astype(BF16), wdn_ref[0], preferred_element_type=F32) + bdn_ref[0]
        g_rows = jnp.sum(jnp.where(hit, g_row, 0.0), axis=1, keepdims=True)
        ys = (yb * g_rows).astype(BF16)
        y_ref[...] += lax.dot_general(p, ys, (((0,), (0,)), ((), ())), preferred_element_type=F32)
        return carry

    lax.fori_loop(0, n_blocks, block, 0)


def _moe(h, rwt, rb, wup, bup, wdn, bdn, tm):
    n, d = h.shape
    tri = (lax.broadcasted_iota(jnp.int32, (tm, tm), 0) < lax.broadcasted_iota(jnp.int32, (tm, tm), 1)).astype(BF16)
    return pl.pallas_call(
        _moe_kernel,
        grid=(n // tm, N_EXPERTS),
        in_specs=[pl.BlockSpec((tm, d), lambda i, e: (i, 0)),
                  pl.BlockSpec((N_EXPERTS, d), lambda i, e: (0, 0)),
                  pl.BlockSpec((N_EXPERTS, 1), lambda i, e: (0, 0)),
                  pl.BlockSpec((tm, tm), lambda i, e: (0, 0)),
                  pl.BlockSpec((1, d, 2 * D_FF), lambda i, e: (e, 0, 0)),
                  pl.BlockSpec((1, 1, 2 * D_FF), lambda i, e: (e, 0, 0)),
                  pl.BlockSpec((1, D_FF, d), lambda i, e: (e, 0, 0)),
                  pl.BlockSpec((1, 1, d), lambda i, e: (e, 0, 0))],
        out_specs=pl.BlockSpec((tm, d), lambda i, e: (i, 0)),
        out_shape=jax.ShapeDtypeStruct((n, d), F32),
        scratch_shapes=[pltpu.VMEM((tm, d), BF16), pltpu.VMEM((N_EXPERTS, tm), F32),
                        pltpu.VMEM((N_EXPERTS, tm), F32)],
        compiler_params=_cparams(("arbitrary", "arbitrary")),
        name="moe",
    )(h, rwt, rb, tri, wup, bup, wdn, bdn)


def _deinterleave_kernel(w_ref, perm_ref, o_ref):
    o_ref[0] = jnp.dot(w_ref[0].astype(BF16), perm_ref[...], preferred_element_type=F32).astype(BF16)


def _deinterleave_cast(w):
    e, d, n = w.shape
    tk = 512
    src = lax.broadcasted_iota(jnp.int32, (n, n), 0)
    dst = lax.broadcasted_iota(jnp.int32, (n, n), 1)
    perm = (src == jnp.where(dst < n // 2, 2 * dst, 2 * (dst - n // 2) + 1)).astype(BF16)
    return pl.pallas_call(
        _deinterleave_kernel,
        grid=(e, d // tk),
        in_specs=[pl.BlockSpec((1, tk, n), lambda i, j: (i, j, 0)), pl.BlockSpec((n, n), lambda i, j: (0, 0))],
        out_specs=pl.BlockSpec((1, tk, n), lambda i, j: (i, j, 0)),
        out_shape=jax.ShapeDtypeStruct((e, d, n), BF16),
        compiler_params=_cparams(("arbitrary", "arbitrary")),
        name="deinterleave",
    )(w, perm)


def _final_kernel(x1_ref, y_ref, gt_ref, g_ref, o_ref):
    x = x1_ref[...] + gt_ref[0] * y_ref[...]
    var = jnp.mean(x * x, axis=-1, keepdims=True)
    o_ref[...] = x * lax.rsqrt(var + NORM_EPS) * g_ref[...]


def _final(x1, y, gt, g, b, t, tm):
    n, d = x1.shape
    nt = t // tm
    row = pl.BlockSpec((tm, d), lambda i: (i, 0))
    return pl.pallas_call(
        _final_kernel,
        grid=(n // tm,),
        in_specs=[row, row, pl.BlockSpec((1, 1, d), lambda i: (i // nt, 0, 0)),
                  pl.BlockSpec((1, d), lambda i: (0, 0))],
        out_specs=row,
        out_shape=jax.ShapeDtypeStruct((n, d), F32),
        compiler_params=_cparams(("arbitrary",)),
        name="final",
    )(x1, y, gt, g.reshape(1, d)).reshape(b, t, d)


MASKED = -1e30
INT_MIN = -2 ** 31


def _dsa_kernel(q_ref, qi_ref, kw_ref, k_ref, vx_ref, ki_ref, tri_ref, o_ref, key_s, bias_s, wib_s,
                *, qb, kt, q0, l_valid, n_sel):
    i = pl.program_id(1)
    ns = kt // LANES
    row = lax.broadcasted_iota(jnp.int32, (qb, 1), 0)
    qpos = q0 + i * qb + row
    lim = jnp.minimum((qpos // CHUNK + 1) * CHUNK, l_valid)
    last_lim = jnp.minimum(((q0 + i * qb + qb - 1) // CHUNK + 1) * CHUNK, l_valid)
    n_kt = (last_lim + kt - 1) // kt
    lane = lax.broadcasted_iota(jnp.int32, (1, LANES), 1)

    for h in range(IDX_HEADS):
        w = kw_ref[:, IDX_DIM + h:IDX_DIM + h + 1] * (IDX_DIM ** -0.5)
        wib_s[h] = jnp.broadcast_to(w, (qb, LANES))
    qi = qi_ref[...].astype(BF16)

    def score_tile(t, c):
        ks = pl.multiple_of(t * kt, kt)
        kit = ki_ref[0, pl.ds(ks, kt), :]
        lg = [lax.dot_general(qi[:, h * IDX_DIM:(h + 1) * IDX_DIM], kit, (((1,), (1,)), ((), ())),
                              preferred_element_type=F32) for h in range(IDX_HEADS)]
        for s in range(ns):
            sc = jnp.zeros((qb, LANES), F32)
            for h in range(IDX_HEADS):
                sc = sc + wib_s[h] * jnp.maximum(lg[h][:, s * LANES:(s + 1) * LANES], 0.0)
            kpos = ks + s * LANES + lane
            sc = jnp.where(kpos < lim, sc + 0.0, -jnp.inf)
            bits = pltpu.bitcast(sc, jnp.int32)
            key_s[t, :, s * LANES:(s + 1) * LANES] = bits ^ ((bits >> 31) & 0x7FFFFFFF)
        return c

    lax.fori_loop(0, n_kt, score_tile, 0)

    def count(pred_fn):
        def tile(t, acc):
            for s in range(ns):
                acc = acc + jnp.where(pred_fn(key_s[t, :, s * LANES:(s + 1) * LANES]), 1.0, 0.0)
            return acc
        acc = lax.fori_loop(0, n_kt, tile, jnp.zeros((qb, LANES), F32))
        return jnp.sum(acc, axis=1, keepdims=True)

    def bit_step(b, lo):
        cand = lo + jnp.left_shift(jnp.int32(1), 31 - b)
        candb = jnp.broadcast_to(cand, (qb, LANES))
        return jnp.where(count(lambda k: k >= candb) >= n_sel, cand, lo)

    thr = lax.fori_loop(0, 32, bit_step, jnp.full((qb, 1), INT_MIN, jnp.int32))
    thrb = jnp.broadcast_to(thr, (qb, LANES))
    need = jnp.broadcast_to(n_sel - count(lambda k: k > thrb), (qb, LANES))

    def sel_tile(t, off):
        for s in range(ns):
            key = key_s[t, :, s * LANES:(s + 1) * LANES]
            eq = key == thrb
            pre = jnp.dot(jnp.where(eq, 1.0, 0.0).astype(BF16), tri_ref[...], preferred_element_type=F32)
            kpos = t * kt + s * LANES + lane
            keep = jnp.logical_or(key > thrb, jnp.logical_and(eq, pre[:, :LANES] + off < need))
            keep = jnp.logical_and(keep, kpos < lim)
            bias_s[t, :, s * LANES:(s + 1) * LANES] = jnp.where(keep, 0.0, MASKED)
            off = off + pre[:, LANES:]
        return off

    lax.fori_loop(0, n_kt, sel_tile, jnp.zeros((qb, LANES), F32))

    rep = ATT_HEADS // KV_HEADS
    for g in range(KV_HEADS):
        qg = jnp.concatenate([q_ref[:, (g * rep + r) * HEAD_DIM:(g * rep + r + 1) * HEAD_DIM] for r in range(rep)],
                             axis=0)
        qg = (qg * (HEAD_DIM ** -0.5)).astype(BF16)

        def att_tile(t, carry):
            m, acc = carry
            ks = pl.multiple_of(t * kt, kt)
            s = lax.dot_general(qg, k_ref[0, g, pl.ds(ks, kt), :], (((1,), (1,)), ((), ())),
                                preferred_element_type=F32)
            s = (s.reshape(rep, qb, kt) + bias_s[t][None]).reshape(rep * qb, kt)
            m_new = jnp.maximum(m, jnp.max(s, axis=1, keepdims=True))
            p = jnp.exp(s - m_new)
            pv = jnp.dot(p.astype(BF16), vx_ref[0, pl.ds(ks, kt), g * LANES:(g + 1) * LANES],
                         preferred_element_type=F32)
            return m_new, acc * jnp.exp(m - m_new) + pv

        m0 = jnp.full((rep * qb, 1), MASKED, F32)
        _, acc = lax.fori_loop(0, n_kt, att_tile, (m0, jnp.zeros((rep * qb, LANES), F32)))
        out = acc * pltpu.roll(1.0 / acc, HEAD_DIM, 1)
        for r in range(rep):
            h = g * rep + r
            o_ref[:, h * HEAD_DIM:(h + 1) * HEAD_DIM] = out[r * qb:(r + 1) * qb, :HEAD_DIM]


def _dsa(q, qi, kw, k_bf, v_bf, ki_bf, *, qb, kt, q0, l_valid, n_sel):
    b, lp = ki_bf.shape[:2]
    nq = q.shape[0] // (b * qb)
    k_g = jnp.moveaxis(k_bf, 2, 1)
    ones = jnp.ones((b, lp, KV_HEADS, HEAD_DIM), BF16)
    vx = jnp.concatenate([v_bf, ones], axis=-1).reshape(b, lp, KV_HEADS * LANES)
    tr = lax.broadcasted_iota(jnp.int32, (LANES, 2 * LANES), 0)
    tc = lax.broadcasted_iota(jnp.int32, (LANES, 2 * LANES), 1)
    tri = jnp.logical_or(tr < tc, tc >= LANES).astype(BF16)
    row = lambda w: pl.BlockSpec((qb, w), lambda bi, i: (bi * nq + i, 0))
    kern = functools.partial(_dsa_kernel, qb=qb, kt=kt, q0=q0, l_valid=l_valid, n_sel=n_sel)
    return pl.pallas_call(
        kern,
        grid=(b, nq),
        in_specs=[row(ATT_WIDTH), row(QI_WIDTH), row(KW_WIDTH),
                  pl.BlockSpec((1, KV_HEADS, lp, HEAD_DIM), lambda bi, i: (bi, 0, 0, 0)),
                  pl.BlockSpec((1, lp, KV_HEADS * LANES), lambda bi, i: (bi, 0, 0)),
                  pl.BlockSpec((1, lp, IDX_DIM), lambda bi, i: (bi, 0, 0)),
                  pl.BlockSpec((LANES, 2 * LANES), lambda bi, i: (0, 0))],
        out_specs=row(ATT_WIDTH),
        out_shape=jax.ShapeDtypeStruct((q.shape[0], ATT_WIDTH), F32),
        scratch_shapes=[pltpu.VMEM((lp // kt, qb, kt), jnp.int32), pltpu.VMEM((lp // kt, qb, kt), F32),
                        pltpu.VMEM((IDX_HEADS, qb, LANES), F32)],
        compiler_params=_cparams(("arbitrary", "arbitrary")),
        name="dsa",
    )(q, qi, kw, k_g, vx, ki_bf, tri)


def _split(t, sizes):
    offs = np.cumsum((0,) + tuple(sizes))
    return tuple(t[..., int(offs[i]):int(offs[i + 1])] for i in range(len(sizes)))


def _rwkv7_mix(p, prev, s0, mu, w0, w2, a0, a2, g2, k_k, k_a, r_k, lnx_w, lnx_b):
    B, T, _ = p.shape
    pf = p
    p_prev = jnp.concatenate([prev, pf[:, :-1]], axis=1)
    xs = pf + (p_prev - pf) * mu
    r, k, v, dw, da, dg = _split(xs, (RWKV_WIDTH, RWKV_WIDTH, RWKV_WIDTH, W_LORA, A_LORA, G_LORA))
    w_log = -jax.nn.softplus(-(w0 + jnp.tanh(dw) @ w2)) - 0.5
    decay = jnp.exp(-jnp.exp(w_log))
    a = jax.nn.sigmoid(a0 + da @ a2)
    g = jax.nn.sigmoid(dg) @ g2
    heads = lambda t: t.reshape(B, T, RWKV_HEADS, HEAD_DIM)
    kk = heads(k * k_k)
    kk = kk / jnp.maximum(jnp.sqrt(jnp.sum(kk * kk, axis=-1, keepdims=True)), 1e-12)
    k = k * (1 + (a - 1) * k_a)
    r_h, w_h, k_h, v_h, a_h = heads(r), heads(decay), heads(k), heads(v), heads(a)
    b_h = kk * a_h
    seq = tuple(jnp.moveaxis(t, 1, 0) for t in (r_h, w_h, k_h, v_h, kk, b_h))

    def step(S, inp):
        r_t, w_t, k_t, v_t, kk_t, b_t = inp
        sa = jnp.einsum('bhvk,bhk->bhv', S, -kk_t)
        S = S * w_t[:, :, None, :] + sa[..., None] * b_t[:, :, None, :] + v_t[..., None] * k_t[:, :, None, :]
        return S, jnp.einsum('bhvk,bhk->bhv', S, r_t)

    s_T, o = lax.scan(step, s0, seq)
    o = jnp.moveaxis(o, 0, 1)
    mean = jnp.mean(o, axis=-1, keepdims=True)
    var = jnp.mean(jnp.square(o - mean), axis=-1, keepdims=True)
    o = ((o - mean) * lax.rsqrt(var + GN_EPS)).reshape(B, T, RWKV_WIDTH) * lnx_w + lnx_b
    bonus = jnp.sum(r_h * k_h * r_k, axis=-1, keepdims=True) * v_h
    o = (o + bonus.reshape(B, T, RWKV_WIDTH)) * g
    return o, s_T, p[:, -1:]


def _branch(x, mods, pos, prev, s0, caches, weights, tm, tm_moe):
    (g_mix, g_ffn, g_final, w_r, w_a, rw, w_out, rwt, rb, wup, bup, wdn, bdn) = weights
    sh1, sc1, gt1, sh2, sc2, gt2 = mods
    b, t, d = x.shape
    tab, tabk = _rope_tables(pos)
    p_r, q, k, v, qi, kw = _inproj(x, g_mix, sh1, sc1, w_r, w_a, tab, tabk, tm)
    o_r, wkv, shift = _rwkv7_mix(p_r.reshape(b, t, RWKV_COLS), prev, s0, *rw)
    k4 = k.reshape(b, t, KV_HEADS, HEAD_DIM)
    v4 = v.reshape(b, t, KV_HEADS, HEAD_DIM)
    ki = kw.reshape(b, t, KW_WIDTH)[..., :IDX_DIM]
    kt = 512
    if caches is None:
        k_all, v_all, ki_all, q0, l_valid = k4, v4, ki, 0, t
    else:
        ck, cv, cki = caches
        q0 = ck.shape[1]
        l_valid = q0 + t
        padn = (-l_valid) % kt
        zpad = lambda a: jnp.concatenate([a, jnp.zeros((b, padn) + a.shape[2:], a.dtype)], axis=1)
        k_all = zpad(jnp.concatenate([ck, k4], axis=1))
        v_all = zpad(jnp.concatenate([cv, v4], axis=1))
        ki_all = zpad(jnp.concatenate([cki, ki], axis=1))
    o_a = _dsa(q, qi, kw, k_all.astype(BF16), v_all.astype(BF16), ki_all.astype(BF16),
               qb=min(Q_BLOCK, t), kt=kt, q0=q0, l_valid=l_valid, n_sel=min(TOPK_MAX, l_valid // 4))
    x1, h2 = _outproj(o_r, o_a, x, w_out, gt1, g_ffn, sh2, sc2, tm)
    y_moe = _moe(h2, rwt, rb, wup, bup, wdn, bdn, tm_moe)
    y = _final(x1, y_moe, gt2, g_final, b, t, tm)
    return y, k4, v4, ki, wkv, shift


def kernel(x_prompt, x_sample, c_prompt, c_sample, cache_k, cache_v, cache_kidx, state_wkv, state_shift,
           w_ada, b_ada, g_mix, g_ffn, g_final, w_in, mu_shift, w0, w_lora_w, a0, w_lora_a, w_lora_g,
           k_k, k_a, r_k, lnx_w, lnx_b, w_out, router_w, router_b, w_up, b_up, w_down, b_down):
    B, T, D = x_prompt.shape
    DB, DT, _ = x_sample.shape
    P = cache_k.shape[2]
    l = 0
    assert w_ada.shape[0] == 1

    rows = B + DB
    pad = (-rows) % 8
    c_all = jnp.concatenate([c_prompt, c_sample, jnp.zeros((pad, D), F32)], axis=0)
    m = _ada(c_all, w_ada[l], b_ada[l])
    mods_p = tuple(t.reshape(B, 1, D) for t in jnp.split(m[:B], 6, axis=-1))
    mods_s = tuple(t.reshape(DB, 1, D) for t in jnp.split(m[B:rows], 6, axis=-1))

    w_r = w_in[l][:, :RWKV_COLS].astype(BF16)
    w_att = w_in[l][:, RWKV_COLS:]
    att_pad = KW_WIDTH - IDX_DIM - IDX_HEADS
    w_a = jnp.concatenate([w_att, jnp.zeros((D, att_pad), F32)], axis=1).astype(BF16)
    rw = (mu_shift[l], w0[l], w_lora_w[l], a0[l], w_lora_a[l], w_lora_g[l], k_k[l], k_a[l], r_k[l],
          lnx_w[l], lnx_b[l])
    wup = _deinterleave_cast(w_up[l])
    bup = jnp.concatenate([b_up[l][..., 0::2], b_up[l][..., 1::2]], axis=-1).reshape(N_EXPERTS, 1, 2 * D_FF)
    wdn = w_down[l].astype(BF16)
    bdn = b_down[l].reshape(N_EXPERTS, 1, D)
    weights = (g_mix[l], g_ffn[l], g_final, w_r, w_a, rw, w_out[l].astype(BF16), router_w[l].T,
               router_b[l].reshape(N_EXPERTS, 1), wup, bup, wdn, bdn)

    pos_p = jnp.arange(T, dtype=jnp.int32)
    pos_s = P + jnp.arange(DT, dtype=jnp.int32)
    yp, kp, vp, kip, wkvp, shp = _branch(
        x_prompt, mods_p, pos_p, jnp.zeros((B, 1, RWKV_COLS), F32),
        jnp.zeros((B, RWKV_HEADS, HEAD_DIM, HEAD_DIM), F32), None, weights, 512, 1024)
    ys, ks, vs, kis, wkvs, shs = _branch(
        x_sample, mods_s, pos_s, state_shift[l], state_wkv[l],
        (cache_k[l], cache_v[l], cache_kidx[l]), weights, DT, DB * DT)
    return (yp, ys, kp[None], vp[None], kip[None], wkvp[None], shp[None],
            ks[None], vs[None], kis[None], wkvs[None], shs[None])
```

```python
import functools

import numpy as np
import jax
import jax.numpy as jnp
from jax import lax
from jax.experimental import pallas as pl
from jax.experimental.pallas import tpu as pltpu

F32 = jnp.float32
BF16 = jnp.bfloat16
HIGHEST = lax.Precision.HIGHEST

D_MODEL = 1024
CHUNK = 64
HEAD_DIM = 64
RWKV_HEADS = 8
RWKV_WIDTH = RWKV_HEADS * HEAD_DIM
W_LORA = 64
A_LORA = 64
G_LORA = 128
RWKV_COLS = 3 * RWKV_WIDTH + W_LORA + A_LORA + G_LORA
GN_EPS = 64e-5
ATT_HEADS = 8
ATT_WIDTH = ATT_HEADS * HEAD_DIM
KV_HEADS = 2
KV_WIDTH = KV_HEADS * HEAD_DIM
IDX_HEADS = 4
IDX_DIM = 64
TOPK_MAX = 256
Q_BLOCK = 128
ROPE_THETA = 500000.0
ROT_DIM = HEAD_DIM // 4
N_EXPERTS = 32
TOP_K = 4
D_FF = 1024
SWIGLU_ALPHA = 1.702
SWIGLU_LIMIT = 7.0
NORM_EPS = 1e-5

LANES = 128
MOE_ROWS = 128
VMEM_LIMIT = 56 * 1024 * 1024

QI_WIDTH = IDX_HEADS * IDX_DIM
KW_WIDTH = LANES


def _cparams(sem):
    return pltpu.CompilerParams(dimension_semantics=sem, vmem_limit_bytes=VMEM_LIMIT)


def _ada_kernel(c_ref, w_ref, b_ref, o_ref):
    c = c_ref[...]
    s = c * jax.nn.sigmoid(c)
    o_ref[...] = jnp.dot(s, w_ref[...], precision=HIGHEST, preferred_element_type=F32) + b_ref[...]


def _ada(c, w, b):
    rows, d = c.shape
    n = w.shape[1]
    tn = 1536
    return pl.pallas_call(
        _ada_kernel,
        grid=(n // tn,),
        in_specs=[pl.BlockSpec((rows, d), lambda j: (0, 0)),
                  pl.BlockSpec((d, tn), lambda j: (0, j)),
                  pl.BlockSpec((1, tn), lambda j: (0, j))],
        out_specs=pl.BlockSpec((rows, tn), lambda j: (0, j)),
        out_shape=jax.ShapeDtypeStruct((rows, n), F32),
        compiler_params=_cparams(("arbitrary",)),
        name="ada",
    )(c, w, b.reshape(1, n))


def _rope_slab(y, tab_ref):
    return (y * tab_ref[0] + pltpu.roll(y, LANES - ROT_DIM // 2, 1) * tab_ref[1]
            + pltpu.roll(y, ROT_DIM // 2, 1) * tab_ref[2])


def _norm_mod(x, g, sh, sc):
    var = jnp.mean(x * x, axis=-1, keepdims=True)
    return (x * lax.rsqrt(var + NORM_EPS) * g) * (1.0 + sc) + sh


def _inproj_kernel(x_ref, g_ref, sh_ref, sc_ref, wr_ref, wa_ref, tab_ref, tabk_ref,
                   rw_ref, q_ref, k_ref, v_ref, qi_ref, kw_ref):
    h = _norm_mod(x_ref[...], g_ref[...], sh_ref[0], sc_ref[0]).astype(BF16)
    rw_ref[...] = jnp.dot(h, wr_ref[...], preferred_element_type=F32)
    pa = jnp.dot(h, wa_ref[...], preferred_element_type=F32)
    off = 0
    for ref, width, rot in ((q_ref, ATT_WIDTH, True), (k_ref, KV_WIDTH, True), (v_ref, KV_WIDTH, False),
                            (qi_ref, QI_WIDTH, True)):
        for s in range(width // LANES):
            slab = pa[:, off + s * LANES: off + (s + 1) * LANES]
            ref[:, s * LANES:(s + 1) * LANES] = _rope_slab(slab, tab_ref) if rot else slab
        off += width
    kw_ref[...] = _rope_slab(pa[:, off:off + LANES], tabk_ref)


def _rope_tables(pos):
    half = ROT_DIM // 2
    inv = ROPE_THETA ** (-jnp.arange(0, ROT_DIM, 2, dtype=F32) / ROT_DIM)
    ang = pos.astype(F32)[:, None] * inv[None, :]
    cos, sin = jnp.cos(ang), jnp.sin(ang)
    t = pos.shape[0]
    one = jnp.ones((t, HEAD_DIM - ROT_DIM), F32)
    zero_r = jnp.zeros((t, HEAD_DIM - ROT_DIM), F32)
    zero_h = jnp.zeros((t, half), F32)
    c_head = jnp.concatenate([cos, cos, one], axis=1)
    up_head = jnp.concatenate([-sin, zero_h, zero_r], axis=1)
    dn_head = jnp.concatenate([zero_h, sin, zero_r], axis=1)
    tab = jnp.stack([jnp.tile(c_head, (1, 2)), jnp.tile(up_head, (1, 2)), jnp.tile(dn_head, (1, 2))])
    wscale = jnp.concatenate([jnp.full((t, IDX_HEADS), IDX_HEADS ** -0.5, F32),
                              jnp.ones((t, HEAD_DIM - IDX_HEADS), F32)], axis=1)
    zero64 = jnp.zeros((t, HEAD_DIM), F32)
    tabk = jnp.stack([jnp.concatenate([c_head, wscale], axis=1),
                      jnp.concatenate([up_head, zero64], axis=1),
                      jnp.concatenate([dn_head, zero64], axis=1)])
    return tab, tabk


def _inproj(x, g, sh, sc, w_r, w_a, tab, tabk, tm):
    b, t, d = x.shape
    nt = t // tm
    n = b * t
    x2 = x.reshape(n, d)
    widths = (RWKV_COLS, ATT_WIDTH, KV_WIDTH, KV_WIDTH, QI_WIDTH, KW_WIDTH)
    row = lambda w: pl.BlockSpec((tm, w), lambda i: (i, 0))
    mod = pl.BlockSpec((1, 1, d), lambda i: (i // nt, 0, 0))
    tabspec = pl.BlockSpec((3, tm, LANES), lambda i: (0, i % nt, 0))
    return pl.pallas_call(
        _inproj_kernel,
        grid=(n // tm,),
        in_specs=[row(d), pl.BlockSpec((1, d), lambda i: (0, 0)), mod, mod,
                  pl.BlockSpec(w_r.shape, lambda i: (0, 0)), pl.BlockSpec(w_a.shape, lambda i: (0, 0)),
                  tabspec, tabspec],
        out_specs=[row(w) for w in widths],
        out_shape=[jax.ShapeDtypeStruct((n, w), F32) for w in widths],
        compiler_params=_cparams(("arbitrary",)),
        name="inproj",
    )(x2, g.reshape(1, d), sh, sc, w_r, w_a, tab, tabk)


def _outproj_kernel(or_ref, oa_ref, x_ref, w_ref, gt_ref, g_ref, sh_ref, sc_ref, x1_ref, h2_ref):
    m = jnp.dot(or_ref[...].astype(BF16), w_ref[:RWKV_WIDTH, :], preferred_element_type=F32)
    m = m + jnp.dot(oa_ref[...].astype(BF16), w_ref[RWKV_WIDTH:, :], preferred_element_type=F32)
    x1 = x_ref[...] + gt_ref[0] * m
    x1_ref[...] = x1
    h2_ref[...] = _norm_mod(x1, g_ref[...], sh_ref[0], sc_ref[0])


def _outproj(o_r, o_a, x, w_out, gt, g, sh, sc, tm):
    b, t, d = x.shape
    nt = t // tm
    n = b * t
    row = lambda w: pl.BlockSpec((tm, w), lambda i: (i, 0))
    mod = pl.BlockSpec((1, 1, d), lambda i: (i // nt, 0, 0))
    return pl.pallas_call(
        _outproj_kernel,
        grid=(n // tm,),
        in_specs=[row(RWKV_WIDTH), row(ATT_WIDTH), row(d), pl.BlockSpec(w_out.shape, lambda i: (0, 0)),
                  mod, pl.BlockSpec((1, d), lambda i: (0, 0)), mod, mod],
        out_specs=[row(d), row(d)],
        out_shape=[jax.ShapeDtypeStruct((n, d), F32)] * 2,
        compiler_params=_cparams(("arbitrary",)),
        name="outproj",
    )(o_r.reshape(n, RWKV_WIDTH), o_a.reshape(n, ATT_WIDTH), x.reshape(n, d), w_out, gt, g.reshape(1, d), sh, sc)


def _moe_kernel(h_ref, rwt_ref, rb_ref, tri_ref, wup_ref, bup_ref, wdn_ref, bdn_ref, y_ref,
                hb_s, rank_s, gate_s):
    e = pl.program_id(1)
    tm = h_ref.shape[0]

    @pl.when(e == 0)
    def _route():
        h = h_ref[...]
        hb_s[...] = h.astype(BF16)
        logits = lax.dot_general(rwt_ref[...], h, (((1,), (1,)), ((), ())), precision=HIGHEST,
                                 preferred_element_type=F32) + rb_ref[...]
        eidx = lax.broadcasted_iota(jnp.int32, logits.shape, 0)
        work = logits
        top = None
        for _ in range(TOP_K):
            m = jnp.max(work, axis=0, keepdims=True)
            if top is None:
                top = m
            first = jnp.min(jnp.where(work == m, eidx, N_EXPERTS), axis=0, keepdims=True)
            work = jnp.where(eidx == first, -jnp.inf, work)
        sel = work != logits
        ex = jnp.where(sel, jnp.exp(logits - top), 0.0)
        gate_s[...] = ex / jnp.sum(ex, axis=0, keepdims=True)
        before = jnp.dot(jnp.where(sel, 1.0, 0.0).astype(BF16), tri_ref[...], preferred_element_type=F32)
        rank_s[...] = jnp.where(sel, before, -1.0)
        y_ref[...] = jnp.zeros(y_ref.shape, F32)

    r_row = rank_s[pl.ds(e, 1), :]
    g_row = gate_s[pl.ds(e, 1), :]
    count = jnp.sum(jnp.where(r_row >= 0.0, 1, 0).astype(jnp.int32))
    n_blocks = (count + MOE_ROWS - 1) // MOE_ROWS

    def block(j, carry):
        rows = (lax.broadcasted_iota(jnp.int32, (MOE_ROWS, tm), 0) + j * MOE_ROWS).astype(F32)
        hit = r_row == rows
        p = jnp.where(hit, 1.0, 0.0).astype(BF16)
        xe = jnp.dot(p, hb_s[...], preferred_element_type=F32).astype(BF16)
        u = jnp.dot(xe, wup_ref[0], preferred_element_type=F32) + bup_ref[0]
        glu = jnp.minimum(u[:, :D_FF], SWIGLU_LIMIT)
        lin = jnp.clip(u[:, D_FF:], -SWIGLU_LIMIT, SWIGLU_LIMIT)
        act = glu * jax.nn.sigmoid(SWIGLU_ALPHA * glu) * (lin + 1.0)
        yb = jnp.dot(act.astype(BF16), wdn_ref[0], preferred_element_type=F32) + bdn_ref[0]
        g_rows = jnp.sum(jnp.where(hit, g_row, 0.0), axis=1, keepdims=True)
        ys = (yb * g_rows).astype(BF16)
        y_ref[...] += lax.dot_general(p, ys, (((0,), (0,)), ((), ())), preferred_element_type=F32)
        return carry

    lax.fori_loop(0, n_blocks, block, 0)


def _moe(h, rwt, rb, wup, bup, wdn, bdn, tm):
    n, d = h.shape
    tri = (lax.broadcasted_iota(jnp.int32, (tm, tm), 0) < lax.broadcasted_iota(jnp.int32, (tm, tm), 1)).astype(BF16)
    return pl.pallas_call(
        _moe_kernel,
        grid=(n // tm, N_EXPERTS),
        in_specs=[pl.BlockSpec((tm, d), lambda i, e: (i, 0)),
                  pl.BlockSpec((N_EXPERTS, d), lambda i, e: (0, 0)),
                  pl.BlockSpec((N_EXPERTS, 1), lambda i, e: (0, 0)),
                  pl.BlockSpec((tm, tm), lambda i, e: (0, 0)),
                  pl.BlockSpec((1, d, 2 * D_FF), lambda i, e: (e, 0, 0)),
                  pl.BlockSpec((1, 1, 2 * D_FF), lambda i, e: (e, 0, 0)),
                  pl.BlockSpec((1, D_FF, d), lambda i, e: (e, 0, 0)),
                  pl.BlockSpec((1, 1, d), lambda i, e: (e, 0, 0))],
        out_specs=pl.BlockSpec((tm, d), lambda i, e: (i, 0)),
        out_shape=jax.ShapeDtypeStruct((n, d), F32),
        scratch_shapes=[pltpu.VMEM((tm, d), BF16), pltpu.VMEM((N_EXPERTS, tm), F32),
                        pltpu.VMEM((N_EXPERTS, tm), F32)],
        compiler_params=_cparams(("arbitrary", "arbitrary")),
        name="moe",
    )(h, rwt, rb, tri, wup, bup, wdn, bdn)


def _deinterleave_kernel(w_ref, perm_ref, o_ref):
    o_ref[0] = jnp.dot(w_ref[0].astype(BF16), perm_ref[...], preferred_element_type=F32).astype(BF16)


def _deinterleave_cast(w):
    e, d, n = w.shape
    tk = 512
    src = lax.broadcasted_iota(jnp.int32, (n, n), 0)
    dst = lax.broadcasted_iota(jnp.int32, (n, n), 1)
    perm = (src == jnp.where(dst < n // 2, 2 * dst, 2 * (dst - n // 2) + 1)).astype(BF16)
    return pl.pallas_call(
        _deinterleave_kernel,
        grid=(e, d // tk),
        in_specs=[pl.BlockSpec((1, tk, n), lambda i, j: (i, j, 0)), pl.BlockSpec((n, n), lambda i, j: (0, 0))],
        out_specs=pl.BlockSpec((1, tk, n), lambda i, j: (i, j, 0)),
        out_shape=jax.ShapeDtypeStruct((e, d, n), BF16),
        compiler_params=_cparams(("arbitrary", "arbitrary")),
        name="deinterleave",
    )(w, perm)


def _final_kernel(x1_ref, y_ref, gt_ref, g_ref, o_ref):
    x = x1_ref[...] + gt_ref[0] * y_ref[...]
    var = jnp.mean(x * x, axis=-1, keepdims=True)
    o_ref[...] = x * lax.rsqrt(var + NORM_EPS) * g_ref[...]


def _final(x1, y, gt, g, b, t, tm):
    n, d = x1.shape
    nt = t // tm
    row = pl.BlockSpec((tm, d), lambda i: (i, 0))
    return pl.pallas_call(
        _final_kernel,
        grid=(n // tm,),
        in_specs=[row, row, pl.BlockSpec((1, 1, d), lambda i: (i // nt, 0, 0)),
                  pl.BlockSpec((1, d), lambda i: (0, 0))],
        out_specs=row,
        out_shape=jax.ShapeDtypeStruct((n, d), F32),
        compiler_params=_cparams(("arbitrary",)),
        name="final",
    )(x1, y, gt, g.reshape(1, d)).reshape(b, t, d)


MASKED = -1e30
INT_MIN = -2 ** 31


def _dsa_kernel(q_ref, qi_ref, kw_ref, k_ref, vx_ref, ki_ref, tri_ref, o_ref, key_s, bias_s, wib_s,
                *, qb, kt, q0, l_valid, n_sel):
    i = pl.program_id(1)
    ns = kt // LANES
    row = lax.broadcasted_iota(jnp.int32, (qb, 1), 0)
    qpos = q0 + i * qb + row
    lim = jnp.minimum((qpos // CHUNK + 1) * CHUNK, l_valid)
    last_lim = jnp.minimum(((q0 + i * qb + qb - 1) // CHUNK + 1) * CHUNK, l_valid)
    n_kt = (last_lim + kt - 1) // kt
    lane = lax.broadcasted_iota(jnp.int32, (1, LANES), 1)

    for h in range(IDX_HEADS):
        w = kw_ref[:, IDX_DIM + h:IDX_DIM + h + 1] * (IDX_DIM ** -0.5)
        wib_s[h] = jnp.broadcast_to(w, (qb, LANES))
    qi = qi_ref[...].astype(BF16)

    def score_tile(t, c):
        ks = pl.multiple_of(t * kt, kt)
        kit = ki_ref[0, pl.ds(ks, kt), :]
        lg = [lax.dot_general(qi[:, h * IDX_DIM:(h + 1) * IDX_DIM], kit, (((1,), (1,)), ((), ())),
                              preferred_element_type=F32) for h in range(IDX_HEADS)]
        for s in range(ns):
            sc = jnp.zeros((qb, LANES), F32)
            for h in range(IDX_HEADS):
                sc = sc + wib_s[h] * jnp.maximum(lg[h][:, s * LANES:(s + 1) * LANES], 0.0)
            kpos = ks + s * LANES + lane
            sc = jnp.where(kpos < lim, sc + 0.0, -jnp.inf)
            bits = pltpu.bitcast(sc, jnp.int32)
            key_s[t, :, s * LANES:(s + 1) * LANES] = bits ^ ((bits >> 31) & 0x7FFFFFFF)
        return c

    lax.fori_loop(0, n_kt, score_tile, 0)

    def count(pred_fn):
        def tile(t, acc):
            for s in range(ns):
                acc = acc + jnp.where(pred_fn(key_s[t, :, s * LANES:(s + 1) * LANES]), 1.0, 0.0)
            return acc
        acc = lax.fori_loop(0, n_kt, tile, jnp.zeros((qb, LANES), F32))
        return jnp.sum(acc, axis=1, keepdims=True)

    def bit_step(b, lo):
        cand = lo + jnp.left_shift(jnp.int32(1), 31 - b)
        candb = jnp.broadcast_to(cand, (qb, LANES))
        return jnp.where(count(lambda k: k >= candb) >= n_sel, cand, lo)

    thr = lax.fori_loop(0, 32, bit_step, jnp.full((qb, 1), INT_MIN, jnp.int32))
    thrb = jnp.broadcast_to(thr, (qb, LANES))
    need = jnp.broadcast_to(n_sel - count(lambda k: k > thrb), (qb, LANES))

    def sel_tile(t, off):
        for s in range(ns):
            key = key_s[t, :, s * LANES:(s + 1) * LANES]
            eq = key == thrb
            pre = jnp.dot(jnp.where(eq, 1.0, 0.0).astype(BF16), tri_ref[...], preferred_element_type=F32)
            kpos = t * kt + s * LANES + lane
            keep = jnp.logical_or(key > thrb, jnp.logical_and(eq, pre[:, :LANES] + off < need))
            keep = jnp.logical_and(keep, kpos < lim)
            bias_s[t, :, s * LANES:(s + 1) * LANES] = jnp.where(keep, 0.0, MASKED)
            off = off + pre[:, LANES:]
        return off

    lax.fori_loop(0, n_kt, sel_tile, jnp.zeros((qb, LANES), F32))

    rep = ATT_HEADS // KV_HEADS
    for g in range(KV_HEADS):
        qg = jnp.concatenate([q_ref[:, (g * rep + r) * HEAD_DIM:(g * rep + r + 1) * HEAD_DIM] for r in range(rep)],
                             axis=0)
        qg = (qg * (HEAD_DIM ** -0.5)).astype(BF16)

        def att_tile(t, carry):
            m, acc = carry
            ks = pl.multiple_of(t * kt, kt)
            s = lax.dot_general(qg, k_ref[0, g, pl.ds(ks, kt), :], (((1,), (1,)), ((), ())),
                                preferred_element_type=F32)
            s = (s.reshape(rep, qb, kt) + bias_s[t][None]).reshape(rep * qb, kt)
            m_new = jnp.maximum(m, jnp.max(s, axis=1, keepdims=True))
            p = jnp.exp(s - m_new)
            pv = jnp.dot(p.astype(BF16), vx_ref[0, pl.ds(ks, kt), g * LANES:(g + 1) * LANES],
                         preferred_element_type=F32)
            return m_new, acc * jnp.exp(m - m_new) + pv

        m0 = jnp.full((rep * qb, 1), MASKED, F32)
        _, acc = lax.fori_loop(0, n_kt, att_tile, (m0, jnp.zeros((rep * qb, LANES), F32)))
        out = acc * pltpu.roll(1.0 / acc, HEAD_DIM, 1)
        for r in range(rep):
            h = g * rep + r
            o_ref[:, h * HEAD_DIM:(h + 1) * HEAD_DIM] = out[r * qb:(r + 1) * qb, :HEAD_DIM]


def _dsa(q, qi, kw, k_bf, v_bf, ki_bf, *, qb, kt, q0, l_valid, n_sel):
    b, lp = ki_bf.shape[:2]
    nq = q.shape[0] // (b * qb)
    k_g = jnp.moveaxis(k_bf, 2, 1)
    ones = jnp.ones((b, lp, KV_HEADS, HEAD_DIM), BF16)
    vx = jnp.concatenate([v_bf, ones], axis=-1).reshape(b, lp, KV_HEADS * LANES)
    tr = lax.broadcasted_iota(jnp.int32, (LANES, 2 * LANES), 0)
    tc = lax.broadcasted_iota(jnp.int32, (LANES, 2 * LANES), 1)
    tri = jnp.logical_or(tr < tc, tc >= LANES).astype(BF16)
    row = lambda w: pl.BlockSpec((qb, w), lambda bi, i: (bi * nq + i, 0))
    kern = functools.partial(_dsa_kernel, qb=qb, kt=kt, q0=q0, l_valid=l_valid, n_sel=n_sel)
    return pl.pallas_call(
        kern,
        grid=(b, nq),
        in_specs=[row(ATT_WIDTH), row(QI_WIDTH), row(KW_WIDTH),
                  pl.BlockSpec((1, KV_HEADS, lp, HEAD_DIM), lambda bi, i: (bi, 0, 0, 0)),
                  pl.BlockSpec((1, lp, KV_HEADS * LANES), lambda bi, i: (bi, 0, 0)),
                  pl.BlockSpec((1, lp, IDX_DIM), lambda bi, i: (bi, 0, 0)),
                  pl.BlockSpec((LANES, 2 * LANES), lambda bi, i: (0, 0))],
        out_specs=row(ATT_WIDTH),
        out_shape=jax.ShapeDtypeStruct((q.shape[0], ATT_WIDTH), F32),
        scratch_shapes=[pltpu.VMEM((lp // kt, qb, kt), jnp.int32), pltpu.VMEM((lp // kt, qb, kt), F32),
                        pltpu.VMEM((IDX_HEADS, qb, LANES), F32)],
        compiler_params=_cparams(("arbitrary", "arbitrary")),
        name="dsa",
    )(q, qi, kw, k_g, vx, ki_bf, tri)


def _split_bf16(x):
    hi = x.astype(BF16)
    return hi, (x - hi.astype(F32)).astype(BF16)


def _dot3(a, b, dims=(((1,), (0,)), ((), ()))):
    ah, al = _split_bf16(a)
    bh, bl = _split_bf16(b)
    d = functools.partial(lax.dot_general, dimension_numbers=dims, preferred_element_type=F32)
    return d(ah, bh) + (d(ah, bl) + d(al, bh))


def _dot2(a, b_exact):
    ah, al = _split_bf16(a)
    return jnp.dot(ah, b_exact, preferred_element_type=F32) + jnp.dot(al, b_exact, preferred_element_type=F32)


_NT = (((1,), (1,)), ((), ()))
_TN = (((0,), (0,)), ((), ()))


def _rwkv_kernel(p_ref, prev_ref, s0_ref, mu_ref, w0_ref, w2_ref, a0_ref, a2_ref, g2_ref, kk_ref, ka_ref,
                 rk_ref, lnw_ref, lnb_ref, hsum_ref, tri_ref, o_ref, st_ref, s_s, prev_s, o_s, *, c):
    j = pl.program_id(1)

    @pl.when(j == 0)
    def _init():
        s_s[...] = s0_ref[0]
        prev_s[...] = prev_ref[0]

    p = p_ref[...]
    row = lax.broadcasted_iota(jnp.int32, (c, 1), 0)
    p_prev = jnp.where(row == 0, prev_s[...], pltpu.roll(p, 1, 0))
    prev_s[...] = p[c - 1:c, :]
    xs = p + (p_prev - p) * mu_ref[...]
    o1, o2, o3 = RWKV_WIDTH, 2 * RWKV_WIDTH, 3 * RWKV_WIDTH
    r, k, v = xs[:, :o1], xs[:, o1:o2], xs[:, o2:o3]
    dw = xs[:, o3:o3 + W_LORA]
    da = xs[:, o3 + W_LORA:o3 + W_LORA + A_LORA]
    dg = xs[:, o3 + W_LORA + A_LORA:]
    hsum = hsum_ref[...]
    w_log = -jax.nn.softplus(-(w0_ref[...] + _dot3(jnp.tanh(dw), w2_ref[...]))) - 0.5
    lw = -jnp.exp(w_log)
    a = jax.nn.sigmoid(a0_ref[...] + _dot3(da, a2_ref[...]))
    g = _dot3(jax.nn.sigmoid(dg), g2_ref[...])
    kk = k * kk_ref[...]
    kk = kk / jnp.maximum(jnp.sqrt(_dot2(kk * kk, hsum)), 1e-12)
    km = k * (1.0 + (a - 1.0) * ka_ref[...])
    bm = kk * a
    lw_hi, lw_lo = _split_bf16(lw)
    tri = tri_ref[...]
    cum = (jnp.dot(tri, lw_hi, preferred_element_type=F32)
           + jnp.dot(tri, lw_lo, preferred_element_type=F32))
    tot = cum[c - 1:c, :]
    e_in = jnp.exp(cum)
    e_out = jnp.exp(-cum)
    e_end = jnp.exp(tot - cum)
    a_t = -kk * jnp.exp(cum - lw)
    r_t = r * e_in
    b_t = bm * e_out
    k_t = km * e_out
    b_h = bm * e_end
    k_h = km * e_end
    gam = jnp.exp(tot)

    ri = lax.broadcasted_iota(jnp.int32, (c, c), 0)
    ci = lax.broadcasted_iota(jnp.int32, (c, c), 1)
    strict = ri > ci
    incl = ri >= ci
    eye = jnp.where(ri == ci, 1.0, 0.0)
    n_double = int(np.log2(c)) - 1
    for h in range(RWKV_HEADS):
        sl = slice(h * HEAD_DIM, (h + 1) * HEAD_DIM)
        ar = jnp.concatenate([a_t[:, sl], r_t[:, sl]], axis=0)
        bk = jnp.concatenate([b_t[:, sl], k_t[:, sl]], axis=0)
        m = _dot3(ar, bk, _NT)
        l_ab = jnp.where(strict, m[:c, :c], 0.0)
        l_ak = jnp.where(strict, m[:c, c:], 0.0)
        m_rb = jnp.where(incl, m[c:, :c], 0.0)
        m_rk = jnp.where(incl, m[c:, c:], 0.0)
        x = l_ab
        tinv = eye + l_ab
        for _ in range(n_double):
            x = _dot3(x, x)
            tinv = tinv + _dot3(tinv, x)
        vh = v[:, sl]
        s0 = s_s[h]
        rhs = _dot3(a_t[:, sl], s0, _NT) + _dot3(l_ak, vh)
        u = _dot3(tinv, rhs)
        o_s[:, sl] = _dot3(r_t[:, sl], s0, _NT) + _dot3(m_rb, u) + _dot3(m_rk, vh)
        s_s[h] = s0 * gam[:, sl] + _dot3(u, b_h[:, sl], _TN) + _dot3(vh, k_h[:, sl], _TN)

    o = o_s[...]
    inv_d = 1.0 / HEAD_DIM
    mean = _dot2(o, hsum) * inv_d
    cen = o - mean
    var = _dot2(cen * cen, hsum) * inv_d
    on = cen * lax.rsqrt(var + GN_EPS) * lnw_ref[...] + lnb_ref[...]
    bonus = _dot2(r * km * rk_ref[...], hsum) * v
    o_ref[...] = (on + bonus) * g

    @pl.when(j == pl.num_programs(1) - 1)
    def _fin():
        st_ref[0] = s_s[...]


def _rwkv(p, prev, s0, rw, c):
    mu, w0, w2, a0, a2, g2, k_k, k_a, r_k, lnx_w, lnx_b = rw
    b = prev.shape[0]
    nc = p.shape[0] // (b * c)
    hid = lax.broadcasted_iota(jnp.int32, (RWKV_WIDTH, RWKV_WIDTH), 0) // HEAD_DIM
    hsum = (hid == hid.T).astype(BF16)
    tri = (lax.broadcasted_iota(jnp.int32, (c, c), 0) >= lax.broadcasted_iota(jnp.int32, (c, c), 1)).astype(BF16)
    vec = lambda a: a.reshape(1, -1)
    full = lambda a: pl.BlockSpec(a.shape, lambda bi, j: (0,) * a.ndim)
    small = [vec(mu), vec(w0), w2, vec(a0), a2, g2, vec(k_k), vec(k_a), vec(r_k), vec(lnx_w), vec(lnx_b), hsum, tri]
    return pl.pallas_call(
        functools.partial(_rwkv_kernel, c=c),
        grid=(b, nc),
        in_specs=[pl.BlockSpec((c, RWKV_COLS), lambda bi, j: (bi * nc + j, 0)),
                  pl.BlockSpec((1, 1, RWKV_COLS), lambda bi, j: (bi, 0, 0)),
                  pl.BlockSpec((1, RWKV_HEADS, HEAD_DIM, HEAD_DIM), lambda bi, j: (bi, 0, 0, 0))]
                 + [full(a) for a in small],
        out_specs=[pl.BlockSpec((c, RWKV_WIDTH), lambda bi, j: (bi * nc + j, 0)),
                   pl.BlockSpec((1, RWKV_HEADS, HEAD_DIM, HEAD_DIM), lambda bi, j: (bi, 0, 0, 0))],
        out_shape=[jax.ShapeDtypeStruct((p.shape[0], RWKV_WIDTH), F32),
                   jax.ShapeDtypeStruct((b, RWKV_HEADS, HEAD_DIM, HEAD_DIM), F32)],
        scratch_shapes=[pltpu.VMEM((RWKV_HEADS, HEAD_DIM, HEAD_DIM), F32), pltpu.VMEM((1, RWKV_COLS), F32),
                        pltpu.VMEM((c, RWKV_WIDTH), F32)],
        compiler_params=_cparams(("arbitrary", "arbitrary")),
        name="rwkv",
    )(p, prev, s0, *small)


def _branch(x, mods, pos, prev, s0, caches, weights, tm, tm_moe):
    (g_mix, g_ffn, g_final, w_r, w_a, rw, w_out, rwt, rb, wup, bup, wdn, bdn) = weights
    sh1, sc1, gt1, sh2, sc2, gt2 = mods
    b, t, d = x.shape
    tab, tabk = _rope_tables(pos)
    p_r, q, k, v, qi, kw = _inproj(x, g_mix, sh1, sc1, w_r, w_a, tab, tabk, tm)
    o_r, wkv = _rwkv(p_r, prev, s0, rw, min(CHUNK, t))
    shift = p_r.reshape(b, t, RWKV_COLS)[:, -1:]
    k4 = k.reshape(b, t, KV_HEADS, HEAD_DIM)
    v4 = v.reshape(b, t, KV_HEADS, HEAD_DIM)
    ki = kw.reshape(b, t, KW_WIDTH)[..., :IDX_DIM]
    kt = 512
    if caches is None:
        k_all, v_all, ki_all, q0, l_valid = k4, v4, ki, 0, t
    else:
        ck, cv, cki = caches
        q0 = ck.shape[1]
        l_valid = q0 + t
        padn = (-l_valid) % kt
        zpad = lambda a: jnp.concatenate([a, jnp.zeros((b, padn) + a.shape[2:], a.dtype)], axis=1)
        k_all = zpad(jnp.concatenate([ck, k4], axis=1))
        v_all = zpad(jnp.concatenate([cv, v4], axis=1))
        ki_all = zpad(jnp.concatenate([cki, ki], axis=1))
    o_a = _dsa(q, qi, kw, k_all.astype(BF16), v_all.astype(BF16), ki_all.astype(BF16),
               qb=min(Q_BLOCK, t), kt=kt, q0=q0, l_valid=l_valid, n_sel=min(TOPK_MAX, l_valid // 4))
    x1, h2 = _outproj(o_r, o_a, x, w_out, gt1, g_ffn, sh2, sc2, tm)
    y_moe = _moe(h2, rwt, rb, wup, bup, wdn, bdn, tm_moe)
    y = _final(x1, y_moe, gt2, g_final, b, t, tm)
    return y, k4, v4, ki, wkv, shift


def kernel(x_prompt, x_sample, c_prompt, c_sample, cache_k, cache_v, cache_kidx, state_wkv, state_shift,
           w_ada, b_ada, g_mix, g_ffn, g_final, w_in, mu_shift, w0, w_lora_w, a0, w_lora_a, w_lora_g,
           k_k, k_a, r_k, lnx_w, lnx_b, w_out, router_w, router_b, w_up, b_up, w_down, b_down):
    B, T, D = x_prompt.shape
    DB, DT, _ = x_sample.shape
    P = cache_k.shape[2]
    l = 0
    assert w_ada.shape[0] == 1

    rows = B + DB
    pad = (-rows) % 8
    c_all = jnp.concatenate([c_prompt, c_sample, jnp.zeros((pad, D), F32)], axis=0)
    m = _ada(c_all, w_ada[l], b_ada[l])
    mods_p = tuple(t.reshape(B, 1, D) for t in jnp.split(m[:B], 6, axis=-1))
    mods_s = tuple(t.reshape(DB, 1, D) for t in jnp.split(m[B:rows], 6, axis=-1))

    w_r = w_in[l][:, :RWKV_COLS].astype(BF16)
    w_att = w_in[l][:, RWKV_COLS:]
    att_pad = KW_WIDTH - IDX_DIM - IDX_HEADS
    w_a = jnp.concatenate([w_att, jnp.zeros((D, att_pad), F32)], axis=1).astype(BF16)
    rw = (mu_shift[l], w0[l], w_lora_w[l], a0[l], w_lora_a[l], w_lora_g[l], k_k[l], k_a[l], r_k[l],
          lnx_w[l], lnx_b[l])
    wup = _deinterleave_cast(w_up[l])
    bup = jnp.concatenate([b_up[l][..., 0::2], b_up[l][..., 1::2]], axis=-1).reshape(N_EXPERTS, 1, 2 * D_FF)
    wdn = w_down[l].astype(BF16)
    bdn = b_down[l].reshape(N_EXPERTS, 1, D)
    weights = (g_mix[l], g_ffn[l], g_final, w_r, w_a, rw, w_out[l].astype(BF16), router_w[l].T,
               router_b[l].reshape(N_EXPERTS, 1), wup, bup, wdn, bdn)

    pos_p = jnp.arange(T, dtype=jnp.int32)
    pos_s = P + jnp.arange(DT, dtype=jnp.int32)
    yp, kp, vp, kip, wkvp, shp = _branch(
        x_prompt, mods_p, pos_p, jnp.zeros((B, 1, RWKV_COLS), F32),
        jnp.zeros((B, RWKV_HEADS, HEAD_DIM, HEAD_DIM), F32), None, weights, 512, 1024)
    ys, ks, vs, kis, wkvs, shs = _branch(
        x_sample, mods_s, pos_s, state_shift[l], state_wkv[l],
        (cache_k[l], cache_v[l], cache_kidx[l]), weights, DT, DB * DT)
    return (yp, ys, kp[None], vp[None], kip[None], wkvp[None], shp[None],
            ks[None], vs[None], kis[None], wkvs[None], shs[None])
```

```python
import functools

import numpy as np
import jax
import jax.numpy as jnp
from jax import lax
from jax.experimental import pallas as pl
from jax.experimental.pallas import tpu as pltpu

F32 = jnp.float32
BF16 = jnp.bfloat16
HIGHEST = lax.Precision.HIGHEST

D_MODEL = 1024
CHUNK = 64
HEAD_DIM = 64
RWKV_HEADS = 8
RWKV_WIDTH = RWKV_HEADS * HEAD_DIM
W_LORA = 64
A_LORA = 64
G_LORA = 128
RWKV_COLS = 3 * RWKV_WIDTH + W_LORA + A_LORA + G_LORA
GN_EPS = 64e-5
ATT_HEADS = 8
ATT_WIDTH = ATT_HEADS * HEAD_DIM
KV_HEADS = 2
KV_WIDTH = KV_HEADS * HEAD_DIM
IDX_HEADS = 4
IDX_DIM = 64
TOPK_MAX = 256
Q_BLOCK = 128
ROPE_THETA = 500000.0
ROT_DIM = HEAD_DIM // 4
N_EXPERTS = 32
TOP_K = 4
D_FF = 1024
SWIGLU_ALPHA = 1.702
SWIGLU_LIMIT = 7.0
NORM_EPS = 1e-5

LANES = 128
MOE_ROWS = 128
VMEM_LIMIT = 56 * 1024 * 1024

QI_WIDTH = IDX_HEADS * IDX_DIM
KW_WIDTH = LANES


def _cparams(sem):
    return pltpu.CompilerParams(dimension_semantics=sem, vmem_limit_bytes=VMEM_LIMIT)


def _ada_kernel(c_ref, w_ref, b_ref, o_ref):
    c = c_ref[...]
    s = c * jax.nn.sigmoid(c)
    o_ref[...] = jnp.dot(s, w_ref[...], precision=HIGHEST, preferred_element_type=F32) + b_ref[...]


def _ada(c, w, b):
    rows, d = c.shape
    n = w.shape[1]
    tn = 1536
    return pl.pallas_call(
        _ada_kernel,
        grid=(n // tn,),
        in_specs=[pl.BlockSpec((rows, d), lambda j: (0, 0)),
                  pl.BlockSpec((d, tn), lambda j: (0, j)),
                  pl.BlockSpec((1, tn), lambda j: (0, j))],
        out_specs=pl.BlockSpec((rows, tn), lambda j: (0, j)),
        out_shape=jax.ShapeDtypeStruct((rows, n), F32),
        compiler_params=_cparams(("arbitrary",)),
        name="ada",
    )(c, w, b.reshape(1, n))


def _rope_slab(y, tab_ref):
    return (y * tab_ref[0] + pltpu.roll(y, LANES - ROT_DIM // 2, 1) * tab_ref[1]
            + pltpu.roll(y, ROT_DIM // 2, 1) * tab_ref[2])


def _norm_mod(x, g, sh, sc):
    var = jnp.mean(x * x, axis=-1, keepdims=True)
    return (x * lax.rsqrt(var + NORM_EPS) * g) * (1.0 + sc) + sh


def _inproj_kernel(x_ref, g_ref, sh_ref, sc_ref, wr_ref, wa_ref, tab_ref, tabk_ref,
                   rw_ref, q_ref, k_ref, v_ref, qi_ref, kw_ref):
    h = _norm_mod(x_ref[...], g_ref[...], sh_ref[0], sc_ref[0]).astype(BF16)
    rw_ref[...] = jnp.dot(h, wr_ref[...], preferred_element_type=F32)
    pa = jnp.dot(h, wa_ref[...], preferred_element_type=F32)
    off = 0
    for ref, width, rot in ((q_ref, ATT_WIDTH, True), (k_ref, KV_WIDTH, True), (v_ref, KV_WIDTH, False),
                            (qi_ref, QI_WIDTH, True)):
        for s in range(width // LANES):
            slab = pa[:, off + s * LANES: off + (s + 1) * LANES]
            ref[:, s * LANES:(s + 1) * LANES] = _rope_slab(slab, tab_ref) if rot else slab
        off += width
    kw_ref[...] = _rope_slab(pa[:, off:off + LANES], tabk_ref)


def _rope_tables(pos):
    half = ROT_DIM // 2
    inv = ROPE_THETA ** (-jnp.arange(0, ROT_DIM, 2, dtype=F32) / ROT_DIM)
    ang = pos.astype(F32)[:, None] * inv[None, :]
    cos, sin = jnp.cos(ang), jnp.sin(ang)
    t = pos.shape[0]
    one = jnp.ones((t, HEAD_DIM - ROT_DIM), F32)
    zero_r = jnp.zeros((t, HEAD_DIM - ROT_DIM), F32)
    zero_h = jnp.zeros((t, half), F32)
    c_head = jnp.concatenate([cos, cos, one], axis=1)
    up_head = jnp.concatenate([-sin, zero_h, zero_r], axis=1)
    dn_head = jnp.concatenate([zero_h, sin, zero_r], axis=1)
    tab = jnp.stack([jnp.tile(c_head, (1, 2)), jnp.tile(up_head, (1, 2)), jnp.tile(dn_head, (1, 2))])
    wscale = jnp.concatenate([jnp.full((t, IDX_HEADS), IDX_HEADS ** -0.5, F32),
                              jnp.ones((t, HEAD_DIM - IDX_HEADS), F32)], axis=1)
    zero64 = jnp.zeros((t, HEAD_DIM), F32)
    tabk = jnp.stack([jnp.concatenate([c_head, wscale], axis=1),
                      jnp.concatenate([up_head, zero64], axis=1),
                      jnp.concatenate([dn_head, zero64], axis=1)])
    return tab, tabk


def _inproj(x, g, sh, sc, w_r, w_a, tab, tabk, tm):
    b, t, d = x.shape
    nt = t // tm
    n = b * t
    x2 = x.reshape(n, d)
    widths = (RWKV_COLS, ATT_WIDTH, KV_WIDTH, KV_WIDTH, QI_WIDTH, KW_WIDTH)
    row = lambda w: pl.BlockSpec((tm, w), lambda i: (i, 0))
    mod = pl.BlockSpec((1, 1, d), lambda i: (i // nt, 0, 0))
    tabspec = pl.BlockSpec((3, tm, LANES), lambda i: (0, i % nt, 0))
    return pl.pallas_call(
        _inproj_kernel,
        grid=(n // tm,),
        in_specs=[row(d), pl.BlockSpec((1, d), lambda i: (0, 0)), mod, mod,
                  pl.BlockSpec(w_r.shape, lambda i: (0, 0)), pl.BlockSpec(w_a.shape, lambda i: (0, 0)),
                  tabspec, tabspec],
        out_specs=[row(w) for w in widths],
        out_shape=[jax.ShapeDtypeStruct((n, w), F32) for w in widths],
        compiler_params=_cparams(("arbitrary",)),
        name="inproj",
    )(x2, g.reshape(1, d), sh, sc, w_r, w_a, tab, tabk)


def _outproj_kernel(or_ref, oa_ref, x_ref, w_ref, gt_ref, g_ref, sh_ref, sc_ref, x1_ref, h2_ref):
    m = jnp.dot(or_ref[...].astype(BF16), w_ref[:RWKV_WIDTH, :], preferred_element_type=F32)
    m = m + jnp.dot(oa_ref[...].astype(BF16), w_ref[RWKV_WIDTH:, :], preferred_element_type=F32)
    x1 = x_ref[...] + gt_ref[0] * m
    x1_ref[...] = x1
    h2_ref[...] = _norm_mod(x1, g_ref[...], sh_ref[0], sc_ref[0])


def _outproj(o_r, o_a, x, w_out, gt, g, sh, sc, tm):
    b, t, d = x.shape
    nt = t // tm
    n = b * t
    row = lambda w: pl.BlockSpec((tm, w), lambda i: (i, 0))
    mod = pl.BlockSpec((1, 1, d), lambda i: (i // nt, 0, 0))
    return pl.pallas_call(
        _outproj_kernel,
        grid=(n // tm,),
        in_specs=[row(RWKV_WIDTH), row(ATT_WIDTH), row(d), pl.BlockSpec(w_out.shape, lambda i: (0, 0)),
                  mod, pl.BlockSpec((1, d), lambda i: (0, 0)), mod, mod],
        out_specs=[row(d), row(d)],
        out_shape=[jax.ShapeDtypeStruct((n, d), F32)] * 2,
        compiler_params=_cparams(("arbitrary",)),
        name="outproj",
    )(o_r.reshape(n, RWKV_WIDTH), o_a.reshape(n, ATT_WIDTH), x.reshape(n, d), w_out, gt, g.reshape(1, d), sh, sc)


def _moe_kernel(h_ref, rwt_ref, rb_ref, tri_ref, wup_ref, bup_ref, wdn_ref, bdn_ref, y_ref,
                hb_s, rank_s, gate_s):
    e = pl.program_id(1)
    tm = h_ref.shape[0]

    @pl.when(e == 0)
    def _route():
        h = h_ref[...]
        hb_s[...] = h.astype(BF16)
        logits = lax.dot_general(rwt_ref[...], h, (((1,), (1,)), ((), ())), precision=HIGHEST,
                                 preferred_element_type=F32) + rb_ref[...]
        eidx = lax.broadcasted_iota(jnp.int32, logits.shape, 0)
        work = logits
        top = None
        for _ in range(TOP_K):
            m = jnp.max(work, axis=0, keepdims=True)
            if top is None:
                top = m
            first = jnp.min(jnp.where(work == m, eidx, N_EXPERTS), axis=0, keepdims=True)
            work = jnp.where(eidx == first, -jnp.inf, work)
        sel = work != logits
        ex = jnp.where(sel, jnp.exp(logits - top), 0.0)
        gate_s[...] = ex / jnp.sum(ex, axis=0, keepdims=True)
        before = jnp.dot(jnp.where(sel, 1.0, 0.0).astype(BF16), tri_ref[...], preferred_element_type=F32)
        rank_s[...] = jnp.where(sel, before, -1.0)
        y_ref[...] = jnp.zeros(y_ref.shape, F32)

    r_row = rank_s[pl.ds(e, 1), :]
    g_row = gate_s[pl.ds(e, 1), :]
    count = jnp.sum(jnp.where(r_row >= 0.0, 1, 0).astype(jnp.int32))
    n_blocks = (count + MOE_ROWS - 1) // MOE_ROWS

    def block(j, carry):
        rows = (lax.broadcasted_iota(jnp.int32, (MOE_ROWS, tm), 0) + j * MOE_ROWS).astype(F32)
        hit = r_row == rows
        p = jnp.where(hit, 1.0, 0.0).astype(BF16)
        xe = jnp.dot(p, hb_s[...], preferred_element_type=F32).astype(BF16)
        u = jnp.dot(xe, wup_ref[0], preferred_element_type=F32) + bup_ref[0]
        glu = jnp.minimum(u[:, :D_FF], SWIGLU_LIMIT)
        lin = jnp.clip(u[:, D_FF:], -SWIGLU_LIMIT, SWIGLU_LIMIT)
        act = glu * jax.nn.sigmoid(SWIGLU_ALPHA * glu) * (lin + 1.0)
        yb = jnp.dot(act.astype(BF16), wdn_ref[0], preferred_element_type=F32) + bdn_ref[0]
        g_rows = jnp.sum(jnp.where(hit, g_row, 0.0), axis=1, keepdims=True)
        ys = (yb * g_rows).astype(BF16)
        y_ref[...] += lax.dot_general(p, ys, (((0,), (0,)), ((), ())), preferred_element_type=F32)
        return carry

    lax.fori_loop(0, n_blocks, block, 0)


def _moe(h, rwt, rb, wup, bup, wdn, bdn, tm):
    n, d = h.shape
    tri = (lax.broadcasted_iota(jnp.int32, (tm, tm), 0) < lax.broadcasted_iota(jnp.int32, (tm, tm), 1)).astype(BF16)
    return pl.pallas_call(
        _moe_kernel,
        grid=(n // tm, N_EXPERTS),
        in_specs=[pl.BlockSpec((tm, d), lambda i, e: (i, 0)),
                  pl.BlockSpec((N_EXPERTS, d), lambda i, e: (0, 0)),
                  pl.BlockSpec((N_EXPERTS, 1), lambda i, e: (0, 0)),
                  pl.BlockSpec((tm, tm), lambda i, e: (0, 0)),
                  pl.BlockSpec((1, d, 2 * D_FF), lambda i, e: (e, 0, 0)),
                  pl.BlockSpec((1, 1, 2 * D_FF), lambda i, e: (e, 0, 0)),
                  pl.BlockSpec((1, D_FF, d), lambda i, e: (e, 0, 0)),
                  pl.BlockSpec((1, 1, d), lambda i, e: (e, 0, 0))],
        out_specs=pl.BlockSpec((tm, d), lambda i, e: (i, 0)),
        out_shape=jax.ShapeDtypeStruct((n, d), F32),
        scratch_shapes=[pltpu.VMEM((tm, d), BF16), pltpu.VMEM((N_EXPERTS, tm), F32),
                        pltpu.VMEM((N_EXPERTS, tm), F32)],
        compiler_params=_cparams(("arbitrary", "arbitrary")),
        name="moe",
    )(h, rwt, rb, tri, wup, bup, wdn, bdn)


def _deinterleave_kernel(w_ref, perm_ref, o_ref):
    o_ref[0] = jnp.dot(w_ref[0].astype(BF16), perm_ref[...], preferred_element_type=F32).astype(BF16)


def _deinterleave_cast(w):
    e, d, n = w.shape
    tk = 512
    src = lax.broadcasted_iota(jnp.int32, (n, n), 0)
    dst = lax.broadcasted_iota(jnp.int32, (n, n), 1)
    perm = (src == jnp.where(dst < n // 2, 2 * dst, 2 * (dst - n // 2) + 1)).astype(BF16)
    return pl.pallas_call(
        _deinterleave_kernel,
        grid=(e, d // tk),
        in_specs=[pl.BlockSpec((1, tk, n), lambda i, j: (i, j, 0)), pl.BlockSpec((n, n), lambda i, j: (0, 0))],
        out_specs=pl.BlockSpec((1, tk, n), lambda i, j: (i, j, 0)),
        out_shape=jax.ShapeDtypeStruct((e, d, n), BF16),
        compiler_params=_cparams(("arbitrary", "arbitrary")),
        name="deinterleave",
    )(w, perm)


def _final_kernel(x1_ref, y_ref, gt_ref, g_ref, o_ref):
    x = x1_ref[...] + gt_ref[0] * y_ref[...]
    var = jnp.mean(x * x, axis=-1, keepdims=True)
    o_ref[...] = x * lax.rsqrt(var + NORM_EPS) * g_ref[...]


def _final(x1, y, gt, g, b, t, tm):
    n, d = x1.shape
    nt = t // tm
    row = pl.BlockSpec((tm, d), lambda i: (i, 0))
    return pl.pallas_call(
        _final_kernel,
        grid=(n // tm,),
        in_specs=[row, row, pl.BlockSpec((1, 1, d), lambda i: (i // nt, 0, 0)),
                  pl.BlockSpec((1, d), lambda i: (0, 0))],
        out_specs=row,
        out_shape=jax.ShapeDtypeStruct((n, d), F32),
        compiler_params=_cparams(("arbitrary",)),
        name="final",
    )(x1, y, gt, g.reshape(1, d)).reshape(b, t, d)


MASKED = -1e30
INT_MIN = -2 ** 31


def _dsa_kernel(q_ref, qi_ref, kw_ref, k_ref, vx_ref, ki_ref, tri_ref, o_ref, key_s, bias_s, wib_s,
                *, qb, kt, q0, l_valid, n_sel):
    i = pl.program_id(1)
    ns = kt // LANES
    row = lax.broadcasted_iota(jnp.int32, (qb, 1), 0)
    qpos = q0 + i * qb + row
    lim = jnp.minimum((qpos // CHUNK + 1) * CHUNK, l_valid)
    last_lim = jnp.minimum(((q0 + i * qb + qb - 1) // CHUNK + 1) * CHUNK, l_valid)
    n_kt = (last_lim + kt - 1) // kt
    lane = lax.broadcasted_iota(jnp.int32, (1, LANES), 1)

    for h in range(IDX_HEADS):
        w = kw_ref[:, IDX_DIM + h:IDX_DIM + h + 1] * (IDX_DIM ** -0.5)
        wib_s[h] = jnp.broadcast_to(w, (qb, LANES))
    qi = qi_ref[...].astype(BF16)

    def f32_key(x):
        bits = pltpu.bitcast(x, jnp.int32)
        return bits ^ ((bits >> 31) & 0x7FFFFFFF)

    def score_tile(t, c):
        ks = pl.multiple_of(t * kt, kt)
        kit = ki_ref[0, pl.ds(ks, kt), :]
        lg = [lax.dot_general(qi[:, h * IDX_DIM:(h + 1) * IDX_DIM], kit, (((1,), (1,)), ((), ())),
                              preferred_element_type=F32) for h in range(IDX_HEADS)]
        for s in range(ns):
            sc = jnp.zeros((qb, LANES), F32)
            for h in range(IDX_HEADS):
                sc = sc + wib_s[h] * jnp.maximum(lg[h][:, s * LANES:(s + 1) * LANES], 0.0)
            kpos = ks + s * LANES + lane
            sc = jnp.where(kpos < lim, sc + 0.0, -jnp.inf)
            key_s[t, :, s * LANES:(s + 1) * LANES] = f32_key(sc)
        return c

    lax.fori_loop(0, n_kt, score_tile, 0)

    def count(pred_fn):
        def tile(t, acc):
            for s in range(ns):
                acc = acc + jnp.where(pred_fn(key_s[t, :, s * LANES:(s + 1) * LANES]), 1.0, 0.0)
            return acc
        acc = lax.fori_loop(0, n_kt, tile, jnp.zeros((qb, LANES), F32))
        return jnp.sum(acc, axis=1, keepdims=True)

    def bit_step(b, lo):
        cand = lo + jnp.left_shift(jnp.int32(1), 31 - b)
        candb = jnp.broadcast_to(cand, (qb, LANES))
        return jnp.where(count(lambda k: k >= candb) >= n_sel, cand, lo)

    thr = lax.fori_loop(0, 32, bit_step, jnp.full((qb, 1), INT_MIN, jnp.int32))
    thrb = jnp.broadcast_to(thr, (qb, LANES))
    need = jnp.broadcast_to(n_sel - count(lambda k: k > thrb), (qb, LANES))

    def sel_tile(t, off):
        for s in range(ns):
            key = key_s[t, :, s * LANES:(s + 1) * LANES]
            eq = key == thrb
            pre = jnp.dot(jnp.where(eq, 1.0, 0.0).astype(BF16), tri_ref[...], preferred_element_type=F32)
            kpos = t * kt + s * LANES + lane
            keep = jnp.logical_or(key > thrb, jnp.logical_and(eq, pre[:, :LANES] + off < need))
            keep = jnp.logical_and(keep, kpos < lim)
            bias_s[t, :, s * LANES:(s + 1) * LANES] = jnp.where(keep, 0.0, MASKED)
            off = off + pre[:, LANES:]
        return off

    lax.fori_loop(0, n_kt, sel_tile, jnp.zeros((qb, LANES), F32))

    rep = ATT_HEADS // KV_HEADS
    qgs = []
    for g in range(KV_HEADS):
        qg = jnp.concatenate([q_ref[:, (g * rep + r) * HEAD_DIM:(g * rep + r + 1) * HEAD_DIM] for r in range(rep)],
                             axis=0)
        qgs.append((qg * (HEAD_DIM ** -0.5)).astype(BF16))

    def att_tile(t, carry):
        ks = pl.multiple_of(t * kt, kt)
        out = []
        for g in range(KV_HEADS):
            m, acc = carry[g]
            s = lax.dot_general(qgs[g], k_ref[0, g, pl.ds(ks, kt), :], (((1,), (1,)), ((), ())),
                                preferred_element_type=F32)
            s = (s.reshape(rep, qb, kt) + bias_s[t][None]).reshape(rep * qb, kt)
            m_new = jnp.maximum(m, jnp.max(s, axis=1, keepdims=True))
            p = jnp.exp(s - m_new)
            pv = jnp.dot(p.astype(BF16), vx_ref[0, pl.ds(ks, kt), g * LANES:(g + 1) * LANES],
                         preferred_element_type=F32)
            out.append((m_new, acc * jnp.exp(m - m_new) + pv))
        return tuple(out)

    init = (jnp.full((rep * qb, 1), MASKED, F32), jnp.zeros((rep * qb, LANES), F32))
    res = lax.fori_loop(0, n_kt, att_tile, (init,) * KV_HEADS)
    for g in range(KV_HEADS):
        acc = res[g][1]
        out = acc * pltpu.roll(1.0 / acc, HEAD_DIM, 1)
        for r in range(rep):
            h = g * rep + r
            o_ref[:, h * HEAD_DIM:(h + 1) * HEAD_DIM] = out[r * qb:(r + 1) * qb, :HEAD_DIM]


def _dsa(q, qi, kw, k_bf, v_bf, ki_bf, *, qb, kt, q0, l_valid, n_sel):
    b, lp = ki_bf.shape[:2]
    nq = q.shape[0] // (b * qb)
    k_g = jnp.moveaxis(k_bf, 2, 1)
    ones = jnp.ones((b, lp, KV_HEADS, HEAD_DIM), BF16)
    vx = jnp.concatenate([v_bf, ones], axis=-1).reshape(b, lp, KV_HEADS * LANES)
    tr = lax.broadcasted_iota(jnp.int32, (LANES, 2 * LANES), 0)
    tc = lax.broadcasted_iota(jnp.int32, (LANES, 2 * LANES), 1)
    tri = jnp.logical_or(tr < tc, tc >= LANES).astype(BF16)
    row = lambda w: pl.BlockSpec((qb, w), lambda bi, i: (bi * nq + i, 0))
    kern = functools.partial(_dsa_kernel, qb=qb, kt=kt, q0=q0, l_valid=l_valid, n_sel=n_sel)
    return pl.pallas_call(
        kern,
        grid=(b, nq),
        in_specs=[row(ATT_WIDTH), row(QI_WIDTH), row(KW_WIDTH),
                  pl.BlockSpec((1, KV_HEADS, lp, HEAD_DIM), lambda bi, i: (bi, 0, 0, 0)),
                  pl.BlockSpec((1, lp, KV_HEADS * LANES), lambda bi, i: (bi, 0, 0)),
                  pl.BlockSpec((1, lp, IDX_DIM), lambda bi, i: (bi, 0, 0)),
                  pl.BlockSpec((LANES, 2 * LANES), lambda bi, i: (0, 0))],
        out_specs=row(ATT_WIDTH),
        out_shape=jax.ShapeDtypeStruct((q.shape[0], ATT_WIDTH), F32),
        scratch_shapes=[pltpu.VMEM((lp // kt, qb, kt), jnp.int32), pltpu.VMEM((lp // kt, qb, kt), F32),
                        pltpu.VMEM((IDX_HEADS, qb, LANES), F32)],
        compiler_params=_cparams(("arbitrary", "arbitrary")),
        name="dsa",
    )(q, qi, kw, k_g, vx, ki_bf, tri)


def _split_bf16(x):
    hi = x.astype(BF16)
    return hi, (x - hi.astype(F32)).astype(BF16)


_NN = (((1,), (0,)), ((), ()))


def _dots(a_sp, b_sp, dims=_NN):
    (ah, al), (bh, bl) = a_sp, b_sp
    d = functools.partial(lax.dot_general, dimension_numbers=dims, preferred_element_type=F32)
    return d(ah, bh) + (d(ah, bl) + d(al, bh))


def _dot3(a, b, dims=_NN):
    return _dots(_split_bf16(a), _split_bf16(b), dims)


def _dot2(a, b_exact):
    ah, al = _split_bf16(a)
    return jnp.dot(ah, b_exact, preferred_element_type=F32) + jnp.dot(al, b_exact, preferred_element_type=F32)


_NT = (((1,), (1,)), ((), ()))
_TN = (((0,), (0,)), ((), ()))


def _rwkv_kernel(p_ref, prev_ref, s0_ref, mu_ref, w0_ref, w2_ref, a0_ref, a2_ref, g2_ref, kk_ref, ka_ref,
                 rk_ref, lnw_ref, lnb_ref, hsum_ref, tri_ref, o_ref, st_ref, s_s, prev_s, o_s, *, c):
    j = pl.program_id(1)

    @pl.when(j == 0)
    def _init():
        s_s[...] = s0_ref[0]
        prev_s[...] = prev_ref[0]

    p = p_ref[...]
    row = lax.broadcasted_iota(jnp.int32, (c, 1), 0)
    p_prev = jnp.where(row == 0, prev_s[...], pltpu.roll(p, 1, 0))
    prev_s[...] = p[c - 1:c, :]
    xs = p + (p_prev - p) * mu_ref[...]
    o1, o2, o3 = RWKV_WIDTH, 2 * RWKV_WIDTH, 3 * RWKV_WIDTH
    r, k, v = xs[:, :o1], xs[:, o1:o2], xs[:, o2:o3]
    dw = xs[:, o3:o3 + W_LORA]
    da = xs[:, o3 + W_LORA:o3 + W_LORA + A_LORA]
    dg = xs[:, o3 + W_LORA + A_LORA:]
    hsum = hsum_ref[...]
    w_log = -jax.nn.softplus(-(w0_ref[...] + _dot3(jnp.tanh(dw), w2_ref[...]))) - 0.5
    lw = -jnp.exp(w_log)
    a = jax.nn.sigmoid(a0_ref[...] + _dot3(da, a2_ref[...]))
    g = _dot3(jax.nn.sigmoid(dg), g2_ref[...])
    kk = k * kk_ref[...]
    kk = kk / jnp.maximum(jnp.sqrt(_dot2(kk * kk, hsum)), 1e-12)
    km = k * (1.0 + (a - 1.0) * ka_ref[...])
    bm = kk * a
    lw_hi, lw_lo = _split_bf16(lw)
    tri = tri_ref[...]
    cum = (jnp.dot(tri, lw_hi, preferred_element_type=F32)
           + jnp.dot(tri, lw_lo, preferred_element_type=F32))
    tot = cum[c - 1:c, :]
    e_in = jnp.exp(cum)
    e_out = jnp.exp(-cum)
    e_end = jnp.exp(tot - cum)
    a_t = -kk * jnp.exp(cum - lw)
    r_t = r * e_in
    b_t = bm * e_out
    k_t = km * e_out
    b_h = bm * e_end
    k_h = km * e_end
    gam = jnp.exp(tot)

    ri = lax.broadcasted_iota(jnp.int32, (c, c), 0)
    ci = lax.broadcasted_iota(jnp.int32, (c, c), 1)
    strict = ri > ci
    incl = ri >= ci
    eye = jnp.where(ri == ci, 1.0, 0.0)
    n_double = int(np.log2(c)) - 1
    heads = range(RWKV_HEADS)
    sls = [slice(h * HEAD_DIM, (h + 1) * HEAD_DIM) for h in heads]
    s0 = [s_s[h] for h in heads]
    s0_sp = [_split_bf16(s) for s in s0]
    ar_sp = [_split_bf16(jnp.concatenate([a_t[:, sl], r_t[:, sl]], axis=0)) for sl in sls]
    bk_sp = [_split_bf16(jnp.concatenate([b_t[:, sl], k_t[:, sl]], axis=0)) for sl in sls]
    v_sp = [_split_bf16(v[:, sl]) for sl in sls]
    m = [_dots(ar_sp[h], bk_sp[h], _NT) for h in heads]
    as0 = [_dots(ar_sp[h], s0_sp[h], _NT) for h in heads]
    x_sp = [_split_bf16(jnp.where(strict, m[h][:c, :c], 0.0)) for h in heads]
    lak_sp = [_split_bf16(jnp.where(strict, m[h][:c, c:], 0.0)) for h in heads]
    tinv = [eye + jnp.where(strict, m[h][:c, :c], 0.0) for h in heads]
    rhs = [as0[h][:c] + _dots(lak_sp[h], v_sp[h]) for h in heads]
    for _ in range(n_double):
        x_sp = [_split_bf16(_dots(x_sp[h], x_sp[h])) for h in heads]
        tinv = [tinv[h] + _dots(_split_bf16(tinv[h]), x_sp[h]) for h in heads]
    u = [_dot3(tinv[h], rhs[h]) for h in heads]
    u_sp = [_split_bf16(uh) for uh in u]
    for h in heads:
        m_rb = jnp.where(incl, m[h][c:, :c], 0.0)
        m_rk = jnp.where(incl, m[h][c:, c:], 0.0)
        o_s[:, sls[h]] = as0[h][c:] + _dots(_split_bf16(m_rb), u_sp[h]) + _dots(_split_bf16(m_rk), v_sp[h])
    s_new = [s0[h] * gam[:, sls[h]] + _dots(u_sp[h], _split_bf16(b_h[:, sls[h]]), _TN)
             + _dots(v_sp[h], _split_bf16(k_h[:, sls[h]]), _TN) for h in heads]
    for h in heads:
        s_s[h] = s_new[h]

    o = o_s[...]
    inv_d = 1.0 / HEAD_DIM
    mean = _dot2(o, hsum) * inv_d
    cen = o - mean
    var = _dot2(cen * cen, hsum) * inv_d
    on = cen * lax.rsqrt(var + GN_EPS) * lnw_ref[...] + lnb_ref[...]
    bonus = _dot2(r * km * rk_ref[...], hsum) * v
    o_ref[...] = (on + bonus) * g

    @pl.when(j == pl.num_programs(1) - 1)
    def _fin():
        st_ref[0] = s_s[...]


def _rwkv(p, prev, s0, rw, c):
    mu, w0, w2, a0, a2, g2, k_k, k_a, r_k, lnx_w, lnx_b = rw
    b = prev.shape[0]
    nc = p.shape[0] // (b * c)
    hid = lax.broadcasted_iota(jnp.int32, (RWKV_WIDTH, RWKV_WIDTH), 0) // HEAD_DIM
    hsum = (hid == hid.T).astype(BF16)
    tri = (lax.broadcasted_iota(jnp.int32, (c, c), 0) >= lax.broadcasted_iota(jnp.int32, (c, c), 1)).astype(BF16)
    vec = lambda a: a.reshape(1, -1)
    full = lambda a: pl.BlockSpec(a.shape, lambda bi, j: (0,) * a.ndim)
    small = [vec(mu), vec(w0), w2, vec(a0), a2, g2, vec(k_k), vec(k_a), vec(r_k), vec(lnx_w), vec(lnx_b), hsum, tri]
    return pl.pallas_call(
        functools.partial(_rwkv_kernel, c=c),
        grid=(b, nc),
        in_specs=[pl.BlockSpec((c, RWKV_COLS), lambda bi, j: (bi * nc + j, 0)),
                  pl.BlockSpec((1, 1, RWKV_COLS), lambda bi, j: (bi, 0, 0)),
                  pl.BlockSpec((1, RWKV_HEADS, HEAD_DIM, HEAD_DIM), lambda bi, j: (bi, 0, 0, 0))]
                 + [full(a) for a in small],
        out_specs=[pl.BlockSpec((c, RWKV_WIDTH), lambda bi, j: (bi * nc + j, 0)),
                   pl.BlockSpec((1, RWKV_HEADS, HEAD_DIM, HEAD_DIM), lambda bi, j: (bi, 0, 0, 0))],
        out_shape=[jax.ShapeDtypeStruct((p.shape[0], RWKV_WIDTH), F32),
                   jax.ShapeDtypeStruct((b, RWKV_HEADS, HEAD_DIM, HEAD_DIM), F32)],
        scratch_shapes=[pltpu.VMEM((RWKV_HEADS, HEAD_DIM, HEAD_DIM), F32), pltpu.VMEM((1, RWKV_COLS), F32),
                        pltpu.VMEM((c, RWKV_WIDTH), F32)],
        compiler_params=_cparams(("arbitrary", "arbitrary")),
        name="rwkv",
    )(p, prev, s0, *small)


def _branch(x, mods, pos, prev, s0, caches, weights, tm, tm_moe):
    (g_mix, g_ffn, g_final, w_r, w_a, rw, w_out, rwt, rb, wup, bup, wdn, bdn) = weights
    sh1, sc1, gt1, sh2, sc2, gt2 = mods
    b, t, d = x.shape
    tab, tabk = _rope_tables(pos)
    p_r, q, k, v, qi, kw = _inproj(x, g_mix, sh1, sc1, w_r, w_a, tab, tabk, tm)
    o_r, wkv = _rwkv(p_r, prev, s0, rw, min(CHUNK, t))
    shift = p_r.reshape(b, t, RWKV_COLS)[:, -1:]
    k4 = k.reshape(b, t, KV_HEADS, HEAD_DIM)
    v4 = v.reshape(b, t, KV_HEADS, HEAD_DIM)
    ki = kw.reshape(b, t, KW_WIDTH)[..., :IDX_DIM]
    kt = 512
    if caches is None:
        k_all, v_all, ki_all, q0, l_valid = k4, v4, ki, 0, t
    else:
        ck, cv, cki = caches
        q0 = ck.shape[1]
        l_valid = q0 + t
        padn = (-l_valid) % kt
        zpad = lambda a: jnp.concatenate([a, jnp.zeros((b, padn) + a.shape[2:], a.dtype)], axis=1)
        k_all = zpad(jnp.concatenate([ck, k4], axis=1))
        v_all = zpad(jnp.concatenate([cv, v4], axis=1))
        ki_all = zpad(jnp.concatenate([cki, ki], axis=1))
    o_a = _dsa(q, qi, kw, k_all.astype(BF16), v_all.astype(BF16), ki_all.astype(BF16),
               qb=min(Q_BLOCK, t), kt=kt, q0=q0, l_valid=l_valid, n_sel=min(TOPK_MAX, l_valid // 4))
    x1, h2 = _outproj(o_r, o_a, x, w_out, gt1, g_ffn, sh2, sc2, tm)
    y_moe = _moe(h2, rwt, rb, wup, bup, wdn, bdn, tm_moe)
    y = _final(x1, y_moe, gt2, g_final, b, t, tm)
    return y, k4, v4, ki, wkv, shift


def kernel(x_prompt, x_sample, c_prompt, c_sample, cache_k, cache_v, cache_kidx, state_wkv, state_shift,
           w_ada, b_ada, g_mix, g_ffn, g_final, w_in, mu_shift, w0, w_lora_w, a0, w_lora_a, w_lora_g,
           k_k, k_a, r_k, lnx_w, lnx_b, w_out, router_w, router_b, w_up, b_up, w_down, b_down):
    B, T, D = x_prompt.shape
    DB, DT, _ = x_sample.shape
    P = cache_k.shape[2]
    l = 0
    assert w_ada.shape[0] == 1

    rows = B + DB
    pad = (-rows) % 8
    c_all = jnp.concatenate([c_prompt, c_sample, jnp.zeros((pad, D), F32)], axis=0)
    m = _ada(c_all, w_ada[l], b_ada[l])
    mods_p = tuple(t.reshape(B, 1, D) for t in jnp.split(m[:B], 6, axis=-1))
    mods_s = tuple(t.reshape(DB, 1, D) for t in jnp.split(m[B:rows], 6, axis=-1))

    w_r = w_in[l][:, :RWKV_COLS].astype(BF16)
    w_att = w_in[l][:, RWKV_COLS:]
    att_pad = KW_WIDTH - IDX_DIM - IDX_HEADS
    w_a = jnp.concatenate([w_att, jnp.zeros((D, att_pad), F32)], axis=1).astype(BF16)
    rw = (mu_shift[l], w0[l], w_lora_w[l], a0[l], w_lora_a[l], w_lora_g[l], k_k[l], k_a[l], r_k[l],
          lnx_w[l], lnx_b[l])
    wup = _deinterleave_cast(w_up[l])
    bup = jnp.concatenate([b_up[l][..., 0::2], b_up[l][..., 1::2]], axis=-1).reshape(N_EXPERTS, 1, 2 * D_FF)
    wdn = w_down[l].astype(BF16)
    bdn = b_down[l].reshape(N_EXPERTS, 1, D)
    weights = (g_mix[l], g_ffn[l], g_final, w_r, w_a, rw, w_out[l].astype(BF16), router_w[l].T,
               router_b[l].reshape(N_EXPERTS, 1), wup, bup, wdn, bdn)

    pos_p = jnp.arange(T, dtype=jnp.int32)
    pos_s = P + jnp.arange(DT, dtype=jnp.int32)
    yp, kp, vp, kip, wkvp, shp = _branch(
        x_prompt, mods_p, pos_p, jnp.zeros((B, 1, RWKV_COLS), F32),
        jnp.zeros((B, RWKV_HEADS, HEAD_DIM, HEAD_DIM), F32), None, weights, 512, 1024)
    ys, ks, vs, kis, wkvs, shs = _branch(
        x_sample, mods_s, pos_s, state_shift[l], state_wkv[l],
        (cache_k[l], cache_v[l], cache_kidx[l]), weights, DT, DB * DT)
    return (yp, ys, kp[None], vp[None], kip[None], wkvp[None], shp[None],
            ks[None], vs[None], kis[None], wkvs[None], shs[None])
```

```python
import functools

import numpy as np
import jax
import jax.numpy as jnp
from jax import lax
from jax.experimental import pallas as pl
from jax.experimental.pallas import tpu as pltpu

F32 = jnp.float32
BF16 = jnp.bfloat16
HIGHEST = lax.Precision.HIGHEST

D_MODEL = 1024
CHUNK = 64
HEAD_DIM = 64
RWKV_HEADS = 8
RWKV_WIDTH = RWKV_HEADS * HEAD_DIM
W_LORA = 64
A_LORA = 64
G_LORA = 128
RWKV_COLS = 3 * RWKV_WIDTH + W_LORA + A_LORA + G_LORA
GN_EPS = 64e-5
ATT_HEADS = 8
ATT_WIDTH = ATT_HEADS * HEAD_DIM
KV_HEADS = 2
KV_WIDTH = KV_HEADS * HEAD_DIM
IDX_HEADS = 4
IDX_DIM = 64
TOPK_MAX = 256
Q_BLOCK = 128
ROPE_THETA = 500000.0
ROT_DIM = HEAD_DIM // 4
N_EXPERTS = 32
TOP_K = 4
D_FF = 1024
SWIGLU_ALPHA = 1.702
SWIGLU_LIMIT = 7.0
NORM_EPS = 1e-5

LANES = 128
MOE_ROWS = 128
MOE_TILE = 896
VMEM_LIMIT = 56 * 1024 * 1024

QI_WIDTH = IDX_HEADS * IDX_DIM
KW_WIDTH = LANES


def _cparams(sem):
    return pltpu.CompilerParams(dimension_semantics=sem, vmem_limit_bytes=VMEM_LIMIT)


def _ada_kernel(c_ref, w_ref, b_ref, o_ref):
    c = c_ref[...]
    s = c * jax.nn.sigmoid(c)
    o_ref[...] = jnp.dot(s, w_ref[...], precision=HIGHEST, preferred_element_type=F32) + b_ref[...]


def _ada(c, w, b):
    rows, d = c.shape
    n = w.shape[1]
    tn = 1536
    return pl.pallas_call(
        _ada_kernel,
        grid=(n // tn,),
        in_specs=[pl.BlockSpec((rows, d), lambda j: (0, 0)),
                  pl.BlockSpec((d, tn), lambda j: (0, j)),
                  pl.BlockSpec((1, tn), lambda j: (0, j))],
        out_specs=pl.BlockSpec((rows, tn), lambda j: (0, j)),
        out_shape=jax.ShapeDtypeStruct((rows, n), F32),
        compiler_params=_cparams(("arbitrary",)),
        name="ada",
    )(c, w, b.reshape(1, n))


def _rope_slab(y, tab_ref):
    return (y * tab_ref[0] + pltpu.roll(y, LANES - ROT_DIM // 2, 1) * tab_ref[1]
            + pltpu.roll(y, ROT_DIM // 2, 1) * tab_ref[2])


def _norm_mod(x, g, sh, sc):
    var = jnp.mean(x * x, axis=-1, keepdims=True)
    return (x * lax.rsqrt(var + NORM_EPS) * g) * (1.0 + sc) + sh


def _inproj_kernel(x_ref, g_ref, sh_ref, sc_ref, wr_ref, wa_ref, tab_ref, tabk_ref,
                   rw_ref, q_ref, k_ref, v_ref, qi_ref, kw_ref):
    h = _norm_mod(x_ref[...], g_ref[...], sh_ref[0], sc_ref[0]).astype(BF16)
    rw_ref[...] = jnp.dot(h, wr_ref[...], preferred_element_type=F32)
    pa = jnp.dot(h, wa_ref[...], preferred_element_type=F32)
    off = 0
    for ref, width, rot in ((q_ref, ATT_WIDTH, True), (k_ref, KV_WIDTH, True), (v_ref, KV_WIDTH, False),
                            (qi_ref, QI_WIDTH, True)):
        for s in range(width // LANES):
            slab = pa[:, off + s * LANES: off + (s + 1) * LANES]
            ref[:, s * LANES:(s + 1) * LANES] = _rope_slab(slab, tab_ref) if rot else slab
        off += width
    kw_ref[...] = _rope_slab(pa[:, off:off + LANES], tabk_ref)


def _rope_tables(pos):
    half = ROT_DIM // 2
    inv = ROPE_THETA ** (-jnp.arange(0, ROT_DIM, 2, dtype=F32) / ROT_DIM)
    ang = pos.astype(F32)[:, None] * inv[None, :]
    cos, sin = jnp.cos(ang), jnp.sin(ang)
    t = pos.shape[0]
    one = jnp.ones((t, HEAD_DIM - ROT_DIM), F32)
    zero_r = jnp.zeros((t, HEAD_DIM - ROT_DIM), F32)
    zero_h = jnp.zeros((t, half), F32)
    c_head = jnp.concatenate([cos, cos, one], axis=1)
    up_head = jnp.concatenate([-sin, zero_h, zero_r], axis=1)
    dn_head = jnp.concatenate([zero_h, sin, zero_r], axis=1)
    tab = jnp.stack([jnp.tile(c_head, (1, 2)), jnp.tile(up_head, (1, 2)), jnp.tile(dn_head, (1, 2))])
    wscale = jnp.concatenate([jnp.full((t, IDX_HEADS), IDX_HEADS ** -0.5, F32),
                              jnp.ones((t, HEAD_DIM - IDX_HEADS), F32)], axis=1)
    zero64 = jnp.zeros((t, HEAD_DIM), F32)
    tabk = jnp.stack([jnp.concatenate([c_head, wscale], axis=1),
                      jnp.concatenate([up_head, zero64], axis=1),
                      jnp.concatenate([dn_head, zero64], axis=1)])
    return tab, tabk


def _inproj(x, g, sh, sc, w_r, w_a, tab, tabk, tm):
    b, t, d = x.shape
    nt = t // tm
    n = b * t
    x2 = x.reshape(n, d)
    widths = (RWKV_COLS, ATT_WIDTH, KV_WIDTH, KV_WIDTH, QI_WIDTH, KW_WIDTH)
    row = lambda w: pl.BlockSpec((tm, w), lambda i: (i, 0))
    mod = pl.BlockSpec((1, 1, d), lambda i: (i // nt, 0, 0))
    tabspec = pl.BlockSpec((3, tm, LANES), lambda i: (0, i % nt, 0))
    return pl.pallas_call(
        _inproj_kernel,
        grid=(n // tm,),
        in_specs=[row(d), pl.BlockSpec((1, d), lambda i: (0, 0)), mod, mod,
                  pl.BlockSpec(w_r.shape, lambda i: (0, 0)), pl.BlockSpec(w_a.shape, lambda i: (0, 0)),
                  tabspec, tabspec],
        out_specs=[row(w) for w in widths],
        out_shape=[jax.ShapeDtypeStruct((n, w), F32) for w in widths],
        compiler_params=_cparams(("arbitrary",)),
        name="inproj",
    )(x2, g.reshape(1, d), sh, sc, w_r, w_a, tab, tabk)


def _outproj_kernel(or_ref, oa_ref, x_ref, w_ref, gt_ref, g_ref, sh_ref, sc_ref, x1_ref, h2_ref):
    m = jnp.dot(or_ref[...].astype(BF16), w_ref[:RWKV_WIDTH, :], preferred_element_type=F32)
    m = m + jnp.dot(oa_ref[...].astype(BF16), w_ref[RWKV_WIDTH:, :], preferred_element_type=F32)
    x1 = x_ref[...] + gt_ref[0] * m
    x1_ref[...] = x1
    h2_ref[...] = _norm_mod(x1, g_ref[...], sh_ref[0], sc_ref[0])


def _outproj(o_r, o_a, x, w_out, gt, g, sh, sc, tm):
    b, t, d = x.shape
    nt = t // tm
    n = b * t
    row = lambda w: pl.BlockSpec((tm, w), lambda i: (i, 0))
    mod = pl.BlockSpec((1, 1, d), lambda i: (i // nt, 0, 0))
    return pl.pallas_call(
        _outproj_kernel,
        grid=(n // tm,),
        in_specs=[row(RWKV_WIDTH), row(ATT_WIDTH), row(d), pl.BlockSpec(w_out.shape, lambda i: (0, 0)),
                  mod, pl.BlockSpec((1, d), lambda i: (0, 0)), mod, mod],
        out_specs=[row(d), row(d)],
        out_shape=[jax.ShapeDtypeStruct((n, d), F32)] * 2,
        compiler_params=_cparams(("arbitrary",)),
        name="outproj",
    )(o_r.reshape(n, RWKV_WIDTH), o_a.reshape(n, ATT_WIDTH), x.reshape(n, d), w_out, gt, g.reshape(1, d), sh, sc)


def _moe_kernel(h_ref, rwt_ref, rb_ref, tri_ref, wup_ref, bup_ref, wdn_ref, bdn_ref, y_ref,
                hb_s, rank_s, gate_s, *, n_tokens):
    e = pl.program_id(1)
    tm = h_ref.shape[0]

    @pl.when(e == 0)
    def _route():
        first_row = pl.program_id(0) * tm
        in_rows = first_row + lax.broadcasted_iota(jnp.int32, (tm, 1), 0) < n_tokens
        in_cols = first_row + lax.broadcasted_iota(jnp.int32, (1, tm), 1) < n_tokens
        hb = jnp.where(in_rows, h_ref[...], 0.0).astype(BF16)
        hb_s[...] = hb
        logits = lax.dot_general(rwt_ref[...].astype(BF16), hb, (((1,), (1,)), ((), ())),
                                 preferred_element_type=F32) + rb_ref[...]
        eidx = lax.broadcasted_iota(jnp.int32, logits.shape, 0)
        work = logits
        top = None
        for _ in range(TOP_K):
            m = jnp.max(work, axis=0, keepdims=True)
            if top is None:
                top = m
            first = jnp.min(jnp.where(work == m, eidx, N_EXPERTS), axis=0, keepdims=True)
            work = jnp.where(eidx == first, -jnp.inf, work)
        ex = jnp.where(work != logits, jnp.exp(logits - top), 0.0)
        sel = jnp.logical_and(work != logits, in_cols)
        gate_s[...] = ex / jnp.sum(ex, axis=0, keepdims=True)
        before = jnp.dot(jnp.where(sel, 1.0, 0.0).astype(BF16), tri_ref[...], preferred_element_type=F32)
        rank_s[...] = jnp.where(sel, before, -1.0)
        y_ref[...] = jnp.zeros(y_ref.shape, F32)

    r_row = rank_s[pl.ds(e, 1), :]
    g_row = gate_s[pl.ds(e, 1), :]
    count = jnp.sum(jnp.where(r_row >= 0.0, 1, 0).astype(jnp.int32))
    n_blocks = (count + MOE_ROWS - 1) // MOE_ROWS

    def block(j, carry):
        rows = (lax.broadcasted_iota(jnp.int32, (MOE_ROWS, tm), 0) + j * MOE_ROWS).astype(F32)
        hit = r_row == rows
        p = jnp.where(hit, 1.0, 0.0).astype(BF16)
        xe = jnp.dot(p, hb_s[...], preferred_element_type=F32).astype(BF16)
        u = jnp.dot(xe, wup_ref[0], preferred_element_type=F32) + bup_ref[0]
        glu = jnp.minimum(u[:, :D_FF], SWIGLU_LIMIT)
        lin = jnp.clip(u[:, D_FF:], -SWIGLU_LIMIT, SWIGLU_LIMIT)
        act = glu * jax.nn.sigmoid(SWIGLU_ALPHA * glu) * (lin + 1.0)
        yb = jnp.dot(act.astype(BF16), wdn_ref[0], preferred_element_type=F32) + bdn_ref[0]
        g_rows = jnp.sum(jnp.where(hit, g_row, 0.0), axis=1, keepdims=True)
        ys = (yb * g_rows).astype(BF16)
        y_ref[...] += lax.dot_general(p, ys, (((0,), (0,)), ((), ())), preferred_element_type=F32)
        return carry

    lax.fori_loop(0, n_blocks, block, 0)


def _moe(h, rwt, rb, wup, bup, wdn, bdn, tm):
    n, d = h.shape
    tri = (lax.broadcasted_iota(jnp.int32, (tm, tm), 0) < lax.broadcasted_iota(jnp.int32, (tm, tm), 1)).astype(BF16)
    return pl.pallas_call(
        functools.partial(_moe_kernel, n_tokens=n),
        grid=(pl.cdiv(n, tm), N_EXPERTS),
        in_specs=[pl.BlockSpec((tm, d), lambda i, e: (i, 0)),
                  pl.BlockSpec((N_EXPERTS, d), lambda i, e: (0, 0)),
                  pl.BlockSpec((N_EXPERTS, 1), lambda i, e: (0, 0)),
                  pl.BlockSpec((tm, tm), lambda i, e: (0, 0)),
                  pl.BlockSpec((1, d, 2 * D_FF), lambda i, e: (e, 0, 0)),
                  pl.BlockSpec((1, 1, 2 * D_FF), lambda i, e: (e, 0, 0)),
                  pl.BlockSpec((1, D_FF, d), lambda i, e: (e, 0, 0)),
                  pl.BlockSpec((1, 1, d), lambda i, e: (e, 0, 0))],
        out_specs=pl.BlockSpec((tm, d), lambda i, e: (i, 0)),
        out_shape=jax.ShapeDtypeStruct((n, d), F32),
        scratch_shapes=[pltpu.VMEM((tm, d), BF16), pltpu.VMEM((N_EXPERTS, tm), F32),
                        pltpu.VMEM((N_EXPERTS, tm), F32)],
        compiler_params=_cparams(("arbitrary", "arbitrary")),
        name="moe",
    )(h, rwt, rb, tri, wup, bup, wdn, bdn)


def _deinterleave_kernel(w_ref, perm_ref, o_ref):
    o_ref[0] = jnp.dot(w_ref[0].astype(BF16), perm_ref[...], preferred_element_type=F32).astype(BF16)


def _deinterleave_cast(w):
    e, d, n = w.shape
    tk = 512
    src = lax.broadcasted_iota(jnp.int32, (n, n), 0)
    dst = lax.broadcasted_iota(jnp.int32, (n, n), 1)
    perm = (src == jnp.where(dst < n // 2, 2 * dst, 2 * (dst - n // 2) + 1)).astype(BF16)
    return pl.pallas_call(
        _deinterleave_kernel,
        grid=(e, d // tk),
        in_specs=[pl.BlockSpec((1, tk, n), lambda i, j: (i, j, 0)), pl.BlockSpec((n, n), lambda i, j: (0, 0))],
        out_specs=pl.BlockSpec((1, tk, n), lambda i, j: (i, j, 0)),
        out_shape=jax.ShapeDtypeStruct((e, d, n), BF16),
        compiler_params=_cparams(("arbitrary", "arbitrary")),
        name="deinterleave",
    )(w, perm)


def _final_kernel(x1_ref, y_ref, gt_ref, g_ref, o_ref):
    x = x1_ref[...] + gt_ref[0] * y_ref[...]
    var = jnp.mean(x * x, axis=-1, keepdims=True)
    o_ref[...] = x * lax.rsqrt(var + NORM_EPS) * g_ref[...]


def _final(x1, y, gt, g, b, t, tm):
    n, d = x1.shape
    nt = t // tm
    row = pl.BlockSpec((tm, d), lambda i: (i, 0))
    return pl.pallas_call(
        _final_kernel,
        grid=(n // tm,),
        in_specs=[row, row, pl.BlockSpec((1, 1, d), lambda i: (i // nt, 0, 0)),
                  pl.BlockSpec((1, d), lambda i: (0, 0))],
        out_specs=row,
        out_shape=jax.ShapeDtypeStruct((n, d), F32),
        compiler_params=_cparams(("arbitrary",)),
        name="final",
    )(x1, y, gt, g.reshape(1, d)).reshape(b, t, d)


MASKED = -1e30
INT16_MIN = -2 ** 15


def _dsa_kernel(q_ref, qi_ref, kw_ref, k_ref, vx_ref, ki_ref, tri_ref, o_ref, key_s, half_s, bias_s, wib_s,
                *, qb, kt, q0, l_valid, n_sel):
    i = pl.program_id(1)
    ns = kt // LANES
    row = lax.broadcasted_iota(jnp.int32, (qb, 1), 0)
    qpos = q0 + i * qb + row
    lim = jnp.minimum((qpos // CHUNK + 1) * CHUNK, l_valid)
    last_lim = jnp.minimum(((q0 + i * qb + qb - 1) // CHUNK + 1) * CHUNK, l_valid)
    n_kt = (last_lim + kt - 1) // kt
    lane = lax.broadcasted_iota(jnp.int32, (1, LANES), 1)

    for h in range(IDX_HEADS):
        w = kw_ref[:, IDX_DIM + h:IDX_DIM + h + 1] * (IDX_DIM ** -0.5)
        wib_s[h] = jnp.broadcast_to(w, (qb, LANES))
    qi = qi_ref[...].astype(BF16)

    def f32_key(x):
        bits = pltpu.bitcast(x, jnp.int32)
        return bits ^ ((bits >> 31) & 0x7FFFFFFF)

    def score_tile(t, c):
        ks = pl.multiple_of(t * kt, kt)
        kit = ki_ref[0, pl.ds(ks, kt), :]
        lg = [lax.dot_general(qi[:, h * IDX_DIM:(h + 1) * IDX_DIM], kit, (((1,), (1,)), ((), ())),
                              preferred_element_type=F32) for h in range(IDX_HEADS)]
        for s in range(ns):
            sc = jnp.zeros((qb, LANES), F32)
            for h in range(IDX_HEADS):
                sc = sc + wib_s[h] * jnp.maximum(lg[h][:, s * LANES:(s + 1) * LANES], 0.0)
            kpos = ks + s * LANES + lane
            sc = jnp.where(kpos < lim, sc + 0.0, -jnp.inf)
            key = f32_key(sc)
            key_s[t, :, s * LANES:(s + 1) * LANES] = key
            half_s[t, :, s * LANES:(s + 1) * LANES] = (key >> 16).astype(jnp.int16)
        return c

    lax.fori_loop(0, n_kt, score_tile, 0)

    def count(pred_fn):
        def tile(t, acc):
            for s in range(ns):
                acc = acc + jnp.where(pred_fn(key_s[t, :, s * LANES:(s + 1) * LANES]), 1.0, 0.0)
            return acc
        acc = lax.fori_loop(0, n_kt, tile, jnp.zeros((qb, LANES), F32))
        return jnp.sum(acc, axis=1, keepdims=True)

    def count16(pred_fn):
        one, zero = jnp.ones((), jnp.int16), jnp.zeros((), jnp.int16)

        def tile(t, acc):
            for s in range(ns):
                acc = acc + jnp.where(pred_fn(half_s[t, :, s * LANES:(s + 1) * LANES]), one, zero)
            return acc
        acc = lax.fori_loop(0, n_kt, tile, jnp.zeros((qb, LANES), jnp.int16))
        return jnp.sum(acc.astype(F32), axis=1, keepdims=True)

    def bcast16(x):
        return jnp.broadcast_to(x, (qb, LANES)).astype(jnp.int16)

    def kth_largest16(target):
        def bit_step(b, lo):
            cand = lo + jnp.left_shift(jnp.int32(1), 15 - b)
            candb = bcast16(cand)
            return jnp.where(count16(lambda k: k >= candb) >= target, cand, lo)
        return lax.fori_loop(0, 16, bit_step, jnp.full((qb, 1), INT16_MIN, jnp.int32))

    t_hi = kth_largest16(n_sel)
    t_hib = bcast16(t_hi)
    need_lo = n_sel - count16(lambda k: k > t_hib)
    t_hi32 = jnp.broadcast_to(t_hi, (qb, LANES))

    def low_tile(t, c):
        for s in range(ns):
            key = key_s[t, :, s * LANES:(s + 1) * LANES]
            low = (key & 0xFFFF) + INT16_MIN
            half_s[t, :, s * LANES:(s + 1) * LANES] = jnp.where((key >> 16) == t_hi32, low,
                                                                INT16_MIN).astype(jnp.int16)
        return c

    lax.fori_loop(0, n_kt, low_tile, 0)
    t_lo = kth_largest16(need_lo)
    thr = t_hi * 65536 + (t_lo - INT16_MIN)
    thrb = jnp.broadcast_to(thr, (qb, LANES))
    need = jnp.broadcast_to(n_sel - count(lambda k: k > thrb), (qb, LANES))

    def sel_tile(t, off):
        for s in range(ns):
            key = key_s[t, :, s * LANES:(s + 1) * LANES]
            eq = key == thrb
            pre = jnp.dot(jnp.where(eq, 1.0, 0.0).astype(BF16), tri_ref[...], preferred_element_type=F32)
            kpos = t * kt + s * LANES + lane
            keep = jnp.logical_or(key > thrb, jnp.logical_and(eq, pre[:, :LANES] + off < need))
            keep = jnp.logical_and(keep, kpos < lim)
            bias_s[t, :, s * LANES:(s + 1) * LANES] = jnp.where(keep, 0.0, MASKED)
            off = off + pre[:, LANES:]
        return off

    lax.fori_loop(0, n_kt, sel_tile, jnp.zeros((qb, LANES), F32))

    rep = ATT_HEADS // KV_HEADS
    qgs = []
    for g in range(KV_HEADS):
        qg = jnp.concatenate([q_ref[:, (g * rep + r) * HEAD_DIM:(g * rep + r + 1) * HEAD_DIM] for r in range(rep)],
                             axis=0)
        qgs.append((qg * (HEAD_DIM ** -0.5)).astype(BF16))

    def att_tile(t, carry):
        ks = pl.multiple_of(t * kt, kt)
        out = []
        for g in range(KV_HEADS):
            m, acc = carry[g]
            s = lax.dot_general(qgs[g], k_ref[0, g, pl.ds(ks, kt), :], (((1,), (1,)), ((), ())),
                                preferred_element_type=F32)
            s = (s.reshape(rep, qb, kt) + bias_s[t][None]).reshape(rep * qb, kt)
            m_new = jnp.maximum(m, jnp.max(s, axis=1, keepdims=True))
            p = jnp.exp(s - m_new)
            pv = jnp.dot(p.astype(BF16), vx_ref[0, pl.ds(ks, kt), g * LANES:(g + 1) * LANES],
                         preferred_element_type=F32)
            out.append((m_new, acc * jnp.exp(m - m_new) + pv))
        return tuple(out)

    init = (jnp.full((rep * qb, 1), MASKED, F32), jnp.zeros((rep * qb, LANES), F32))
    res = lax.fori_loop(0, n_kt, att_tile, (init,) * KV_HEADS)
    for g in range(KV_HEADS):
        acc = res[g][1]
        out = acc * pltpu.roll(1.0 / acc, HEAD_DIM, 1)
        for r in range(rep):
            h = g * rep + r
            o_ref[:, h * HEAD_DIM:(h + 1) * HEAD_DIM] = out[r * qb:(r + 1) * qb, :HEAD_DIM]


def _dsa(q, qi, kw, k_bf, v_bf, ki_bf, *, qb, kt, q0, l_valid, n_sel):
    b, lp = ki_bf.shape[:2]
    nq = q.shape[0] // (b * qb)
    k_g = jnp.moveaxis(k_bf, 2, 1)
    ones = jnp.ones((b, lp, KV_HEADS, HEAD_DIM), BF16)
    vx = jnp.concatenate([v_bf, ones], axis=-1).reshape(b, lp, KV_HEADS * LANES)
    tr = lax.broadcasted_iota(jnp.int32, (LANES, 2 * LANES), 0)
    tc = lax.broadcasted_iota(jnp.int32, (LANES, 2 * LANES), 1)
    tri = jnp.logical_or(tr < tc, tc >= LANES).astype(BF16)
    row = lambda w: pl.BlockSpec((qb, w), lambda bi, i: (bi * nq + i, 0))
    kern = functools.partial(_dsa_kernel, qb=qb, kt=kt, q0=q0, l_valid=l_valid, n_sel=n_sel)
    return pl.pallas_call(
        kern,
        grid=(b, nq),
        in_specs=[row(ATT_WIDTH), row(QI_WIDTH), row(KW_WIDTH),
                  pl.BlockSpec((1, KV_HEADS, lp, HEAD_DIM), lambda bi, i: (bi, 0, 0, 0)),
                  pl.BlockSpec((1, lp, KV_HEADS * LANES), lambda bi, i: (bi, 0, 0)),
                  pl.BlockSpec((1, lp, IDX_DIM), lambda bi, i: (bi, 0, 0)),
                  pl.BlockSpec((LANES, 2 * LANES), lambda bi, i: (0, 0))],
        out_specs=row(ATT_WIDTH),
        out_shape=jax.ShapeDtypeStruct((q.shape[0], ATT_WIDTH), F32),
        scratch_shapes=[pltpu.VMEM((lp // kt, qb, kt), jnp.int32), pltpu.VMEM((lp // kt, qb, kt), jnp.int16),
                        pltpu.VMEM((lp // kt, qb, kt), F32),
                        pltpu.VMEM((IDX_HEADS, qb, LANES), F32)],
        compiler_params=_cparams(("arbitrary", "arbitrary")),
        name="dsa",
    )(q, qi, kw, k_g, vx, ki_bf, tri)


def _split_bf16(x):
    hi = x.astype(BF16)
    return hi, (x - hi.astype(F32)).astype(BF16)


_NN = (((1,), (0,)), ((), ()))


def _dots(a_sp, b_sp, dims=_NN):
    (ah, al), (bh, bl) = a_sp, b_sp
    d = functools.partial(lax.dot_general, dimension_numbers=dims, preferred_element_type=F32)
    return d(ah, bh) + (d(ah, bl) + d(al, bh))


def _dot3(a, b, dims=_NN):
    return _dots(_split_bf16(a), _split_bf16(b), dims)


def _dot2(a, b_exact):
    ah, al = _split_bf16(a)
    return jnp.dot(ah, b_exact, preferred_element_type=F32) + jnp.dot(al, b_exact, preferred_element_type=F32)


_NT = (((1,), (1,)), ((), ()))
_TN = (((0,), (0,)), ((), ()))


def _rwkv_kernel(p_ref, prev_ref, s0_ref, mu_ref, w0_ref, w2_ref, a0_ref, a2_ref, g2_ref, kk_ref, ka_ref,
                 rk_ref, lnw_ref, lnb_ref, hsum_ref, tri_ref, o_ref, st_ref, s_s, prev_s, o_s, *, c):
    j = pl.program_id(1)

    @pl.when(j == 0)
    def _init():
        s_s[...] = s0_ref[0]
        prev_s[...] = prev_ref[0]

    p = p_ref[...]
    row = lax.broadcasted_iota(jnp.int32, (c, 1), 0)
    p_prev = jnp.where(row == 0, prev_s[...], pltpu.roll(p, 1, 0))
    prev_s[...] = p[c - 1:c, :]
    xs = p + (p_prev - p) * mu_ref[...]
    o1, o2, o3 = RWKV_WIDTH, 2 * RWKV_WIDTH, 3 * RWKV_WIDTH
    r, k, v = xs[:, :o1], xs[:, o1:o2], xs[:, o2:o3]
    dw = xs[:, o3:o3 + W_LORA]
    da = xs[:, o3 + W_LORA:o3 + W_LORA + A_LORA]
    dg = xs[:, o3 + W_LORA + A_LORA:]
    hsum = hsum_ref[...]
    w_log = -jax.nn.softplus(-(w0_ref[...] + _dot3(jnp.tanh(dw), w2_ref[...]))) - 0.5
    lw = -jnp.exp(w_log)
    a = jax.nn.sigmoid(a0_ref[...] + _dot3(da, a2_ref[...]))
    g = _dot3(jax.nn.sigmoid(dg), g2_ref[...])
    kk = k * kk_ref[...]
    kk = kk / jnp.maximum(jnp.sqrt(_dot2(kk * kk, hsum)), 1e-12)
    km = k * (1.0 + (a - 1.0) * ka_ref[...])
    bm = kk * a
    lw_hi, lw_lo = _split_bf16(lw)
    tri = tri_ref[...]
    cum = (jnp.dot(tri, lw_hi, preferred_element_type=F32)
           + jnp.dot(tri, lw_lo, preferred_element_type=F32))
    tot = cum[c - 1:c, :]
    e_in = jnp.exp(cum)
    e_out = jnp.exp(-cum)
    e_end = jnp.exp(tot - cum)
    a_t = -kk * jnp.exp(cum - lw)
    r_t = r * e_in
    b_t = bm * e_out
    k_t = km * e_out
    b_h = bm * e_end
    k_h = km * e_end
    gam = jnp.exp(tot)

    ri = lax.broadcasted_iota(jnp.int32, (c, c), 0)
    ci = lax.broadcasted_iota(jnp.int32, (c, c), 1)
    strict = ri > ci
    incl = ri >= ci
    eye = jnp.where(ri == ci, 1.0, 0.0)
    n_double = int(np.log2(c)) - 1
    heads = range(RWKV_HEADS)
    sls = [slice(h * HEAD_DIM, (h + 1) * HEAD_DIM) for h in heads]
    s0 = [s_s[h] for h in heads]
    s0_sp = [_split_bf16(s) for s in s0]
    ar_sp = [_split_bf16(jnp.concatenate([a_t[:, sl], r_t[:, sl]], axis=0)) for sl in sls]
    bk_sp = [_split_bf16(jnp.concatenate([b_t[:, sl], k_t[:, sl]], axis=0)) for sl in sls]
    v_sp = [_split_bf16(v[:, sl]) for sl in sls]
    m = [_dots(ar_sp[h], bk_sp[h], _NT) for h in heads]
    as0 = [_dots(ar_sp[h], s0_sp[h], _NT) for h in heads]
    x_sp = [_split_bf16(jnp.where(strict, m[h][:c, :c], 0.0)) for h in heads]
    lak_sp = [_split_bf16(jnp.where(strict, m[h][:c, c:], 0.0)) for h in heads]
    tinv = [eye + jnp.where(strict, m[h][:c, :c], 0.0) for h in heads]
    rhs = [as0[h][:c] + _dots(lak_sp[h], v_sp[h]) for h in heads]
    for _ in range(n_double):
        x_sp = [_split_bf16(_dots(x_sp[h], x_sp[h])) for h in heads]
        tinv = [tinv[h] + _dots(_split_bf16(tinv[h]), x_sp[h]) for h in heads]
    u = [_dot3(tinv[h], rhs[h]) for h in heads]
    u_sp = [_split_bf16(uh) for uh in u]
    for h in heads:
        m_rb = jnp.where(incl, m[h][c:, :c], 0.0)
        m_rk = jnp.where(incl, m[h][c:, c:], 0.0)
        o_s[:, sls[h]] = as0[h][c:] + _dots(_split_bf16(m_rb), u_sp[h]) + _dots(_split_bf16(m_rk), v_sp[h])
    s_new = [s0[h] * gam[:, sls[h]] + _dots(u_sp[h], _split_bf16(b_h[:, sls[h]]), _TN)
             + _dots(v_sp[h], _split_bf16(k_h[:, sls[h]]), _TN) for h in heads]
    for h in heads:
        s_s[h] = s_new[h]

    o = o_s[...]
    inv_d = 1.0 / HEAD_DIM
    mean = _dot2(o, hsum) * inv_d
    cen = o - mean
    var = _dot2(cen * cen, hsum) * inv_d
    on = cen * lax.rsqrt(var + GN_EPS) * lnw_ref[...] + lnb_ref[...]
    bonus = _dot2(r * km * rk_ref[...], hsum) * v
    o_ref[...] = (on + bonus) * g

    @pl.when(j == pl.num_programs(1) - 1)
    def _fin():
        st_ref[0] = s_s[...]


def _rwkv(p, prev, s0, rw, c):
    mu, w0, w2, a0, a2, g2, k_k, k_a, r_k, lnx_w, lnx_b = rw
    b = prev.shape[0]
    nc = p.shape[0] // (b * c)
    hid = lax.broadcasted_iota(jnp.int32, (RWKV_WIDTH, RWKV_WIDTH), 0) // HEAD_DIM
    hsum = (hid == hid.T).astype(BF16)
    tri = (lax.broadcasted_iota(jnp.int32, (c, c), 0) >= lax.broadcasted_iota(jnp.int32, (c, c), 1)).astype(BF16)
    vec = lambda a: a.reshape(1, -1)
    full = lambda a: pl.BlockSpec(a.shape, lambda bi, j: (0,) * a.ndim)
    small = [vec(mu), vec(w0), w2, vec(a0), a2, g2, vec(k_k), vec(k_a), vec(r_k), vec(lnx_w), vec(lnx_b), hsum, tri]
    return pl.pallas_call(
        functools.partial(_rwkv_kernel, c=c),
        grid=(b, nc),
        in_specs=[pl.BlockSpec((c, RWKV_COLS), lambda bi, j: (bi * nc + j, 0)),
                  pl.BlockSpec((1, 1, RWKV_COLS), lambda bi, j: (bi, 0, 0)),
                  pl.BlockSpec((1, RWKV_HEADS, HEAD_DIM, HEAD_DIM), lambda bi, j: (bi, 0, 0, 0))]
                 + [full(a) for a in small],
        out_specs=[pl.BlockSpec((c, RWKV_WIDTH), lambda bi, j: (bi * nc + j, 0)),
                   pl.BlockSpec((1, RWKV_HEADS, HEAD_DIM, HEAD_DIM), lambda bi, j: (bi, 0, 0, 0))],
        out_shape=[jax.ShapeDtypeStruct((p.shape[0], RWKV_WIDTH), F32),
                   jax.ShapeDtypeStruct((b, RWKV_HEADS, HEAD_DIM, HEAD_DIM), F32)],
        scratch_shapes=[pltpu.VMEM((RWKV_HEADS, HEAD_DIM, HEAD_DIM), F32), pltpu.VMEM((1, RWKV_COLS), F32),
                        pltpu.VMEM((c, RWKV_WIDTH), F32)],
        compiler_params=_cparams(("arbitrary", "arbitrary")),
        name="rwkv",
    )(p, prev, s0, *small)


def _branch(x, mods, pos, prev, s0, caches, weights, tm, tm_moe):
    (g_mix, g_ffn, g_final, w_r, w_a, rw, w_out, rwt, rb, wup, bup, wdn, bdn) = weights
    sh1, sc1, gt1, sh2, sc2, gt2 = mods
    b, t, d = x.shape
    tab, tabk = _rope_tables(pos)
    p_r, q, k, v, qi, kw = _inproj(x, g_mix, sh1, sc1, w_r, w_a, tab, tabk, tm)
    o_r, wkv = _rwkv(p_r, prev, s0, rw, min(CHUNK, t))
    shift = p_r.reshape(b, t, RWKV_COLS)[:, -1:]
    k4 = k.reshape(b, t, KV_HEADS, HEAD_DIM)
    v4 = v.reshape(b, t, KV_HEADS, HEAD_DIM)
    ki = kw.reshape(b, t, KW_WIDTH)[..., :IDX_DIM]
    kt = 512
    if caches is None:
        k_all, v_all, ki_all, q0, l_valid = k4, v4, ki, 0, t
    else:
        ck, cv, cki = caches
        q0 = ck.shape[1]
        l_valid = q0 + t
        padn = (-l_valid) % kt
        zpad = lambda a: jnp.concatenate([a, jnp.zeros((b, padn) + a.shape[2:], a.dtype)], axis=1)
        k_all = zpad(jnp.concatenate([ck, k4], axis=1))
        v_all = zpad(jnp.concatenate([cv, v4], axis=1))
        ki_all = zpad(jnp.concatenate([cki, ki], axis=1))
    o_a = _dsa(q, qi, kw, k_all.astype(BF16), v_all.astype(BF16), ki_all.astype(BF16),
               qb=min(Q_BLOCK, t), kt=kt, q0=q0, l_valid=l_valid, n_sel=min(TOPK_MAX, l_valid // 4))
    x1, h2 = _outproj(o_r, o_a, x, w_out, gt1, g_ffn, sh2, sc2, tm)
    y_moe = _moe(h2, rwt, rb, wup, bup, wdn, bdn, tm_moe)
    y = _final(x1, y_moe, gt2, g_final, b, t, tm)
    return y, k4, v4, ki, wkv, shift


def kernel(x_prompt, x_sample, c_prompt, c_sample, cache_k, cache_v, cache_kidx, state_wkv, state_shift,
           w_ada, b_ada, g_mix, g_ffn, g_final, w_in, mu_shift, w0, w_lora_w, a0, w_lora_a, w_lora_g,
           k_k, k_a, r_k, lnx_w, lnx_b, w_out, router_w, router_b, w_up, b_up, w_down, b_down):
    B, T, D = x_prompt.shape
    DB, DT, _ = x_sample.shape
    P = cache_k.shape[2]
    l = 0
    assert w_ada.shape[0] == 1

    rows = B + DB
    pad = (-rows) % 8
    c_all = jnp.concatenate([c_prompt, c_sample, jnp.zeros((pad, D), F32)], axis=0)
    m = _ada(c_all, w_ada[l], b_ada[l])
    mods_p = tuple(t.reshape(B, 1, D) for t in jnp.split(m[:B], 6, axis=-1))
    mods_s = tuple(t.reshape(DB, 1, D) for t in jnp.split(m[B:rows], 6, axis=-1))

    w_r = w_in[l][:, :RWKV_COLS].astype(BF16)
    w_att = w_in[l][:, RWKV_COLS:]
    att_pad = KW_WIDTH - IDX_DIM - IDX_HEADS
    w_a = jnp.concatenate([w_att, jnp.zeros((D, att_pad), F32)], axis=1).astype(BF16)
    rw = (mu_shift[l], w0[l], w_lora_w[l], a0[l], w_lora_a[l], w_lora_g[l], k_k[l], k_a[l], r_k[l],
          lnx_w[l], lnx_b[l])
    wup = _deinterleave_cast(w_up[l])
    bup = jnp.concatenate([b_up[l][..., 0::2], b_up[l][..., 1::2]], axis=-1).reshape(N_EXPERTS, 1, 2 * D_FF)
    wdn = w_down[l].astype(BF16)
    bdn = b_down[l].reshape(N_EXPERTS, 1, D)
    weights = (g_mix[l], g_ffn[l], g_final, w_r, w_a, rw, w_out[l].astype(BF16), router_w[l].T,
               router_b[l].reshape(N_EXPERTS, 1), wup, bup, wdn, bdn)

    pos_p = jnp.arange(T, dtype=jnp.int32)
    pos_s = P + jnp.arange(DT, dtype=jnp.int32)
    yp, kp, vp, kip, wkvp, shp = _branch(
        x_prompt, mods_p, pos_p, jnp.zeros((B, 1, RWKV_COLS), F32),
        jnp.zeros((B, RWKV_HEADS, HEAD_DIM, HEAD_DIM), F32), None, weights, 512, MOE_TILE)
    ys, ks, vs, kis, wkvs, shs = _branch(
        x_sample, mods_s, pos_s, state_shift[l], state_wkv[l],
        (cache_k[l], cache_v[l], cache_kidx[l]), weights, DT, DB * DT)
    return (yp, ys, kp[None], vp[None], kip[None], wkvp[None], shp[None],
            ks[None], vs[None], kis[None], wkvs[None], shs[None])
```

```python
import functools

import numpy as np
import jax
import jax.numpy as jnp
from jax import lax
from jax.experimental import pallas as pl
from jax.experimental.pallas import tpu as pltpu

F32 = jnp.float32
BF16 = jnp.bfloat16
HIGHEST = lax.Precision.HIGHEST

D_MODEL = 1024
CHUNK = 64
HEAD_DIM = 64
RWKV_HEADS = 8
RWKV_WIDTH = RWKV_HEADS * HEAD_DIM
W_LORA = 64
A_LORA = 64
G_LORA = 128
RWKV_COLS = 3 * RWKV_WIDTH + W_LORA + A_LORA + G_LORA
GN_EPS = 64e-5
ATT_HEADS = 8
ATT_WIDTH = ATT_HEADS * HEAD_DIM
KV_HEADS = 2
KV_WIDTH = KV_HEADS * HEAD_DIM
IDX_HEADS = 4
IDX_DIM = 64
TOPK_MAX = 256
Q_BLOCK = 128
ROPE_THETA = 500000.0
ROT_DIM = HEAD_DIM // 4
N_EXPERTS = 32
TOP_K = 4
D_FF = 1024
SWIGLU_ALPHA = 1.702
SWIGLU_LIMIT = 7.0
NORM_EPS = 1e-5

LANES = 128
MOE_ROWS = 128
MOE_SUB = 896
MOE_NSUB = 2
VMEM_LIMIT = 56 * 1024 * 1024

QI_WIDTH = IDX_HEADS * IDX_DIM
KW_WIDTH = LANES


def _cparams(sem):
    return pltpu.CompilerParams(dimension_semantics=sem, vmem_limit_bytes=VMEM_LIMIT)


def _ada_kernel(c_ref, w_ref, b_ref, o_ref):
    c = c_ref[...]
    s = c * jax.nn.sigmoid(c)
    o_ref[...] = jnp.dot(s, w_ref[...], precision=HIGHEST, preferred_element_type=F32) + b_ref[...]


def _ada(c, w, b):
    rows, d = c.shape
    n = w.shape[1]
    tn = 1536
    return pl.pallas_call(
        _ada_kernel,
        grid=(n // tn,),
        in_specs=[pl.BlockSpec((rows, d), lambda j: (0, 0)),
                  pl.BlockSpec((d, tn), lambda j: (0, j)),
                  pl.BlockSpec((1, tn), lambda j: (0, j))],
        out_specs=pl.BlockSpec((rows, tn), lambda j: (0, j)),
        out_shape=jax.ShapeDtypeStruct((rows, n), F32),
        compiler_params=_cparams(("arbitrary",)),
        name="ada",
    )(c, w, b.reshape(1, n))


def _rope_slab(y, tab_ref):
    return (y * tab_ref[0] + pltpu.roll(y, LANES - ROT_DIM // 2, 1) * tab_ref[1]
            + pltpu.roll(y, ROT_DIM // 2, 1) * tab_ref[2])


def _norm_mod(x, g, sh, sc):
    var = jnp.mean(x * x, axis=-1, keepdims=True)
    return (x * lax.rsqrt(var + NORM_EPS) * g) * (1.0 + sc) + sh


def _inproj_kernel(x_ref, g_ref, sh_ref, sc_ref, wr_ref, wa_ref, tab_ref, tabk_ref,
                   rw_ref, q_ref, k_ref, v_ref, qi_ref, kw_ref):
    h = _norm_mod(x_ref[...], g_ref[...], sh_ref[0], sc_ref[0]).astype(BF16)
    rw_ref[...] = jnp.dot(h, wr_ref[...], preferred_element_type=F32)
    pa = jnp.dot(h, wa_ref[...], preferred_element_type=F32)
    off = 0
    for ref, width, rot in ((q_ref, ATT_WIDTH, True), (k_ref, KV_WIDTH, True), (v_ref, KV_WIDTH, False),
                            (qi_ref, QI_WIDTH, True)):
        for s in range(width // LANES):
            slab = pa[:, off + s * LANES: off + (s + 1) * LANES]
            ref[:, s * LANES:(s + 1) * LANES] = _rope_slab(slab, tab_ref) if rot else slab
        off += width
    kw_ref[...] = _rope_slab(pa[:, off:off + LANES], tabk_ref)


def _rope_tables(pos):
    half = ROT_DIM // 2
    inv = ROPE_THETA ** (-jnp.arange(0, ROT_DIM, 2, dtype=F32) / ROT_DIM)
    ang = pos.astype(F32)[:, None] * inv[None, :]
    cos, sin = jnp.cos(ang), jnp.sin(ang)
    t = pos.shape[0]
    one = jnp.ones((t, HEAD_DIM - ROT_DIM), F32)
    zero_r = jnp.zeros((t, HEAD_DIM - ROT_DIM), F32)
    zero_h = jnp.zeros((t, half), F32)
    c_head = jnp.concatenate([cos, cos, one], axis=1)
    up_head = jnp.concatenate([-sin, zero_h, zero_r], axis=1)
    dn_head = jnp.concatenate([zero_h, sin, zero_r], axis=1)
    tab = jnp.stack([jnp.tile(c_head, (1, 2)), jnp.tile(up_head, (1, 2)), jnp.tile(dn_head, (1, 2))])
    wscale = jnp.concatenate([jnp.full((t, IDX_HEADS), IDX_HEADS ** -0.5, F32),
                              jnp.ones((t, HEAD_DIM - IDX_HEADS), F32)], axis=1)
    zero64 = jnp.zeros((t, HEAD_DIM), F32)
    tabk = jnp.stack([jnp.concatenate([c_head, wscale], axis=1),
                      jnp.concatenate([up_head, zero64], axis=1),
                      jnp.concatenate([dn_head, zero64], axis=1)])
    return tab, tabk


def _inproj(x, g, sh, sc, w_r, w_a, tab, tabk, tm):
    b, t, d = x.shape
    nt = t // tm
    n = b * t
    x2 = x.reshape(n, d)
    widths = (RWKV_COLS, ATT_WIDTH, KV_WIDTH, KV_WIDTH, QI_WIDTH, KW_WIDTH)
    row = lambda w: pl.BlockSpec((tm, w), lambda i: (i, 0))
    mod = pl.BlockSpec((1, 1, d), lambda i: (i // nt, 0, 0))
    tabspec = pl.BlockSpec((3, tm, LANES), lambda i: (0, i % nt, 0))
    return pl.pallas_call(
        _inproj_kernel,
        grid=(n // tm,),
        in_specs=[row(d), pl.BlockSpec((1, d), lambda i: (0, 0)), mod, mod,
                  pl.BlockSpec(w_r.shape, lambda i: (0, 0)), pl.BlockSpec(w_a.shape, lambda i: (0, 0)),
                  tabspec, tabspec],
        out_specs=[row(w) for w in widths],
        out_shape=[jax.ShapeDtypeStruct((n, w), F32) for w in widths],
        compiler_params=_cparams(("arbitrary",)),
        name="inproj",
    )(x2, g.reshape(1, d), sh, sc, w_r, w_a, tab, tabk)


def _outproj_kernel(or_ref, oa_ref, x_ref, w_ref, gt_ref, g_ref, sh_ref, sc_ref, x1_ref, h2_ref):
    m = jnp.dot(or_ref[...].astype(BF16), w_ref[:RWKV_WIDTH, :], preferred_element_type=F32)
    m = m + jnp.dot(oa_ref[...].astype(BF16), w_ref[RWKV_WIDTH:, :], preferred_element_type=F32)
    x1 = x_ref[...] + gt_ref[0] * m
    x1_ref[...] = x1
    h2_ref[...] = _norm_mod(x1, g_ref[...], sh_ref[0], sc_ref[0])


def _outproj(o_r, o_a, x, w_out, gt, g, sh, sc, tm):
    b, t, d = x.shape
    nt = t // tm
    n = b * t
    row = lambda w: pl.BlockSpec((tm, w), lambda i: (i, 0))
    mod = pl.BlockSpec((1, 1, d), lambda i: (i // nt, 0, 0))
    return pl.pallas_call(
        _outproj_kernel,
        grid=(n // tm,),
        in_specs=[row(RWKV_WIDTH), row(ATT_WIDTH), row(d), pl.BlockSpec(w_out.shape, lambda i: (0, 0)),
                  mod, pl.BlockSpec((1, d), lambda i: (0, 0)), mod, mod],
        out_specs=[row(d), row(d)],
        out_shape=[jax.ShapeDtypeStruct((n, d), F32)] * 2,
        compiler_params=_cparams(("arbitrary",)),
        name="outproj",
    )(o_r.reshape(n, RWKV_WIDTH), o_a.reshape(n, ATT_WIDTH), x.reshape(n, d), w_out, gt, g.reshape(1, d), sh, sc)


def _moe_kernel(h_ref, rwt_ref, rb_ref, tri_ref, wup_ref, bup_ref, wdn_ref, bdn_ref, y_ref,
                hb_s, rank_s, gate_s, *, n_tokens, sub):
    e = pl.program_id(1)
    n_sub = h_ref.shape[0] // sub

    @pl.when(e == 0)
    def _route():
        rwt = rwt_ref[...].astype(BF16)
        for s in range(n_sub):
            first_row = pl.program_id(0) * (n_sub * sub) + s * sub
            in_rows = first_row + lax.broadcasted_iota(jnp.int32, (sub, 1), 0) < n_tokens
            in_cols = first_row + lax.broadcasted_iota(jnp.int32, (1, sub), 1) < n_tokens
            hb = jnp.where(in_rows, h_ref[s * sub:(s + 1) * sub, :], 0.0).astype(BF16)
            hb_s[s * sub:(s + 1) * sub, :] = hb
            logits = lax.dot_general(rwt, hb, (((1,), (1,)), ((), ())), preferred_element_type=F32) + rb_ref[...]
            eidx = lax.broadcasted_iota(jnp.int32, logits.shape, 0)
            work = logits
            top = None
            for _ in range(TOP_K):
                m = jnp.max(work, axis=0, keepdims=True)
                if top is None:
                    top = m
                first = jnp.min(jnp.where(work == m, eidx, N_EXPERTS), axis=0, keepdims=True)
                work = jnp.where(eidx == first, -jnp.inf, work)
            ex = jnp.where(work != logits, jnp.exp(logits - top), 0.0)
            sel = jnp.logical_and(work != logits, in_cols)
            gate_s[s] = ex / jnp.sum(ex, axis=0, keepdims=True)
            before = jnp.dot(jnp.where(sel, 1.0, 0.0).astype(BF16), tri_ref[...], preferred_element_type=F32)
            rank_s[s] = jnp.where(sel, before, -1.0)
        y_ref[...] = jnp.zeros(y_ref.shape, F32)

    for s in range(n_sub):
        r_row = rank_s[s, pl.ds(e, 1), :]
        g_row = gate_s[s, pl.ds(e, 1), :]
        count = jnp.sum(jnp.where(r_row >= 0.0, 1, 0).astype(jnp.int32))
        n_blocks = (count + MOE_ROWS - 1) // MOE_ROWS

        def block(j, carry, s=s, r_row=r_row, g_row=g_row):
            rows = (lax.broadcasted_iota(jnp.int32, (MOE_ROWS, sub), 0) + j * MOE_ROWS).astype(F32)
            hit = r_row == rows
            p = jnp.where(hit, 1.0, 0.0).astype(BF16)
            xe = jnp.dot(p, hb_s[s * sub:(s + 1) * sub, :], preferred_element_type=F32).astype(BF16)
            u = jnp.dot(xe, wup_ref[0], preferred_element_type=F32) + bup_ref[0]
            glu = jnp.minimum(u[:, :D_FF], SWIGLU_LIMIT)
            lin = jnp.clip(u[:, D_FF:], -SWIGLU_LIMIT, SWIGLU_LIMIT)
            act = glu * jax.nn.sigmoid(SWIGLU_ALPHA * glu) * (lin + 1.0)
            yb = jnp.dot(act.astype(BF16), wdn_ref[0], preferred_element_type=F32) + bdn_ref[0]
            g_rows = jnp.sum(jnp.where(hit, g_row, 0.0), axis=1, keepdims=True)
            ys = (yb * g_rows).astype(BF16)
            y_ref[s * sub:(s + 1) * sub, :] += lax.dot_general(p, ys, (((0,), (0,)), ((), ())),
                                                               preferred_element_type=F32)
            return carry

        lax.fori_loop(0, n_blocks, block, 0)


def _moe(h, rwt, rb, wup, bup, wdn, bdn, sub, n_sub):
    n, d = h.shape
    tm = sub * n_sub
    tri = (lax.broadcasted_iota(jnp.int32, (sub, sub), 0) < lax.broadcasted_iota(jnp.int32, (sub, sub), 1)).astype(BF16)
    return pl.pallas_call(
        functools.partial(_moe_kernel, n_tokens=n, sub=sub),
        grid=(pl.cdiv(n, tm), N_EXPERTS),
        in_specs=[pl.BlockSpec((tm, d), lambda i, e: (i, 0)),
                  pl.BlockSpec((N_EXPERTS, d), lambda i, e: (0, 0)),
                  pl.BlockSpec((N_EXPERTS, 1), lambda i, e: (0, 0)),
                  pl.BlockSpec((sub, sub), lambda i, e: (0, 0)),
                  pl.BlockSpec((1, d, 2 * D_FF), lambda i, e: (e, 0, 0)),
                  pl.BlockSpec((1, 1, 2 * D_FF), lambda i, e: (e, 0, 0)),
                  pl.BlockSpec((1, D_FF, d), lambda i, e: (e, 0, 0)),
                  pl.BlockSpec((1, 1, d), lambda i, e: (e, 0, 0))],
        out_specs=pl.BlockSpec((tm, d), lambda i, e: (i, 0)),
        out_shape=jax.ShapeDtypeStruct((n, d), F32),
        scratch_shapes=[pltpu.VMEM((tm, d), BF16), pltpu.VMEM((n_sub, N_EXPERTS, sub), F32),
                        pltpu.VMEM((n_sub, N_EXPERTS, sub), F32)],
        compiler_params=_cparams(("arbitrary", "arbitrary")),
        name="moe",
    )(h, rwt, rb, tri, wup, bup, wdn, bdn)


def _deinterleave_kernel(w_ref, perm_ref, o_ref):
    o_ref[0] = jnp.dot(w_ref[0].astype(BF16), perm_ref[...], preferred_element_type=F32).astype(BF16)


def _deinterleave_cast(w):
    e, d, n = w.shape
    tk = 512
    src = lax.broadcasted_iota(jnp.int32, (n, n), 0)
    dst = lax.broadcasted_iota(jnp.int32, (n, n), 1)
    perm = (src == jnp.where(dst < n // 2, 2 * dst, 2 * (dst - n // 2) + 1)).astype(BF16)
    return pl.pallas_call(
        _deinterleave_kernel,
        grid=(e, d // tk),
        in_specs=[pl.BlockSpec((1, tk, n), lambda i, j: (i, j, 0)), pl.BlockSpec((n, n), lambda i, j: (0, 0))],
        out_specs=pl.BlockSpec((1, tk, n), lambda i, j: (i, j, 0)),
        out_shape=jax.ShapeDtypeStruct((e, d, n), BF16),
        compiler_params=_cparams(("arbitrary", "arbitrary")),
        name="deinterleave",
    )(w, perm)


def _final_kernel(x1_ref, y_ref, gt_ref, g_ref, o_ref):
    x = x1_ref[...] + gt_ref[0] * y_ref[...]
    var = jnp.mean(x * x, axis=-1, keepdims=True)
    o_ref[...] = x * lax.rsqrt(var + NORM_EPS) * g_ref[...]


def _final(x1, y, gt, g, b, t, tm):
    n, d = x1.shape
    nt = t // tm
    row = pl.BlockSpec((tm, d), lambda i: (i, 0))
    return pl.pallas_call(
        _final_kernel,
        grid=(n // tm,),
        in_specs=[row, row, pl.BlockSpec((1, 1, d), lambda i: (i // nt, 0, 0)),
                  pl.BlockSpec((1, d), lambda i: (0, 0))],
        out_specs=row,
        out_shape=jax.ShapeDtypeStruct((n, d), F32),
        compiler_params=_cparams(("arbitrary",)),
        name="final",
    )(x1, y, gt, g.reshape(1, d)).reshape(b, t, d)


MASKED = -1e30
INT_MIN = -2 ** 31


def _dsa_kernel(q_ref, qi_ref, kw_ref, k_ref, vx_ref, ki_ref, tri_ref, o_ref, key_s, bias_s, wib_s,
                *, qb, kt, q0, l_valid, n_sel):
    i = pl.program_id(1)
    ns = kt // LANES
    row = lax.broadcasted_iota(jnp.int32, (qb, 1), 0)
    qpos = q0 + i * qb + row
    lim = jnp.minimum((qpos // CHUNK + 1) * CHUNK, l_valid)
    last_lim = jnp.minimum(((q0 + i * qb + qb - 1) // CHUNK + 1) * CHUNK, l_valid)
    n_kt = (last_lim + kt - 1) // kt
    lane = lax.broadcasted_iota(jnp.int32, (1, LANES), 1)

    for h in range(IDX_HEADS):
        w = kw_ref[:, IDX_DIM + h:IDX_DIM + h + 1] * (IDX_DIM ** -0.5)
        wib_s[h] = jnp.broadcast_to(w, (qb, LANES))
    qi = qi_ref[...].astype(BF16)

    def f32_key(x):
        bits = pltpu.bitcast(x, jnp.int32)
        return bits ^ ((bits >> 31) & 0x7FFFFFFF)

    def score_tile(t, c):
        ks = pl.multiple_of(t * kt, kt)
        kit = ki_ref[0, pl.ds(ks, kt), :]
        lg = [lax.dot_general(qi[:, h * IDX_DIM:(h + 1) * IDX_DIM], kit, (((1,), (1,)), ((), ())),
                              preferred_element_type=F32) for h in range(IDX_HEADS)]
        for s in range(ns):
            sc = jnp.zeros((qb, LANES), F32)
            for h in range(IDX_HEADS):
                sc = sc + wib_s[h] * jnp.maximum(lg[h][:, s * LANES:(s + 1) * LANES], 0.0)
            kpos = ks + s * LANES + lane
            sc = jnp.where(kpos < lim, sc + 0.0, -jnp.inf)
            key_s[t, :, s * LANES:(s + 1) * LANES] = f32_key(sc)
        return c

    lax.fori_loop(0, n_kt, score_tile, 0)

    def count(pred_fn):
        def tile(t, acc):
            for s in range(ns):
                acc = acc + jnp.where(pred_fn(key_s[t, :, s * LANES:(s + 1) * LANES]), 1.0, 0.0)
            return acc
        acc = lax.fori_loop(0, n_kt, tile, jnp.zeros((qb, LANES), F32))
        return jnp.sum(acc, axis=1, keepdims=True)

    def bit_step(b, lo):
        cand = lo + jnp.left_shift(jnp.int32(1), 31 - b)
        candb = jnp.broadcast_to(cand, (qb, LANES))
        return jnp.where(count(lambda k: k >= candb) >= n_sel, cand, lo)

    thr = lax.fori_loop(0, 32, bit_step, jnp.full((qb, 1), INT_MIN, jnp.int32))
    thrb = jnp.broadcast_to(thr, (qb, LANES))
    need = jnp.broadcast_to(n_sel - count(lambda k: k > thrb), (qb, LANES))

    def sel_tile(t, off):
        for s in range(ns):
            key = key_s[t, :, s * LANES:(s + 1) * LANES]
            eq = key == thrb
            pre = jnp.dot(jnp.where(eq, 1.0, 0.0).astype(BF16), tri_ref[...], preferred_element_type=F32)
            kpos = t * kt + s * LANES + lane
            keep = jnp.logical_or(key > thrb, jnp.logical_and(eq, pre[:, :LANES] + off < need))
            keep = jnp.logical_and(keep, kpos < lim)
            bias_s[t, :, s * LANES:(s + 1) * LANES] = jnp.where(keep, 0.0, MASKED)
            off = off + pre[:, LANES:]
        return off

    lax.fori_loop(0, n_kt, sel_tile, jnp.zeros((qb, LANES), F32))

    rep = ATT_HEADS // KV_HEADS
    qgs = []
    for g in range(KV_HEADS):
        qg = jnp.concatenate([q_ref[:, (g * rep + r) * HEAD_DIM:(g * rep + r + 1) * HEAD_DIM] for r in range(rep)],
                             axis=0)
        qgs.append((qg * (HEAD_DIM ** -0.5)).astype(BF16))

    def att_tile(t, carry):
        ks = pl.multiple_of(t * kt, kt)
        out = []
        for g in range(KV_HEADS):
            m, acc = carry[g]
            s = lax.dot_general(qgs[g], k_ref[0, g, pl.ds(ks, kt), :], (((1,), (1,)), ((), ())),
                                preferred_element_type=F32)
            s = (s.reshape(rep, qb, kt) + bias_s[t][None]).reshape(rep * qb, kt)
            m_new = jnp.maximum(m, jnp.max(s, axis=1, keepdims=True))
            p = jnp.exp(s - m_new)
            pv = jnp.dot(p.astype(BF16), vx_ref[0, pl.ds(ks, kt), g * LANES:(g + 1) * LANES],
                         preferred_element_type=F32)
            out.append((m_new, acc * jnp.exp(m - m_new) + pv))
        return tuple(out)

    init = (jnp.full((rep * qb, 1), MASKED, F32), jnp.zeros((rep * qb, LANES), F32))
    res = lax.fori_loop(0, n_kt, att_tile, (init,) * KV_HEADS)
    for g in range(KV_HEADS):
        acc = res[g][1]
        out = acc * pltpu.roll(1.0 / acc, HEAD_DIM, 1)
        for r in range(rep):
            h = g * rep + r
            o_ref[:, h * HEAD_DIM:(h + 1) * HEAD_DIM] = out[r * qb:(r + 1) * qb, :HEAD_DIM]


def _dsa(q, qi, kw, k_bf, v_bf, ki_bf, *, qb, kt, q0, l_valid, n_sel):
    b, lp = ki_bf.shape[:2]
    nq = q.shape[0] // (b * qb)
    k_g = jnp.moveaxis(k_bf, 2, 1)
    ones = jnp.ones((b, lp, KV_HEADS, HEAD_DIM), BF16)
    vx = jnp.concatenate([v_bf, ones], axis=-1).reshape(b, lp, KV_HEADS * LANES)
    tr = lax.broadcasted_iota(jnp.int32, (LANES, 2 * LANES), 0)
    tc = lax.broadcasted_iota(jnp.int32, (LANES, 2 * LANES), 1)
    tri = jnp.logical_or(tr < tc, tc >= LANES).astype(BF16)
    row = lambda w: pl.BlockSpec((qb, w), lambda bi, i: (bi * nq + i, 0))
    kern = functools.partial(_dsa_kernel, qb=qb, kt=kt, q0=q0, l_valid=l_valid, n_sel=n_sel)
    return pl.pallas_call(
        kern,
        grid=(b, nq),
        in_specs=[row(ATT_WIDTH), row(QI_WIDTH), row(KW_WIDTH),
                  pl.BlockSpec((1, KV_HEADS, lp, HEAD_DIM), lambda bi, i: (bi, 0, 0, 0)),
                  pl.BlockSpec((1, lp, KV_HEADS * LANES), lambda bi, i: (bi, 0, 0)),
                  pl.BlockSpec((1, lp, IDX_DIM), lambda bi, i: (bi, 0, 0)),
                  pl.BlockSpec((LANES, 2 * LANES), lambda bi, i: (0, 0))],
        out_specs=row(ATT_WIDTH),
        out_shape=jax.ShapeDtypeStruct((q.shape[0], ATT_WIDTH), F32),
        scratch_shapes=[pltpu.VMEM((lp // kt, qb, kt), jnp.int32), pltpu.VMEM((lp // kt, qb, kt), F32),
                        pltpu.VMEM((IDX_HEADS, qb, LANES), F32)],
        compiler_params=_cparams(("arbitrary", "arbitrary")),
        name="dsa",
    )(q, qi, kw, k_g, vx, ki_bf, tri)


def _split_bf16(x):
    hi = x.astype(BF16)
    return hi, (x - hi.astype(F32)).astype(BF16)


_NN = (((1,), (0,)), ((), ()))


def _dots(a_sp, b_sp, dims=_NN):
    (ah, al), (bh, bl) = a_sp, b_sp
    d = functools.partial(lax.dot_general, dimension_numbers=dims, preferred_element_type=F32)
    return d(ah, bh) + (d(ah, bl) + d(al, bh))


def _dot3(a, b, dims=_NN):
    return _dots(_split_bf16(a), _split_bf16(b), dims)


def _dot2(a, b_exact):
    ah, al = _split_bf16(a)
    return jnp.dot(ah, b_exact, preferred_element_type=F32) + jnp.dot(al, b_exact, preferred_element_type=F32)


_NT = (((1,), (1,)), ((), ()))
_TN = (((0,), (0,)), ((), ()))


def _rwkv_kernel(p_ref, prev_ref, s0_ref, mu_ref, w0_ref, w2_ref, a0_ref, a2_ref, g2_ref, kk_ref, ka_ref,
                 rk_ref, lnw_ref, lnb_ref, hsum_ref, tri_ref, o_ref, st_ref, s_s, prev_s, o_s, *, c):
    j = pl.program_id(1)

    @pl.when(j == 0)
    def _init():
        s_s[...] = s0_ref[0]
        prev_s[...] = prev_ref[0]

    p = p_ref[...]
    row = lax.broadcasted_iota(jnp.int32, (c, 1), 0)
    p_prev = jnp.where(row == 0, prev_s[...], pltpu.roll(p, 1, 0))
    prev_s[...] = p[c - 1:c, :]
    xs = p + (p_prev - p) * mu_ref[...]
    o1, o2, o3 = RWKV_WIDTH, 2 * RWKV_WIDTH, 3 * RWKV_WIDTH
    r, k, v = xs[:, :o1], xs[:, o1:o2], xs[:, o2:o3]
    dw = xs[:, o3:o3 + W_LORA]
    da = xs[:, o3 + W_LORA:o3 + W_LORA + A_LORA]
    dg = xs[:, o3 + W_LORA + A_LORA:]
    hsum = hsum_ref[...]
    w_log = -jax.nn.softplus(-(w0_ref[...] + _dot3(jnp.tanh(dw), w2_ref[...]))) - 0.5
    lw = -jnp.exp(w_log)
    a = jax.nn.sigmoid(a0_ref[...] + _dot3(da, a2_ref[...]))
    g = _dot3(jax.nn.sigmoid(dg), g2_ref[...])
    kk = k * kk_ref[...]
    kk = kk / jnp.maximum(jnp.sqrt(_dot2(kk * kk, hsum)), 1e-12)
    km = k * (1.0 + (a - 1.0) * ka_ref[...])
    bm = kk * a
    lw_hi, lw_lo = _split_bf16(lw)
    tri = tri_ref[...]
    cum = (jnp.dot(tri, lw_hi, preferred_element_type=F32)
           + jnp.dot(tri, lw_lo, preferred_element_type=F32))
    tot = cum[c - 1:c, :]
    e_in = jnp.exp(cum)
    e_out = jnp.exp(-cum)
    e_end = jnp.exp(tot - cum)
    a_t = -kk * jnp.exp(cum - lw)
    r_t = r * e_in
    b_t = bm * e_out
    k_t = km * e_out
    b_h = bm * e_end
    k_h = km * e_end
    gam = jnp.exp(tot)

    ri = lax.broadcasted_iota(jnp.int32, (c, c), 0)
    ci = lax.broadcasted_iota(jnp.int32, (c, c), 1)
    strict = ri > ci
    incl = ri >= ci
    eye = jnp.where(ri == ci, 1.0, 0.0)
    n_double = int(np.log2(c)) - 1
    heads = range(RWKV_HEADS)
    sls = [slice(h * HEAD_DIM, (h + 1) * HEAD_DIM) for h in heads]
    s0 = [s_s[h] for h in heads]
    s0_sp = [_split_bf16(s) for s in s0]
    ar_sp = [_split_bf16(jnp.concatenate([a_t[:, sl], r_t[:, sl]], axis=0)) for sl in sls]
    bk_sp = [_split_bf16(jnp.concatenate([b_t[:, sl], k_t[:, sl]], axis=0)) for sl in sls]
    v_sp = [_split_bf16(v[:, sl]) for sl in sls]
    m = [_dots(ar_sp[h], bk_sp[h], _NT) for h in heads]
    as0 = [_dots(ar_sp[h], s0_sp[h], _NT) for h in heads]
    x_sp = [_split_bf16(jnp.where(strict, m[h][:c, :c], 0.0)) for h in heads]
    lak_sp = [_split_bf16(jnp.where(strict, m[h][:c, c:], 0.0)) for h in heads]
    tinv = [eye + jnp.where(strict, m[h][:c, :c], 0.0) for h in heads]
    rhs = [as0[h][:c] + _dots(lak_sp[h], v_sp[h]) for h in heads]
    for _ in range(n_double):
        x_sp = [_split_bf16(_dots(x_sp[h], x_sp[h])) for h in heads]
        tinv = [tinv[h] + _dots(_split_bf16(tinv[h]), x_sp[h]) for h in heads]
    u = [_dot3(tinv[h], rhs[h]) for h in heads]
    u_sp = [_split_bf16(uh) for uh in u]
    for h in heads:
        m_rb = jnp.where(incl, m[h][c:, :c], 0.0)
        m_rk = jnp.where(incl, m[h][c:, c:], 0.0)
        o_s[:, sls[h]] = as0[h][c:] + _dots(_split_bf16(m_rb), u_sp[h]) + _dots(_split_bf16(m_rk), v_sp[h])
    s_new = [s0[h] * gam[:, sls[h]] + _dots(u_sp[h], _split_bf16(b_h[:, sls[h]]), _TN)
             + _dots(v_sp[h], _split_bf16(k_h[:, sls[h]]), _TN) for h in heads]
    for h in heads:
        s_s[h] = s_new[h]

    o = o_s[...]
    inv_d = 1.0 / HEAD_DIM
    mean = _dot2(o, hsum) * inv_d
    cen = o - mean
    var = _dot2(cen * cen, hsum) * inv_d
    on = cen * lax.rsqrt(var + GN_EPS) * lnw_ref[...] + lnb_ref[...]
    bonus = _dot2(r * km * rk_ref[...], hsum) * v
    o_ref[...] = (on + bonus) * g

    @pl.when(j == pl.num_programs(1) - 1)
    def _fin():
        st_ref[0] = s_s[...]


def _rwkv(p, prev, s0, rw, c):
    mu, w0, w2, a0, a2, g2, k_k, k_a, r_k, lnx_w, lnx_b = rw
    b = prev.shape[0]
    nc = p.shape[0] // (b * c)
    hid = lax.broadcasted_iota(jnp.int32, (RWKV_WIDTH, RWKV_WIDTH), 0) // HEAD_DIM
    hsum = (hid == hid.T).astype(BF16)
    tri = (lax.broadcasted_iota(jnp.int32, (c, c), 0) >= lax.broadcasted_iota(jnp.int32, (c, c), 1)).astype(BF16)
    vec = lambda a: a.reshape(1, -1)
    full = lambda a: pl.BlockSpec(a.shape, lambda bi, j: (0,) * a.ndim)
    small = [vec(mu), vec(w0), w2, vec(a0), a2, g2, vec(k_k), vec(k_a), vec(r_k), vec(lnx_w), vec(lnx_b), hsum, tri]
    return pl.pallas_call(
        functools.partial(_rwkv_kernel, c=c),
        grid=(b, nc),
        in_specs=[pl.BlockSpec((c, RWKV_COLS), lambda bi, j: (bi * nc + j, 0)),
                  pl.BlockSpec((1, 1, RWKV_COLS), lambda bi, j: (bi, 0, 0)),
                  pl.BlockSpec((1, RWKV_HEADS, HEAD_DIM, HEAD_DIM), lambda bi, j: (bi, 0, 0, 0))]
                 + [full(a) for a in small],
        out_specs=[pl.BlockSpec((c, RWKV_WIDTH), lambda bi, j: (bi * nc + j, 0)),
                   pl.BlockSpec((1, RWKV_HEADS, HEAD_DIM, HEAD_DIM), lambda bi, j: (bi, 0, 0, 0))],
        out_shape=[jax.ShapeDtypeStruct((p.shape[0], RWKV_WIDTH), F32),
                   jax.ShapeDtypeStruct((b, RWKV_HEADS, HEAD_DIM, HEAD_DIM), F32)],
        scratch_shapes=[pltpu.VMEM((RWKV_HEADS, HEAD_DIM, HEAD_DIM), F32), pltpu.VMEM((1, RWKV_COLS), F32),
                        pltpu.VMEM((c, RWKV_WIDTH), F32)],
        compiler_params=_cparams(("arbitrary", "arbitrary")),
        name="rwkv",
    )(p, prev, s0, *small)


def _branch(x, mods, pos, prev, s0, caches, weights, tm, tm_moe):
    (g_mix, g_ffn, g_final, w_r, w_a, rw, w_out, rwt, rb, wup, bup, wdn, bdn) = weights
    sh1, sc1, gt1, sh2, sc2, gt2 = mods
    b, t, d = x.shape
    tab, tabk = _rope_tables(pos)
    p_r, q, k, v, qi, kw = _inproj(x, g_mix, sh1, sc1, w_r, w_a, tab, tabk, tm)
    o_r, wkv = _rwkv(p_r, prev, s0, rw, min(CHUNK, t))
    shift = p_r.reshape(b, t, RWKV_COLS)[:, -1:]
    k4 = k.reshape(b, t, KV_HEADS, HEAD_DIM)
    v4 = v.reshape(b, t, KV_HEADS, HEAD_DIM)
    ki = kw.reshape(b, t, KW_WIDTH)[..., :IDX_DIM]
    kt = 1024 if caches is None else 512
    if caches is None:
        k_all, v_all, ki_all, q0, l_valid = k4, v4, ki, 0, t
    else:
        ck, cv, cki = caches
        q0 = ck.shape[1]
        l_valid = q0 + t
        padn = (-l_valid) % kt
        zpad = lambda a: jnp.concatenate([a, jnp.zeros((b, padn) + a.shape[2:], a.dtype)], axis=1)
        k_all = zpad(jnp.concatenate([ck, k4], axis=1))
        v_all = zpad(jnp.concatenate([cv, v4], axis=1))
        ki_all = zpad(jnp.concatenate([cki, ki], axis=1))
    o_a = _dsa(q, qi, kw, k_all.astype(BF16), v_all.astype(BF16), ki_all.astype(BF16),
               qb=min(Q_BLOCK, t), kt=kt, q0=q0, l_valid=l_valid, n_sel=min(TOPK_MAX, l_valid // 4))
    x1, h2 = _outproj(o_r, o_a, x, w_out, gt1, g_ffn, sh2, sc2, tm)
    y_moe = _moe(h2, rwt, rb, wup, bup, wdn, bdn, *tm_moe)
    y = _final(x1, y_moe, gt2, g_final, b, t, tm)
    return y, k4, v4, ki, wkv, shift


def kernel(x_prompt, x_sample, c_prompt, c_sample, cache_k, cache_v, cache_kidx, state_wkv, state_shift,
           w_ada, b_ada, g_mix, g_ffn, g_final, w_in, mu_shift, w0, w_lora_w, a0, w_lora_a, w_lora_g,
           k_k, k_a, r_k, lnx_w, lnx_b, w_out, router_w, router_b, w_up, b_up, w_down, b_down):
    B, T, D = x_prompt.shape
    DB, DT, _ = x_sample.shape
    P = cache_k.shape[2]
    l = 0
    assert w_ada.shape[0] == 1

    rows = B + DB
    pad = (-rows) % 8
    c_all = jnp.concatenate([c_prompt, c_sample, jnp.zeros((pad, D), F32)], axis=0)
    m = _ada(c_all, w_ada[l], b_ada[l])
    mods_p = tuple(t.reshape(B, 1, D) for t in jnp.split(m[:B], 6, axis=-1))
    mods_s = tuple(t.reshape(DB, 1, D) for t in jnp.split(m[B:rows], 6, axis=-1))

    w_r = w_in[l][:, :RWKV_COLS].astype(BF16)
    w_att = w_in[l][:, RWKV_COLS:]
    att_pad = KW_WIDTH - IDX_DIM - IDX_HEADS
    w_a = jnp.concatenate([w_att, jnp.zeros((D, att_pad), F32)], axis=1).astype(BF16)
    rw = (mu_shift[l], w0[l], w_lora_w[l], a0[l], w_lora_a[l], w_lora_g[l], k_k[l], k_a[l], r_k[l],
          lnx_w[l], lnx_b[l])
    wup = _deinterleave_cast(w_up[l])
    bup = jnp.concatenate([b_up[l][..., 0::2], b_up[l][..., 1::2]], axis=-1).reshape(N_EXPERTS, 1, 2 * D_FF)
    wdn = w_down[l].astype(BF16)
    bdn = b_down[l].reshape(N_EXPERTS, 1, D)
    weights = (g_mix[l], g_ffn[l], g_final, w_r, w_a, rw, w_out[l].astype(BF16), router_w[l].T,
               router_b[l].reshape(N_EXPERTS, 1), wup, bup, wdn, bdn)

    pos_p = jnp.arange(T, dtype=jnp.int32)
    pos_s = P + jnp.arange(DT, dtype=jnp.int32)
    yp, kp, vp, kip, wkvp, shp = _branch(
        x_prompt, mods_p, pos_p, jnp.zeros((B, 1, RWKV_COLS), F32),
        jnp.zeros((B, RWKV_HEADS, HEAD_DIM, HEAD_DIM), F32), None, weights, 512, (MOE_SUB, MOE_NSUB))
    ys, ks, vs, kis, wkvs, shs = _branch(
        x_sample, mods_s, pos_s, state_shift[l], state_wkv[l],
        (cache_k[l], cache_v[l], cache_kidx[l]), weights, DT, (DB * DT, 1))
    return (yp, ys, kp[None], vp[None], kip[None], wkvp[None], shp[None],
            ks[None], vs[None], kis[None], wkvs[None], shs[None])
```

```python
import functools

import numpy as np
import jax
import jax.numpy as jnp
from jax import lax
from jax.experimental import pallas as pl
from jax.experimental.pallas import tpu as pltpu

F32 = jnp.float32
BF16 = jnp.bfloat16
HIGHEST = lax.Precision.HIGHEST

D_MODEL = 1024
CHUNK = 64
HEAD_DIM = 64
RWKV_HEADS = 8
RWKV_WIDTH = RWKV_HEADS * HEAD_DIM
W_LORA = 64
A_LORA = 64
G_LORA = 128
RWKV_COLS = 3 * RWKV_WIDTH + W_LORA + A_LORA + G_LORA
GN_EPS = 64e-5
ATT_HEADS = 8
ATT_WIDTH = ATT_HEADS * HEAD_DIM
KV_HEADS = 2
KV_WIDTH = KV_HEADS * HEAD_DIM
IDX_HEADS = 4
IDX_DIM = 64
TOPK_MAX = 256
Q_BLOCK = 128
ROPE_THETA = 500000.0
ROT_DIM = HEAD_DIM // 4
N_EXPERTS = 32
TOP_K = 4
D_FF = 1024
SWIGLU_ALPHA = 1.702
SWIGLU_LIMIT = 7.0
NORM_EPS = 1e-5

LANES = 128
MOE_ROWS = 128
MOE_SUB = 896
MOE_NSUB = 2
VMEM_LIMIT = 56 * 1024 * 1024

QI_WIDTH = IDX_HEADS * IDX_DIM
KW_WIDTH = LANES


def _cparams(sem):
    return pltpu.CompilerParams(dimension_semantics=sem, vmem_limit_bytes=VMEM_LIMIT)


def _ada_kernel(c_ref, w_ref, b_ref, o_ref):
    c = c_ref[...]
    s = c * jax.nn.sigmoid(c)
    o_ref[...] = jnp.dot(s, w_ref[...], precision=HIGHEST, preferred_element_type=F32) + b_ref[...]


def _ada(c, w, b):
    rows, d = c.shape
    n = w.shape[1]
    tn = 1536
    return pl.pallas_call(
        _ada_kernel,
        grid=(n // tn,),
        in_specs=[pl.BlockSpec((rows, d), lambda j: (0, 0)),
                  pl.BlockSpec((d, tn), lambda j: (0, j)),
                  pl.BlockSpec((1, tn), lambda j: (0, j))],
        out_specs=pl.BlockSpec((rows, tn), lambda j: (0, j)),
        out_shape=jax.ShapeDtypeStruct((rows, n), F32),
        compiler_params=_cparams(("arbitrary",)),
        name="ada",
    )(c, w, b.reshape(1, n))


def _rope_slab(y, tab_ref):
    return (y * tab_ref[0] + pltpu.roll(y, LANES - ROT_DIM // 2, 1) * tab_ref[1]
            + pltpu.roll(y, ROT_DIM // 2, 1) * tab_ref[2])


def _norm_mod(x, g, sh, sc):
    var = jnp.mean(x * x, axis=-1, keepdims=True)
    return (x * lax.rsqrt(var + NORM_EPS) * g) * (1.0 + sc) + sh


def _inproj_kernel(x_ref, g_ref, sh_ref, sc_ref, wr_ref, wa_ref, tab_ref, tabk_ref,
                   rw_ref, q_ref, k_ref, v_ref, qi_ref, kw_ref):
    h = _norm_mod(x_ref[...], g_ref[...], sh_ref[0], sc_ref[0]).astype(BF16)
    rw_ref[...] = jnp.dot(h, wr_ref[...], preferred_element_type=F32)
    pa = jnp.dot(h, wa_ref[...], preferred_element_type=F32)
    off = 0
    for ref, width, rot in ((q_ref, ATT_WIDTH, True), (k_ref, KV_WIDTH, True), (v_ref, KV_WIDTH, False),
                            (qi_ref, QI_WIDTH, True)):
        for s in range(width // LANES):
            slab = pa[:, off + s * LANES: off + (s + 1) * LANES]
            ref[:, s * LANES:(s + 1) * LANES] = _rope_slab(slab, tab_ref) if rot else slab
        off += width
    kw_ref[...] = _rope_slab(pa[:, off:off + LANES], tabk_ref)


def _rope_tables(pos):
    half = ROT_DIM // 2
    inv = ROPE_THETA ** (-jnp.arange(0, ROT_DIM, 2, dtype=F32) / ROT_DIM)
    ang = pos.astype(F32)[:, None] * inv[None, :]
    cos, sin = jnp.cos(ang), jnp.sin(ang)
    t = pos.shape[0]
    one = jnp.ones((t, HEAD_DIM - ROT_DIM), F32)
    zero_r = jnp.zeros((t, HEAD_DIM - ROT_DIM), F32)
    zero_h = jnp.zeros((t, half), F32)
    c_head = jnp.concatenate([cos, cos, one], axis=1)
    up_head = jnp.concatenate([-sin, zero_h, zero_r], axis=1)
    dn_head = jnp.concatenate([zero_h, sin, zero_r], axis=1)
    tab = jnp.stack([jnp.tile(c_head, (1, 2)), jnp.tile(up_head, (1, 2)), jnp.tile(dn_head, (1, 2))])
    wscale = jnp.concatenate([jnp.full((t, IDX_HEADS), IDX_HEADS ** -0.5, F32),
                              jnp.ones((t, HEAD_DIM - IDX_HEADS), F32)], axis=1)
    zero64 = jnp.zeros((t, HEAD_DIM), F32)
    tabk = jnp.stack([jnp.concatenate([c_head, wscale], axis=1),
                      jnp.concatenate([up_head, zero64], axis=1),
                      jnp.concatenate([dn_head, zero64], axis=1)])
    return tab, tabk


def _inproj(x, g, sh, sc, w_r, w_a, tab, tabk, tm):
    b, t, d = x.shape
    nt = t // tm
    n = b * t
    x2 = x.reshape(n, d)
    widths = (RWKV_COLS, ATT_WIDTH, KV_WIDTH, KV_WIDTH, QI_WIDTH, KW_WIDTH)
    row = lambda w: pl.BlockSpec((tm, w), lambda i: (i, 0))
    mod = pl.BlockSpec((1, 1, d), lambda i: (i // nt, 0, 0))
    tabspec = pl.BlockSpec((3, tm, LANES), lambda i: (0, i % nt, 0))
    return pl.pallas_call(
        _inproj_kernel,
        grid=(n // tm,),
        in_specs=[row(d), pl.BlockSpec((1, d), lambda i: (0, 0)), mod, mod,
                  pl.BlockSpec(w_r.shape, lambda i: (0, 0)), pl.BlockSpec(w_a.shape, lambda i: (0, 0)),
                  tabspec, tabspec],
        out_specs=[row(w) for w in widths],
        out_shape=[jax.ShapeDtypeStruct((n, w), F32) for w in widths],
        compiler_params=_cparams(("arbitrary",)),
        name="inproj",
    )(x2, g.reshape(1, d), sh, sc, w_r, w_a, tab, tabk)


def _outproj_kernel(or_ref, oa_ref, x_ref, w_ref, gt_ref, g_ref, sh_ref, sc_ref, x1_ref, h2_ref):
    m = jnp.dot(or_ref[...].astype(BF16), w_ref[:RWKV_WIDTH, :], preferred_element_type=F32)
    m = m + jnp.dot(oa_ref[...].astype(BF16), w_ref[RWKV_WIDTH:, :], preferred_element_type=F32)
    x1 = x_ref[...] + gt_ref[0] * m
    x1_ref[...] = x1
    h2_ref[...] = _norm_mod(x1, g_ref[...], sh_ref[0], sc_ref[0])


def _outproj(o_r, o_a, x, w_out, gt, g, sh, sc, tm):
    b, t, d = x.shape
    nt = t // tm
    n = b * t
    row = lambda w: pl.BlockSpec((tm, w), lambda i: (i, 0))
    mod = pl.BlockSpec((1, 1, d), lambda i: (i // nt, 0, 0))
    return pl.pallas_call(
        _outproj_kernel,
        grid=(n // tm,),
        in_specs=[row(RWKV_WIDTH), row(ATT_WIDTH), row(d), pl.BlockSpec(w_out.shape, lambda i: (0, 0)),
                  mod, pl.BlockSpec((1, d), lambda i: (0, 0)), mod, mod],
        out_specs=[row(d), row(d)],
        out_shape=[jax.ShapeDtypeStruct((n, d), F32)] * 2,
        compiler_params=_cparams(("arbitrary",)),
        name="outproj",
    )(o_r.reshape(n, RWKV_WIDTH), o_a.reshape(n, ATT_WIDTH), x.reshape(n, d), w_out, gt, g.reshape(1, d), sh, sc)


def _moe_kernel(h_ref, rwt_ref, rb_ref, tri_ref, wup_ref, bup_ref, wdn_ref, bdn_ref, y_ref,
                hb_s, rank_s, gate_s, *, n_tokens, sub):
    e = pl.program_id(1)
    n_sub = h_ref.shape[0] // sub

    @pl.when(e == 0)
    def _route():
        rwt = rwt_ref[...].astype(BF16)
        for s in range(n_sub):
            first_row = pl.program_id(0) * (n_sub * sub) + s * sub
            in_rows = first_row + lax.broadcasted_iota(jnp.int32, (sub, 1), 0) < n_tokens
            in_cols = first_row + lax.broadcasted_iota(jnp.int32, (1, sub), 1) < n_tokens
            hb = jnp.where(in_rows, h_ref[s * sub:(s + 1) * sub, :], 0.0).astype(BF16)
            hb_s[s * sub:(s + 1) * sub, :] = hb
            logits = lax.dot_general(rwt, hb, (((1,), (1,)), ((), ())), preferred_element_type=F32) + rb_ref[...]
            eidx = lax.broadcasted_iota(jnp.int32, logits.shape, 0)
            work = logits
            top = None
            for _ in range(TOP_K):
                m = jnp.max(work, axis=0, keepdims=True)
                if top is None:
                    top = m
                first = jnp.min(jnp.where(work == m, eidx, N_EXPERTS), axis=0, keepdims=True)
                work = jnp.where(eidx == first, -jnp.inf, work)
            ex = jnp.where(work != logits, jnp.exp(logits - top), 0.0)
            sel = jnp.logical_and(work != logits, in_cols)
            gate_s[s] = ex / jnp.sum(ex, axis=0, keepdims=True)
            before = jnp.dot(jnp.where(sel, 1.0, 0.0).astype(BF16), tri_ref[...], preferred_element_type=F32)
            rank_s[s] = jnp.where(sel, before, -1.0)
        y_ref[...] = jnp.zeros(y_ref.shape, F32)

    for s in range(n_sub):
        r_row = rank_s[s, pl.ds(e, 1), :]
        g_row = gate_s[s, pl.ds(e, 1), :]
        count = jnp.sum(jnp.where(r_row >= 0.0, 1, 0).astype(jnp.int32))
        n_blocks = (count + MOE_ROWS - 1) // MOE_ROWS

        def block(j, carry, s=s, r_row=r_row, g_row=g_row):
            rows = (lax.broadcasted_iota(jnp.int32, (MOE_ROWS, sub), 0) + j * MOE_ROWS).astype(F32)
            hit = r_row == rows
            p = jnp.where(hit, 1.0, 0.0).astype(BF16)
            xe = jnp.dot(p, hb_s[s * sub:(s + 1) * sub, :], preferred_element_type=F32).astype(BF16)
            u = jnp.dot(xe, wup_ref[0], preferred_element_type=F32) + bup_ref[0]
            glu = jnp.minimum(u[:, :D_FF], SWIGLU_LIMIT)
            lin = jnp.clip(u[:, D_FF:], -SWIGLU_LIMIT, SWIGLU_LIMIT)
            act = glu * jax.nn.sigmoid(SWIGLU_ALPHA * glu) * (lin + 1.0)
            yb = jnp.dot(act.astype(BF16), wdn_ref[0], preferred_element_type=F32) + bdn_ref[0]
            g_rows = jnp.sum(jnp.where(hit, g_row, 0.0), axis=1, keepdims=True)
            ys = (yb * g_rows).astype(BF16)
            y_ref[s * sub:(s + 1) * sub, :] += lax.dot_general(p, ys, (((0,), (0,)), ((), ())),
                                                               preferred_element_type=F32)
            return carry

        lax.fori_loop(0, n_blocks, block, 0)


def _moe(h, rwt, rb, wup, bup, wdn, bdn, sub, n_sub):
    n, d = h.shape
    tm = sub * n_sub
    tri = (lax.broadcasted_iota(jnp.int32, (sub, sub), 0) < lax.broadcasted_iota(jnp.int32, (sub, sub), 1)).astype(BF16)
    return pl.pallas_call(
        functools.partial(_moe_kernel, n_tokens=n, sub=sub),
        grid=(pl.cdiv(n, tm), N_EXPERTS),
        in_specs=[pl.BlockSpec((tm, d), lambda i, e: (i, 0)),
                  pl.BlockSpec((N_EXPERTS, d), lambda i, e: (0, 0)),
                  pl.BlockSpec((N_EXPERTS, 1), lambda i, e: (0, 0)),
                  pl.BlockSpec((sub, sub), lambda i, e: (0, 0)),
                  pl.BlockSpec((1, d, 2 * D_FF), lambda i, e: (e, 0, 0)),
                  pl.BlockSpec((1, 1, 2 * D_FF), lambda i, e: (e, 0, 0)),
                  pl.BlockSpec((1, D_FF, d), lambda i, e: (e, 0, 0)),
                  pl.BlockSpec((1, 1, d), lambda i, e: (e, 0, 0))],
        out_specs=pl.BlockSpec((tm, d), lambda i, e: (i, 0)),
        out_shape=jax.ShapeDtypeStruct((n, d), F32),
        scratch_shapes=[pltpu.VMEM((tm, d), BF16), pltpu.VMEM((n_sub, N_EXPERTS, sub), F32),
                        pltpu.VMEM((n_sub, N_EXPERTS, sub), F32)],
        compiler_params=_cparams(("arbitrary", "arbitrary")),
        name="moe",
    )(h, rwt, rb, tri, wup, bup, wdn, bdn)


def _deinterleave_kernel(w_ref, perm_ref, o_ref):
    o_ref[0] = jnp.dot(w_ref[0].astype(BF16), perm_ref[...], preferred_element_type=F32).astype(BF16)


def _deinterleave_cast(w):
    e, d, n = w.shape
    tk = 512
    src = lax.broadcasted_iota(jnp.int32, (n, n), 0)
    dst = lax.broadcasted_iota(jnp.int32, (n, n), 1)
    perm = (src == jnp.where(dst < n // 2, 2 * dst, 2 * (dst - n // 2) + 1)).astype(BF16)
    return pl.pallas_call(
        _deinterleave_kernel,
        grid=(e, d // tk),
        in_specs=[pl.BlockSpec((1, tk, n), lambda i, j: (i, j, 0)), pl.BlockSpec((n, n), lambda i, j: (0, 0))],
        out_specs=pl.BlockSpec((1, tk, n), lambda i, j: (i, j, 0)),
        out_shape=jax.ShapeDtypeStruct((e, d, n), BF16),
        compiler_params=_cparams(("arbitrary", "arbitrary")),
        name="deinterleave",
    )(w, perm)


def _final_kernel(x1_ref, y_ref, gt_ref, g_ref, o_ref):
    x = x1_ref[...] + gt_ref[0] * y_ref[...]
    var = jnp.mean(x * x, axis=-1, keepdims=True)
    o_ref[...] = x * lax.rsqrt(var + NORM_EPS) * g_ref[...]


def _final(x1, y, gt, g, b, t, tm):
    n, d = x1.shape
    nt = t // tm
    row = pl.BlockSpec((tm, d), lambda i: (i, 0))
    return pl.pallas_call(
        _final_kernel,
        grid=(n // tm,),
        in_specs=[row, row, pl.BlockSpec((1, 1, d), lambda i: (i // nt, 0, 0)),
                  pl.BlockSpec((1, d), lambda i: (0, 0))],
        out_specs=row,
        out_shape=jax.ShapeDtypeStruct((n, d), F32),
        compiler_params=_cparams(("arbitrary",)),
        name="final",
    )(x1, y, gt, g.reshape(1, d)).reshape(b, t, d)


MASKED = -1e30
INT_MIN = -2 ** 31
COUNT_ROWS = 64


def _dsa_kernel(q_ref, qi_ref, kw_ref, k_ref, vx_ref, ki_ref, tri_ref, o_ref, key_s, bias_s, wib_s,
                *, qb, kt, q0, l_valid, n_sel):
    i = pl.program_id(1)
    ns = kt // LANES
    row = lax.broadcasted_iota(jnp.int32, (qb, 1), 0)
    qpos = q0 + i * qb + row
    lim = jnp.minimum((qpos // CHUNK + 1) * CHUNK, l_valid)
    last_lim = jnp.minimum(((q0 + i * qb + qb - 1) // CHUNK + 1) * CHUNK, l_valid)
    n_kt = (last_lim + kt - 1) // kt
    lane = lax.broadcasted_iota(jnp.int32, (1, LANES), 1)

    for h in range(IDX_HEADS):
        w = kw_ref[:, IDX_DIM + h:IDX_DIM + h + 1] * (IDX_DIM ** -0.5)
        wib_s[h] = jnp.broadcast_to(w, (qb, LANES))
    qi = qi_ref[...].astype(BF16)

    def f32_key(x):
        bits = pltpu.bitcast(x, jnp.int32)
        return bits ^ ((bits >> 31) & 0x7FFFFFFF)

    def score_tile(t, c):
        ks = pl.multiple_of(t * kt, kt)
        kit = ki_ref[0, pl.ds(ks, kt), :]
        lg = [lax.dot_general(qi[:, h * IDX_DIM:(h + 1) * IDX_DIM], kit, (((1,), (1,)), ((), ())),
                              preferred_element_type=F32) for h in range(IDX_HEADS)]
        for s in range(ns):
            sc = jnp.zeros((qb, LANES), F32)
            for h in range(IDX_HEADS):
                sc = sc + wib_s[h] * jnp.maximum(lg[h][:, s * LANES:(s + 1) * LANES], 0.0)
            kpos = ks + s * LANES + lane
            sc = jnp.where(kpos < lim, sc + 0.0, -jnp.inf)
            key_s[t, :, s * LANES:(s + 1) * LANES] = f32_key(sc)
        return c

    lax.fori_loop(0, n_kt, score_tile, 0)

    def count(pred_fn):
        def tile(t, acc):
            for s in range(ns):
                acc = acc + jnp.where(pred_fn(key_s[t, :, s * LANES:(s + 1) * LANES]), 1.0, 0.0)
            return acc
        acc = lax.fori_loop(0, n_kt, tile, jnp.zeros((qb, LANES), F32))
        return jnp.sum(acc, axis=1, keepdims=True)

    def bit_step(b, lo):
        cand = lo + jnp.left_shift(jnp.int32(1), 31 - b)
        candb = jnp.broadcast_to(cand, (qb, LANES))
        return jnp.where(count(lambda k: k >= candb) >= n_sel, cand, lo)

    thr = lax.fori_loop(0, 32, bit_step, jnp.full((qb, 1), INT_MIN, jnp.int32))
    thrb = jnp.broadcast_to(thr, (qb, LANES))
    need = jnp.broadcast_to(n_sel - count(lambda k: k > thrb), (qb, LANES))

    def sel_tile(t, off):
        for s in range(ns):
            key = key_s[t, :, s * LANES:(s + 1) * LANES]
            eq = key == thrb
            pre = jnp.dot(jnp.where(eq, 1.0, 0.0).astype(BF16), tri_ref[...], preferred_element_type=F32)
            kpos = t * kt + s * LANES + lane
            keep = jnp.logical_or(key > thrb, jnp.logical_and(eq, pre[:, :LANES] + off < need))
            keep = jnp.logical_and(keep, kpos < lim)
            bias_s[t, :, s * LANES:(s + 1) * LANES] = jnp.where(keep, 0.0, MASKED)
            off = off + pre[:, LANES:]
        return off

    lax.fori_loop(0, n_kt, sel_tile, jnp.zeros((qb, LANES), F32))

    rep = ATT_HEADS // KV_HEADS
    qgs = []
    for g in range(KV_HEADS):
        qg = jnp.concatenate([q_ref[:, (g * rep + r) * HEAD_DIM:(g * rep + r + 1) * HEAD_DIM] for r in range(rep)],
                             axis=0)
        qgs.append((qg * (HEAD_DIM ** -0.5)).astype(BF16))

    def att_tile(t, carry):
        ks = pl.multiple_of(t * kt, kt)
        out = []
        for g in range(KV_HEADS):
            m, acc = carry[g]
            s = lax.dot_general(qgs[g], k_ref[0, g, pl.ds(ks, kt), :], (((1,), (1,)), ((), ())),
                                preferred_element_type=F32)
            s = (s.reshape(rep, qb, kt) + bias_s[t][None]).reshape(rep * qb, kt)
            m_new = jnp.maximum(m, jnp.max(s, axis=1, keepdims=True))
            p = jnp.exp(s - m_new)
            pv = jnp.dot(p.astype(BF16), vx_ref[0, pl.ds(ks, kt), g * LANES:(g + 1) * LANES],
                         preferred_element_type=F32)
            out.append((m_new, acc * jnp.exp(m - m_new) + pv))
        return tuple(out)

    init = (jnp.full((rep * qb, 1), MASKED, F32), jnp.zeros((rep * qb, LANES), F32))
    res = lax.fori_loop(0, n_kt, att_tile, (init,) * KV_HEADS)
    for g in range(KV_HEADS):
        acc = res[g][1]
        out = acc * pltpu.roll(1.0 / acc, HEAD_DIM, 1)
        for r in range(rep):
            h = g * rep + r
            o_ref[:, h * HEAD_DIM:(h + 1) * HEAD_DIM] = out[r * qb:(r + 1) * qb, :HEAD_DIM]


def _dsa(q, qi, kw, k_bf, v_bf, ki_bf, *, qb, kt, q0, l_valid, n_sel):
    b, lp = ki_bf.shape[:2]
    nq = q.shape[0] // (b * qb)
    k_g = jnp.moveaxis(k_bf, 2, 1)
    ones = jnp.ones((b, lp, KV_HEADS, HEAD_DIM), BF16)
    vx = jnp.concatenate([v_bf, ones], axis=-1).reshape(b, lp, KV_HEADS * LANES)
    tr = lax.broadcasted_iota(jnp.int32, (LANES, 2 * LANES), 0)
    tc = lax.broadcasted_iota(jnp.int32, (LANES, 2 * LANES), 1)
    tri = jnp.logical_or(tr < tc, tc >= LANES).astype(BF16)
    row = lambda w: pl.BlockSpec((qb, w), lambda bi, i: (bi * nq + i, 0))
    kern = functools.partial(_dsa_kernel, qb=qb, kt=kt, q0=q0, l_valid=l_valid, n_sel=n_sel)
    return pl.pallas_call(
        kern,
        grid=(b, nq),
        in_specs=[row(ATT_WIDTH), row(QI_WIDTH), row(KW_WIDTH),
                  pl.BlockSpec((1, KV_HEADS, lp, HEAD_DIM), lambda bi, i: (bi, 0, 0, 0)),
                  pl.BlockSpec((1, lp, KV_HEADS * LANES), lambda bi, i: (bi, 0, 0)),
                  pl.BlockSpec((1, lp, IDX_DIM), lambda bi, i: (bi, 0, 0)),
                  pl.BlockSpec((LANES, 2 * LANES), lambda bi, i: (0, 0))],
        out_specs=row(ATT_WIDTH),
        out_shape=jax.ShapeDtypeStruct((q.shape[0], ATT_WIDTH), F32),
        scratch_shapes=[pltpu.VMEM((lp // kt, qb, kt), jnp.int32), pltpu.VMEM((lp // kt, qb, kt), F32),
                        pltpu.VMEM((IDX_HEADS, qb, LANES), F32)],
        compiler_params=_cparams(("arbitrary", "arbitrary")),
        name="dsa",
    )(q, qi, kw, k_g, vx, ki_bf, tri)


def _dsa_t_kernel(q_ref, qi_ref, kw_ref, k_ref, vxt_ref, ki_ref, tri_ref, o_ref, key_s, bias_s,
                  *, qb, kt, q0, l_valid, n_sel):
    i = pl.program_id(1)
    ns = kt // LANES
    qpos = q0 + i * qb + lax.broadcasted_iota(jnp.int32, (1, qb), 1)
    lim = jnp.minimum((qpos // CHUNK + 1) * CHUNK, l_valid)
    last_lim = jnp.minimum(((q0 + i * qb + qb - 1) // CHUNK + 1) * CHUNK, l_valid)
    n_kt = (last_lim + kt - 1) // kt
    rep = ATT_HEADS // KV_HEADS

    def by_head(x_t, n):
        return jnp.concatenate([x_t[h * HEAD_DIM:(h + 1) * HEAD_DIM, :] for h in range(n)], axis=1)

    w_qi = by_head(qi_ref[...].T, IDX_HEADS).astype(BF16)
    q_t = q_ref[...].T * (HEAD_DIM ** -0.5)
    w_q = [by_head(q_t[g * rep * HEAD_DIM:(g + 1) * rep * HEAD_DIM, :], rep).astype(BF16) for g in range(KV_HEADS)]
    kw_t = kw_ref[...].T
    wi = [kw_t[IDX_DIM + h:IDX_DIM + h + 1, :] * (IDX_DIM ** -0.5) for h in range(IDX_HEADS)]

    def f32_key(x):
        bits = pltpu.bitcast(x, jnp.int32)
        return bits ^ ((bits >> 31) & 0x7FFFFFFF)

    def score_tile(t, c):
        ks = pl.multiple_of(t * kt, kt)
        lg = jnp.dot(ki_ref[0, pl.ds(ks, kt), :], w_qi, preferred_element_type=F32)
        sc = jnp.zeros((kt, qb), F32)
        for h in range(IDX_HEADS):
            sc = sc + wi[h] * jnp.maximum(lg[:, h * qb:(h + 1) * qb], 0.0)
        kpos = ks + lax.broadcasted_iota(jnp.int32, (kt, qb), 0)
        sc = jnp.where(kpos < lim, sc + 0.0, -jnp.inf)
        key_s[pl.ds(ks, kt), :] = f32_key(sc)
        return c

    lax.fori_loop(0, n_kt, score_tile, 0)

    def count(pred_fn):
        def tile(t, acc):
            ks = pl.multiple_of(t * kt, kt)
            for c in range(kt // COUNT_ROWS):
                chunk = key_s[pl.ds(ks + c * COUNT_ROWS, COUNT_ROWS), :]
                acc = acc + jnp.where(pred_fn(chunk), 1.0, 0.0)
            return acc
        acc = lax.fori_loop(0, n_kt, tile, jnp.zeros((COUNT_ROWS, qb), F32))
        return jnp.sum(acc, axis=0, keepdims=True)

    def bit_step(b, lo):
        cand = lo + jnp.left_shift(jnp.int32(1), 31 - b)
        return jnp.where(count(lambda k: k >= cand) >= n_sel, cand, lo)

    thr = lax.fori_loop(0, 32, bit_step, jnp.full((1, qb), INT_MIN, jnp.int32))
    need = n_sel - count(lambda k: k > thr)

    def sel_tile(t, off):
        for s in range(ns):
            ks = pl.multiple_of(t * kt + s * LANES, LANES)
            key = key_s[pl.ds(ks, LANES), :]
            eq = key == thr
            pre = jnp.dot(tri_ref[...], jnp.where(eq, 1.0, 0.0).astype(BF16), preferred_element_type=F32)
            kpos = ks + lax.broadcasted_iota(jnp.int32, (LANES, qb), 0)
            keep = jnp.logical_or(key > thr, jnp.logical_and(eq, pre[:LANES] + off < need))
            keep = jnp.logical_and(keep, kpos < lim)
            bias_s[pl.ds(ks, LANES), :] = jnp.where(keep, 0.0, MASKED)
            off = off + pre[LANES:]
        return off

    lax.fori_loop(0, n_kt, sel_tile, jnp.zeros((LANES, qb), F32))

    def att_tile(t, carry):
        ks = pl.multiple_of(t * kt, kt)
        bias = bias_s[pl.ds(ks, kt), :]
        out = []
        for g in range(KV_HEADS):
            m, acc = carry[g]
            s = jnp.dot(k_ref[0, g, pl.ds(ks, kt), :], w_q[g], preferred_element_type=F32)
            s = jnp.concatenate([s[:, r * qb:(r + 1) * qb] + bias for r in range(rep)], axis=1)
            m_new = jnp.maximum(m, jnp.max(s, axis=0, keepdims=True))
            p = jnp.exp(s - m_new).astype(BF16)
            pv = jnp.dot(vxt_ref[0, t, g * LANES:(g + 1) * LANES, :], p, preferred_element_type=F32)
            out.append((m_new, acc * jnp.exp(m - m_new) + pv))
        return tuple(out)

    init = (jnp.full((1, rep * qb), MASKED, F32), jnp.zeros((LANES, rep * qb), F32))
    res = lax.fori_loop(0, n_kt, att_tile, (init,) * KV_HEADS)
    for g in range(KV_HEADS):
        acc = res[g][1]
        out = acc * (1.0 / acc[HEAD_DIM:HEAD_DIM + 1, :])
        for r in range(rep):
            h = g * rep + r
            o_ref[:, h * HEAD_DIM:(h + 1) * HEAD_DIM] = out[:, r * qb:(r + 1) * qb].T[:, :HEAD_DIM]


def _dsa_t(q, qi, kw, k_bf, v_bf, ki_bf, *, qb, kt, q0, l_valid, n_sel):
    assert qb == LANES
    b, lp = ki_bf.shape[:2]
    nq = q.shape[0] // (b * qb)
    k_g = jnp.moveaxis(k_bf, 2, 1)
    ones = jnp.ones((b, lp, KV_HEADS, HEAD_DIM), BF16)
    vx = jnp.concatenate([v_bf, ones], axis=-1).reshape(b, lp // kt, kt, KV_HEADS * LANES)
    vxt = jnp.swapaxes(vx, 2, 3)
    tr = lax.broadcasted_iota(jnp.int32, (2 * LANES, LANES), 0)
    tc = lax.broadcasted_iota(jnp.int32, (2 * LANES, LANES), 1)
    tri = jnp.logical_or(tc < tr, tr >= LANES).astype(BF16)
    row = lambda w: pl.BlockSpec((qb, w), lambda bi, i: (bi * nq + i, 0))
    kern = functools.partial(_dsa_t_kernel, qb=qb, kt=kt, q0=q0, l_valid=l_valid, n_sel=n_sel)
    return pl.pallas_call(
        kern,
        grid=(b, nq),
        in_specs=[row(ATT_WIDTH), row(QI_WIDTH), row(KW_WIDTH),
                  pl.BlockSpec((1, KV_HEADS, lp, HEAD_DIM), lambda bi, i: (bi, 0, 0, 0)),
                  pl.BlockSpec((1, lp // kt, KV_HEADS * LANES, kt), lambda bi, i: (bi, 0, 0, 0)),
                  pl.BlockSpec((1, lp, IDX_DIM), lambda bi, i: (bi, 0, 0)),
                  pl.BlockSpec((2 * LANES, LANES), lambda bi, i: (0, 0))],
        out_specs=row(ATT_WIDTH),
        out_shape=jax.ShapeDtypeStruct((q.shape[0], ATT_WIDTH), F32),
        scratch_shapes=[pltpu.VMEM((lp, qb), jnp.int32), pltpu.VMEM((lp, qb), F32)],
        compiler_params=_cparams(("arbitrary", "arbitrary")),
        name="dsa_t",
    )(q, qi, kw, k_g, vxt, ki_bf, tri)


def _split_bf16(x):
    hi = x.astype(BF16)
    return hi, (x - hi.astype(F32)).astype(BF16)


_NN = (((1,), (0,)), ((), ()))


def _dots(a_sp, b_sp, dims=_NN):
    (ah, al), (bh, bl) = a_sp, b_sp
    d = functools.partial(lax.dot_general, dimension_numbers=dims, preferred_element_type=F32)
    return d(ah, bh) + (d(ah, bl) + d(al, bh))


def _dot3(a, b, dims=_NN):
    return _dots(_split_bf16(a), _split_bf16(b), dims)


def _dot2(a, b_exact):
    ah, al = _split_bf16(a)
    return jnp.dot(ah, b_exact, preferred_element_type=F32) + jnp.dot(al, b_exact, preferred_element_type=F32)


_NT = (((1,), (1,)), ((), ()))
_TN = (((0,), (0,)), ((), ()))


def _rwkv_kernel(p_ref, prev_ref, s0_ref, mu_ref, w0_ref, w2_ref, a0_ref, a2_ref, g2_ref, kk_ref, ka_ref,
                 rk_ref, lnw_ref, lnb_ref, hsum_ref, tri_ref, o_ref, st_ref, s_s, prev_s, o_s, *, c):
    j = pl.program_id(1)

    @pl.when(j == 0)
    def _init():
        s_s[...] = s0_ref[0]
        prev_s[...] = prev_ref[0]

    p = p_ref[...]
    row = lax.broadcasted_iota(jnp.int32, (c, 1), 0)
    p_prev = jnp.where(row == 0, prev_s[...], pltpu.roll(p, 1, 0))
    prev_s[...] = p[c - 1:c, :]
    xs = p + (p_prev - p) * mu_ref[...]
    o1, o2, o3 = RWKV_WIDTH, 2 * RWKV_WIDTH, 3 * RWKV_WIDTH
    r, k, v = xs[:, :o1], xs[:, o1:o2], xs[:, o2:o3]
    dw = xs[:, o3:o3 + W_LORA]
    da = xs[:, o3 + W_LORA:o3 + W_LORA + A_LORA]
    dg = xs[:, o3 + W_LORA + A_LORA:]
    hsum = hsum_ref[...]
    w_log = -jax.nn.softplus(-(w0_ref[...] + _dot3(jnp.tanh(dw), w2_ref[...]))) - 0.5
    lw = -jnp.exp(w_log)
    a = jax.nn.sigmoid(a0_ref[...] + _dot3(da, a2_ref[...]))
    g = _dot3(jax.nn.sigmoid(dg), g2_ref[...])
    kk = k * kk_ref[...]
    kk = kk / jnp.maximum(jnp.sqrt(_dot2(kk * kk, hsum)), 1e-12)
    km = k * (1.0 + (a - 1.0) * ka_ref[...])
    bm = kk * a
    lw_hi, lw_lo = _split_bf16(lw)
    tri = tri_ref[...]
    cum = (jnp.dot(tri, lw_hi, preferred_element_type=F32)
           + jnp.dot(tri, lw_lo, preferred_element_type=F32))
    tot = cum[c - 1:c, :]
    e_in = jnp.exp(cum)
    e_out = jnp.exp(-cum)
    e_end = jnp.exp(tot - cum)
    a_t = -kk * jnp.exp(cum - lw)
    r_t = r * e_in
    b_t = bm * e_out
    k_t = km * e_out
    b_h = bm * e_end
    k_h = km * e_end
    gam = jnp.exp(tot)

    ri = lax.broadcasted_iota(jnp.int32, (c, c), 0)
    ci = lax.broadcasted_iota(jnp.int32, (c, c), 1)
    strict = ri > ci
    incl = ri >= ci
    eye = jnp.where(ri == ci, 1.0, 0.0)
    n_double = int(np.log2(c)) - 1
    heads = range(RWKV_HEADS)
    sls = [slice(h * HEAD_DIM, (h + 1) * HEAD_DIM) for h in heads]
    s0 = [s_s[h] for h in heads]
    s0_sp = [_split_bf16(s) for s in s0]
    ar_sp = [_split_bf16(jnp.concatenate([a_t[:, sl], r_t[:, sl]], axis=0)) for sl in sls]
    bk_sp = [_split_bf16(jnp.concatenate([b_t[:, sl], k_t[:, sl]], axis=0)) for sl in sls]
    v_sp = [_split_bf16(v[:, sl]) for sl in sls]
    m = [_dots(ar_sp[h], bk_sp[h], _NT) for h in heads]
    as0 = [_dots(ar_sp[h], s0_sp[h], _NT) for h in heads]
    x_sp = [_split_bf16(jnp.where(strict, m[h][:c, :c], 0.0)) for h in heads]
    lak_sp = [_split_bf16(jnp.where(strict, m[h][:c, c:], 0.0)) for h in heads]
    tinv = [eye + jnp.where(strict, m[h][:c, :c], 0.0) for h in heads]
    rhs = [as0[h][:c] + _dots(lak_sp[h], v_sp[h]) for h in heads]
    for _ in range(n_double):
        x_sp = [_split_bf16(_dots(x_sp[h], x_sp[h])) for h in heads]
        tinv = [tinv[h] + _dots(_split_bf16(tinv[h]), x_sp[h]) for h in heads]
    u = [_dot3(tinv[h], rhs[h]) for h in heads]
    u_sp = [_split_bf16(uh) for uh in u]
    for h in heads:
        m_rb = jnp.where(incl, m[h][c:, :c], 0.0)
        m_rk = jnp.where(incl, m[h][c:, c:], 0.0)
        o_s[:, sls[h]] = as0[h][c:] + _dots(_split_bf16(m_rb), u_sp[h]) + _dots(_split_bf16(m_rk), v_sp[h])
    s_new = [s0[h] * gam[:, sls[h]] + _dots(u_sp[h], _split_bf16(b_h[:, sls[h]]), _TN)
             + _dots(v_sp[h], _split_bf16(k_h[:, sls[h]]), _TN) for h in heads]
    for h in heads:
        s_s[h] = s_new[h]

    o = o_s[...]
    inv_d = 1.0 / HEAD_DIM
    mean = _dot2(o, hsum) * inv_d
    cen = o - mean
    var = _dot2(cen * cen, hsum) * inv_d
    on = cen * lax.rsqrt(var + GN_EPS) * lnw_ref[...] + lnb_ref[...]
    bonus = _dot2(r * km * rk_ref[...], hsum) * v
    o_ref[...] = (on + bonus) * g

    @pl.when(j == pl.num_programs(1) - 1)
    def _fin():
        st_ref[0] = s_s[...]


def _rwkv(p, prev, s0, rw, c):
    mu, w0, w2, a0, a2, g2, k_k, k_a, r_k, lnx_w, lnx_b = rw
    b = prev.shape[0]
    nc = p.shape[0] // (b * c)
    hid = lax.broadcasted_iota(jnp.int32, (RWKV_WIDTH, RWKV_WIDTH), 0) // HEAD_DIM
    hsum = (hid == hid.T).astype(BF16)
    tri = (lax.broadcasted_iota(jnp.int32, (c, c), 0) >= lax.broadcasted_iota(jnp.int32, (c, c), 1)).astype(BF16)
    vec = lambda a: a.reshape(1, -1)
    full = lambda a: pl.BlockSpec(a.shape, lambda bi, j: (0,) * a.ndim)
    small = [vec(mu), vec(w0), w2, vec(a0), a2, g2, vec(k_k), vec(k_a), vec(r_k), vec(lnx_w), vec(lnx_b), hsum, tri]
    return pl.pallas_call(
        functools.partial(_rwkv_kernel, c=c),
        grid=(b, nc),
        in_specs=[pl.BlockSpec((c, RWKV_COLS), lambda bi, j: (bi * nc + j, 0)),
                  pl.BlockSpec((1, 1, RWKV_COLS), lambda bi, j: (bi, 0, 0)),
                  pl.BlockSpec((1, RWKV_HEADS, HEAD_DIM, HEAD_DIM), lambda bi, j: (bi, 0, 0, 0))]
                 + [full(a) for a in small],
        out_specs=[pl.BlockSpec((c, RWKV_WIDTH), lambda bi, j: (bi * nc + j, 0)),
                   pl.BlockSpec((1, RWKV_HEADS, HEAD_DIM, HEAD_DIM), lambda bi, j: (bi, 0, 0, 0))],
        out_shape=[jax.ShapeDtypeStruct((p.shape[0], RWKV_WIDTH), F32),
                   jax.ShapeDtypeStruct((b, RWKV_HEADS, HEAD_DIM, HEAD_DIM), F32)],
        scratch_shapes=[pltpu.VMEM((RWKV_HEADS, HEAD_DIM, HEAD_DIM), F32), pltpu.VMEM((1, RWKV_COLS), F32),
                        pltpu.VMEM((c, RWKV_WIDTH), F32)],
        compiler_params=_cparams(("arbitrary", "arbitrary")),
        name="rwkv",
    )(p, prev, s0, *small)


def _branch(x, mods, pos, prev, s0, caches, weights, tm, tm_moe):
    (g_mix, g_ffn, g_final, w_r, w_a, rw, w_out, rwt, rb, wup, bup, wdn, bdn) = weights
    sh1, sc1, gt1, sh2, sc2, gt2 = mods
    b, t, d = x.shape
    tab, tabk = _rope_tables(pos)
    p_r, q, k, v, qi, kw = _inproj(x, g_mix, sh1, sc1, w_r, w_a, tab, tabk, tm)
    o_r, wkv = _rwkv(p_r, prev, s0, rw, min(CHUNK, t))
    shift = p_r.reshape(b, t, RWKV_COLS)[:, -1:]
    k4 = k.reshape(b, t, KV_HEADS, HEAD_DIM)
    v4 = v.reshape(b, t, KV_HEADS, HEAD_DIM)
    ki = kw.reshape(b, t, KW_WIDTH)[..., :IDX_DIM]
    kt = 1024 if caches is None else 512
    if caches is None:
        k_all, v_all, ki_all, q0, l_valid = k4, v4, ki, 0, t
    else:
        ck, cv, cki = caches
        q0 = ck.shape[1]
        l_valid = q0 + t
        padn = (-l_valid) % kt
        zpad = lambda a: jnp.concatenate([a, jnp.zeros((b, padn) + a.shape[2:], a.dtype)], axis=1)
        k_all = zpad(jnp.concatenate([ck, k4], axis=1))
        v_all = zpad(jnp.concatenate([cv, v4], axis=1))
        ki_all = zpad(jnp.concatenate([cki, ki], axis=1))
    qb = min(Q_BLOCK, t)
    dsa = _dsa_t if qb == LANES else _dsa
    o_a = dsa(q, qi, kw, k_all.astype(BF16), v_all.astype(BF16), ki_all.astype(BF16),
              qb=qb, kt=kt, q0=q0, l_valid=l_valid, n_sel=min(TOPK_MAX, l_valid // 4))
    x1, h2 = _outproj(o_r, o_a, x, w_out, gt1, g_ffn, sh2, sc2, tm)
    y_moe = _moe(h2, rwt, rb, wup, bup, wdn, bdn, *tm_moe)
    y = _final(x1, y_moe, gt2, g_final, b, t, tm)
    return y, k4, v4, ki, wkv, shift


def kernel(x_prompt, x_sample, c_prompt, c_sample, cache_k, cache_v, cache_kidx, state_wkv, state_shift,
           w_ada, b_ada, g_mix, g_ffn, g_final, w_in, mu_shift, w0, w_lora_w, a0, w_lora_a, w_lora_g,
           k_k, k_a, r_k, lnx_w, lnx_b, w_out, router_w, router_b, w_up, b_up, w_down, b_down):
    B, T, D = x_prompt.shape
    DB, DT, _ = x_sample.shape
    P = cache_k.shape[2]
    l = 0
    assert w_ada.shape[0] == 1

    rows = B + DB
    pad = (-rows) % 8
    c_all = jnp.concatenate([c_prompt, c_sample, jnp.zeros((pad, D), F32)], axis=0)
    m = _ada(c_all, w_ada[l], b_ada[l])
    mods_p = tuple(t.reshape(B, 1, D) for t in jnp.split(m[:B], 6, axis=-1))
    mods_s = tuple(t.reshape(DB, 1, D) for t in jnp.split(m[B:rows], 6, axis=-1))

    w_r = w_in[l][:, :RWKV_COLS].astype(BF16)
    w_att = w_in[l][:, RWKV_COLS:]
    att_pad = KW_WIDTH - IDX_DIM - IDX_HEADS
    w_a = jnp.concatenate([w_att, jnp.zeros((D, att_pad), F32)], axis=1).astype(BF16)
    rw = (mu_shift[l], w0[l], w_lora_w[l], a0[l], w_lora_a[l], w_lora_g[l], k_k[l], k_a[l], r_k[l],
          lnx_w[l], lnx_b[l])
    wup = _deinterleave_cast(w_up[l])
    bup = jnp.concatenate([b_up[l][..., 0::2], b_up[l][..., 1::2]], axis=-1).reshape(N_EXPERTS, 1, 2 * D_FF)
    wdn = w_down[l].astype(BF16)
    bdn = b_down[l].reshape(N_EXPERTS, 1, D)
    weights = (g_mix[l], g_ffn[l], g_final, w_r, w_a, rw, w_out[l].astype(BF16), router_w[l].T,
               router_b[l].reshape(N_EXPERTS, 1), wup, bup, wdn, bdn)

    pos_p = jnp.arange(T, dtype=jnp.int32)
    pos_s = P + jnp.arange(DT, dtype=jnp.int32)
    yp, kp, vp, kip, wkvp, shp = _branch(
        x_prompt, mods_p, pos_p, jnp.zeros((B, 1, RWKV_COLS), F32),
        jnp.zeros((B, RWKV_HEADS, HEAD_DIM, HEAD_DIM), F32), None, weights, 512, (MOE_SUB, MOE_NSUB))
    ys, ks, vs, kis, wkvs, shs = _branch(
        x_sample, mods_s, pos_s, state_shift[l], state_wkv[l],
        (cache_k[l], cache_v[l], cache_kidx[l]), weights, DT, (DB * DT, 1))
    return (yp, ys, kp[None], vp[None], kip[None], wkvp[None], shp[None],
            ks[None], vs[None], kis[None], wkvs[None], shs[None])
```

```python
import functools

import numpy as np
import jax
import jax.numpy as jnp
from jax import lax
from jax.experimental import pallas as pl
from jax.experimental.pallas import tpu as pltpu

F32 = jnp.float32
BF16 = jnp.bfloat16
HIGHEST = lax.Precision.HIGHEST

D_MODEL = 1024
CHUNK = 64
HEAD_DIM = 64
RWKV_HEADS = 8
RWKV_WIDTH = RWKV_HEADS * HEAD_DIM
W_LORA = 64
A_LORA = 64
G_LORA = 128
RWKV_COLS = 3 * RWKV_WIDTH + W_LORA + A_LORA + G_LORA
GN_EPS = 64e-5
ATT_HEADS = 8
ATT_WIDTH = ATT_HEADS * HEAD_DIM
KV_HEADS = 2
KV_WIDTH = KV_HEADS * HEAD_DIM
IDX_HEADS = 4
IDX_DIM = 64
TOPK_MAX = 256
Q_BLOCK = 128
ROPE_THETA = 500000.0
ROT_DIM = HEAD_DIM // 4
N_EXPERTS = 32
TOP_K = 4
D_FF = 1024
SWIGLU_ALPHA = 1.702
SWIGLU_LIMIT = 7.0
NORM_EPS = 1e-5

LANES = 128
MOE_ROWS = 128
MOE_SUB = 896
MOE_NSUB = 2
VMEM_LIMIT = 56 * 1024 * 1024

QI_WIDTH = IDX_HEADS * IDX_DIM
KW_WIDTH = LANES


def _cparams(sem):
    return pltpu.CompilerParams(dimension_semantics=sem, vmem_limit_bytes=VMEM_LIMIT)


def _ada_kernel(c_ref, w_ref, b_ref, o_ref):
    c = c_ref[...]
    s = c * jax.nn.sigmoid(c)
    o_ref[...] = jnp.dot(s, w_ref[...], precision=HIGHEST, preferred_element_type=F32) + b_ref[...]


def _ada(c, w, b):
    rows, d = c.shape
    n = w.shape[1]
    tn = 1536
    return pl.pallas_call(
        _ada_kernel,
        grid=(n // tn,),
        in_specs=[pl.BlockSpec((rows, d), lambda j: (0, 0)),
                  pl.BlockSpec((d, tn), lambda j: (0, j)),
                  pl.BlockSpec((1, tn), lambda j: (0, j))],
        out_specs=pl.BlockSpec((rows, tn), lambda j: (0, j)),
        out_shape=jax.ShapeDtypeStruct((rows, n), F32),
        compiler_params=_cparams(("arbitrary",)),
        name="ada",
    )(c, w, b.reshape(1, n))


def _rope_slab(y, tab_ref):
    return (y * tab_ref[0] + pltpu.roll(y, LANES - ROT_DIM // 2, 1) * tab_ref[1]
            + pltpu.roll(y, ROT_DIM // 2, 1) * tab_ref[2])


def _norm_mod(x, g, sh, sc):
    var = jnp.mean(x * x, axis=-1, keepdims=True)
    return (x * lax.rsqrt(var + NORM_EPS) * g) * (1.0 + sc) + sh


def _inproj_kernel(x_ref, g_ref, sh_ref, sc_ref, wr_ref, wa_ref, tab_ref, tabk_ref,
                   rw_ref, q_ref, k_ref, v_ref, qi_ref, kw_ref):
    h = _norm_mod(x_ref[...], g_ref[...], sh_ref[0], sc_ref[0]).astype(BF16)
    rw_ref[...] = jnp.dot(h, wr_ref[...], preferred_element_type=F32)
    pa = jnp.dot(h, wa_ref[...], preferred_element_type=F32)
    off = 0
    for ref, width, rot in ((q_ref, ATT_WIDTH, True), (k_ref, KV_WIDTH, True), (v_ref, KV_WIDTH, False),
                            (qi_ref, QI_WIDTH, True)):
        for s in range(width // LANES):
            slab = pa[:, off + s * LANES: off + (s + 1) * LANES]
            ref[:, s * LANES:(s + 1) * LANES] = _rope_slab(slab, tab_ref) if rot else slab
        off += width
    kw_ref[...] = _rope_slab(pa[:, off:off + LANES], tabk_ref)


def _rope_tables(pos):
    half = ROT_DIM // 2
    inv = ROPE_THETA ** (-jnp.arange(0, ROT_DIM, 2, dtype=F32) / ROT_DIM)
    ang = pos.astype(F32)[:, None] * inv[None, :]
    cos, sin = jnp.cos(ang), jnp.sin(ang)
    t = pos.shape[0]
    one = jnp.ones((t, HEAD_DIM - ROT_DIM), F32)
    zero_r = jnp.zeros((t, HEAD_DIM - ROT_DIM), F32)
    zero_h = jnp.zeros((t, half), F32)
    c_head = jnp.concatenate([cos, cos, one], axis=1)
    up_head = jnp.concatenate([-sin, zero_h, zero_r], axis=1)
    dn_head = jnp.concatenate([zero_h, sin, zero_r], axis=1)
    tab = jnp.stack([jnp.tile(c_head, (1, 2)), jnp.tile(up_head, (1, 2)), jnp.tile(dn_head, (1, 2))])
    wscale = jnp.concatenate([jnp.full((t, IDX_HEADS), IDX_HEADS ** -0.5, F32),
                              jnp.ones((t, HEAD_DIM - IDX_HEADS), F32)], axis=1)
    zero64 = jnp.zeros((t, HEAD_DIM), F32)
    tabk = jnp.stack([jnp.concatenate([c_head, wscale], axis=1),
                      jnp.concatenate([up_head, zero64], axis=1),
                      jnp.concatenate([dn_head, zero64], axis=1)])
    return tab, tabk


def _inproj(x, g, sh, sc, w_r, w_a, tab, tabk, tm):
    b, t, d = x.shape
    nt = t // tm
    n = b * t
    x2 = x.reshape(n, d)
    widths = (RWKV_COLS, ATT_WIDTH, KV_WIDTH, KV_WIDTH, QI_WIDTH, KW_WIDTH)
    row = lambda w: pl.BlockSpec((tm, w), lambda i: (i, 0))
    mod = pl.BlockSpec((1, 1, d), lambda i: (i // nt, 0, 0))
    tabspec = pl.BlockSpec((3, tm, LANES), lambda i: (0, i % nt, 0))
    return pl.pallas_call(
        _inproj_kernel,
        grid=(n // tm,),
        in_specs=[row(d), pl.BlockSpec((1, d), lambda i: (0, 0)), mod, mod,
                  pl.BlockSpec(w_r.shape, lambda i: (0, 0)), pl.BlockSpec(w_a.shape, lambda i: (0, 0)),
                  tabspec, tabspec],
        out_specs=[row(w) for w in widths],
        out_shape=[jax.ShapeDtypeStruct((n, w), F32) for w in widths],
        compiler_params=_cparams(("arbitrary",)),
        name="inproj",
    )(x2, g.reshape(1, d), sh, sc, w_r, w_a, tab, tabk)


def _outproj_kernel(or_ref, oa_ref, x_ref, w_ref, gt_ref, g_ref, sh_ref, sc_ref, x1_ref, h2_ref):
    m = jnp.dot(or_ref[...].astype(BF16), w_ref[:RWKV_WIDTH, :], preferred_element_type=F32)
    m = m + jnp.dot(oa_ref[...].astype(BF16), w_ref[RWKV_WIDTH:, :], preferred_element_type=F32)
    x1 = x_ref[...] + gt_ref[0] * m
    x1_ref[...] = x1
    h2_ref[...] = _norm_mod(x1, g_ref[...], sh_ref[0], sc_ref[0])


def _outproj(o_r, o_a, x, w_out, gt, g, sh, sc, tm):
    b, t, d = x.shape
    nt = t // tm
    n = b * t
    row = lambda w: pl.BlockSpec((tm, w), lambda i: (i, 0))
    mod = pl.BlockSpec((1, 1, d), lambda i: (i // nt, 0, 0))
    return pl.pallas_call(
        _outproj_kernel,
        grid=(n // tm,),
        in_specs=[row(RWKV_WIDTH), row(ATT_WIDTH), row(d), pl.BlockSpec(w_out.shape, lambda i: (0, 0)),
                  mod, pl.BlockSpec((1, d), lambda i: (0, 0)), mod, mod],
        out_specs=[row(d), row(d)],
        out_shape=[jax.ShapeDtypeStruct((n, d), F32)] * 2,
        compiler_params=_cparams(("arbitrary",)),
        name="outproj",
    )(o_r.reshape(n, RWKV_WIDTH), o_a.reshape(n, ATT_WIDTH), x.reshape(n, d), w_out, gt, g.reshape(1, d), sh, sc)


def _moe_kernel(h_ref, rwt_ref, rb_ref, tri_ref, wup_ref, bup_ref, wdn_ref, bdn_ref, y_ref,
                hb_s, rank_s, gate_s, *, n_tokens, sub):
    e = pl.program_id(1)
    n_sub = h_ref.shape[0] // sub

    @pl.when(e == 0)
    def _route():
        rwt = rwt_ref[...].astype(BF16)
        for s in range(n_sub):
            first_row = pl.program_id(0) * (n_sub * sub) + s * sub
            in_rows = first_row + lax.broadcasted_iota(jnp.int32, (sub, 1), 0) < n_tokens
            in_cols = first_row + lax.broadcasted_iota(jnp.int32, (1, sub), 1) < n_tokens
            hb = jnp.where(in_rows, h_ref[s * sub:(s + 1) * sub, :], 0.0).astype(BF16)
            hb_s[s * sub:(s + 1) * sub, :] = hb
            logits = lax.dot_general(rwt, hb, (((1,), (1,)), ((), ())), preferred_element_type=F32) + rb_ref[...]
            eidx = lax.broadcasted_iota(jnp.int32, logits.shape, 0)
            work = logits
            top = None
            for _ in range(TOP_K):
                m = jnp.max(work, axis=0, keepdims=True)
                if top is None:
                    top = m
                first = jnp.min(jnp.where(work == m, eidx, N_EXPERTS), axis=0, keepdims=True)
                work = jnp.where(eidx == first, -jnp.inf, work)
            ex = jnp.where(work != logits, jnp.exp(logits - top), 0.0)
            sel = jnp.logical_and(work != logits, in_cols)
            gate_s[s] = ex / jnp.sum(ex, axis=0, keepdims=True)
            before = jnp.dot(jnp.where(sel, 1.0, 0.0).astype(BF16), tri_ref[...], preferred_element_type=F32)
            rank_s[s] = jnp.where(sel, before, -1.0)
        y_ref[...] = jnp.zeros(y_ref.shape, F32)

    for s in range(n_sub):
        r_row = rank_s[s, pl.ds(e, 1), :]
        g_row = gate_s[s, pl.ds(e, 1), :]
        count = jnp.sum(jnp.where(r_row >= 0.0, 1, 0).astype(jnp.int32))
        n_blocks = (count + MOE_ROWS - 1) // MOE_ROWS

        def block(j, carry, s=s, r_row=r_row, g_row=g_row):
            rows = (lax.broadcasted_iota(jnp.int32, (MOE_ROWS, sub), 0) + j * MOE_ROWS).astype(F32)
            hit = r_row == rows
            p = jnp.where(hit, 1.0, 0.0).astype(BF16)
            xe = jnp.dot(p, hb_s[s * sub:(s + 1) * sub, :], preferred_element_type=F32).astype(BF16)
            u = jnp.dot(xe, wup_ref[0], preferred_element_type=F32) + bup_ref[0]
            glu = jnp.minimum(u[:, :D_FF], SWIGLU_LIMIT)
            lin = jnp.clip(u[:, D_FF:], -SWIGLU_LIMIT, SWIGLU_LIMIT)
            act = glu * jax.nn.sigmoid(SWIGLU_ALPHA * glu) * (lin + 1.0)
            yb = jnp.dot(act.astype(BF16), wdn_ref[0], preferred_element_type=F32) + bdn_ref[0]
            g_rows = jnp.sum(jnp.where(hit, g_row, 0.0), axis=1, keepdims=True)
            ys = (yb * g_rows).astype(BF16)
            y_ref[s * sub:(s + 1) * sub, :] += lax.dot_general(p, ys, (((0,), (0,)), ((), ())),
                                                               preferred_element_type=F32)
            return carry

        lax.fori_loop(0, n_blocks, block, 0)


def _moe(h, rwt, rb, wup, bup, wdn, bdn, sub, n_sub):
    n, d = h.shape
    tm = sub * n_sub
    tri = (lax.broadcasted_iota(jnp.int32, (sub, sub), 0) < lax.broadcasted_iota(jnp.int32, (sub, sub), 1)).astype(BF16)
    return pl.pallas_call(
        functools.partial(_moe_kernel, n_tokens=n, sub=sub),
        grid=(pl.cdiv(n, tm), N_EXPERTS),
        in_specs=[pl.BlockSpec((tm, d), lambda i, e: (i, 0)),
                  pl.BlockSpec((N_EXPERTS, d), lambda i, e: (0, 0)),
                  pl.BlockSpec((N_EXPERTS, 1), lambda i, e: (0, 0)),
                  pl.BlockSpec((sub, sub), lambda i, e: (0, 0)),
                  pl.BlockSpec((1, d, 2 * D_FF), lambda i, e: (e, 0, 0)),
                  pl.BlockSpec((1, 1, 2 * D_FF), lambda i, e: (e, 0, 0)),
                  pl.BlockSpec((1, D_FF, d), lambda i, e: (e, 0, 0)),
                  pl.BlockSpec((1, 1, d), lambda i, e: (e, 0, 0))],
        out_specs=pl.BlockSpec((tm, d), lambda i, e: (i, 0)),
        out_shape=jax.ShapeDtypeStruct((n, d), F32),
        scratch_shapes=[pltpu.VMEM((tm, d), BF16), pltpu.VMEM((n_sub, N_EXPERTS, sub), F32),
                        pltpu.VMEM((n_sub, N_EXPERTS, sub), F32)],
        compiler_params=_cparams(("arbitrary", "arbitrary")),
        name="moe",
    )(h, rwt, rb, tri, wup, bup, wdn, bdn)


def _deinterleave_kernel(w_ref, perm_ref, o_ref):
    o_ref[0] = jnp.dot(w_ref[0].astype(BF16), perm_ref[...], preferred_element_type=F32).astype(BF16)


def _deinterleave_cast(w):
    e, d, n = w.shape
    tk = 512
    src = lax.broadcasted_iota(jnp.int32, (n, n), 0)
    dst = lax.broadcasted_iota(jnp.int32, (n, n), 1)
    perm = (src == jnp.where(dst < n // 2, 2 * dst, 2 * (dst - n // 2) + 1)).astype(BF16)
    return pl.pallas_call(
        _deinterleave_kernel,
        grid=(e, d // tk),
        in_specs=[pl.BlockSpec((1, tk, n), lambda i, j: (i, j, 0)), pl.BlockSpec((n, n), lambda i, j: (0, 0))],
        out_specs=pl.BlockSpec((1, tk, n), lambda i, j: (i, j, 0)),
        out_shape=jax.ShapeDtypeStruct((e, d, n), BF16),
        compiler_params=_cparams(("arbitrary", "arbitrary")),
        name="deinterleave",
    )(w, perm)


def _final_kernel(x1_ref, y_ref, gt_ref, g_ref, o_ref):
    x = x1_ref[...] + gt_ref[0] * y_ref[...]
    var = jnp.mean(x * x, axis=-1, keepdims=True)
    o_ref[...] = x * lax.rsqrt(var + NORM_EPS) * g_ref[...]


def _final(x1, y, gt, g, b, t, tm):
    n, d = x1.shape
    nt = t // tm
    row = pl.BlockSpec((tm, d), lambda i: (i, 0))
    return pl.pallas_call(
        _final_kernel,
        grid=(n // tm,),
        in_specs=[row, row, pl.BlockSpec((1, 1, d), lambda i: (i // nt, 0, 0)),
                  pl.BlockSpec((1, d), lambda i: (0, 0))],
        out_specs=row,
        out_shape=jax.ShapeDtypeStruct((n, d), F32),
        compiler_params=_cparams(("arbitrary",)),
        name="final",
    )(x1, y, gt, g.reshape(1, d)).reshape(b, t, d)


MASKED = -1e30
INT_MIN = -2 ** 31
COUNT_ROWS = 64
ATT_ROWS = 128


def _dsa_kernel(q_ref, qi_ref, kw_ref, k_ref, vx_ref, ki_ref, tri_ref, o_ref, key_s, bias_s, wib_s,
                *, qb, kt, q0, l_valid, n_sel):
    i = pl.program_id(1)
    ns = kt // LANES
    row = lax.broadcasted_iota(jnp.int32, (qb, 1), 0)
    qpos = q0 + i * qb + row
    lim = jnp.minimum((qpos // CHUNK + 1) * CHUNK, l_valid)
    last_lim = jnp.minimum(((q0 + i * qb + qb - 1) // CHUNK + 1) * CHUNK, l_valid)
    n_kt = (last_lim + kt - 1) // kt
    lane = lax.broadcasted_iota(jnp.int32, (1, LANES), 1)

    for h in range(IDX_HEADS):
        w = kw_ref[:, IDX_DIM + h:IDX_DIM + h + 1] * (IDX_DIM ** -0.5)
        wib_s[h] = jnp.broadcast_to(w, (qb, LANES))
    qi = qi_ref[...].astype(BF16)

    def f32_key(x):
        bits = pltpu.bitcast(x, jnp.int32)
        return bits ^ ((bits >> 31) & 0x7FFFFFFF)

    def score_tile(t, c):
        ks = pl.multiple_of(t * kt, kt)
        kit = ki_ref[0, pl.ds(ks, kt), :]
        lg = [lax.dot_general(qi[:, h * IDX_DIM:(h + 1) * IDX_DIM], kit, (((1,), (1,)), ((), ())),
                              preferred_element_type=F32) for h in range(IDX_HEADS)]
        for s in range(ns):
            sc = jnp.zeros((qb, LANES), F32)
            for h in range(IDX_HEADS):
                sc = sc + wib_s[h] * jnp.maximum(lg[h][:, s * LANES:(s + 1) * LANES], 0.0)
            kpos = ks + s * LANES + lane
            sc = jnp.where(kpos < lim, sc + 0.0, -jnp.inf)
            key_s[t, :, s * LANES:(s + 1) * LANES] = f32_key(sc)
        return c

    lax.fori_loop(0, n_kt, score_tile, 0)

    def count(pred_fn):
        def tile(t, acc):
            for s in range(ns):
                acc = acc + jnp.where(pred_fn(key_s[t, :, s * LANES:(s + 1) * LANES]), 1.0, 0.0)
            return acc
        acc = lax.fori_loop(0, n_kt, tile, jnp.zeros((qb, LANES), F32))
        return jnp.sum(acc, axis=1, keepdims=True)

    def bit_step(b, lo):
        cand = lo + jnp.left_shift(jnp.int32(1), 31 - b)
        candb = jnp.broadcast_to(cand, (qb, LANES))
        return jnp.where(count(lambda k: k >= candb) >= n_sel, cand, lo)

    thr = lax.fori_loop(0, 32, bit_step, jnp.full((qb, 1), INT_MIN, jnp.int32))
    thrb = jnp.broadcast_to(thr, (qb, LANES))
    need = jnp.broadcast_to(n_sel - count(lambda k: k > thrb), (qb, LANES))

    def sel_tile(t, off):
        for s in range(ns):
            key = key_s[t, :, s * LANES:(s + 1) * LANES]
            eq = key == thrb
            pre = jnp.dot(jnp.where(eq, 1.0, 0.0).astype(BF16), tri_ref[...], preferred_element_type=F32)
            kpos = t * kt + s * LANES + lane
            keep = jnp.logical_or(key > thrb, jnp.logical_and(eq, pre[:, :LANES] + off < need))
            keep = jnp.logical_and(keep, kpos < lim)
            bias_s[t, :, s * LANES:(s + 1) * LANES] = jnp.where(keep, 0.0, MASKED)
            off = off + pre[:, LANES:]
        return off

    lax.fori_loop(0, n_kt, sel_tile, jnp.zeros((qb, LANES), F32))

    rep = ATT_HEADS // KV_HEADS
    qgs = []
    for g in range(KV_HEADS):
        qg = jnp.concatenate([q_ref[:, (g * rep + r) * HEAD_DIM:(g * rep + r + 1) * HEAD_DIM] for r in range(rep)],
                             axis=0)
        qgs.append((qg * (HEAD_DIM ** -0.5)).astype(BF16))

    def att_tile(t, carry):
        ks = pl.multiple_of(t * kt, kt)
        out = []
        for g in range(KV_HEADS):
            m, acc = carry[g]
            s = lax.dot_general(qgs[g], k_ref[0, g, pl.ds(ks, kt), :], (((1,), (1,)), ((), ())),
                                preferred_element_type=F32)
            s = (s.reshape(rep, qb, kt) + bias_s[t][None]).reshape(rep * qb, kt)
            m_new = jnp.maximum(m, jnp.max(s, axis=1, keepdims=True))
            p = jnp.exp(s - m_new)
            pv = jnp.dot(p.astype(BF16), vx_ref[0, pl.ds(ks, kt), g * LANES:(g + 1) * LANES],
                         preferred_element_type=F32)
            out.append((m_new, acc * jnp.exp(m - m_new) + pv))
        return tuple(out)

    init = (jnp.full((rep * qb, 1), MASKED, F32), jnp.zeros((rep * qb, LANES), F32))
    res = lax.fori_loop(0, n_kt, att_tile, (init,) * KV_HEADS)
    for g in range(KV_HEADS):
        acc = res[g][1]
        out = acc * pltpu.roll(1.0 / acc, HEAD_DIM, 1)
        for r in range(rep):
            h = g * rep + r
            o_ref[:, h * HEAD_DIM:(h + 1) * HEAD_DIM] = out[r * qb:(r + 1) * qb, :HEAD_DIM]


def _dsa(q, qi, kw, k_bf, v_bf, ki_bf, *, qb, kt, q0, l_valid, n_sel):
    b, lp = ki_bf.shape[:2]
    nq = q.shape[0] // (b * qb)
    k_g = jnp.moveaxis(k_bf, 2, 1)
    ones = jnp.ones((b, lp, KV_HEADS, HEAD_DIM), BF16)
    vx = jnp.concatenate([v_bf, ones], axis=-1).reshape(b, lp, KV_HEADS * LANES)
    tr = lax.broadcasted_iota(jnp.int32, (LANES, 2 * LANES), 0)
    tc = lax.broadcasted_iota(jnp.int32, (LANES, 2 * LANES), 1)
    tri = jnp.logical_or(tr < tc, tc >= LANES).astype(BF16)
    row = lambda w: pl.BlockSpec((qb, w), lambda bi, i: (bi * nq + i, 0))
    kern = functools.partial(_dsa_kernel, qb=qb, kt=kt, q0=q0, l_valid=l_valid, n_sel=n_sel)
    return pl.pallas_call(
        kern,
        grid=(b, nq),
        in_specs=[row(ATT_WIDTH), row(QI_WIDTH), row(KW_WIDTH),
                  pl.BlockSpec((1, KV_HEADS, lp, HEAD_DIM), lambda bi, i: (bi, 0, 0, 0)),
                  pl.BlockSpec((1, lp, KV_HEADS * LANES), lambda bi, i: (bi, 0, 0)),
                  pl.BlockSpec((1, lp, IDX_DIM), lambda bi, i: (bi, 0, 0)),
                  pl.BlockSpec((LANES, 2 * LANES), lambda bi, i: (0, 0))],
        out_specs=row(ATT_WIDTH),
        out_shape=jax.ShapeDtypeStruct((q.shape[0], ATT_WIDTH), F32),
        scratch_shapes=[pltpu.VMEM((lp // kt, qb, kt), jnp.int32), pltpu.VMEM((lp // kt, qb, kt), F32),
                        pltpu.VMEM((IDX_HEADS, qb, LANES), F32)],
        compiler_params=_cparams(("arbitrary", "arbitrary")),
        name="dsa",
    )(q, qi, kw, k_g, vx, ki_bf, tri)


def _dsa_t_kernel(q_ref, qi_ref, kw_ref, k_ref, vxt_ref, ki_ref, tri_ref, o_ref, key_s, bias_s, s0_s, s1_s,
                  p0_s, p1_s, acc_s, *, qb, kt, q0, l_valid, n_sel):
    i = pl.program_id(1)
    ns = kt // LANES
    qpos = q0 + i * qb + lax.broadcasted_iota(jnp.int32, (1, qb), 1)
    lim = jnp.minimum((qpos // CHUNK + 1) * CHUNK, l_valid)
    last_lim = jnp.minimum(((q0 + i * qb + qb - 1) // CHUNK + 1) * CHUNK, l_valid)
    n_kt = (last_lim + kt - 1) // kt
    rep = ATT_HEADS // KV_HEADS

    def by_head(x_t, n):
        return jnp.concatenate([x_t[h * HEAD_DIM:(h + 1) * HEAD_DIM, :] for h in range(n)], axis=1)

    w_qi = by_head(qi_ref[...].T, IDX_HEADS).astype(BF16)
    q_t = q_ref[...].T * (HEAD_DIM ** -0.5)
    w_q = [by_head(q_t[g * rep * HEAD_DIM:(g + 1) * rep * HEAD_DIM, :], rep).astype(BF16) for g in range(KV_HEADS)]
    kw_t = kw_ref[...].T
    wi = [kw_t[IDX_DIM + h:IDX_DIM + h + 1, :] * (IDX_DIM ** -0.5) for h in range(IDX_HEADS)]

    def f32_key(x):
        bits = pltpu.bitcast(x, jnp.int32)
        return bits ^ ((bits >> 31) & 0x7FFFFFFF)

    def score_tile(t, c):
        ks = pl.multiple_of(t * kt, kt)
        lg = jnp.dot(ki_ref[0, pl.ds(ks, kt), :], w_qi, preferred_element_type=F32)
        sc = jnp.zeros((kt, qb), F32)
        for h in range(IDX_HEADS):
            sc = sc + wi[h] * jnp.maximum(lg[:, h * qb:(h + 1) * qb], 0.0)
        kpos = ks + lax.broadcasted_iota(jnp.int32, (kt, qb), 0)
        sc = jnp.where(kpos < lim, sc + 0.0, -jnp.inf)
        key_s[pl.ds(ks, kt), :] = f32_key(sc)
        return c

    lax.fori_loop(0, n_kt, score_tile, 0)

    def count(pred_fn):
        def tile(t, acc):
            ks = pl.multiple_of(t * kt, kt)
            for c in range(kt // COUNT_ROWS):
                chunk = key_s[pl.ds(ks + c * COUNT_ROWS, COUNT_ROWS), :]
                acc = acc + jnp.where(pred_fn(chunk), 1.0, 0.0)
            return acc
        acc = lax.fori_loop(0, n_kt, tile, jnp.zeros((COUNT_ROWS, qb), F32))
        return jnp.sum(acc, axis=0, keepdims=True)

    def bit_step(b, lo):
        cand = lo + jnp.left_shift(jnp.int32(1), 31 - b)
        return jnp.where(count(lambda k: k >= cand) >= n_sel, cand, lo)

    thr = lax.fori_loop(0, 32, bit_step, jnp.full((1, qb), INT_MIN, jnp.int32))
    need = n_sel - count(lambda k: k > thr)

    def sel_tile(t, off):
        for s in range(ns):
            ks = pl.multiple_of(t * kt + s * LANES, LANES)
            key = key_s[pl.ds(ks, LANES), :]
            eq = key == thr
            pre = jnp.dot(tri_ref[...], jnp.where(eq, 1.0, 0.0).astype(BF16), preferred_element_type=F32)
            kpos = ks + lax.broadcasted_iota(jnp.int32, (LANES, qb), 0)
            keep = jnp.logical_or(key > thr, jnp.logical_and(eq, pre[:LANES] + off < need))
            keep = jnp.logical_and(keep, kpos < lim)
            bias_s[pl.ds(ks, LANES), :] = jnp.where(keep, 0.0, MASKED)
            off = off + pre[LANES:]
        return off

    lax.fori_loop(0, n_kt, sel_tile, jnp.zeros((LANES, qb), F32))

    nc = kt // ATT_ROWS
    width = rep * qb
    s_refs = (s0_s, s1_s)
    p_refs = (p0_s, p1_s)

    def score_chunk(t, g, c, mx):
        ks = pl.multiple_of(t * kt + c * ATT_ROWS, ATT_ROWS)
        sc = jnp.dot(k_ref[0, g, pl.ds(ks, ATT_ROWS), :], w_q[g], preferred_element_type=F32)
        bias = bias_s[pl.ds(ks, ATT_ROWS), :]
        sc = jnp.concatenate([sc[:, r * qb:(r + 1) * qb] + bias for r in range(rep)], axis=1)
        s_refs[g][c * ATT_ROWS:(c + 1) * ATT_ROWS, :] = sc
        return jnp.maximum(mx, jnp.max(sc.reshape(ATT_ROWS // 8, 8, width), axis=0))

    def prob_chunk(g, c, m_new):
        rows = slice(c * ATT_ROWS, (c + 1) * ATT_ROWS)
        p_refs[g][rows, :] = jnp.exp(s_refs[g][rows, :] - m_new).astype(BF16)

    def finish(t, g, m, m_new):
        pv = jnp.dot(vxt_ref[0, t, g * LANES:(g + 1) * LANES, :], p_refs[g][...], preferred_element_type=F32)
        acc_s[g] = acc_s[g] * jnp.exp(m - m_new) + pv

    mx_init = jnp.full((8, width), MASKED, F32)

    def att_tile(t, carry):
        m0, m1, mx0 = carry
        m0_new = jnp.maximum(m0, jnp.max(mx0, axis=0, keepdims=True))
        mx1 = mx_init
        for c in range(nc):
            mx1 = score_chunk(t, 1, c, mx1)
            prob_chunk(0, c, m0_new)
        m1_new = jnp.maximum(m1, jnp.max(mx1, axis=0, keepdims=True))
        finish(t, 0, m0, m0_new)
        t_next = jnp.minimum(t + 1, n_kt - 1)
        mx0 = mx_init
        for c in range(nc):
            mx0 = score_chunk(t_next, 0, c, mx0)
            prob_chunk(1, c, m1_new)
        finish(t, 1, m1, m1_new)
        return m0_new, m1_new, mx0

    acc_s[...] = jnp.zeros(acc_s.shape, F32)
    mx0 = mx_init
    for c in range(nc):
        mx0 = score_chunk(0, 0, c, mx0)
    m_init = jnp.full((1, width), MASKED, F32)
    lax.fori_loop(0, n_kt, att_tile, (m_init, m_init, mx0))
    for g in range(KV_HEADS):
        acc = acc_s[g]
        out = acc * (1.0 / acc[HEAD_DIM:HEAD_DIM + 1, :])
        for r in range(rep):
            h = g * rep + r
            o_ref[:, h * HEAD_DIM:(h + 1) * HEAD_DIM] = out[:, r * qb:(r + 1) * qb].T[:, :HEAD_DIM]


def _dsa_t(q, qi, kw, k_bf, v_bf, ki_bf, *, qb, kt, q0, l_valid, n_sel):
    assert qb == LANES
    b, lp = ki_bf.shape[:2]
    nq = q.shape[0] // (b * qb)
    k_g = jnp.moveaxis(k_bf, 2, 1)
    ones = jnp.ones((b, lp, KV_HEADS, HEAD_DIM), BF16)
    vx = jnp.concatenate([v_bf, ones], axis=-1).reshape(b, lp // kt, kt, KV_HEADS * LANES)
    vxt = jnp.swapaxes(vx, 2, 3)
    tr = lax.broadcasted_iota(jnp.int32, (2 * LANES, LANES), 0)
    tc = lax.broadcasted_iota(jnp.int32, (2 * LANES, LANES), 1)
    tri = jnp.logical_or(tc < tr, tr >= LANES).astype(BF16)
    row = lambda w: pl.BlockSpec((qb, w), lambda bi, i: (bi * nq + i, 0))
    kern = functools.partial(_dsa_t_kernel, qb=qb, kt=kt, q0=q0, l_valid=l_valid, n_sel=n_sel)
    return pl.pallas_call(
        kern,
        grid=(b, nq),
        in_specs=[row(ATT_WIDTH), row(QI_WIDTH), row(KW_WIDTH),
                  pl.BlockSpec((1, KV_HEADS, lp, HEAD_DIM), lambda bi, i: (bi, 0, 0, 0)),
                  pl.BlockSpec((1, lp // kt, KV_HEADS * LANES, kt), lambda bi, i: (bi, 0, 0, 0)),
                  pl.BlockSpec((1, lp, IDX_DIM), lambda bi, i: (bi, 0, 0)),
                  pl.BlockSpec((2 * LANES, LANES), lambda bi, i: (0, 0))],
        out_specs=row(ATT_WIDTH),
        out_shape=jax.ShapeDtypeStruct((q.shape[0], ATT_WIDTH), F32),
        scratch_shapes=[pltpu.VMEM((lp, qb), jnp.int32), pltpu.VMEM((lp, qb), F32),
                        pltpu.VMEM((kt, 4 * qb), F32), pltpu.VMEM((kt, 4 * qb), F32),
                        pltpu.VMEM((kt, 4 * qb), BF16), pltpu.VMEM((kt, 4 * qb), BF16),
                        pltpu.VMEM((KV_HEADS, LANES, 4 * qb), F32)],
        compiler_params=_cparams(("arbitrary", "arbitrary")),
        name="dsa_t",
    )(q, qi, kw, k_g, vxt, ki_bf, tri)


def _split_bf16(x):
    hi = x.astype(BF16)
    return hi, (x - hi.astype(F32)).astype(BF16)


_NN = (((1,), (0,)), ((), ()))


def _dots(a_sp, b_sp, dims=_NN):
    (ah, al), (bh, bl) = a_sp, b_sp
    d = functools.partial(lax.dot_general, dimension_numbers=dims, preferred_element_type=F32)
    return d(ah, bh) + (d(ah, bl) + d(al, bh))


def _dot3(a, b, dims=_NN):
    return _dots(_split_bf16(a), _split_bf16(b), dims)


def _dot2(a, b_exact):
    ah, al = _split_bf16(a)
    return jnp.dot(ah, b_exact, preferred_element_type=F32) + jnp.dot(al, b_exact, preferred_element_type=F32)


_NT = (((1,), (1,)), ((), ()))
_TN = (((0,), (0,)), ((), ()))


def _rwkv_kernel(p_ref, prev_ref, s0_ref, mu_ref, w0_ref, w2_ref, a0_ref, a2_ref, g2_ref, kk_ref, ka_ref,
                 rk_ref, lnw_ref, lnb_ref, hsum_ref, tri_ref, o_ref, st_ref, s_s, prev_s, o_s, *, c):
    j = pl.program_id(1)

    @pl.when(j == 0)
    def _init():
        s_s[...] = s0_ref[0]
        prev_s[...] = prev_ref[0]

    p = p_ref[...]
    row = lax.broadcasted_iota(jnp.int32, (c, 1), 0)
    p_prev = jnp.where(row == 0, prev_s[...], pltpu.roll(p, 1, 0))
    prev_s[...] = p[c - 1:c, :]
    xs = p + (p_prev - p) * mu_ref[...]
    o1, o2, o3 = RWKV_WIDTH, 2 * RWKV_WIDTH, 3 * RWKV_WIDTH
    r, k, v = xs[:, :o1], xs[:, o1:o2], xs[:, o2:o3]
    dw = xs[:, o3:o3 + W_LORA]
    da = xs[:, o3 + W_LORA:o3 + W_LORA + A_LORA]
    dg = xs[:, o3 + W_LORA + A_LORA:]
    hsum = hsum_ref[...]
    w_log = -jax.nn.softplus(-(w0_ref[...] + _dot3(jnp.tanh(dw), w2_ref[...]))) - 0.5
    lw = -jnp.exp(w_log)
    a = jax.nn.sigmoid(a0_ref[...] + _dot3(da, a2_ref[...]))
    g = _dot3(jax.nn.sigmoid(dg), g2_ref[...])
    kk = k * kk_ref[...]
    kk = kk / jnp.maximum(jnp.sqrt(_dot2(kk * kk, hsum)), 1e-12)
    km = k * (1.0 + (a - 1.0) * ka_ref[...])
    bm = kk * a
    lw_hi, lw_lo = _split_bf16(lw)
    tri = tri_ref[...]
    cum = (jnp.dot(tri, lw_hi, preferred_element_type=F32)
           + jnp.dot(tri, lw_lo, preferred_element_type=F32))
    tot = cum[c - 1:c, :]
    e_in = jnp.exp(cum)
    e_out = jnp.exp(-cum)
    e_end = jnp.exp(tot - cum)
    a_t = -kk * jnp.exp(cum - lw)
    r_t = r * e_in
    b_t = bm * e_out
    k_t = km * e_out
    b_h = bm * e_end
    k_h = km * e_end
    gam = jnp.exp(tot)

    ri = lax.broadcasted_iota(jnp.int32, (c, c), 0)
    ci = lax.broadcasted_iota(jnp.int32, (c, c), 1)
    strict = ri > ci
    incl = ri >= ci
    eye = jnp.where(ri == ci, 1.0, 0.0)
    n_double = int(np.log2(c)) - 1
    heads = range(RWKV_HEADS)
    sls = [slice(h * HEAD_DIM, (h + 1) * HEAD_DIM) for h in heads]
    s0 = [s_s[h] for h in heads]
    s0_sp = [_split_bf16(s) for s in s0]
    ar_sp = [_split_bf16(jnp.concatenate([a_t[:, sl], r_t[:, sl]], axis=0)) for sl in sls]
    bk_sp = [_split_bf16(jnp.concatenate([b_t[:, sl], k_t[:, sl]], axis=0)) for sl in sls]
    v_sp = [_split_bf16(v[:, sl]) for sl in sls]
    m = [_dots(ar_sp[h], bk_sp[h], _NT) for h in heads]
    as0 = [_dots(ar_sp[h], s0_sp[h], _NT) for h in heads]
    x_sp = [_split_bf16(jnp.where(strict, m[h][:c, :c], 0.0)) for h in heads]
    lak_sp = [_split_bf16(jnp.where(strict, m[h][:c, c:], 0.0)) for h in heads]
    tinv = [eye + jnp.where(strict, m[h][:c, :c], 0.0) for h in heads]
    rhs = [as0[h][:c] + _dots(lak_sp[h], v_sp[h]) for h in heads]
    for _ in range(n_double):
        x_sp = [_split_bf16(_dots(x_sp[h], x_sp[h])) for h in heads]
        tinv = [tinv[h] + _dots(_split_bf16(tinv[h]), x_sp[h]) for h in heads]
    u = [_dot3(tinv[h], rhs[h]) for h in heads]
    u_sp = [_split_bf16(uh) for uh in u]
    for h in heads:
        m_rb = jnp.where(incl, m[h][c:, :c], 0.0)
        m_rk = jnp.where(incl, m[h][c:, c:], 0.0)
        o_s[:, sls[h]] = as0[h][c:] + _dots(_split_bf16(m_rb), u_sp[h]) + _dots(_split_bf16(m_rk), v_sp[h])
    s_new = [s0[h] * gam[:, sls[h]] + _dots(u_sp[h], _split_bf16(b_h[:, sls[h]]), _TN)
             + _dots(v_sp[h], _split_bf16(k_h[:, sls[h]]), _TN) for h in heads]
    for h in heads:
        s_s[h] = s_new[h]

    o = o_s[...]
    inv_d = 1.0 / HEAD_DIM
    mean = _dot2(o, hsum) * inv_d
    cen = o - mean
    var = _dot2(cen * cen, hsum) * inv_d
    on = cen * lax.rsqrt(var + GN_EPS) * lnw_ref[...] + lnb_ref[...]
    bonus = _dot2(r * km * rk_ref[...], hsum) * v
    o_ref[...] = (on + bonus) * g

    @pl.when(j == pl.num_programs(1) - 1)
    def _fin():
        st_ref[0] = s_s[...]


def _rwkv(p, prev, s0, rw, c):
    mu, w0, w2, a0, a2, g2, k_k, k_a, r_k, lnx_w, lnx_b = rw
    b = prev.shape[0]
    nc = p.shape[0] // (b * c)
    hid = lax.broadcasted_iota(jnp.int32, (RWKV_WIDTH, RWKV_WIDTH), 0) // HEAD_DIM
    hsum = (hid == hid.T).astype(BF16)
    tri = (lax.broadcasted_iota(jnp.int32, (c, c), 0) >= lax.broadcasted_iota(jnp.int32, (c, c), 1)).astype(BF16)
    vec = lambda a: a.reshape(1, -1)
    full = lambda a: pl.BlockSpec(a.shape, lambda bi, j: (0,) * a.ndim)
    small = [vec(mu), vec(w0), w2, vec(a0), a2, g2, vec(k_k), vec(k_a), vec(r_k), vec(lnx_w), vec(lnx_b), hsum, tri]
    return pl.pallas_call(
        functools.partial(_rwkv_kernel, c=c),
        grid=(b, nc),
        in_specs=[pl.BlockSpec((c, RWKV_COLS), lambda bi, j: (bi * nc + j, 0)),
                  pl.BlockSpec((1, 1, RWKV_COLS), lambda bi, j: (bi, 0, 0)),
                  pl.BlockSpec((1, RWKV_HEADS, HEAD_DIM, HEAD_DIM), lambda bi, j: (bi, 0, 0, 0))]
                 + [full(a) for a in small],
        out_specs=[pl.BlockSpec((c, RWKV_WIDTH), lambda bi, j: (bi * nc + j, 0)),
                   pl.BlockSpec((1, RWKV_HEADS, HEAD_DIM, HEAD_DIM), lambda bi, j: (bi, 0, 0, 0))],
        out_shape=[jax.ShapeDtypeStruct((p.shape[0], RWKV_WIDTH), F32),
                   jax.ShapeDtypeStruct((b, RWKV_HEADS, HEAD_DIM, HEAD_DIM), F32)],
        scratch_shapes=[pltpu.VMEM((RWKV_HEADS, HEAD_DIM, HEAD_DIM), F32), pltpu.VMEM((1, RWKV_COLS), F32),
                        pltpu.VMEM((c, RWKV_WIDTH), F32)],
        compiler_params=_cparams(("arbitrary", "arbitrary")),
        name="rwkv",
    )(p, prev, s0, *small)


def _branch(x, mods, pos, prev, s0, caches, weights, tm, tm_moe):
    (g_mix, g_ffn, g_final, w_r, w_a, rw, w_out, rwt, rb, wup, bup, wdn, bdn) = weights
    sh1, sc1, gt1, sh2, sc2, gt2 = mods
    b, t, d = x.shape
    tab, tabk = _rope_tables(pos)
    p_r, q, k, v, qi, kw = _inproj(x, g_mix, sh1, sc1, w_r, w_a, tab, tabk, tm)
    o_r, wkv = _rwkv(p_r, prev, s0, rw, min(CHUNK, t))
    shift = p_r.reshape(b, t, RWKV_COLS)[:, -1:]
    k4 = k.reshape(b, t, KV_HEADS, HEAD_DIM)
    v4 = v.reshape(b, t, KV_HEADS, HEAD_DIM)
    ki = kw.reshape(b, t, KW_WIDTH)[..., :IDX_DIM]
    kt = 1024 if caches is None else 512
    if caches is None:
        k_all, v_all, ki_all, q0, l_valid = k4, v4, ki, 0, t
    else:
        ck, cv, cki = caches
        q0 = ck.shape[1]
        l_valid = q0 + t
        padn = (-l_valid) % kt
        zpad = lambda a: jnp.concatenate([a, jnp.zeros((b, padn) + a.shape[2:], a.dtype)], axis=1)
        k_all = zpad(jnp.concatenate([ck, k4], axis=1))
        v_all = zpad(jnp.concatenate([cv, v4], axis=1))
        ki_all = zpad(jnp.concatenate([cki, ki], axis=1))
    qb = min(Q_BLOCK, t)
    dsa = _dsa_t if qb == LANES else _dsa
    o_a = dsa(q, qi, kw, k_all.astype(BF16), v_all.astype(BF16), ki_all.astype(BF16),
              qb=qb, kt=kt, q0=q0, l_valid=l_valid, n_sel=min(TOPK_MAX, l_valid // 4))
    x1, h2 = _outproj(o_r, o_a, x, w_out, gt1, g_ffn, sh2, sc2, tm)
    y_moe = _moe(h2, rwt, rb, wup, bup, wdn, bdn, *tm_moe)
    y = _final(x1, y_moe, gt2, g_final, b, t, tm)
    return y, k4, v4, ki, wkv, shift


def kernel(x_prompt, x_sample, c_prompt, c_sample, cache_k, cache_v, cache_kidx, state_wkv, state_shift,
           w_ada, b_ada, g_mix, g_ffn, g_final, w_in, mu_shift, w0, w_lora_w, a0, w_lora_a, w_lora_g,
           k_k, k_a, r_k, lnx_w, lnx_b, w_out, router_w, router_b, w_up, b_up, w_down, b_down):
    B, T, D = x_prompt.shape
    DB, DT, _ = x_sample.shape
    P = cache_k.shape[2]
    l = 0
    assert w_ada.shape[0] == 1

    rows = B + DB
    pad = (-rows) % 8
    c_all = jnp.concatenate([c_prompt, c_sample, jnp.zeros((pad, D), F32)], axis=0)
    m = _ada(c_all, w_ada[l], b_ada[l])
    mods_p = tuple(t.reshape(B, 1, D) for t in jnp.split(m[:B], 6, axis=-1))
    mods_s = tuple(t.reshape(DB, 1, D) for t in jnp.split(m[B:rows], 6, axis=-1))

    w_r = w_in[l][:, :RWKV_COLS].astype(BF16)
    w_att = w_in[l][:, RWKV_COLS:]
    att_pad = KW_WIDTH - IDX_DIM - IDX_HEADS
    w_a = jnp.concatenate([w_att, jnp.zeros((D, att_pad), F32)], axis=1).astype(BF16)
    rw = (mu_shift[l], w0[l], w_lora_w[l], a0[l], w_lora_a[l], w_lora_g[l], k_k[l], k_a[l], r_k[l],
          lnx_w[l], lnx_b[l])
    wup = _deinterleave_cast(w_up[l])
    bup = jnp.concatenate([b_up[l][..., 0::2], b_up[l][..., 1::2]], axis=-1).reshape(N_EXPERTS, 1, 2 * D_FF)
    wdn = w_down[l].astype(BF16)
    bdn = b_down[l].reshape(N_EXPERTS, 1, D)
    weights = (g_mix[l], g_ffn[l], g_final, w_r, w_a, rw, w_out[l].astype(BF16), router_w[l].T,
               router_b[l].reshape(N_EXPERTS, 1), wup, bup, wdn, bdn)

    pos_p = jnp.arange(T, dtype=jnp.int32)
    pos_s = P + jnp.arange(DT, dtype=jnp.int32)
    yp, kp, vp, kip, wkvp, shp = _branch(
        x_prompt, mods_p, pos_p, jnp.zeros((B, 1, RWKV_COLS), F32),
        jnp.zeros((B, RWKV_HEADS, HEAD_DIM, HEAD_DIM), F32), None, weights, 512, (MOE_SUB, MOE_NSUB))
    ys, ks, vs, kis, wkvs, shs = _branch(
        x_sample, mods_s, pos_s, state_shift[l], state_wkv[l],
        (cache_k[l], cache_v[l], cache_kidx[l]), weights, DT, (DB * DT, 1))
    return (yp, ys, kp[None], vp[None], kip[None], wkvp[None], shp[None],
            ks[None], vs[None], kis[None], wkvs[None], shs[None])
```

```python
import functools

import numpy as np
import jax
import jax.numpy as jnp
from jax import lax
from jax.experimental import pallas as pl
from jax.experimental.pallas import tpu as pltpu

F32 = jnp.float32
BF16 = jnp.bfloat16
HIGHEST = lax.Precision.HIGHEST

D_MODEL = 1024
CHUNK = 64
HEAD_DIM = 64
RWKV_HEADS = 8
RWKV_WIDTH = RWKV_HEADS * HEAD_DIM
W_LORA = 64
A_LORA = 64
G_LORA = 128
RWKV_COLS = 3 * RWKV_WIDTH + W_LORA + A_LORA + G_LORA
GN_EPS = 64e-5
ATT_HEADS = 8
ATT_WIDTH = ATT_HEADS * HEAD_DIM
KV_HEADS = 2
KV_WIDTH = KV_HEADS * HEAD_DIM
IDX_HEADS = 4
IDX_DIM = 64
TOPK_MAX = 256
Q_BLOCK = 128
ROPE_THETA = 500000.0
ROT_DIM = HEAD_DIM // 4
N_EXPERTS = 32
TOP_K = 4
D_FF = 1024
SWIGLU_ALPHA = 1.702
SWIGLU_LIMIT = 7.0
NORM_EPS = 1e-5

LANES = 128
MOE_ROWS = 128
MOE_SUB = 768
MOE_NSUB = 2
VMEM_LIMIT = 56 * 1024 * 1024

QI_WIDTH = IDX_HEADS * IDX_DIM
KW_WIDTH = LANES


def _cparams(sem):
    return pltpu.CompilerParams(dimension_semantics=sem, vmem_limit_bytes=VMEM_LIMIT)


def _ada_kernel(c_ref, w_ref, b_ref, o_ref):
    c = c_ref[...]
    s = c * jax.nn.sigmoid(c)
    o_ref[...] = jnp.dot(s, w_ref[...], precision=HIGHEST, preferred_element_type=F32) + b_ref[...]


def _ada(c, w, b):
    rows, d = c.shape
    n = w.shape[1]
    tn = 1536
    return pl.pallas_call(
        _ada_kernel,
        grid=(n // tn,),
        in_specs=[pl.BlockSpec((rows, d), lambda j: (0, 0)),
                  pl.BlockSpec((d, tn), lambda j: (0, j)),
                  pl.BlockSpec((1, tn), lambda j: (0, j))],
        out_specs=pl.BlockSpec((rows, tn), lambda j: (0, j)),
        out_shape=jax.ShapeDtypeStruct((rows, n), F32),
        compiler_params=_cparams(("arbitrary",)),
        name="ada",
    )(c, w, b.reshape(1, n))


def _rope_slab(y, tab_ref):
    return (y * tab_ref[0] + pltpu.roll(y, LANES - ROT_DIM // 2, 1) * tab_ref[1]
            + pltpu.roll(y, ROT_DIM // 2, 1) * tab_ref[2])


def _norm_mod(x, g, sh, sc):
    var = jnp.mean(x * x, axis=-1, keepdims=True)
    return (x * lax.rsqrt(var + NORM_EPS) * g) * (1.0 + sc) + sh


def _inproj_kernel(x_ref, g_ref, sh_ref, sc_ref, wr_ref, wa_ref, tab_ref, tabk_ref,
                   rw_ref, q_ref, k_ref, v_ref, qi_ref, kw_ref):
    h = _norm_mod(x_ref[...], g_ref[...], sh_ref[0], sc_ref[0]).astype(BF16)
    rw_ref[...] = jnp.dot(h, wr_ref[...], preferred_element_type=F32)
    pa = jnp.dot(h, wa_ref[...], preferred_element_type=F32)
    off = 0
    for ref, width, rot in ((q_ref, ATT_WIDTH, True), (k_ref, KV_WIDTH, True), (v_ref, KV_WIDTH, False),
                            (qi_ref, QI_WIDTH, True)):
        for s in range(width // LANES):
            slab = pa[:, off + s * LANES: off + (s + 1) * LANES]
            ref[:, s * LANES:(s + 1) * LANES] = _rope_slab(slab, tab_ref) if rot else slab
        off += width
    kw_ref[...] = _rope_slab(pa[:, off:off + LANES], tabk_ref)


def _rope_tables(pos):
    half = ROT_DIM // 2
    inv = ROPE_THETA ** (-jnp.arange(0, ROT_DIM, 2, dtype=F32) / ROT_DIM)
    ang = pos.astype(F32)[:, None] * inv[None, :]
    cos, sin = jnp.cos(ang), jnp.sin(ang)
    t = pos.shape[0]
    one = jnp.ones((t, HEAD_DIM - ROT_DIM), F32)
    zero_r = jnp.zeros((t, HEAD_DIM - ROT_DIM), F32)
    zero_h = jnp.zeros((t, half), F32)
    c_head = jnp.concatenate([cos, cos, one], axis=1)
    up_head = jnp.concatenate([-sin, zero_h, zero_r], axis=1)
    dn_head = jnp.concatenate([zero_h, sin, zero_r], axis=1)
    tab = jnp.stack([jnp.tile(c_head, (1, 2)), jnp.tile(up_head, (1, 2)), jnp.tile(dn_head, (1, 2))])
    wscale = jnp.concatenate([jnp.full((t, IDX_HEADS), IDX_HEADS ** -0.5, F32),
                              jnp.ones((t, HEAD_DIM - IDX_HEADS), F32)], axis=1)
    zero64 = jnp.zeros((t, HEAD_DIM), F32)
    tabk = jnp.stack([jnp.concatenate([c_head, wscale], axis=1),
                      jnp.concatenate([up_head, zero64], axis=1),
                      jnp.concatenate([dn_head, zero64], axis=1)])
    return tab, tabk


def _inproj(x, g, sh, sc, w_r, w_a, tab, tabk, tm):
    b, t, d = x.shape
    nt = t // tm
    n = b * t
    x2 = x.reshape(n, d)
    widths = (RWKV_COLS, ATT_WIDTH, KV_WIDTH, KV_WIDTH, QI_WIDTH, KW_WIDTH)
    row = lambda w: pl.BlockSpec((tm, w), lambda i: (i, 0))
    mod = pl.BlockSpec((1, 1, d), lambda i: (i // nt, 0, 0))
    tabspec = pl.BlockSpec((3, tm, LANES), lambda i: (0, i % nt, 0))
    return pl.pallas_call(
        _inproj_kernel,
        grid=(n // tm,),
        in_specs=[row(d), pl.BlockSpec((1, d), lambda i: (0, 0)), mod, mod,
                  pl.BlockSpec(w_r.shape, lambda i: (0, 0)), pl.BlockSpec(w_a.shape, lambda i: (0, 0)),
                  tabspec, tabspec],
        out_specs=[row(w) for w in widths],
        out_shape=[jax.ShapeDtypeStruct((n, w), F32) for w in widths],
        compiler_params=_cparams(("arbitrary",)),
        name="inproj",
    )(x2, g.reshape(1, d), sh, sc, w_r, w_a, tab, tabk)


def _outproj_kernel(or_ref, oa_ref, x_ref, w_ref, gt_ref, g_ref, sh_ref, sc_ref, x1_ref, h2_ref):
    m = jnp.dot(or_ref[...].astype(BF16), w_ref[:RWKV_WIDTH, :], preferred_element_type=F32)
    m = m + jnp.dot(oa_ref[...].astype(BF16), w_ref[RWKV_WIDTH:, :], preferred_element_type=F32)
    x1 = x_ref[...] + gt_ref[0] * m
    x1_ref[...] = x1
    h2_ref[...] = _norm_mod(x1, g_ref[...], sh_ref[0], sc_ref[0])


def _outproj(o_r, o_a, x, w_out, gt, g, sh, sc, tm):
    b, t, d = x.shape
    nt = t // tm
    n = b * t
    row = lambda w: pl.BlockSpec((tm, w), lambda i: (i, 0))
    mod = pl.BlockSpec((1, 1, d), lambda i: (i // nt, 0, 0))
    return pl.pallas_call(
        _outproj_kernel,
        grid=(n // tm,),
        in_specs=[row(RWKV_WIDTH), row(ATT_WIDTH), row(d), pl.BlockSpec(w_out.shape, lambda i: (0, 0)),
                  mod, pl.BlockSpec((1, d), lambda i: (0, 0)), mod, mod],
        out_specs=[row(d), row(d)],
        out_shape=[jax.ShapeDtypeStruct((n, d), F32)] * 2,
        compiler_params=_cparams(("arbitrary",)),
        name="outproj",
    )(o_r.reshape(n, RWKV_WIDTH), o_a.reshape(n, ATT_WIDTH), x.reshape(n, d), w_out, gt, g.reshape(1, d), sh, sc)


def _moe_kernel(h_ref, rwt_ref, rb_ref, tri_ref, wup_ref, bup_ref, wdn_ref, bdn_ref, y_ref,
                hb_s, rank_s, gate_s, *, n_tokens, sub):
    e = pl.program_id(1)
    n_sub = h_ref.shape[0] // sub

    @pl.when(e == 0)
    def _route():
        rwt = rwt_ref[...].astype(BF16)
        for s in range(n_sub):
            first_row = pl.program_id(0) * (n_sub * sub) + s * sub
            in_rows = first_row + lax.broadcasted_iota(jnp.int32, (sub, 1), 0) < n_tokens
            in_cols = first_row + lax.broadcasted_iota(jnp.int32, (1, sub), 1) < n_tokens
            hb = jnp.where(in_rows, h_ref[s * sub:(s + 1) * sub, :], 0.0).astype(BF16)
            hb_s[s * sub:(s + 1) * sub, :] = hb
            logits = lax.dot_general(rwt, hb, (((1,), (1,)), ((), ())), preferred_element_type=F32) + rb_ref[...]
            eidx = lax.broadcasted_iota(jnp.int32, logits.shape, 0)
            work = logits
            top = None
            for _ in range(TOP_K):
                m = jnp.max(work, axis=0, keepdims=True)
                if top is None:
                    top = m
                first = jnp.min(jnp.where(work == m, eidx, N_EXPERTS), axis=0, keepdims=True)
                work = jnp.where(eidx == first, -jnp.inf, work)
            ex = jnp.where(work != logits, jnp.exp(logits - top), 0.0)
            sel = jnp.logical_and(work != logits, in_cols)
            gate_s[s] = ex / jnp.sum(ex, axis=0, keepdims=True)
            before = jnp.dot(jnp.where(sel, 1.0, 0.0).astype(BF16), tri_ref[...], preferred_element_type=F32)
            rank_s[s] = jnp.where(sel, before, -1.0)
        y_ref[...] = jnp.zeros(y_ref.shape, F32)

    for s in range(n_sub):
        r_row = rank_s[s, pl.ds(e, 1), :]
        g_row = gate_s[s, pl.ds(e, 1), :]
        count = jnp.sum(jnp.where(r_row >= 0.0, 1, 0).astype(jnp.int32))
        n_blocks = (count + MOE_ROWS - 1) // MOE_ROWS

        def block(j, carry, s=s, r_row=r_row, g_row=g_row):
            rows = (lax.broadcasted_iota(jnp.int32, (MOE_ROWS, sub), 0) + j * MOE_ROWS).astype(F32)
            hit = r_row == rows
            p = jnp.where(hit, 1.0, 0.0).astype(BF16)
            xe = jnp.dot(p, hb_s[s * sub:(s + 1) * sub, :], preferred_element_type=F32).astype(BF16)
            u = jnp.dot(xe, wup_ref[0], preferred_element_type=F32) + bup_ref[0]
            glu = jnp.minimum(u[:, :D_FF], SWIGLU_LIMIT)
            lin = jnp.clip(u[:, D_FF:], -SWIGLU_LIMIT, SWIGLU_LIMIT)
            act = glu * jax.nn.sigmoid(SWIGLU_ALPHA * glu) * (lin + 1.0)
            yb = jnp.dot(act.astype(BF16), wdn_ref[0], preferred_element_type=F32) + bdn_ref[0]
            g_rows = jnp.sum(jnp.where(hit, g_row, 0.0), axis=1, keepdims=True)
            ys = (yb * g_rows).astype(BF16)
            y_ref[s * sub:(s + 1) * sub, :] += lax.dot_general(p, ys, (((0,), (0,)), ((), ())),
                                                               preferred_element_type=F32)
            return carry

        lax.fori_loop(0, n_blocks, block, 0)


def _moe(h, rwt, rb, wup, bup, wdn, bdn, sub, n_sub):
    n, d = h.shape
    tm = sub * n_sub
    tri = (lax.broadcasted_iota(jnp.int32, (sub, sub), 0) < lax.broadcasted_iota(jnp.int32, (sub, sub), 1)).astype(BF16)
    return pl.pallas_call(
        functools.partial(_moe_kernel, n_tokens=n, sub=sub),
        grid=(pl.cdiv(n, tm), N_EXPERTS),
        in_specs=[pl.BlockSpec((tm, d), lambda i, e: (i, 0)),
                  pl.BlockSpec((N_EXPERTS, d), lambda i, e: (0, 0)),
                  pl.BlockSpec((N_EXPERTS, 1), lambda i, e: (0, 0)),
                  pl.BlockSpec((sub, sub), lambda i, e: (0, 0)),
                  pl.BlockSpec((1, d, 2 * D_FF), lambda i, e: (e, 0, 0)),
                  pl.BlockSpec((1, 1, 2 * D_FF), lambda i, e: (e, 0, 0)),
                  pl.BlockSpec((1, D_FF, d), lambda i, e: (e, 0, 0)),
                  pl.BlockSpec((1, 1, d), lambda i, e: (e, 0, 0))],
        out_specs=pl.BlockSpec((tm, d), lambda i, e: (i, 0)),
        out_shape=jax.ShapeDtypeStruct((n, d), F32),
        scratch_shapes=[pltpu.VMEM((tm, d), BF16), pltpu.VMEM((n_sub, N_EXPERTS, sub), F32),
                        pltpu.VMEM((n_sub, N_EXPERTS, sub), F32)],
        compiler_params=_cparams(("arbitrary", "arbitrary")),
        name="moe",
    )(h, rwt, rb, tri, wup, bup, wdn, bdn)


def _deinterleave_kernel(w_ref, perm_ref, o_ref):
    n = w_ref.shape[2]
    blk = 2 * LANES
    for j in range(n // blk):
        r = jnp.dot(w_ref[0, :, j * blk:(j + 1) * blk].astype(BF16), perm_ref[...],
                    preferred_element_type=F32).astype(BF16)
        o_ref[0, :, j * LANES:(j + 1) * LANES] = r[:, :LANES]
        o_ref[0, :, n // 2 + j * LANES:n // 2 + (j + 1) * LANES] = r[:, LANES:]


def _deinterleave_cast(w):
    e, d, n = w.shape
    tk = 512
    blk = 2 * LANES
    src = lax.broadcasted_iota(jnp.int32, (blk, blk), 0)
    dst = lax.broadcasted_iota(jnp.int32, (blk, blk), 1)
    perm = (src == jnp.where(dst < LANES, 2 * dst, 2 * (dst - LANES) + 1)).astype(BF16)
    return pl.pallas_call(
        _deinterleave_kernel,
        grid=(e, d // tk),
        in_specs=[pl.BlockSpec((1, tk, n), lambda i, j: (i, j, 0)), pl.BlockSpec((blk, blk), lambda i, j: (0, 0))],
        out_specs=pl.BlockSpec((1, tk, n), lambda i, j: (i, j, 0)),
        out_shape=jax.ShapeDtypeStruct((e, d, n), BF16),
        compiler_params=_cparams(("arbitrary", "arbitrary")),
        name="deinterleave",
    )(w, perm)


def _final_kernel(x1_ref, y_ref, gt_ref, g_ref, o_ref):
    x = x1_ref[...] + gt_ref[0] * y_ref[...]
    var = jnp.mean(x * x, axis=-1, keepdims=True)
    o_ref[...] = x * lax.rsqrt(var + NORM_EPS) * g_ref[...]


def _final(x1, y, gt, g, b, t, tm):
    n, d = x1.shape
    nt = t // tm
    row = pl.BlockSpec((tm, d), lambda i: (i, 0))
    return pl.pallas_call(
        _final_kernel,
        grid=(n // tm,),
        in_specs=[row, row, pl.BlockSpec((1, 1, d), lambda i: (i // nt, 0, 0)),
                  pl.BlockSpec((1, d), lambda i: (0, 0))],
        out_specs=row,
        out_shape=jax.ShapeDtypeStruct((n, d), F32),
        compiler_params=_cparams(("arbitrary",)),
        name="final",
    )(x1, y, gt, g.reshape(1, d)).reshape(b, t, d)


MASKED = -1e30
INT_MIN = -2 ** 31
COUNT_ROWS = 64
ATT_ROWS = 128
RWKV_BATCH = 2


def _dsa_kernel(q_ref, qi_ref, kw_ref, k_ref, vx_ref, ki_ref, tri_ref, o_ref, key_s, bias_s, wib_s,
                *, qb, kt, q0, l_valid, n_sel):
    i = pl.program_id(1)
    ns = kt // LANES
    row = lax.broadcasted_iota(jnp.int32, (qb, 1), 0)
    qpos = q0 + i * qb + row
    lim = jnp.minimum((qpos // CHUNK + 1) * CHUNK, l_valid)
    last_lim = jnp.minimum(((q0 + i * qb + qb - 1) // CHUNK + 1) * CHUNK, l_valid)
    n_kt = (last_lim + kt - 1) // kt
    lane = lax.broadcasted_iota(jnp.int32, (1, LANES), 1)

    for h in range(IDX_HEADS):
        w = kw_ref[:, IDX_DIM + h:IDX_DIM + h + 1] * (IDX_DIM ** -0.5)
        wib_s[h] = jnp.broadcast_to(w, (qb, LANES))
    qi = qi_ref[...].astype(BF16)

    def f32_key(x):
        bits = pltpu.bitcast(x, jnp.int32)
        return bits ^ ((bits >> 31) & 0x7FFFFFFF)

    def score_tile(t, c):
        ks = pl.multiple_of(t * kt, kt)
        kit = ki_ref[0, pl.ds(ks, kt), :]
        lg = [lax.dot_general(qi[:, h * IDX_DIM:(h + 1) * IDX_DIM], kit, (((1,), (1,)), ((), ())),
                              preferred_element_type=F32) for h in range(IDX_HEADS)]
        for s in range(ns):
            sc = jnp.zeros((qb, LANES), F32)
            for h in range(IDX_HEADS):
                sc = sc + wib_s[h] * jnp.maximum(lg[h][:, s * LANES:(s + 1) * LANES], 0.0)
            kpos = ks + s * LANES + lane
            sc = jnp.where(kpos < lim, sc + 0.0, -jnp.inf)
            key_s[t, :, s * LANES:(s + 1) * LANES] = f32_key(sc)
        return c

    lax.fori_loop(0, n_kt, score_tile, 0)

    def count(pred_fn):
        def tile(t, acc):
            for s in range(ns):
                acc = acc + jnp.where(pred_fn(key_s[t, :, s * LANES:(s + 1) * LANES]), 1.0, 0.0)
            return acc
        acc = lax.fori_loop(0, n_kt, tile, jnp.zeros((qb, LANES), F32))
        return jnp.sum(acc, axis=1, keepdims=True)

    def bit_step(b, lo):
        cand = lo + jnp.left_shift(jnp.int32(1), 31 - b)
        candb = jnp.broadcast_to(cand, (qb, LANES))
        return jnp.where(count(lambda k: k >= candb) >= n_sel, cand, lo)

    thr = lax.fori_loop(0, 32, bit_step, jnp.full((qb, 1), INT_MIN, jnp.int32))
    thrb = jnp.broadcast_to(thr, (qb, LANES))
    need = jnp.broadcast_to(n_sel - count(lambda k: k > thrb), (qb, LANES))

    def sel_tile(t, off):
        for s in range(ns):
            key = key_s[t, :, s * LANES:(s + 1) * LANES]
            eq = key == thrb
            pre = jnp.dot(jnp.where(eq, 1.0, 0.0).astype(BF16), tri_ref[...], preferred_element_type=F32)
            kpos = t * kt + s * LANES + lane
            keep = jnp.logical_or(key > thrb, jnp.logical_and(eq, pre[:, :LANES] + off < need))
            keep = jnp.logical_and(keep, kpos < lim)
            bias_s[t, :, s * LANES:(s + 1) * LANES] = jnp.where(keep, 0.0, MASKED)
            off = off + pre[:, LANES:]
        return off

    lax.fori_loop(0, n_kt, sel_tile, jnp.zeros((qb, LANES), F32))

    rep = ATT_HEADS // KV_HEADS
    qgs = []
    for g in range(KV_HEADS):
        qg = jnp.concatenate([q_ref[:, (g * rep + r) * HEAD_DIM:(g * rep + r + 1) * HEAD_DIM] for r in range(rep)],
                             axis=0)
        qgs.append((qg * (HEAD_DIM ** -0.5)).astype(BF16))

    def att_tile(t, carry):
        ks = pl.multiple_of(t * kt, kt)
        out = []
        for g in range(KV_HEADS):
            m, acc = carry[g]
            s = lax.dot_general(qgs[g], k_ref[0, g, pl.ds(ks, kt), :], (((1,), (1,)), ((), ())),
                                preferred_element_type=F32)
            s = (s.reshape(rep, qb, kt) + bias_s[t][None]).reshape(rep * qb, kt)
            m_new = jnp.maximum(m, jnp.max(s, axis=1, keepdims=True))
            p = jnp.exp(s - m_new)
            pv = jnp.dot(p.astype(BF16), vx_ref[0, pl.ds(ks, kt), g * LANES:(g + 1) * LANES],
                         preferred_element_type=F32)
            out.append((m_new, acc * jnp.exp(m - m_new) + pv))
        return tuple(out)

    init = (jnp.full((rep * qb, 1), MASKED, F32), jnp.zeros((rep * qb, LANES), F32))
    res = lax.fori_loop(0, n_kt, att_tile, (init,) * KV_HEADS)
    for g in range(KV_HEADS):
        acc = res[g][1]
        out = acc * pltpu.roll(1.0 / acc, HEAD_DIM, 1)
        for r in range(rep):
            h = g * rep + r
            o_ref[:, h * HEAD_DIM:(h + 1) * HEAD_DIM] = out[r * qb:(r + 1) * qb, :HEAD_DIM]


def _dsa(q, qi, kw, k_bf, v_bf, ki_bf, *, qb, kt, q0, l_valid, n_sel):
    b, lp = ki_bf.shape[:2]
    nq = q.shape[0] // (b * qb)
    k_g = jnp.moveaxis(k_bf, 2, 1)
    ones = jnp.ones((b, lp, KV_HEADS, HEAD_DIM), BF16)
    vx = jnp.concatenate([v_bf, ones], axis=-1).reshape(b, lp, KV_HEADS * LANES)
    tr = lax.broadcasted_iota(jnp.int32, (LANES, 2 * LANES), 0)
    tc = lax.broadcasted_iota(jnp.int32, (LANES, 2 * LANES), 1)
    tri = jnp.logical_or(tr < tc, tc >= LANES).astype(BF16)
    row = lambda w: pl.BlockSpec((qb, w), lambda bi, i: (bi * nq + i, 0))
    kern = functools.partial(_dsa_kernel, qb=qb, kt=kt, q0=q0, l_valid=l_valid, n_sel=n_sel)
    return pl.pallas_call(
        kern,
        grid=(b, nq),
        in_specs=[row(ATT_WIDTH), row(QI_WIDTH), row(KW_WIDTH),
                  pl.BlockSpec((1, KV_HEADS, lp, HEAD_DIM), lambda bi, i: (bi, 0, 0, 0)),
                  pl.BlockSpec((1, lp, KV_HEADS * LANES), lambda bi, i: (bi, 0, 0)),
                  pl.BlockSpec((1, lp, IDX_DIM), lambda bi, i: (bi, 0, 0)),
                  pl.BlockSpec((LANES, 2 * LANES), lambda bi, i: (0, 0))],
        out_specs=row(ATT_WIDTH),
        out_shape=jax.ShapeDtypeStruct((q.shape[0], ATT_WIDTH), F32),
        scratch_shapes=[pltpu.VMEM((lp // kt, qb, kt), jnp.int32), pltpu.VMEM((lp // kt, qb, kt), F32),
                        pltpu.VMEM((IDX_HEADS, qb, LANES), F32)],
        compiler_params=_cparams(("arbitrary", "arbitrary")),
        name="dsa",
    )(q, qi, kw, k_g, vx, ki_bf, tri)


def _dsa_t_kernel(q_ref, qi_ref, kw_ref, k_ref, vxt_ref, ki_ref, tri_ref, o_ref, key_s, bias_s, s0_s, s1_s,
                  p0_s, p1_s, acc_s, *, qb, kt, q0, l_valid, n_sel):
    i = pl.program_id(1)
    ns = kt // LANES
    qpos = q0 + i * qb + lax.broadcasted_iota(jnp.int32, (1, qb), 1)
    lim = jnp.minimum((qpos // CHUNK + 1) * CHUNK, l_valid)
    last_lim = jnp.minimum(((q0 + i * qb + qb - 1) // CHUNK + 1) * CHUNK, l_valid)
    n_kt = (last_lim + kt - 1) // kt
    rep = ATT_HEADS // KV_HEADS

    def by_head(x_t, n):
        return jnp.concatenate([x_t[h * HEAD_DIM:(h + 1) * HEAD_DIM, :] for h in range(n)], axis=1)

    w_qi = by_head(qi_ref[...].T, IDX_HEADS).astype(BF16)
    q_t = q_ref[...].T * (HEAD_DIM ** -0.5)
    w_q = [by_head(q_t[g * rep * HEAD_DIM:(g + 1) * rep * HEAD_DIM, :], rep).astype(BF16) for g in range(KV_HEADS)]
    kw_t = kw_ref[...].T
    wi = [kw_t[IDX_DIM + h:IDX_DIM + h + 1, :] * (IDX_DIM ** -0.5) for h in range(IDX_HEADS)]

    def f32_key(x):
        bits = pltpu.bitcast(x, jnp.int32)
        return bits ^ ((bits >> 31) & 0x7FFFFFFF)

    def score_tile(t, c):
        ks = pl.multiple_of(t * kt, kt)
        lg = jnp.dot(ki_ref[0, pl.ds(ks, kt), :], w_qi, preferred_element_type=F32)
        sc = jnp.zeros((kt, qb), F32)
        for h in range(IDX_HEADS):
            sc = sc + wi[h] * jnp.maximum(lg[:, h * qb:(h + 1) * qb], 0.0)
        kpos = ks + lax.broadcasted_iota(jnp.int32, (kt, qb), 0)
        sc = jnp.where(kpos < lim, sc + 0.0, -jnp.inf)
        key_s[pl.ds(ks, kt), :] = f32_key(sc)
        return c

    lax.fori_loop(0, n_kt, score_tile, 0)

    def count(pred_fn):
        def tile(t, acc):
            ks = pl.multiple_of(t * kt, kt)
            for c in range(kt // COUNT_ROWS):
                chunk = key_s[pl.ds(ks + c * COUNT_ROWS, COUNT_ROWS), :]
                acc = acc + jnp.where(pred_fn(chunk), 1.0, 0.0)
            return acc
        acc = lax.fori_loop(0, n_kt, tile, jnp.zeros((COUNT_ROWS, qb), F32))
        return jnp.sum(acc, axis=0, keepdims=True)

    def bit_step(b, lo):
        cand = lo + jnp.left_shift(jnp.int32(1), 31 - b)
        return jnp.where(count(lambda k: k >= cand) >= n_sel, cand, lo)

    thr = lax.fori_loop(0, 32, bit_step, jnp.full((1, qb), INT_MIN, jnp.int32))
    need = n_sel - count(lambda k: k > thr)

    def sel_tile(t, off):
        for s in range(ns):
            ks = pl.multiple_of(t * kt + s * LANES, LANES)
            key = key_s[pl.ds(ks, LANES), :]
            eq = key == thr
            pre = jnp.dot(tri_ref[...], jnp.where(eq, 1.0, 0.0).astype(BF16), preferred_element_type=F32)
            kpos = ks + lax.broadcasted_iota(jnp.int32, (LANES, qb), 0)
            keep = jnp.logical_or(key > thr, jnp.logical_and(eq, pre[:LANES] + off < need))
            keep = jnp.logical_and(keep, kpos < lim)
            bias_s[pl.ds(ks, LANES), :] = jnp.where(keep, 0.0, MASKED)
            off = off + pre[LANES:]
        return off

    lax.fori_loop(0, n_kt, sel_tile, jnp.zeros((LANES, qb), F32))

    nc = kt // ATT_ROWS
    width = rep * qb
    s_refs = (s0_s, s1_s)
    p_refs = (p0_s, p1_s)

    def score_chunk(t, g, c, mx):
        ks = pl.multiple_of(t * kt + c * ATT_ROWS, ATT_ROWS)
        sc = jnp.dot(k_ref[0, g, pl.ds(ks, ATT_ROWS), :], w_q[g], preferred_element_type=F32)
        bias = bias_s[pl.ds(ks, ATT_ROWS), :]
        sc = jnp.concatenate([sc[:, r * qb:(r + 1) * qb] + bias for r in range(rep)], axis=1)
        s_refs[g][c * ATT_ROWS:(c + 1) * ATT_ROWS, :] = sc
        return jnp.maximum(mx, jnp.max(sc.reshape(ATT_ROWS // 8, 8, width), axis=0))

    def prob_chunk(g, c, m_new):
        rows = slice(c * ATT_ROWS, (c + 1) * ATT_ROWS)
        p_refs[g][rows, :] = jnp.exp(s_refs[g][rows, :] - m_new).astype(BF16)

    def finish(t, g, m, m_new):
        pv = jnp.dot(vxt_ref[0, t, g * LANES:(g + 1) * LANES, :], p_refs[g][...], preferred_element_type=F32)
        acc_s[g] = acc_s[g] * jnp.exp(m - m_new) + pv

    mx_init = jnp.full((8, width), MASKED, F32)

    def att_tile(t, carry):
        m0, m1, mx0 = carry
        m0_new = jnp.maximum(m0, jnp.max(mx0, axis=0, keepdims=True))
        mx1 = mx_init
        for c in range(nc):
            mx1 = score_chunk(t, 1, c, mx1)
            prob_chunk(0, c, m0_new)
        m1_new = jnp.maximum(m1, jnp.max(mx1, axis=0, keepdims=True))
        finish(t, 0, m0, m0_new)
        t_next = jnp.minimum(t + 1, n_kt - 1)
        mx0 = mx_init
        for c in range(nc):
            mx0 = score_chunk(t_next, 0, c, mx0)
            prob_chunk(1, c, m1_new)
        finish(t, 1, m1, m1_new)
        return m0_new, m1_new, mx0

    acc_s[...] = jnp.zeros(acc_s.shape, F32)
    mx0 = mx_init
    for c in range(nc):
        mx0 = score_chunk(0, 0, c, mx0)
    m_init = jnp.full((1, width), MASKED, F32)
    lax.fori_loop(0, n_kt, att_tile, (m_init, m_init, mx0))
    for g in range(KV_HEADS):
        acc = acc_s[g]
        out = acc * (1.0 / acc[HEAD_DIM:HEAD_DIM + 1, :])
        for r in range(rep):
            h = g * rep + r
            o_ref[:, h * HEAD_DIM:(h + 1) * HEAD_DIM] = out[:, r * qb:(r + 1) * qb].T[:, :HEAD_DIM]


def _dsa_t(q, qi, kw, k_bf, v_bf, ki_bf, *, qb, kt, q0, l_valid, n_sel):
    assert qb == LANES
    b, lp = ki_bf.shape[:2]
    nq = q.shape[0] // (b * qb)
    k_g = jnp.moveaxis(k_bf, 2, 1)
    ones = jnp.ones((b, lp, KV_HEADS, HEAD_DIM), BF16)
    vx = jnp.concatenate([v_bf, ones], axis=-1).reshape(b, lp // kt, kt, KV_HEADS * LANES)
    vxt = jnp.swapaxes(vx, 2, 3)
    tr = lax.broadcasted_iota(jnp.int32, (2 * LANES, LANES), 0)
    tc = lax.broadcasted_iota(jnp.int32, (2 * LANES, LANES), 1)
    tri = jnp.logical_or(tc < tr, tr >= LANES).astype(BF16)
    row = lambda w: pl.BlockSpec((qb, w), lambda bi, i: (bi * nq + i, 0))
    kern = functools.partial(_dsa_t_kernel, qb=qb, kt=kt, q0=q0, l_valid=l_valid, n_sel=n_sel)
    return pl.pallas_call(
        kern,
        grid=(b, nq),
        in_specs=[row(ATT_WIDTH), row(QI_WIDTH), row(KW_WIDTH),
                  pl.BlockSpec((1, KV_HEADS, lp, HEAD_DIM), lambda bi, i: (bi, 0, 0, 0)),
                  pl.BlockSpec((1, lp // kt, KV_HEADS * LANES, kt), lambda bi, i: (bi, 0, 0, 0)),
                  pl.BlockSpec((1, lp, IDX_DIM), lambda bi, i: (bi, 0, 0)),
                  pl.BlockSpec((2 * LANES, LANES), lambda bi, i: (0, 0))],
        out_specs=row(ATT_WIDTH),
        out_shape=jax.ShapeDtypeStruct((q.shape[0], ATT_WIDTH), F32),
        scratch_shapes=[pltpu.VMEM((lp, qb), jnp.int32), pltpu.VMEM((lp, qb), F32),
                        pltpu.VMEM((kt, 4 * qb), F32), pltpu.VMEM((kt, 4 * qb), F32),
                        pltpu.VMEM((kt, 4 * qb), BF16), pltpu.VMEM((kt, 4 * qb), BF16),
                        pltpu.VMEM((KV_HEADS, LANES, 4 * qb), F32)],
        compiler_params=_cparams(("arbitrary", "arbitrary")),
        name="dsa_t",
    )(q, qi, kw, k_g, vxt, ki_bf, tri)


def _split_bf16(x):
    hi = x.astype(BF16)
    return hi, (x - hi.astype(F32)).astype(BF16)


_NN = (((1,), (0,)), ((), ()))


def _dots(a_sp, b_sp, dims=_NN):
    (ah, al), (bh, bl) = a_sp, b_sp
    d = functools.partial(lax.dot_general, dimension_numbers=dims, preferred_element_type=F32)
    return d(ah, bh) + (d(ah, bl) + d(al, bh))


def _dot3(a, b, dims=_NN):
    return _dots(_split_bf16(a), _split_bf16(b), dims)


def _dot2(a, b_exact):
    ah, al = _split_bf16(a)
    return jnp.dot(ah, b_exact, preferred_element_type=F32) + jnp.dot(al, b_exact, preferred_element_type=F32)


_NT = (((1,), (1,)), ((), ()))
_TN = (((0,), (0,)), ((), ()))


def _rwkv_kernel(p_ref, prev_ref, s0_ref, mu_ref, w0_ref, w2_ref, a0_ref, a2_ref, g2_ref, kk_ref, ka_ref,
                 rk_ref, lnw_ref, lnb_ref, hsum_ref, tri_ref, o_ref, st_ref, s_s, prev_s, o_s, *, c, nb):
    j = pl.program_id(1)

    @pl.when(j == 0)
    def _init():
        s_s[...] = s0_ref[...]
        prev_s[...] = prev_ref[...]

    hsum = hsum_ref[...]
    tri = tri_ref[...]
    row = lax.broadcasted_iota(jnp.int32, (c, 1), 0)
    o1, o2, o3 = RWKV_WIDTH, 2 * RWKV_WIDTH, 3 * RWKV_WIDTH

    def prepare(bi):
        p = p_ref[bi]
        p_prev = jnp.where(row == 0, prev_s[bi], pltpu.roll(p, 1, 0))
        prev_s[bi] = p[c - 1:c, :]
        xs = p + (p_prev - p) * mu_ref[...]
        r, k, v = xs[:, :o1], xs[:, o1:o2], xs[:, o2:o3]
        dw = xs[:, o3:o3 + W_LORA]
        da = xs[:, o3 + W_LORA:o3 + W_LORA + A_LORA]
        dg = xs[:, o3 + W_LORA + A_LORA:]
        w_log = -jax.nn.softplus(-(w0_ref[...] + _dot3(jnp.tanh(dw), w2_ref[...]))) - 0.5
        lw = -jnp.exp(w_log)
        a = jax.nn.sigmoid(a0_ref[...] + _dot3(da, a2_ref[...]))
        g = _dot3(jax.nn.sigmoid(dg), g2_ref[...])
        kk = k * kk_ref[...]
        kk = kk / jnp.maximum(jnp.sqrt(_dot2(kk * kk, hsum)), 1e-12)
        km = k * (1.0 + (a - 1.0) * ka_ref[...])
        bm = kk * a
        lw_hi, lw_lo = _split_bf16(lw)
        cum = (jnp.dot(tri, lw_hi, preferred_element_type=F32)
               + jnp.dot(tri, lw_lo, preferred_element_type=F32))
        tot = cum[c - 1:c, :]
        e_out = jnp.exp(-cum)
        e_end = jnp.exp(tot - cum)
        return dict(a_t=-kk * jnp.exp(cum - lw), r_t=r * jnp.exp(cum), b_t=bm * e_out, k_t=km * e_out,
                    b_h=bm * e_end, k_h=km * e_end, gam=jnp.exp(tot), r=r, v=v, km=km, g=g)

    pre = [prepare(bi) for bi in range(nb)]

    ri = lax.broadcasted_iota(jnp.int32, (c, c), 0)
    ci = lax.broadcasted_iota(jnp.int32, (c, c), 1)
    strict = ri > ci
    incl = ri >= ci
    eye = jnp.where(ri == ci, 1.0, 0.0)
    n_double = int(np.log2(c)) - 1
    chains = [(bi, h) for bi in range(nb) for h in range(RWKV_HEADS)]
    ids = range(len(chains))
    col = lambda name, i: pre[chains[i][0]][name][:, chains[i][1] * HEAD_DIM:(chains[i][1] + 1) * HEAD_DIM]
    s0 = [s_s[bi, h] for bi, h in chains]
    s0_sp = [_split_bf16(s) for s in s0]
    ar_sp = [_split_bf16(jnp.concatenate([col("a_t", i), col("r_t", i)], axis=0)) for i in ids]
    bk_sp = [_split_bf16(jnp.concatenate([col("b_t", i), col("k_t", i)], axis=0)) for i in ids]
    v_sp = [_split_bf16(col("v", i)) for i in ids]
    m = [_dots(ar_sp[i], bk_sp[i], _NT) for i in ids]
    as0 = [_dots(ar_sp[i], s0_sp[i], _NT) for i in ids]
    x_sp = [_split_bf16(jnp.where(strict, m[i][:c, :c], 0.0)) for i in ids]
    lak_sp = [_split_bf16(jnp.where(strict, m[i][:c, c:], 0.0)) for i in ids]
    tinv = [eye + jnp.where(strict, m[i][:c, :c], 0.0) for i in ids]
    rhs = [as0[i][:c] + _dots(lak_sp[i], v_sp[i]) for i in ids]
    for _ in range(n_double):
        x_sp = [_split_bf16(_dots(x_sp[i], x_sp[i])) for i in ids]
        tinv = [tinv[i] + _dots(_split_bf16(tinv[i]), x_sp[i]) for i in ids]
    u_sp = [_split_bf16(_dot3(tinv[i], rhs[i])) for i in ids]
    for i in ids:
        bi, h = chains[i]
        m_rb = jnp.where(incl, m[i][c:, :c], 0.0)
        m_rk = jnp.where(incl, m[i][c:, c:], 0.0)
        o_s[bi, :, h * HEAD_DIM:(h + 1) * HEAD_DIM] = (as0[i][c:] + _dots(_split_bf16(m_rb), u_sp[i])
                                                       + _dots(_split_bf16(m_rk), v_sp[i]))
    s_new = [s0[i] * col("gam", i) + _dots(u_sp[i], _split_bf16(col("b_h", i)), _TN)
             + _dots(v_sp[i], _split_bf16(col("k_h", i)), _TN) for i in ids]
    for i in ids:
        s_s[chains[i][0], chains[i][1]] = s_new[i]

    inv_d = 1.0 / HEAD_DIM
    for bi in range(nb):
        o = o_s[bi]
        mean = _dot2(o, hsum) * inv_d
        cen = o - mean
        var = _dot2(cen * cen, hsum) * inv_d
        on = cen * lax.rsqrt(var + GN_EPS) * lnw_ref[...] + lnb_ref[...]
        bonus = _dot2(pre[bi]["r"] * pre[bi]["km"] * rk_ref[...], hsum) * pre[bi]["v"]
        o_ref[bi] = (on + bonus) * pre[bi]["g"]

    @pl.when(j == pl.num_programs(1) - 1)
    def _fin():
        st_ref[...] = s_s[...]


def _rwkv(p, prev, s0, rw, c, nb):
    mu, w0, w2, a0, a2, g2, k_k, k_a, r_k, lnx_w, lnx_b = rw
    b, t, _ = p.shape
    nc = t // c
    hid = lax.broadcasted_iota(jnp.int32, (RWKV_WIDTH, RWKV_WIDTH), 0) // HEAD_DIM
    hsum = (hid == hid.T).astype(BF16)
    tri = (lax.broadcasted_iota(jnp.int32, (c, c), 0) >= lax.broadcasted_iota(jnp.int32, (c, c), 1)).astype(BF16)
    vec = lambda a: a.reshape(1, -1)
    full = lambda a: pl.BlockSpec(a.shape, lambda bi, j: (0,) * a.ndim)
    small = [vec(mu), vec(w0), w2, vec(a0), a2, g2, vec(k_k), vec(k_a), vec(r_k), vec(lnx_w), vec(lnx_b), hsum, tri]
    state = pl.BlockSpec((nb, RWKV_HEADS, HEAD_DIM, HEAD_DIM), lambda bi, j: (bi, 0, 0, 0))
    return pl.pallas_call(
        functools.partial(_rwkv_kernel, c=c, nb=nb),
        grid=(b // nb, nc),
        in_specs=[pl.BlockSpec((nb, c, RWKV_COLS), lambda bi, j: (bi, j, 0)),
                  pl.BlockSpec((nb, 1, RWKV_COLS), lambda bi, j: (bi, 0, 0)), state] + [full(a) for a in small],
        out_specs=[pl.BlockSpec((nb, c, RWKV_WIDTH), lambda bi, j: (bi, j, 0)), state],
        out_shape=[jax.ShapeDtypeStruct((b, t, RWKV_WIDTH), F32),
                   jax.ShapeDtypeStruct((b, RWKV_HEADS, HEAD_DIM, HEAD_DIM), F32)],
        scratch_shapes=[pltpu.VMEM((nb, RWKV_HEADS, HEAD_DIM, HEAD_DIM), F32), pltpu.VMEM((nb, 1, RWKV_COLS), F32),
                        pltpu.VMEM((nb, c, RWKV_WIDTH), F32)],
        compiler_params=_cparams(("arbitrary", "arbitrary")),
        name="rwkv",
    )(p, prev, s0, *small)


def _branch(x, mods, pos, prev, s0, caches, weights, tm, tm_moe):
    (g_mix, g_ffn, g_final, w_r, w_a, rw, w_out, rwt, rb, wup, bup, wdn, bdn) = weights
    sh1, sc1, gt1, sh2, sc2, gt2 = mods
    b, t, d = x.shape
    tab, tabk = _rope_tables(pos)
    p_r, q, k, v, qi, kw = _inproj(x, g_mix, sh1, sc1, w_r, w_a, tab, tabk, tm)
    p_r3 = p_r.reshape(b, t, RWKV_COLS)
    o_r, wkv = _rwkv(p_r3, prev, s0, rw, min(CHUNK, t), RWKV_BATCH)
    shift = p_r3[:, -1:]
    k4 = k.reshape(b, t, KV_HEADS, HEAD_DIM)
    v4 = v.reshape(b, t, KV_HEADS, HEAD_DIM)
    ki = kw.reshape(b, t, KW_WIDTH)[..., :IDX_DIM]
    kt = 1024 if caches is None else 512
    if caches is None:
        k_all, v_all, ki_all, q0, l_valid = k4, v4, ki, 0, t
    else:
        ck, cv, cki = caches
        q0 = ck.shape[1]
        l_valid = q0 + t
        padn = (-l_valid) % kt
        zpad = lambda a: jnp.concatenate([a, jnp.zeros((b, padn) + a.shape[2:], a.dtype)], axis=1)
        k_all = zpad(jnp.concatenate([ck, k4], axis=1))
        v_all = zpad(jnp.concatenate([cv, v4], axis=1))
        ki_all = zpad(jnp.concatenate([cki, ki], axis=1))
    qb = min(Q_BLOCK, t)
    dsa = _dsa_t if qb == LANES else _dsa
    o_a = dsa(q, qi, kw, k_all.astype(BF16), v_all.astype(BF16), ki_all.astype(BF16),
              qb=qb, kt=kt, q0=q0, l_valid=l_valid, n_sel=min(TOPK_MAX, l_valid // 4))
    x1, h2 = _outproj(o_r, o_a, x, w_out, gt1, g_ffn, sh2, sc2, tm)
    y_moe = _moe(h2, rwt, rb, wup, bup, wdn, bdn, *tm_moe)
    y = _final(x1, y_moe, gt2, g_final, b, t, tm)
    return y, k4, v4, ki, wkv, shift


def kernel(x_prompt, x_sample, c_prompt, c_sample, cache_k, cache_v, cache_kidx, state_wkv, state_shift,
           w_ada, b_ada, g_mix, g_ffn, g_final, w_in, mu_shift, w0, w_lora_w, a0, w_lora_a, w_lora_g,
           k_k, k_a, r_k, lnx_w, lnx_b, w_out, router_w, router_b, w_up, b_up, w_down, b_down):
    B, T, D = x_prompt.shape
    DB, DT, _ = x_sample.shape
    P = cache_k.shape[2]
    l = 0
    assert w_ada.shape[0] == 1

    rows = B + DB
    pad = (-rows) % 8
    c_all = jnp.concatenate([c_prompt, c_sample, jnp.zeros((pad, D), F32)], axis=0)
    m = _ada(c_all, w_ada[l], b_ada[l])
    mods_p = tuple(t.reshape(B, 1, D) for t in jnp.split(m[:B], 6, axis=-1))
    mods_s = tuple(t.reshape(DB, 1, D) for t in jnp.split(m[B:rows], 6, axis=-1))

    w_r = w_in[l][:, :RWKV_COLS].astype(BF16)
    w_att = w_in[l][:, RWKV_COLS:]
    att_pad = KW_WIDTH - IDX_DIM - IDX_HEADS
    w_a = jnp.concatenate([w_att, jnp.zeros((D, att_pad), F32)], axis=1).astype(BF16)
    rw = (mu_shift[l], w0[l], w_lora_w[l], a0[l], w_lora_a[l], w_lora_g[l], k_k[l], k_a[l], r_k[l],
          lnx_w[l], lnx_b[l])
    wup = _deinterleave_cast(w_up[l])
    bup = jnp.concatenate([b_up[l][..., 0::2], b_up[l][..., 1::2]], axis=-1).reshape(N_EXPERTS, 1, 2 * D_FF)
    wdn = w_down[l].astype(BF16)
    bdn = b_down[l].reshape(N_EXPERTS, 1, D)
    weights = (g_mix[l], g_ffn[l], g_final, w_r, w_a, rw, w_out[l].astype(BF16), router_w[l].T,
               router_b[l].reshape(N_EXPERTS, 1), wup, bup, wdn, bdn)

    pos_p = jnp.arange(T, dtype=jnp.int32)
    pos_s = P + jnp.arange(DT, dtype=jnp.int32)
    yp, kp, vp, kip, wkvp, shp = _branch(
        x_prompt, mods_p, pos_p, jnp.zeros((B, 1, RWKV_COLS), F32),
        jnp.zeros((B, RWKV_HEADS, HEAD_DIM, HEAD_DIM), F32), None, weights, 512, (MOE_SUB, MOE_NSUB))
    ys, ks, vs, kis, wkvs, shs = _branch(
        x_sample, mods_s, pos_s, state_shift[l], state_wkv[l],
        (cache_k[l], cache_v[l], cache_kidx[l]), weights, DT, (DB * DT, 1))
    return (yp, ys, kp[None], vp[None], kip[None], wkvp[None], shp[None],
            ks[None], vs[None], kis[None], wkvs[None], shs[None])
```

```python
import functools

import numpy as np
import jax
import jax.numpy as jnp
from jax import lax
from jax.experimental import pallas as pl
from jax.experimental.pallas import tpu as pltpu

F32 = jnp.float32
BF16 = jnp.bfloat16
HIGHEST = lax.Precision.HIGHEST

D_MODEL = 1024
CHUNK = 64
HEAD_DIM = 64
RWKV_HEADS = 8
RWKV_WIDTH = RWKV_HEADS * HEAD_DIM
W_LORA = 64
A_LORA = 64
G_LORA = 128
RWKV_COLS = 3 * RWKV_WIDTH + W_LORA + A_LORA + G_LORA
GN_EPS = 64e-5
ATT_HEADS = 8
ATT_WIDTH = ATT_HEADS * HEAD_DIM
KV_HEADS = 2
KV_WIDTH = KV_HEADS * HEAD_DIM
IDX_HEADS = 4
IDX_DIM = 64
TOPK_MAX = 256
Q_BLOCK = 128
ROPE_THETA = 500000.0
ROT_DIM = HEAD_DIM // 4
N_EXPERTS = 32
TOP_K = 4
D_FF = 1024
SWIGLU_ALPHA = 1.702
SWIGLU_LIMIT = 7.0
NORM_EPS = 1e-5

LANES = 128
MOE_ROWS = 128
MOE_SUB = 640
MOE_NSUB = 3
VMEM_LIMIT = 56 * 1024 * 1024

QI_WIDTH = IDX_HEADS * IDX_DIM
KW_WIDTH = LANES


def _cparams(sem):
    return pltpu.CompilerParams(dimension_semantics=sem, vmem_limit_bytes=VMEM_LIMIT)


def _ada_kernel(c_ref, w_ref, b_ref, o_ref):
    c = c_ref[...]
    s = c * jax.nn.sigmoid(c)
    o_ref[...] = jnp.dot(s, w_ref[...], precision=HIGHEST, preferred_element_type=F32) + b_ref[...]


def _ada(c, w, b):
    rows, d = c.shape
    n = w.shape[1]
    tn = 1536
    return pl.pallas_call(
        _ada_kernel,
        grid=(n // tn,),
        in_specs=[pl.BlockSpec((rows, d), lambda j: (0, 0)),
                  pl.BlockSpec((d, tn), lambda j: (0, j)),
                  pl.BlockSpec((1, tn), lambda j: (0, j))],
        out_specs=pl.BlockSpec((rows, tn), lambda j: (0, j)),
        out_shape=jax.ShapeDtypeStruct((rows, n), F32),
        compiler_params=_cparams(("arbitrary",)),
        name="ada",
    )(c, w, b.reshape(1, n))


def _rope_slab(y, tab_ref):
    return (y * tab_ref[0] + pltpu.roll(y, LANES - ROT_DIM // 2, 1) * tab_ref[1]
            + pltpu.roll(y, ROT_DIM // 2, 1) * tab_ref[2])


def _norm_mod(x, g, sh, sc):
    var = jnp.mean(x * x, axis=-1, keepdims=True)
    return (x * lax.rsqrt(var + NORM_EPS) * g) * (1.0 + sc) + sh


def _inproj_kernel(x_ref, g_ref, sh_ref, sc_ref, wr_ref, wa_ref, tab_ref, tabk_ref,
                   rw_ref, q_ref, k_ref, v_ref, qi_ref, kw_ref):
    h = _norm_mod(x_ref[...], g_ref[...], sh_ref[0], sc_ref[0]).astype(BF16)
    rw_ref[...] = jnp.dot(h, wr_ref[...], preferred_element_type=F32)
    pa = jnp.dot(h, wa_ref[...], preferred_element_type=F32)
    off = 0
    for ref, width, rot in ((q_ref, ATT_WIDTH, True), (k_ref, KV_WIDTH, True), (v_ref, KV_WIDTH, False),
                            (qi_ref, QI_WIDTH, True)):
        for s in range(width // LANES):
            slab = pa[:, off + s * LANES: off + (s + 1) * LANES]
            ref[:, s * LANES:(s + 1) * LANES] = _rope_slab(slab, tab_ref) if rot else slab
        off += width
    kw_ref[...] = _rope_slab(pa[:, off:off + LANES], tabk_ref)


def _rope_tables(pos):
    half = ROT_DIM // 2
    inv = ROPE_THETA ** (-jnp.arange(0, ROT_DIM, 2, dtype=F32) / ROT_DIM)
    ang = pos.astype(F32)[:, None] * inv[None, :]
    cos, sin = jnp.cos(ang), jnp.sin(ang)
    t = pos.shape[0]
    one = jnp.ones((t, HEAD_DIM - ROT_DIM), F32)
    zero_r = jnp.zeros((t, HEAD_DIM - ROT_DIM), F32)
    zero_h = jnp.zeros((t, half), F32)
    c_head = jnp.concatenate([cos, cos, one], axis=1)
    up_head = jnp.concatenate([-sin, zero_h, zero_r], axis=1)
    dn_head = jnp.concatenate([zero_h, sin, zero_r], axis=1)
    tab = jnp.stack([jnp.tile(c_head, (1, 2)), jnp.tile(up_head, (1, 2)), jnp.tile(dn_head, (1, 2))])
    wscale = jnp.concatenate([jnp.full((t, IDX_HEADS), IDX_HEADS ** -0.5, F32),
                              jnp.ones((t, HEAD_DIM - IDX_HEADS), F32)], axis=1)
    zero64 = jnp.zeros((t, HEAD_DIM), F32)
    tabk = jnp.stack([jnp.concatenate([c_head, wscale], axis=1),
                      jnp.concatenate([up_head, zero64], axis=1),
                      jnp.concatenate([dn_head, zero64], axis=1)])
    return tab, tabk


def _inproj(x, g, sh, sc, w_r, w_a, tab, tabk, tm):
    b, t, d = x.shape
    nt = t // tm
    n = b * t
    x2 = x.reshape(n, d)
    widths = (RWKV_COLS, ATT_WIDTH, KV_WIDTH, KV_WIDTH, QI_WIDTH, KW_WIDTH)
    row = lambda w: pl.BlockSpec((tm, w), lambda i: (i, 0))
    mod = pl.BlockSpec((1, 1, d), lambda i: (i // nt, 0, 0))
    tabspec = pl.BlockSpec((3, tm, LANES), lambda i: (0, i % nt, 0))
    return pl.pallas_call(
        _inproj_kernel,
        grid=(n // tm,),
        in_specs=[row(d), pl.BlockSpec((1, d), lambda i: (0, 0)), mod, mod,
                  pl.BlockSpec(w_r.shape, lambda i: (0, 0)), pl.BlockSpec(w_a.shape, lambda i: (0, 0)),
                  tabspec, tabspec],
        out_specs=[row(w) for w in widths],
        out_shape=[jax.ShapeDtypeStruct((n, w), F32) for w in widths],
        compiler_params=_cparams(("arbitrary",)),
        name="inproj",
    )(x2, g.reshape(1, d), sh, sc, w_r, w_a, tab, tabk)


def _outproj_kernel(or_ref, oa_ref, x_ref, w_ref, gt_ref, g_ref, sh_ref, sc_ref, x1_ref, h2_ref):
    m = jnp.dot(or_ref[...].astype(BF16), w_ref[:RWKV_WIDTH, :], preferred_element_type=F32)
    m = m + jnp.dot(oa_ref[...].astype(BF16), w_ref[RWKV_WIDTH:, :], preferred_element_type=F32)
    x1 = x_ref[...] + gt_ref[0] * m
    x1_ref[...] = x1
    h2_ref[...] = _norm_mod(x1, g_ref[...], sh_ref[0], sc_ref[0])


def _outproj(o_r, o_a, x, w_out, gt, g, sh, sc, tm):
    b, t, d = x.shape
    nt = t // tm
    n = b * t
    row = lambda w: pl.BlockSpec((tm, w), lambda i: (i, 0))
    mod = pl.BlockSpec((1, 1, d), lambda i: (i // nt, 0, 0))
    return pl.pallas_call(
        _outproj_kernel,
        grid=(n // tm,),
        in_specs=[row(RWKV_WIDTH), row(ATT_WIDTH), row(d), pl.BlockSpec(w_out.shape, lambda i: (0, 0)),
                  mod, pl.BlockSpec((1, d), lambda i: (0, 0)), mod, mod],
        out_specs=[row(d), row(d)],
        out_shape=[jax.ShapeDtypeStruct((n, d), F32)] * 2,
        compiler_params=_cparams(("arbitrary",)),
        name="outproj",
    )(o_r.reshape(n, RWKV_WIDTH), o_a.reshape(n, ATT_WIDTH), x.reshape(n, d), w_out, gt, g.reshape(1, d), sh, sc)


def _moe_kernel(h_ref, rwt_ref, rb_ref, tri_ref, wup_ref, bup_ref, wdn_ref, bdn_ref, y_ref,
                hb_s, rank_s, gate_s, *, n_tokens, sub):
    e = pl.program_id(1)
    n_sub = h_ref.shape[0] // sub

    @pl.when(e == 0)
    def _route():
        rwt = rwt_ref[...].astype(BF16)
        for s in range(n_sub):
            first_row = pl.program_id(0) * (n_sub * sub) + s * sub
            in_rows = first_row + lax.broadcasted_iota(jnp.int32, (sub, 1), 0) < n_tokens
            in_cols = first_row + lax.broadcasted_iota(jnp.int32, (1, sub), 1) < n_tokens
            hb = jnp.where(in_rows, h_ref[s * sub:(s + 1) * sub, :], 0.0).astype(BF16)
            hb_s[s * sub:(s + 1) * sub, :] = hb
            logits = lax.dot_general(rwt, hb, (((1,), (1,)), ((), ())), preferred_element_type=F32) + rb_ref[...]
            eidx = lax.broadcasted_iota(jnp.int32, logits.shape, 0)
            work = logits
            top = None
            for _ in range(TOP_K):
                m = jnp.max(work, axis=0, keepdims=True)
                if top is None:
                    top = m
                first = jnp.min(jnp.where(work == m, eidx, N_EXPERTS), axis=0, keepdims=True)
                work = jnp.where(eidx == first, -jnp.inf, work)
            ex = jnp.where(work != logits, jnp.exp(logits - top), 0.0)
            sel = jnp.logical_and(work != logits, in_cols)
            gate_s[s] = ex / jnp.sum(ex, axis=0, keepdims=True)
            before = jnp.dot(jnp.where(sel, 1.0, 0.0).astype(BF16), tri_ref[...], preferred_element_type=F32)
            rank_s[s] = jnp.where(sel, before, -1.0)
        y_ref[...] = jnp.zeros(y_ref.shape, F32)

    for s in range(n_sub):
        r_row = rank_s[s, pl.ds(e, 1), :]
        g_row = gate_s[s, pl.ds(e, 1), :]
        count = jnp.sum(jnp.where(r_row >= 0.0, 1, 0).astype(jnp.int32))
        n_blocks = (count + MOE_ROWS - 1) // MOE_ROWS

        def block(j, carry, s=s, r_row=r_row, g_row=g_row):
            rows = (lax.broadcasted_iota(jnp.int32, (MOE_ROWS, sub), 0) + j * MOE_ROWS).astype(F32)
            hit = r_row == rows
            p = jnp.where(hit, 1.0, 0.0).astype(BF16)
            xe = jnp.dot(p, hb_s[s * sub:(s + 1) * sub, :], preferred_element_type=F32).astype(BF16)
            u = jnp.dot(xe, wup_ref[0], preferred_element_type=F32) + bup_ref[0]
            glu = jnp.minimum(u[:, :D_FF], SWIGLU_LIMIT)
            lin = jnp.clip(u[:, D_FF:], -SWIGLU_LIMIT, SWIGLU_LIMIT)
            act = glu * jax.nn.sigmoid(SWIGLU_ALPHA * glu) * (lin + 1.0)
            yb = jnp.dot(act.astype(BF16), wdn_ref[0], preferred_element_type=F32) + bdn_ref[0]
            g_rows = jnp.sum(jnp.where(hit, g_row, 0.0), axis=1, keepdims=True)
            ys = (yb * g_rows).astype(BF16)
            y_ref[s * sub:(s + 1) * sub, :] += lax.dot_general(p, ys, (((0,), (0,)), ((), ())),
                                                               preferred_element_type=F32)
            return carry

        lax.fori_loop(0, n_blocks, block, 0)


def _moe(h, rwt, rb, wup, bup, wdn, bdn, sub, n_sub):
    n, d = h.shape
    tm = sub * n_sub
    tri = (lax.broadcasted_iota(jnp.int32, (sub, sub), 0) < lax.broadcasted_iota(jnp.int32, (sub, sub), 1)).astype(BF16)
    return pl.pallas_call(
        functools.partial(_moe_kernel, n_tokens=n, sub=sub),
        grid=(pl.cdiv(n, tm), N_EXPERTS),
        in_specs=[pl.BlockSpec((tm, d), lambda i, e: (i, 0)),
                  pl.BlockSpec((N_EXPERTS, d), lambda i, e: (0, 0)),
                  pl.BlockSpec((N_EXPERTS, 1), lambda i, e: (0, 0)),
                  pl.BlockSpec((sub, sub), lambda i, e: (0, 0)),
                  pl.BlockSpec((1, d, 2 * D_FF), lambda i, e: (e, 0, 0)),
                  pl.BlockSpec((1, 1, 2 * D_FF), lambda i, e: (e, 0, 0)),
                  pl.BlockSpec((1, D_FF, d), lambda i, e: (e, 0, 0)),
                  pl.BlockSpec((1, 1, d), lambda i, e: (e, 0, 0))],
        out_specs=pl.BlockSpec((tm, d), lambda i, e: (i, 0)),
        out_shape=jax.ShapeDtypeStruct((n, d), F32),
        scratch_shapes=[pltpu.VMEM((tm, d), BF16), pltpu.VMEM((n_sub, N_EXPERTS, sub), F32),
                        pltpu.VMEM((n_sub, N_EXPERTS, sub), F32)],
        compiler_params=_cparams(("arbitrary", "arbitrary")),
        name="moe",
    )(h, rwt, rb, tri, wup, bup, wdn, bdn)


def _deinterleave_kernel(w_ref, perm_ref, o_ref):
    n = w_ref.shape[2]
    blk = 2 * LANES
    for j in range(n // blk):
        r = jnp.dot(w_ref[0, :, j * blk:(j + 1) * blk].astype(BF16), perm_ref[...],
                    preferred_element_type=F32).astype(BF16)
        o_ref[0, :, j * LANES:(j + 1) * LANES] = r[:, :LANES]
        o_ref[0, :, n // 2 + j * LANES:n // 2 + (j + 1) * LANES] = r[:, LANES:]


def _deinterleave_cast(w):
    e, d, n = w.shape
    tk = 512
    blk = 2 * LANES
    src = lax.broadcasted_iota(jnp.int32, (blk, blk), 0)
    dst = lax.broadcasted_iota(jnp.int32, (blk, blk), 1)
    perm = (src == jnp.where(dst < LANES, 2 * dst, 2 * (dst - LANES) + 1)).astype(BF16)
    return pl.pallas_call(
        _deinterleave_kernel,
        grid=(e, d // tk),
        in_specs=[pl.BlockSpec((1, tk, n), lambda i, j: (i, j, 0)), pl.BlockSpec((blk, blk), lambda i, j: (0, 0))],
        out_specs=pl.BlockSpec((1, tk, n), lambda i, j: (i, j, 0)),
        out_shape=jax.ShapeDtypeStruct((e, d, n), BF16),
        compiler_params=_cparams(("arbitrary", "arbitrary")),
        name="deinterleave",
    )(w, perm)


def _final_kernel(x1_ref, y_ref, gt_ref, g_ref, o_ref):
    x = x1_ref[...] + gt_ref[0] * y_ref[...]
    var = jnp.mean(x * x, axis=-1, keepdims=True)
    o_ref[...] = x * lax.rsqrt(var + NORM_EPS) * g_ref[...]


def _final(x1, y, gt, g, b, t, tm):
    n, d = x1.shape
    nt = t // tm
    row = pl.BlockSpec((tm, d), lambda i: (i, 0))
    return pl.pallas_call(
        _final_kernel,
        grid=(n // tm,),
        in_specs=[row, row, pl.BlockSpec((1, 1, d), lambda i: (i // nt, 0, 0)),
                  pl.BlockSpec((1, d), lambda i: (0, 0))],
        out_specs=row,
        out_shape=jax.ShapeDtypeStruct((n, d), F32),
        compiler_params=_cparams(("arbitrary",)),
        name="final",
    )(x1, y, gt, g.reshape(1, d)).reshape(b, t, d)


MASKED = -1e30
INT_MIN = -2 ** 31
COUNT_ROWS = 64
ATT_ROWS = 128
RWKV_BATCH = 2


def _dsa_kernel(q_ref, qi_ref, kw_ref, k_ref, vx_ref, ki_ref, tri_ref, o_ref, key_s, bias_s, wib_s,
                *, qb, kt, q0, l_valid, n_sel):
    i = pl.program_id(1)
    ns = kt // LANES
    row = lax.broadcasted_iota(jnp.int32, (qb, 1), 0)
    qpos = q0 + i * qb + row
    lim = jnp.minimum((qpos // CHUNK + 1) * CHUNK, l_valid)
    last_lim = jnp.minimum(((q0 + i * qb + qb - 1) // CHUNK + 1) * CHUNK, l_valid)
    n_kt = (last_lim + kt - 1) // kt
    lane = lax.broadcasted_iota(jnp.int32, (1, LANES), 1)

    for h in range(IDX_HEADS):
        w = kw_ref[:, IDX_DIM + h:IDX_DIM + h + 1] * (IDX_DIM ** -0.5)
        wib_s[h] = jnp.broadcast_to(w, (qb, LANES))
    qi = qi_ref[...].astype(BF16)

    def f32_key(x):
        bits = pltpu.bitcast(x, jnp.int32)
        return bits ^ ((bits >> 31) & 0x7FFFFFFF)

    def score_tile(t, c):
        ks = pl.multiple_of(t * kt, kt)
        kit = ki_ref[0, pl.ds(ks, kt), :]
        lg = [lax.dot_general(qi[:, h * IDX_DIM:(h + 1) * IDX_DIM], kit, (((1,), (1,)), ((), ())),
                              preferred_element_type=F32) for h in range(IDX_HEADS)]
        for s in range(ns):
            sc = jnp.zeros((qb, LANES), F32)
            for h in range(IDX_HEADS):
                sc = sc + wib_s[h] * jnp.maximum(lg[h][:, s * LANES:(s + 1) * LANES], 0.0)
            kpos = ks + s * LANES + lane
            sc = jnp.where(kpos < lim, sc + 0.0, -jnp.inf)
            key_s[t, :, s * LANES:(s + 1) * LANES] = f32_key(sc)
        return c

    lax.fori_loop(0, n_kt, score_tile, 0)

    def count(pred_fn):
        def tile(t, acc):
            for s in range(ns):
                acc = acc + jnp.where(pred_fn(key_s[t, :, s * LANES:(s + 1) * LANES]), 1.0, 0.0)
            return acc
        acc = lax.fori_loop(0, n_kt, tile, jnp.zeros((qb, LANES), F32))
        return jnp.sum(acc, axis=1, keepdims=True)

    def bit_step(b, lo):
        cand = lo + jnp.left_shift(jnp.int32(1), 31 - b)
        candb = jnp.broadcast_to(cand, (qb, LANES))
        return jnp.where(count(lambda k: k >= candb) >= n_sel, cand, lo)

    thr = lax.fori_loop(0, 32, bit_step, jnp.full((qb, 1), INT_MIN, jnp.int32))
    thrb = jnp.broadcast_to(thr, (qb, LANES))
    need = jnp.broadcast_to(n_sel - count(lambda k: k > thrb), (qb, LANES))

    def sel_tile(t, off):
        for s in range(ns):
            key = key_s[t, :, s * LANES:(s + 1) * LANES]
            eq = key == thrb
            pre = jnp.dot(jnp.where(eq, 1.0, 0.0).astype(BF16), tri_ref[...], preferred_element_type=F32)
            kpos = t * kt + s * LANES + lane
            keep = jnp.logical_or(key > thrb, jnp.logical_and(eq, pre[:, :LANES] + off < need))
            keep = jnp.logical_and(keep, kpos < lim)
            bias_s[t, :, s * LANES:(s + 1) * LANES] = jnp.where(keep, 0.0, MASKED)
            off = off + pre[:, LANES:]
        return off

    lax.fori_loop(0, n_kt, sel_tile, jnp.zeros((qb, LANES), F32))

    rep = ATT_HEADS // KV_HEADS
    qgs = []
    for g in range(KV_HEADS):
        qg = jnp.concatenate([q_ref[:, (g * rep + r) * HEAD_DIM:(g * rep + r + 1) * HEAD_DIM] for r in range(rep)],
                             axis=0)
        qgs.append((qg * (HEAD_DIM ** -0.5)).astype(BF16))

    def att_tile(t, carry):
        ks = pl.multiple_of(t * kt, kt)
        out = []
        for g in range(KV_HEADS):
            m, acc = carry[g]
            s = lax.dot_general(qgs[g], k_ref[0, g, pl.ds(ks, kt), :], (((1,), (1,)), ((), ())),
                                preferred_element_type=F32)
            s = (s.reshape(rep, qb, kt) + bias_s[t][None]).reshape(rep * qb, kt)
            m_new = jnp.maximum(m, jnp.max(s, axis=1, keepdims=True))
            p = jnp.exp(s - m_new)
            pv = jnp.dot(p.astype(BF16), vx_ref[0, pl.ds(ks, kt), g * LANES:(g + 1) * LANES],
                         preferred_element_type=F32)
            out.append((m_new, acc * jnp.exp(m - m_new) + pv))
        return tuple(out)

    init = (jnp.full((rep * qb, 1), MASKED, F32), jnp.zeros((rep * qb, LANES), F32))
    res = lax.fori_loop(0, n_kt, att_tile, (init,) * KV_HEADS)
    for g in range(KV_HEADS):
        acc = res[g][1]
        out = acc * pltpu.roll(1.0 / acc, HEAD_DIM, 1)
        for r in range(rep):
            h = g * rep + r
            o_ref[:, h * HEAD_DIM:(h + 1) * HEAD_DIM] = out[r * qb:(r + 1) * qb, :HEAD_DIM]


def _dsa(q, qi, kw, k_bf, v_bf, ki_bf, *, qb, kt, q0, l_valid, n_sel):
    b, lp = ki_bf.shape[:2]
    nq = q.shape[0] // (b * qb)
    k_g = jnp.moveaxis(k_bf, 2, 1)
    ones = jnp.ones((b, lp, KV_HEADS, HEAD_DIM), BF16)
    vx = jnp.concatenate([v_bf, ones], axis=-1).reshape(b, lp, KV_HEADS * LANES)
    tr = lax.broadcasted_iota(jnp.int32, (LANES, 2 * LANES), 0)
    tc = lax.broadcasted_iota(jnp.int32, (LANES, 2 * LANES), 1)
    tri = jnp.logical_or(tr < tc, tc >= LANES).astype(BF16)
    row = lambda w: pl.BlockSpec((qb, w), lambda bi, i: (bi * nq + i, 0))
    kern = functools.partial(_dsa_kernel, qb=qb, kt=kt, q0=q0, l_valid=l_valid, n_sel=n_sel)
    return pl.pallas_call(
        kern,
        grid=(b, nq),
        in_specs=[row(ATT_WIDTH), row(QI_WIDTH), row(KW_WIDTH),
                  pl.BlockSpec((1, KV_HEADS, lp, HEAD_DIM), lambda bi, i: (bi, 0, 0, 0)),
                  pl.BlockSpec((1, lp, KV_HEADS * LANES), lambda bi, i: (bi, 0, 0)),
                  pl.BlockSpec((1, lp, IDX_DIM), lambda bi, i: (bi, 0, 0)),
                  pl.BlockSpec((LANES, 2 * LANES), lambda bi, i: (0, 0))],
        out_specs=row(ATT_WIDTH),
        out_shape=jax.ShapeDtypeStruct((q.shape[0], ATT_WIDTH), F32),
        scratch_shapes=[pltpu.VMEM((lp // kt, qb, kt), jnp.int32), pltpu.VMEM((lp // kt, qb, kt), F32),
                        pltpu.VMEM((IDX_HEADS, qb, LANES), F32)],
        compiler_params=_cparams(("arbitrary", "arbitrary")),
        name="dsa",
    )(q, qi, kw, k_g, vx, ki_bf, tri)


def _dsa_t_kernel(q_ref, qi_ref, kw_ref, k_ref, vxt_ref, ki_ref, tri_ref, o_ref, key_s, bias_s, s0_s, s1_s,
                  p0_s, p1_s, acc_s, *, qb, kt, q0, l_valid, n_sel):
    i = pl.program_id(1)
    ns = kt // LANES
    qpos = q0 + i * qb + lax.broadcasted_iota(jnp.int32, (1, qb), 1)
    lim = jnp.minimum((qpos // CHUNK + 1) * CHUNK, l_valid)
    last_lim = jnp.minimum(((q0 + i * qb + qb - 1) // CHUNK + 1) * CHUNK, l_valid)
    n_kt = (last_lim + kt - 1) // kt
    rep = ATT_HEADS // KV_HEADS

    def by_head(x_t, n):
        return jnp.concatenate([x_t[h * HEAD_DIM:(h + 1) * HEAD_DIM, :] for h in range(n)], axis=1)

    w_qi = by_head(qi_ref[...].T, IDX_HEADS).astype(BF16)
    q_t = q_ref[...].T * (HEAD_DIM ** -0.5)
    w_q = [by_head(q_t[g * rep * HEAD_DIM:(g + 1) * rep * HEAD_DIM, :], rep).astype(BF16) for g in range(KV_HEADS)]
    kw_t = kw_ref[...].T
    wi = [kw_t[IDX_DIM + h:IDX_DIM + h + 1, :] * (IDX_DIM ** -0.5) for h in range(IDX_HEADS)]

    def f32_key(x):
        bits = pltpu.bitcast(x, jnp.int32)
        return bits ^ ((bits >> 31) & 0x7FFFFFFF)

    def score_tile(t, c):
        ks = pl.multiple_of(t * kt, kt)
        lg = jnp.dot(ki_ref[0, pl.ds(ks, kt), :], w_qi, preferred_element_type=F32)
        sc = jnp.zeros((kt, qb), F32)
        for h in range(IDX_HEADS):
            sc = sc + wi[h] * jnp.maximum(lg[:, h * qb:(h + 1) * qb], 0.0)
        kpos = ks + lax.broadcasted_iota(jnp.int32, (kt, qb), 0)
        sc = jnp.where(kpos < lim, sc + 0.0, -jnp.inf)
        key_s[pl.ds(ks, kt), :] = f32_key(sc)
        return c

    lax.fori_loop(0, n_kt, score_tile, 0)

    def count(pred_fn):
        def tile(t, acc):
            ks = pl.multiple_of(t * kt, kt)
            for c in range(kt // COUNT_ROWS):
                chunk = key_s[pl.ds(ks + c * COUNT_ROWS, COUNT_ROWS), :]
                acc = acc + jnp.where(pred_fn(chunk), 1.0, 0.0)
            return acc
        acc = lax.fori_loop(0, n_kt, tile, jnp.zeros((COUNT_ROWS, qb), F32))
        return jnp.sum(acc, axis=0, keepdims=True)

    def bit_step(b, lo):
        cand = lo + jnp.left_shift(jnp.int32(1), 31 - b)
        return jnp.where(count(lambda k: k >= cand) >= n_sel, cand, lo)

    thr = lax.fori_loop(0, 32, bit_step, jnp.full((1, qb), INT_MIN, jnp.int32))
    need = n_sel - count(lambda k: k > thr)

    def sel_tile(t, off):
        for s in range(ns):
            ks = pl.multiple_of(t * kt + s * LANES, LANES)
            key = key_s[pl.ds(ks, LANES), :]
            eq = key == thr
            pre = jnp.dot(tri_ref[...], jnp.where(eq, 1.0, 0.0).astype(BF16), preferred_element_type=F32)
            kpos = ks + lax.broadcasted_iota(jnp.int32, (LANES, qb), 0)
            keep = jnp.logical_or(key > thr, jnp.logical_and(eq, pre[:LANES] + off < need))
            keep = jnp.logical_and(keep, kpos < lim)
            bias_s[pl.ds(ks, LANES), :] = jnp.where(keep, 0.0, MASKED)
            off = off + pre[LANES:]
        return off

    lax.fori_loop(0, n_kt, sel_tile, jnp.zeros((LANES, qb), F32))

    nc = kt // ATT_ROWS
    width = rep * qb
    s_refs = (s0_s, s1_s)
    p_refs = (p0_s, p1_s)

    def score_chunk(t, g, c, mx):
        ks = pl.multiple_of(t * kt + c * ATT_ROWS, ATT_ROWS)
        sc = jnp.dot(k_ref[0, g, pl.ds(ks, ATT_ROWS), :], w_q[g], preferred_element_type=F32)
        bias = bias_s[pl.ds(ks, ATT_ROWS), :]
        sc = jnp.concatenate([sc[:, r * qb:(r + 1) * qb] + bias for r in range(rep)], axis=1)
        s_refs[g][c * ATT_ROWS:(c + 1) * ATT_ROWS, :] = sc
        return jnp.maximum(mx, jnp.max(sc.reshape(ATT_ROWS // 8, 8, width), axis=0))

    def prob_chunk(g, c, m_new):
        rows = slice(c * ATT_ROWS, (c + 1) * ATT_ROWS)
        p_refs[g][rows, :] = jnp.exp(s_refs[g][rows, :] - m_new).astype(BF16)

    def finish(t, g, m, m_new):
        pv = jnp.dot(vxt_ref[0, t, g * LANES:(g + 1) * LANES, :], p_refs[g][...], preferred_element_type=F32)
        acc_s[g] = acc_s[g] * jnp.exp(m - m_new) + pv

    mx_init = jnp.full((8, width), MASKED, F32)

    def att_tile(t, carry):
        m0, m1, mx0 = carry
        m0_new = jnp.maximum(m0, jnp.max(mx0, axis=0, keepdims=True))
        mx1 = mx_init
        for c in range(nc):
            mx1 = score_chunk(t, 1, c, mx1)
            prob_chunk(0, c, m0_new)
        m1_new = jnp.maximum(m1, jnp.max(mx1, axis=0, keepdims=True))
        finish(t, 0, m0, m0_new)
        t_next = jnp.minimum(t + 1, n_kt - 1)
        mx0 = mx_init
        for c in range(nc):
            mx0 = score_chunk(t_next, 0, c, mx0)
            prob_chunk(1, c, m1_new)
        finish(t, 1, m1, m1_new)
        return m0_new, m1_new, mx0

    acc_s[...] = jnp.zeros(acc_s.shape, F32)
    mx0 = mx_init
    for c in range(nc):
        mx0 = score_chunk(0, 0, c, mx0)
    m_init = jnp.full((1, width), MASKED, F32)
    lax.fori_loop(0, n_kt, att_tile, (m_init, m_init, mx0))
    for g in range(KV_HEADS):
        acc = acc_s[g]
        out = acc * (1.0 / acc[HEAD_DIM:HEAD_DIM + 1, :])
        for r in range(rep):
            h = g * rep + r
            o_ref[:, h * HEAD_DIM:(h + 1) * HEAD_DIM] = out[:, r * qb:(r + 1) * qb].T[:, :HEAD_DIM]


def _dsa_t(q, qi, kw, k_bf, v_bf, ki_bf, *, qb, kt, q0, l_valid, n_sel):
    assert qb == LANES
    b, lp = ki_bf.shape[:2]
    nq = q.shape[0] // (b * qb)
    k_g = jnp.moveaxis(k_bf, 2, 1)
    ones = jnp.ones((b, lp, KV_HEADS, HEAD_DIM), BF16)
    vx = jnp.concatenate([v_bf, ones], axis=-1).reshape(b, lp // kt, kt, KV_HEADS * LANES)
    vxt = jnp.swapaxes(vx, 2, 3)
    tr = lax.broadcasted_iota(jnp.int32, (2 * LANES, LANES), 0)
    tc = lax.broadcasted_iota(jnp.int32, (2 * LANES, LANES), 1)
    tri = jnp.logical_or(tc < tr, tr >= LANES).astype(BF16)
    row = lambda w: pl.BlockSpec((qb, w), lambda bi, i: (bi * nq + i, 0))
    kern = functools.partial(_dsa_t_kernel, qb=qb, kt=kt, q0=q0, l_valid=l_valid, n_sel=n_sel)
    return pl.pallas_call(
        kern,
        grid=(b, nq),
        in_specs=[row(ATT_WIDTH), row(QI_WIDTH), row(KW_WIDTH),
                  pl.BlockSpec((1, KV_HEADS, lp, HEAD_DIM), lambda bi, i: (bi, 0, 0, 0)),
                  pl.BlockSpec((1, lp // kt, KV_HEADS * LANES, kt), lambda bi, i: (bi, 0, 0, 0)),
                  pl.BlockSpec((1, lp, IDX_DIM), lambda bi, i: (bi, 0, 0)),
                  pl.BlockSpec((2 * LANES, LANES), lambda bi, i: (0, 0))],
        out_specs=row(ATT_WIDTH),
        out_shape=jax.ShapeDtypeStruct((q.shape[0], ATT_WIDTH), F32),
        scratch_shapes=[pltpu.VMEM((lp, qb), jnp.int32), pltpu.VMEM((lp, qb), F32),
                        pltpu.VMEM((kt, 4 * qb), F32), pltpu.VMEM((kt, 4 * qb), F32),
                        pltpu.VMEM((kt, 4 * qb), BF16), pltpu.VMEM((kt, 4 * qb), BF16),
                        pltpu.VMEM((KV_HEADS, LANES, 4 * qb), F32)],
        compiler_params=_cparams(("arbitrary", "arbitrary")),
        name="dsa_t",
    )(q, qi, kw, k_g, vxt, ki_bf, tri)


def _split_bf16(x):
    hi = x.astype(BF16)
    return hi, (x - hi.astype(F32)).astype(BF16)


_NN = (((1,), (0,)), ((), ()))


def _dots(a_sp, b_sp, dims=_NN):
    (ah, al), (bh, bl) = a_sp, b_sp
    d = functools.partial(lax.dot_general, dimension_numbers=dims, preferred_element_type=F32)
    return d(ah, bh) + (d(ah, bl) + d(al, bh))


def _dot3(a, b, dims=_NN):
    return _dots(_split_bf16(a), _split_bf16(b), dims)


def _dot2(a, b_exact):
    ah, al = _split_bf16(a)
    return jnp.dot(ah, b_exact, preferred_element_type=F32) + jnp.dot(al, b_exact, preferred_element_type=F32)


_NT = (((1,), (1,)), ((), ()))
_TN = (((0,), (0,)), ((), ()))


def _rwkv_kernel(p_ref, prev_ref, s0_ref, mu_ref, w0_ref, w2_ref, a0_ref, a2_ref, g2_ref, kk_ref, ka_ref,
                 rk_ref, lnw_ref, lnb_ref, hsum_ref, tri_ref, o_ref, st_ref, s_s, prev_s, o_s, *, c, nb):
    j = pl.program_id(1)

    @pl.when(j == 0)
    def _init():
        s_s[...] = s0_ref[...]
        prev_s[...] = prev_ref[...]

    hsum = hsum_ref[...]
    tri = tri_ref[...]
    row = lax.broadcasted_iota(jnp.int32, (c, 1), 0)
    o1, o2, o3 = RWKV_WIDTH, 2 * RWKV_WIDTH, 3 * RWKV_WIDTH

    def prepare(bi):
        p = p_ref[bi]
        p_prev = jnp.where(row == 0, prev_s[bi], pltpu.roll(p, 1, 0))
        prev_s[bi] = p[c - 1:c, :]
        xs = p + (p_prev - p) * mu_ref[...]
        r, k, v = xs[:, :o1], xs[:, o1:o2], xs[:, o2:o3]
        dw = xs[:, o3:o3 + W_LORA]
        da = xs[:, o3 + W_LORA:o3 + W_LORA + A_LORA]
        dg = xs[:, o3 + W_LORA + A_LORA:]
        w_log = -jax.nn.softplus(-(w0_ref[...] + _dot3(jnp.tanh(dw), w2_ref[...]))) - 0.5
        lw = -jnp.exp(w_log)
        a = jax.nn.sigmoid(a0_ref[...] + _dot3(da, a2_ref[...]))
        g = _dot3(jax.nn.sigmoid(dg), g2_ref[...])
        kk = k * kk_ref[...]
        kk = kk / jnp.maximum(jnp.sqrt(_dot2(kk * kk, hsum)), 1e-12)
        km = k * (1.0 + (a - 1.0) * ka_ref[...])
        bm = kk * a
        lw_hi, lw_lo = _split_bf16(lw)
        cum = (jnp.dot(tri, lw_hi, preferred_element_type=F32)
               + jnp.dot(tri, lw_lo, preferred_element_type=F32))
        tot = cum[c - 1:c, :]
        e_out = jnp.exp(-cum)
        e_end = jnp.exp(tot - cum)
        return dict(a_t=-kk * jnp.exp(cum - lw), r_t=r * jnp.exp(cum), b_t=bm * e_out, k_t=km * e_out,
                    b_h=bm * e_end, k_h=km * e_end, gam=jnp.exp(tot), r=r, v=v, km=km, g=g)

    pre = [prepare(bi) for bi in range(nb)]

    ri = lax.broadcasted_iota(jnp.int32, (c, c), 0)
    ci = lax.broadcasted_iota(jnp.int32, (c, c), 1)
    strict = ri > ci
    incl = ri >= ci
    eye = jnp.where(ri == ci, 1.0, 0.0)
    n_double = int(np.log2(c)) - 1
    chains = [(bi, h) for bi in range(nb) for h in range(RWKV_HEADS)]
    ids = range(len(chains))
    col = lambda name, i: pre[chains[i][0]][name][:, chains[i][1] * HEAD_DIM:(chains[i][1] + 1) * HEAD_DIM]
    s0 = [s_s[bi, h] for bi, h in chains]
    s0_sp = [_split_bf16(s) for s in s0]
    ar_sp = [_split_bf16(jnp.concatenate([col("a_t", i), col("r_t", i)], axis=0)) for i in ids]
    bk_sp = [_split_bf16(jnp.concatenate([col("b_t", i), col("k_t", i)], axis=0)) for i in ids]
    v_sp = [_split_bf16(col("v", i)) for i in ids]
    m = [_dots(ar_sp[i], bk_sp[i], _NT) for i in ids]
    as0 = [_dots(ar_sp[i], s0_sp[i], _NT) for i in ids]
    x_sp = [_split_bf16(jnp.where(strict, m[i][:c, :c], 0.0)) for i in ids]
    lak_sp = [_split_bf16(jnp.where(strict, m[i][:c, c:], 0.0)) for i in ids]
    tinv = [eye + jnp.where(strict, m[i][:c, :c], 0.0) for i in ids]
    rhs = [as0[i][:c] + _dots(lak_sp[i], v_sp[i]) for i in ids]
    for _ in range(n_double):
        x_sp = [_split_bf16(_dots(x_sp[i], x_sp[i])) for i in ids]
        tinv = [tinv[i] + _dots(_split_bf16(tinv[i]), x_sp[i]) for i in ids]
    u_sp = [_split_bf16(_dot3(tinv[i], rhs[i])) for i in ids]
    for i in ids:
        bi, h = chains[i]
        m_rb = jnp.where(incl, m[i][c:, :c], 0.0)
        m_rk = jnp.where(incl, m[i][c:, c:], 0.0)
        o_s[bi, :, h * HEAD_DIM:(h + 1) * HEAD_DIM] = (as0[i][c:] + _dots(_split_bf16(m_rb), u_sp[i])
                                                       + _dots(_split_bf16(m_rk), v_sp[i]))
    s_new = [s0[i] * col("gam", i) + _dots(u_sp[i], _split_bf16(col("b_h", i)), _TN)
             + _dots(v_sp[i], _split_bf16(col("k_h", i)), _TN) for i in ids]
    for i in ids:
        s_s[chains[i][0], chains[i][1]] = s_new[i]

    inv_d = 1.0 / HEAD_DIM
    for bi in range(nb):
        o = o_s[bi]
        mean = _dot2(o, hsum) * inv_d
        cen = o - mean
        var = _dot2(cen * cen, hsum) * inv_d
        on = cen * lax.rsqrt(var + GN_EPS) * lnw_ref[...] + lnb_ref[...]
        bonus = _dot2(pre[bi]["r"] * pre[bi]["km"] * rk_ref[...], hsum) * pre[bi]["v"]
        o_ref[bi] = (on + bonus) * pre[bi]["g"]

    @pl.when(j == pl.num_programs(1) - 1)
    def _fin():
        st_ref[...] = s_s[...]


def _rwkv(p, prev, s0, rw, c, nb):
    mu, w0, w2, a0, a2, g2, k_k, k_a, r_k, lnx_w, lnx_b = rw
    b, t, _ = p.shape
    nc = t // c
    hid = lax.broadcasted_iota(jnp.int32, (RWKV_WIDTH, RWKV_WIDTH), 0) // HEAD_DIM
    hsum = (hid == hid.T).astype(BF16)
    tri = (lax.broadcasted_iota(jnp.int32, (c, c), 0) >= lax.broadcasted_iota(jnp.int32, (c, c), 1)).astype(BF16)
    vec = lambda a: a.reshape(1, -1)
    full = lambda a: pl.BlockSpec(a.shape, lambda bi, j: (0,) * a.ndim)
    small = [vec(mu), vec(w0), w2, vec(a0), a2, g2, vec(k_k), vec(k_a), vec(r_k), vec(lnx_w), vec(lnx_b), hsum, tri]
    state = pl.BlockSpec((nb, RWKV_HEADS, HEAD_DIM, HEAD_DIM), lambda bi, j: (bi, 0, 0, 0))
    return pl.pallas_call(
        functools.partial(_rwkv_kernel, c=c, nb=nb),
        grid=(b // nb, nc),
        in_specs=[pl.BlockSpec((nb, c, RWKV_COLS), lambda bi, j: (bi, j, 0)),
                  pl.BlockSpec((nb, 1, RWKV_COLS), lambda bi, j: (bi, 0, 0)), state] + [full(a) for a in small],
        out_specs=[pl.BlockSpec((nb, c, RWKV_WIDTH), lambda bi, j: (bi, j, 0)), state],
        out_shape=[jax.ShapeDtypeStruct((b, t, RWKV_WIDTH), F32),
                   jax.ShapeDtypeStruct((b, RWKV_HEADS, HEAD_DIM, HEAD_DIM), F32)],
        scratch_shapes=[pltpu.VMEM((nb, RWKV_HEADS, HEAD_DIM, HEAD_DIM), F32), pltpu.VMEM((nb, 1, RWKV_COLS), F32),
                        pltpu.VMEM((nb, c, RWKV_WIDTH), F32)],
        compiler_params=_cparams(("arbitrary", "arbitrary")),
        name="rwkv",
    )(p, prev, s0, *small)


def _branch(x, mods, pos, prev, s0, caches, weights, tm, tm_moe):
    (g_mix, g_ffn, g_final, w_r, w_a, rw, w_out, rwt, rb, wup, bup, wdn, bdn) = weights
    sh1, sc1, gt1, sh2, sc2, gt2 = mods
    b, t, d = x.shape
    tab, tabk = _rope_tables(pos)
    p_r, q, k, v, qi, kw = _inproj(x, g_mix, sh1, sc1, w_r, w_a, tab, tabk, tm)
    p_r3 = p_r.reshape(b, t, RWKV_COLS)
    o_r, wkv = _rwkv(p_r3, prev, s0, rw, min(CHUNK, t), RWKV_BATCH)
    shift = p_r3[:, -1:]
    k4 = k.reshape(b, t, KV_HEADS, HEAD_DIM)
    v4 = v.reshape(b, t, KV_HEADS, HEAD_DIM)
    ki = kw.reshape(b, t, KW_WIDTH)[..., :IDX_DIM]
    kt = 1024 if caches is None else 512
    if caches is None:
        k_all, v_all, ki_all, q0, l_valid = k4, v4, ki, 0, t
    else:
        ck, cv, cki = caches
        q0 = ck.shape[1]
        l_valid = q0 + t
        padn = (-l_valid) % kt
        zpad = lambda a: jnp.concatenate([a, jnp.zeros((b, padn) + a.shape[2:], a.dtype)], axis=1)
        k_all = zpad(jnp.concatenate([ck, k4], axis=1))
        v_all = zpad(jnp.concatenate([cv, v4], axis=1))
        ki_all = zpad(jnp.concatenate([cki, ki], axis=1))
    qb = min(Q_BLOCK, t)
    dsa = _dsa_t if qb == LANES else _dsa
    o_a = dsa(q, qi, kw, k_all.astype(BF16), v_all.astype(BF16), ki_all.astype(BF16),
              qb=qb, kt=kt, q0=q0, l_valid=l_valid, n_sel=min(TOPK_MAX, l_valid // 4))
    x1, h2 = _outproj(o_r, o_a, x, w_out, gt1, g_ffn, sh2, sc2, tm)
    y_moe = _moe(h2, rwt, rb, wup, bup, wdn, bdn, *tm_moe)
    y = _final(x1, y_moe, gt2, g_final, b, t, tm)
    return y, k4, v4, ki, wkv, shift


def kernel(x_prompt, x_sample, c_prompt, c_sample, cache_k, cache_v, cache_kidx, state_wkv, state_shift,
           w_ada, b_ada, g_mix, g_ffn, g_final, w_in, mu_shift, w0, w_lora_w, a0, w_lora_a, w_lora_g,
           k_k, k_a, r_k, lnx_w, lnx_b, w_out, router_w, router_b, w_up, b_up, w_down, b_down):
    B, T, D = x_prompt.shape
    DB, DT, _ = x_sample.shape
    P = cache_k.shape[2]
    l = 0
    assert w_ada.shape[0] == 1

    rows = B + DB
    pad = (-rows) % 8
    c_all = jnp.concatenate([c_prompt, c_sample, jnp.zeros((pad, D), F32)], axis=0)
    m = _ada(c_all, w_ada[l], b_ada[l])
    mods_p = tuple(t.reshape(B, 1, D) for t in jnp.split(m[:B], 6, axis=-1))
    mods_s = tuple(t.reshape(DB, 1, D) for t in jnp.split(m[B:rows], 6, axis=-1))

    w_r = w_in[l][:, :RWKV_COLS].astype(BF16)
    w_att = w_in[l][:, RWKV_COLS:]
    att_pad = KW_WIDTH - IDX_DIM - IDX_HEADS
    w_a = jnp.concatenate([w_att, jnp.zeros((D, att_pad), F32)], axis=1).astype(BF16)
    rw = (mu_shift[l], w0[l], w_lora_w[l], a0[l], w_lora_a[l], w_lora_g[l], k_k[l], k_a[l], r_k[l],
          lnx_w[l], lnx_b[l])
    wup = _deinterleave_cast(w_up[l])
    bup = jnp.concatenate([b_up[l][..., 0::2], b_up[l][..., 1::2]], axis=-1).reshape(N_EXPERTS, 1, 2 * D_FF)
    wdn = w_down[l].astype(BF16)
    bdn = b_down[l].reshape(N_EXPERTS, 1, D)
    weights = (g_mix[l], g_ffn[l], g_final, w_r, w_a, rw, w_out[l].astype(BF16), router_w[l].T,
               router_b[l].reshape(N_EXPERTS, 1), wup, bup, wdn, bdn)

    pos_p = jnp.arange(T, dtype=jnp.int32)
    pos_s = P + jnp.arange(DT, dtype=jnp.int32)
    yp, kp, vp, kip, wkvp, shp = _branch(
        x_prompt, mods_p, pos_p, jnp.zeros((B, 1, RWKV_COLS), F32),
        jnp.zeros((B, RWKV_HEADS, HEAD_DIM, HEAD_DIM), F32), None, weights, 512, (MOE_SUB, MOE_NSUB))
    ys, ks, vs, kis, wkvs, shs = _branch(
        x_sample, mods_s, pos_s, state_shift[l], state_wkv[l],
        (cache_k[l], cache_v[l], cache_kidx[l]), weights, DT, (DB * DT, 1))
    return (yp, ys, kp[None], vp[None], kip[None], wkvp[None], shp[None],
            ks[None], vs[None], kis[None], wkvs[None], shs[None])
```

```python
import functools

import numpy as np
import jax
import jax.numpy as jnp
from jax import lax
from jax.experimental import pallas as pl
from jax.experimental.pallas import tpu as pltpu

F32 = jnp.float32
BF16 = jnp.bfloat16
HIGHEST = lax.Precision.HIGHEST

D_MODEL = 1024
CHUNK = 64
HEAD_DIM = 64
RWKV_HEADS = 8
RWKV_WIDTH = RWKV_HEADS * HEAD_DIM
W_LORA = 64
A_LORA = 64
G_LORA = 128
RWKV_COLS = 3 * RWKV_WIDTH + W_LORA + A_LORA + G_LORA
GN_EPS = 64e-5
ATT_HEADS = 8
ATT_WIDTH = ATT_HEADS * HEAD_DIM
KV_HEADS = 2
KV_WIDTH = KV_HEADS * HEAD_DIM
IDX_HEADS = 4
IDX_DIM = 64
TOPK_MAX = 256
Q_BLOCK = 128
ROPE_THETA = 500000.0
ROT_DIM = HEAD_DIM // 4
N_EXPERTS = 32
TOP_K = 4
D_FF = 1024
SWIGLU_ALPHA = 1.702
SWIGLU_LIMIT = 7.0
NORM_EPS = 1e-5

LANES = 128
MOE_ROWS = 128
MOE_SUB = 768
MOE_NSUB = 2
VMEM_LIMIT = 56 * 1024 * 1024

QI_WIDTH = IDX_HEADS * IDX_DIM
KW_WIDTH = LANES


def _cparams(sem):
    return pltpu.CompilerParams(dimension_semantics=sem, vmem_limit_bytes=VMEM_LIMIT)


def _ada_kernel(c_ref, w_ref, b_ref, o_ref):
    c = c_ref[...]
    s = c * jax.nn.sigmoid(c)
    o_ref[...] = jnp.dot(s, w_ref[...], precision=HIGHEST, preferred_element_type=F32) + b_ref[...]


def _ada(c, w, b):
    rows, d = c.shape
    n = w.shape[1]
    tn = 1536
    return pl.pallas_call(
        _ada_kernel,
        grid=(n // tn,),
        in_specs=[pl.BlockSpec((rows, d), lambda j: (0, 0)),
                  pl.BlockSpec((d, tn), lambda j: (0, j)),
                  pl.BlockSpec((1, tn), lambda j: (0, j))],
        out_specs=pl.BlockSpec((rows, tn), lambda j: (0, j)),
        out_shape=jax.ShapeDtypeStruct((rows, n), F32),
        compiler_params=_cparams(("arbitrary",)),
        name="ada",
    )(c, w, b.reshape(1, n))


def _rope_slab(y, tab_ref):
    return (y * tab_ref[0] + pltpu.roll(y, LANES - ROT_DIM // 2, 1) * tab_ref[1]
            + pltpu.roll(y, ROT_DIM // 2, 1) * tab_ref[2])


def _norm_mod(x, g, sh, sc):
    var = jnp.mean(x * x, axis=-1, keepdims=True)
    return (x * lax.rsqrt(var + NORM_EPS) * g) * (1.0 + sc) + sh


def _inproj_kernel(x_ref, g_ref, sh_ref, sc_ref, wr_ref, wa_ref, tab_ref, tabk_ref,
                   rw_ref, q_ref, k_ref, v_ref, qi_ref, kw_ref):
    h = _norm_mod(x_ref[...], g_ref[...], sh_ref[0], sc_ref[0]).astype(BF16)
    rw_ref[...] = jnp.dot(h, wr_ref[...], preferred_element_type=F32)
    pa = jnp.dot(h, wa_ref[...], preferred_element_type=F32)
    off = 0
    for ref, width, rot in ((q_ref, ATT_WIDTH, True), (k_ref, KV_WIDTH, True), (v_ref, KV_WIDTH, False),
                            (qi_ref, QI_WIDTH, True)):
        for s in range(width // LANES):
            slab = pa[:, off + s * LANES: off + (s + 1) * LANES]
            ref[:, s * LANES:(s + 1) * LANES] = _rope_slab(slab, tab_ref) if rot else slab
        off += width
    kw_ref[...] = _rope_slab(pa[:, off:off + LANES], tabk_ref)


def _rope_tables(pos):
    half = ROT_DIM // 2
    inv = ROPE_THETA ** (-jnp.arange(0, ROT_DIM, 2, dtype=F32) / ROT_DIM)
    ang = pos.astype(F32)[:, None] * inv[None, :]
    cos, sin = jnp.cos(ang), jnp.sin(ang)
    t = pos.shape[0]
    one = jnp.ones((t, HEAD_DIM - ROT_DIM), F32)
    zero_r = jnp.zeros((t, HEAD_DIM - ROT_DIM), F32)
    zero_h = jnp.zeros((t, half), F32)
    c_head = jnp.concatenate([cos, cos, one], axis=1)
    up_head = jnp.concatenate([-sin, zero_h, zero_r], axis=1)
    dn_head = jnp.concatenate([zero_h, sin, zero_r], axis=1)
    tab = jnp.stack([jnp.tile(c_head, (1, 2)), jnp.tile(up_head, (1, 2)), jnp.tile(dn_head, (1, 2))])
    wscale = jnp.concatenate([jnp.full((t, IDX_HEADS), IDX_HEADS ** -0.5, F32),
                              jnp.ones((t, HEAD_DIM - IDX_HEADS), F32)], axis=1)
    zero64 = jnp.zeros((t, HEAD_DIM), F32)
    tabk = jnp.stack([jnp.concatenate([c_head, wscale], axis=1),
                      jnp.concatenate([up_head, zero64], axis=1),
                      jnp.concatenate([dn_head, zero64], axis=1)])
    return tab, tabk


def _inproj(x, g, sh, sc, w_r, w_a, tab, tabk, tm):
    b, t, d = x.shape
    nt = t // tm
    n = b * t
    x2 = x.reshape(n, d)
    widths = (RWKV_COLS, ATT_WIDTH, KV_WIDTH, KV_WIDTH, QI_WIDTH, KW_WIDTH)
    row = lambda w: pl.BlockSpec((tm, w), lambda i: (i, 0))
    mod = pl.BlockSpec((1, 1, d), lambda i: (i // nt, 0, 0))
    tabspec = pl.BlockSpec((3, tm, LANES), lambda i: (0, i % nt, 0))
    return pl.pallas_call(
        _inproj_kernel,
        grid=(n // tm,),
        in_specs=[row(d), pl.BlockSpec((1, d), lambda i: (0, 0)), mod, mod,
                  pl.BlockSpec(w_r.shape, lambda i: (0, 0)), pl.BlockSpec(w_a.shape, lambda i: (0, 0)),
                  tabspec, tabspec],
        out_specs=[row(w) for w in widths],
        out_shape=[jax.ShapeDtypeStruct((n, w), F32) for w in widths],
        compiler_params=_cparams(("arbitrary",)),
        name="inproj",
    )(x2, g.reshape(1, d), sh, sc, w_r, w_a, tab, tabk)


def _outproj_kernel(or_ref, oa_ref, x_ref, w_ref, gt_ref, g_ref, sh_ref, sc_ref, x1_ref, h2_ref):
    m = jnp.dot(or_ref[...].astype(BF16), w_ref[:RWKV_WIDTH, :], preferred_element_type=F32)
    m = m + jnp.dot(oa_ref[...].astype(BF16), w_ref[RWKV_WIDTH:, :], preferred_element_type=F32)
    x1 = x_ref[...] + gt_ref[0] * m
    x1_ref[...] = x1
    h2_ref[...] = _norm_mod(x1, g_ref[...], sh_ref[0], sc_ref[0])


def _outproj(o_r, o_a, x, w_out, gt, g, sh, sc, tm):
    b, t, d = x.shape
    nt = t // tm
    n = b * t
    row = lambda w: pl.BlockSpec((tm, w), lambda i: (i, 0))
    mod = pl.BlockSpec((1, 1, d), lambda i: (i // nt, 0, 0))
    return pl.pallas_call(
        _outproj_kernel,
        grid=(n // tm,),
        in_specs=[row(RWKV_WIDTH), row(ATT_WIDTH), row(d), pl.BlockSpec(w_out.shape, lambda i: (0, 0)),
                  mod, pl.BlockSpec((1, d), lambda i: (0, 0)), mod, mod],
        out_specs=[row(d), row(d)],
        out_shape=[jax.ShapeDtypeStruct((n, d), F32)] * 2,
        compiler_params=_cparams(("arbitrary",)),
        name="outproj",
    )(o_r.reshape(n, RWKV_WIDTH), o_a.reshape(n, ATT_WIDTH), x.reshape(n, d), w_out, gt, g.reshape(1, d), sh, sc)


def _moe_kernel(h_ref, rwt_ref, rb_ref, tri_ref, wup_ref, bup_ref, wdn_ref, bdn_ref, y_ref,
                hb_s, rank_s, gate_s, count_s, *, n_tokens, sub):
    e = pl.program_id(1)
    n_sub = h_ref.shape[0] // sub

    @pl.when(e == 0)
    def _route():
        rwt = rwt_ref[...].astype(BF16)
        for s in range(n_sub):
            first_row = pl.program_id(0) * (n_sub * sub) + s * sub
            in_rows = first_row + lax.broadcasted_iota(jnp.int32, (sub, 1), 0) < n_tokens
            in_cols = first_row + lax.broadcasted_iota(jnp.int32, (1, sub), 1) < n_tokens
            hb = jnp.where(in_rows, h_ref[s * sub:(s + 1) * sub, :], 0.0).astype(BF16)
            hb_s[s * sub:(s + 1) * sub, :] = hb
            logits = lax.dot_general(rwt, hb, (((1,), (1,)), ((), ())), preferred_element_type=F32) + rb_ref[...]
            eidx = lax.broadcasted_iota(jnp.int32, logits.shape, 0)
            work = logits
            top = None
            for _ in range(TOP_K):
                m = jnp.max(work, axis=0, keepdims=True)
                if top is None:
                    top = m
                first = jnp.min(jnp.where(work == m, eidx, N_EXPERTS), axis=0, keepdims=True)
                work = jnp.where(eidx == first, -jnp.inf, work)
            ex = jnp.where(work != logits, jnp.exp(logits - top), 0.0)
            sel = jnp.logical_and(work != logits, in_cols)
            gate_s[s] = ex / jnp.sum(ex, axis=0, keepdims=True)
            before = jnp.dot(jnp.where(sel, 1.0, 0.0).astype(BF16), tri_ref[...], preferred_element_type=F32)
            rank_s[s] = jnp.where(sel, before, -1.0)
            for ex_id in range(N_EXPERTS):
                count_s[s, ex_id] = jnp.sum(jnp.where(sel[ex_id:ex_id + 1, :], 1, 0))
        y_ref[...] = jnp.zeros(y_ref.shape, F32)

    for s in range(n_sub):
        r_row = rank_s[s, pl.ds(e, 1), :]
        g_row = gate_s[s, pl.ds(e, 1), :]
        n_blocks = (count_s[s, e] + MOE_ROWS - 1) // MOE_ROWS

        def block(j, carry, s=s, r_row=r_row, g_row=g_row):
            rows = (lax.broadcasted_iota(jnp.int32, (MOE_ROWS, sub), 0) + j * MOE_ROWS).astype(F32)
            hit = r_row == rows
            p = jnp.where(hit, 1.0, 0.0).astype(BF16)
            xe = jnp.dot(p, hb_s[s * sub:(s + 1) * sub, :], preferred_element_type=F32).astype(BF16)
            u = jnp.dot(xe, wup_ref[0], preferred_element_type=F32) + bup_ref[0]
            glu = jnp.minimum(u[:, :D_FF], SWIGLU_LIMIT)
            lin = jnp.clip(u[:, D_FF:], -SWIGLU_LIMIT, SWIGLU_LIMIT)
            act = glu * jax.nn.sigmoid(SWIGLU_ALPHA * glu) * (lin + 1.0)
            yb = jnp.dot(act.astype(BF16), wdn_ref[0], preferred_element_type=F32) + bdn_ref[0]
            g_rows = jnp.sum(jnp.where(hit, g_row, 0.0), axis=1, keepdims=True)
            ys = (yb * g_rows).astype(BF16)
            y_ref[s * sub:(s + 1) * sub, :] += lax.dot_general(p, ys, (((0,), (0,)), ((), ())),
                                                               preferred_element_type=F32)
            return carry

        lax.fori_loop(0, n_blocks, block, 0)


def _moe(h, rwt, rb, wup, bup, wdn, bdn, sub, n_sub):
    n, d = h.shape
    tm = sub * n_sub
    tri = (lax.broadcasted_iota(jnp.int32, (sub, sub), 0) < lax.broadcasted_iota(jnp.int32, (sub, sub), 1)).astype(BF16)
    return pl.pallas_call(
        functools.partial(_moe_kernel, n_tokens=n, sub=sub),
        grid=(pl.cdiv(n, tm), N_EXPERTS),
        in_specs=[pl.BlockSpec((tm, d), lambda i, e: (i, 0)),
                  pl.BlockSpec((N_EXPERTS, d), lambda i, e: (0, 0)),
                  pl.BlockSpec((N_EXPERTS, 1), lambda i, e: (0, 0)),
                  pl.BlockSpec((sub, sub), lambda i, e: (0, 0)),
                  pl.BlockSpec((1, d, 2 * D_FF), lambda i, e: (e, 0, 0)),
                  pl.BlockSpec((1, 1, 2 * D_FF), lambda i, e: (e, 0, 0)),
                  pl.BlockSpec((1, D_FF, d), lambda i, e: (e, 0, 0)),
                  pl.BlockSpec((1, 1, d), lambda i, e: (e, 0, 0))],
        out_specs=pl.BlockSpec((tm, d), lambda i, e: (i, 0)),
        out_shape=jax.ShapeDtypeStruct((n, d), F32),
        scratch_shapes=[pltpu.VMEM((tm, d), BF16), pltpu.VMEM((n_sub, N_EXPERTS, sub), F32),
                        pltpu.VMEM((n_sub, N_EXPERTS, sub), F32), pltpu.SMEM((n_sub, N_EXPERTS), jnp.int32)],
        compiler_params=_cparams(("arbitrary", "arbitrary")),
        name="moe",
    )(h, rwt, rb, tri, wup, bup, wdn, bdn)


def _deinterleave_kernel(w_ref, perm_ref, o_ref):
    n = w_ref.shape[2]
    blk = 2 * LANES
    for j in range(n // blk):
        r = jnp.dot(w_ref[0, :, j * blk:(j + 1) * blk].astype(BF16), perm_ref[...],
                    preferred_element_type=F32).astype(BF16)
        o_ref[0, :, j * LANES:(j + 1) * LANES] = r[:, :LANES]
        o_ref[0, :, n // 2 + j * LANES:n // 2 + (j + 1) * LANES] = r[:, LANES:]


def _deinterleave_cast(w):
    e, d, n = w.shape
    tk = 512
    blk = 2 * LANES
    src = lax.broadcasted_iota(jnp.int32, (blk, blk), 0)
    dst = lax.broadcasted_iota(jnp.int32, (blk, blk), 1)
    perm = (src == jnp.where(dst < LANES, 2 * dst, 2 * (dst - LANES) + 1)).astype(BF16)
    return pl.pallas_call(
        _deinterleave_kernel,
        grid=(e, d // tk),
        in_specs=[pl.BlockSpec((1, tk, n), lambda i, j: (i, j, 0)), pl.BlockSpec((blk, blk), lambda i, j: (0, 0))],
        out_specs=pl.BlockSpec((1, tk, n), lambda i, j: (i, j, 0)),
        out_shape=jax.ShapeDtypeStruct((e, d, n), BF16),
        compiler_params=_cparams(("arbitrary", "arbitrary")),
        name="deinterleave",
    )(w, perm)


def _final_kernel(x1_ref, y_ref, gt_ref, g_ref, o_ref):
    x = x1_ref[...] + gt_ref[0] * y_ref[...]
    var = jnp.mean(x * x, axis=-1, keepdims=True)
    o_ref[...] = x * lax.rsqrt(var + NORM_EPS) * g_ref[...]


def _final(x1, y, gt, g, b, t, tm):
    n, d = x1.shape
    nt = t // tm
    row = pl.BlockSpec((tm, d), lambda i: (i, 0))
    return pl.pallas_call(
        _final_kernel,
        grid=(n // tm,),
        in_specs=[row, row, pl.BlockSpec((1, 1, d), lambda i: (i // nt, 0, 0)),
                  pl.BlockSpec((1, d), lambda i: (0, 0))],
        out_specs=row,
        out_shape=jax.ShapeDtypeStruct((n, d), F32),
        compiler_params=_cparams(("arbitrary",)),
        name="final",
    )(x1, y, gt, g.reshape(1, d)).reshape(b, t, d)


MASKED = -1e30
INT_MIN = -2 ** 31
COUNT_ROWS = 64
ATT_ROWS = 128
RWKV_BATCH = 2


def _dsa_kernel(q_ref, qi_ref, kw_ref, k_ref, vx_ref, ki_ref, tri_ref, o_ref, key_s, bias_s, wib_s,
                *, qb, kt, q0, l_valid, n_sel):
    i = pl.program_id(1)
    ns = kt // LANES
    row = lax.broadcasted_iota(jnp.int32, (qb, 1), 0)
    qpos = q0 + i * qb + row
    lim = jnp.minimum((qpos // CHUNK + 1) * CHUNK, l_valid)
    last_lim = jnp.minimum(((q0 + i * qb + qb - 1) // CHUNK + 1) * CHUNK, l_valid)
    n_kt = (last_lim + kt - 1) // kt
    lane = lax.broadcasted_iota(jnp.int32, (1, LANES), 1)

    for h in range(IDX_HEADS):
        w = kw_ref[:, IDX_DIM + h:IDX_DIM + h + 1] * (IDX_DIM ** -0.5)
        wib_s[h] = jnp.broadcast_to(w, (qb, LANES))
    qi = qi_ref[...].astype(BF16)

    def f32_key(x):
        bits = pltpu.bitcast(x, jnp.int32)
        return bits ^ ((bits >> 31) & 0x7FFFFFFF)

    def score_tile(t, c):
        ks = pl.multiple_of(t * kt, kt)
        kit = ki_ref[0, pl.ds(ks, kt), :]
        lg = [lax.dot_general(qi[:, h * IDX_DIM:(h + 1) * IDX_DIM], kit, (((1,), (1,)), ((), ())),
                              preferred_element_type=F32) for h in range(IDX_HEADS)]
        for s in range(ns):
            sc = jnp.zeros((qb, LANES), F32)
            for h in range(IDX_HEADS):
                sc = sc + wib_s[h] * jnp.maximum(lg[h][:, s * LANES:(s + 1) * LANES], 0.0)
            kpos = ks + s * LANES + lane
            sc = jnp.where(kpos < lim, sc + 0.0, -jnp.inf)
            key_s[t, :, s * LANES:(s + 1) * LANES] = f32_key(sc)
        return c

    lax.fori_loop(0, n_kt, score_tile, 0)

    def count(pred_fn):
        def tile(t, acc):
            for s in range(ns):
                acc = acc + jnp.where(pred_fn(key_s[t, :, s * LANES:(s + 1) * LANES]), 1.0, 0.0)
            return acc
        acc = lax.fori_loop(0, n_kt, tile, jnp.zeros((qb, LANES), F32))
        return jnp.sum(acc, axis=1, keepdims=True)

    def bit_step(b, lo):
        cand = lo + jnp.left_shift(jnp.int32(1), 31 - b)
        candb = jnp.broadcast_to(cand, (qb, LANES))
        return jnp.where(count(lambda k: k >= candb) >= n_sel, cand, lo)

    thr = lax.fori_loop(0, 32, bit_step, jnp.full((qb, 1), INT_MIN, jnp.int32))
    thrb = jnp.broadcast_to(thr, (qb, LANES))
    need = jnp.broadcast_to(n_sel - count(lambda k: k > thrb), (qb, LANES))

    def sel_tile(t, off):
        for s in range(ns):
            key = key_s[t, :, s * LANES:(s + 1) * LANES]
            eq = key == thrb
            pre = jnp.dot(jnp.where(eq, 1.0, 0.0).astype(BF16), tri_ref[...], preferred_element_type=F32)
            kpos = t * kt + s * LANES + lane
            keep = jnp.logical_or(key > thrb, jnp.logical_and(eq, pre[:, :LANES] + off < need))
            keep = jnp.logical_and(keep, kpos < lim)
            bias_s[t, :, s * LANES:(s + 1) * LANES] = jnp.where(keep, 0.0, MASKED)
            off = off + pre[:, LANES:]
        return off

    lax.fori_loop(0, n_kt, sel_tile, jnp.zeros((qb, LANES), F32))

    rep = ATT_HEADS // KV_HEADS
    qgs = []
    for g in range(KV_HEADS):
        qg = jnp.concatenate([q_ref[:, (g * rep + r) * HEAD_DIM:(g * rep + r + 1) * HEAD_DIM] for r in range(rep)],
                             axis=0)
        qgs.append((qg * (HEAD_DIM ** -0.5)).astype(BF16))

    def att_tile(t, carry):
        ks = pl.multiple_of(t * kt, kt)
        out = []
        for g in range(KV_HEADS):
            m, acc = carry[g]
            s = lax.dot_general(qgs[g], k_ref[0, g, pl.ds(ks, kt), :], (((1,), (1,)), ((), ())),
                                preferred_element_type=F32)
            s = (s.reshape(rep, qb, kt) + bias_s[t][None]).reshape(rep * qb, kt)
            m_new = jnp.maximum(m, jnp.max(s, axis=1, keepdims=True))
            p = jnp.exp(s - m_new)
            pv = jnp.dot(p.astype(BF16), vx_ref[0, pl.ds(ks, kt), g * LANES:(g + 1) * LANES],
                         preferred_element_type=F32)
            out.append((m_new, acc * jnp.exp(m - m_new) + pv))
        return tuple(out)

    init = (jnp.full((rep * qb, 1), MASKED, F32), jnp.zeros((rep * qb, LANES), F32))
    res = lax.fori_loop(0, n_kt, att_tile, (init,) * KV_HEADS)
    for g in range(KV_HEADS):
        acc = res[g][1]
        out = acc * pltpu.roll(1.0 / acc, HEAD_DIM, 1)
        for r in range(rep):
            h = g * rep + r
            o_ref[:, h * HEAD_DIM:(h + 1) * HEAD_DIM] = out[r * qb:(r + 1) * qb, :HEAD_DIM]


def _dsa(q, qi, kw, k_bf, v_bf, ki_bf, *, qb, kt, q0, l_valid, n_sel):
    b, lp = ki_bf.shape[:2]
    nq = q.shape[0] // (b * qb)
    k_g = jnp.moveaxis(k_bf, 2, 1)
    ones = jnp.ones((b, lp, KV_HEADS, HEAD_DIM), BF16)
    vx = jnp.concatenate([v_bf, ones], axis=-1).reshape(b, lp, KV_HEADS * LANES)
    tr = lax.broadcasted_iota(jnp.int32, (LANES, 2 * LANES), 0)
    tc = lax.broadcasted_iota(jnp.int32, (LANES, 2 * LANES), 1)
    tri = jnp.logical_or(tr < tc, tc >= LANES).astype(BF16)
    row = lambda w: pl.BlockSpec((qb, w), lambda bi, i: (bi * nq + i, 0))
    kern = functools.partial(_dsa_kernel, qb=qb, kt=kt, q0=q0, l_valid=l_valid, n_sel=n_sel)
    return pl.pallas_call(
        kern,
        grid=(b, nq),
        in_specs=[row(ATT_WIDTH), row(QI_WIDTH), row(KW_WIDTH),
                  pl.BlockSpec((1, KV_HEADS, lp, HEAD_DIM), lambda bi, i: (bi, 0, 0, 0)),
                  pl.BlockSpec((1, lp, KV_HEADS * LANES), lambda bi, i: (bi, 0, 0)),
                  pl.BlockSpec((1, lp, IDX_DIM), lambda bi, i: (bi, 0, 0)),
                  pl.BlockSpec((LANES, 2 * LANES), lambda bi, i: (0, 0))],
        out_specs=row(ATT_WIDTH),
        out_shape=jax.ShapeDtypeStruct((q.shape[0], ATT_WIDTH), F32),
        scratch_shapes=[pltpu.VMEM((lp // kt, qb, kt), jnp.int32), pltpu.VMEM((lp // kt, qb, kt), F32),
                        pltpu.VMEM((IDX_HEADS, qb, LANES), F32)],
        compiler_params=_cparams(("arbitrary", "arbitrary")),
        name="dsa",
    )(q, qi, kw, k_g, vx, ki_bf, tri)


def _dsa_t_kernel(q_ref, qi_ref, kw_ref, k_ref, vxt_ref, ki_ref, tri_ref, o_ref, key_s, bias_s, s0_s, s1_s,
                  p0_s, p1_s, acc_s, *, qb, kt, q0, l_valid, n_sel):
    i = pl.program_id(1)
    ns = kt // LANES
    qpos = q0 + i * qb + lax.broadcasted_iota(jnp.int32, (1, qb), 1)
    lim = jnp.minimum((qpos // CHUNK + 1) * CHUNK, l_valid)
    last_lim = jnp.minimum(((q0 + i * qb + qb - 1) // CHUNK + 1) * CHUNK, l_valid)
    n_kt = (last_lim + kt - 1) // kt
    rep = ATT_HEADS // KV_HEADS

    def by_head(x_t, n):
        return jnp.concatenate([x_t[h * HEAD_DIM:(h + 1) * HEAD_DIM, :] for h in range(n)], axis=1)

    w_qi = by_head(qi_ref[...].T, IDX_HEADS).astype(BF16)
    q_t = q_ref[...].T * (HEAD_DIM ** -0.5)
    w_q = [by_head(q_t[g * rep * HEAD_DIM:(g + 1) * rep * HEAD_DIM, :], rep).astype(BF16) for g in range(KV_HEADS)]
    kw_t = kw_ref[...].T
    wi = [kw_t[IDX_DIM + h:IDX_DIM + h + 1, :] * (IDX_DIM ** -0.5) for h in range(IDX_HEADS)]

    def f32_key(x):
        bits = pltpu.bitcast(x, jnp.int32)
        return bits ^ ((bits >> 31) & 0x7FFFFFFF)

    def score_tile(t, c):
        ks = pl.multiple_of(t * kt, kt)
        lg = jnp.dot(ki_ref[0, pl.ds(ks, kt), :], w_qi, preferred_element_type=F32)
        sc = jnp.zeros((kt, qb), F32)
        for h in range(IDX_HEADS):
            sc = sc + wi[h] * jnp.maximum(lg[:, h * qb:(h + 1) * qb], 0.0)
        kpos = ks + lax.broadcasted_iota(jnp.int32, (kt, qb), 0)
        sc = jnp.where(kpos < lim, sc + 0.0, -jnp.inf)
        key_s[pl.ds(ks, kt), :] = f32_key(sc)
        return c

    lax.fori_loop(0, n_kt, score_tile, 0)

    def count(pred_fn):
        def tile(t, acc):
            ks = pl.multiple_of(t * kt, kt)
            for c in range(kt // COUNT_ROWS):
                chunk = key_s[pl.ds(ks + c * COUNT_ROWS, COUNT_ROWS), :]
                acc = acc + jnp.where(pred_fn(chunk), 1.0, 0.0)
            return acc
        acc = lax.fori_loop(0, n_kt, tile, jnp.zeros((COUNT_ROWS, qb), F32))
        return jnp.sum(acc, axis=0, keepdims=True)

    def bit_step(b, lo):
        cand = lo + jnp.left_shift(jnp.int32(1), 31 - b)
        return jnp.where(count(lambda k: k >= cand) >= n_sel, cand, lo)

    thr = lax.fori_loop(0, 32, bit_step, jnp.full((1, qb), INT_MIN, jnp.int32))
    need = n_sel - count(lambda k: k > thr)

    def sel_tile(t, off):
        for s in range(ns):
            ks = pl.multiple_of(t * kt + s * LANES, LANES)
            key = key_s[pl.ds(ks, LANES), :]
            eq = key == thr
            pre = jnp.dot(tri_ref[...], jnp.where(eq, 1.0, 0.0).astype(BF16), preferred_element_type=F32)
            kpos = ks + lax.broadcasted_iota(jnp.int32, (LANES, qb), 0)
            keep = jnp.logical_or(key > thr, jnp.logical_and(eq, pre[:LANES] + off < need))
            keep = jnp.logical_and(keep, kpos < lim)
            bias_s[pl.ds(ks, LANES), :] = jnp.where(keep, 0.0, MASKED)
            off = off + pre[LANES:]
        return off

    lax.fori_loop(0, n_kt, sel_tile, jnp.zeros((LANES, qb), F32))

    nc = kt // ATT_ROWS
    width = rep * qb
    s_refs = (s0_s, s1_s)
    p_refs = (p0_s, p1_s)

    def score_chunk(t, g, c, mx):
        ks = pl.multiple_of(t * kt + c * ATT_ROWS, ATT_ROWS)
        sc = jnp.dot(k_ref[0, g, pl.ds(ks, ATT_ROWS), :], w_q[g], preferred_element_type=F32)
        bias = bias_s[pl.ds(ks, ATT_ROWS), :]
        sc = jnp.concatenate([sc[:, r * qb:(r + 1) * qb] + bias for r in range(rep)], axis=1)
        s_refs[g][c * ATT_ROWS:(c + 1) * ATT_ROWS, :] = sc
        return jnp.maximum(mx, jnp.max(sc.reshape(ATT_ROWS // 8, 8, width), axis=0))

    def prob_chunk(g, c, m_new):
        rows = slice(c * ATT_ROWS, (c + 1) * ATT_ROWS)
        p_refs[g][rows, :] = jnp.exp(s_refs[g][rows, :] - m_new).astype(BF16)

    def finish(t, g, m, m_new):
        pv = jnp.dot(vxt_ref[0, t, g * LANES:(g + 1) * LANES, :], p_refs[g][...], preferred_element_type=F32)
        acc_s[g] = acc_s[g] * jnp.exp(m - m_new) + pv

    mx_init = jnp.full((8, width), MASKED, F32)

    def att_tile(t, carry):
        m0, m1, mx0 = carry
        m0_new = jnp.maximum(m0, jnp.max(mx0, axis=0, keepdims=True))
        mx1 = mx_init
        for c in range(nc):
            mx1 = score_chunk(t, 1, c, mx1)
            prob_chunk(0, c, m0_new)
        m1_new = jnp.maximum(m1, jnp.max(mx1, axis=0, keepdims=True))
        finish(t, 0, m0, m0_new)
        t_next = jnp.minimum(t + 1, n_kt - 1)
        mx0 = mx_init
        for c in range(nc):
            mx0 = score_chunk(t_next, 0, c, mx0)
            prob_chunk(1, c, m1_new)
        finish(t, 1, m1, m1_new)
        return m0_new, m1_new, mx0

    acc_s[...] = jnp.zeros(acc_s.shape, F32)
    mx0 = mx_init
    for c in range(nc):
        mx0 = score_chunk(0, 0, c, mx0)
    m_init = jnp.full((1, width), MASKED, F32)
    lax.fori_loop(0, n_kt, att_tile, (m_init, m_init, mx0))
    for g in range(KV_HEADS):
        acc = acc_s[g]
        out = acc * (1.0 / acc[HEAD_DIM:HEAD_DIM + 1, :])
        for r in range(rep):
            h = g * rep + r
            o_ref[:, h * HEAD_DIM:(h + 1) * HEAD_DIM] = out[:, r * qb:(r + 1) * qb].T[:, :HEAD_DIM]


def _dsa_t(q, qi, kw, k_bf, v_bf, ki_bf, *, qb, kt, q0, l_valid, n_sel):
    assert qb == LANES
    b, lp = ki_bf.shape[:2]
    nq = q.shape[0] // (b * qb)
    k_g = jnp.moveaxis(k_bf, 2, 1)
    ones = jnp.ones((b, lp, KV_HEADS, HEAD_DIM), BF16)
    vx = jnp.concatenate([v_bf, ones], axis=-1).reshape(b, lp // kt, kt, KV_HEADS * LANES)
    vxt = jnp.swapaxes(vx, 2, 3)
    tr = lax.broadcasted_iota(jnp.int32, (2 * LANES, LANES), 0)
    tc = lax.broadcasted_iota(jnp.int32, (2 * LANES, LANES), 1)
    tri = jnp.logical_or(tc < tr, tr >= LANES).astype(BF16)
    row = lambda w: pl.BlockSpec((qb, w), lambda bi, i: (bi * nq + i, 0))
    kern = functools.partial(_dsa_t_kernel, qb=qb, kt=kt, q0=q0, l_valid=l_valid, n_sel=n_sel)
    return pl.pallas_call(
        kern,
        grid=(b, nq),
        in_specs=[row(ATT_WIDTH), row(QI_WIDTH), row(KW_WIDTH),
                  pl.BlockSpec((1, KV_HEADS, lp, HEAD_DIM), lambda bi, i: (bi, 0, 0, 0)),
                  pl.BlockSpec((1, lp // kt, KV_HEADS * LANES, kt), lambda bi, i: (bi, 0, 0, 0)),
                  pl.BlockSpec((1, lp, IDX_DIM), lambda bi, i: (bi, 0, 0)),
                  pl.BlockSpec((2 * LANES, LANES), lambda bi, i: (0, 0))],
        out_specs=row(ATT_WIDTH),
        out_shape=jax.ShapeDtypeStruct((q.shape[0], ATT_WIDTH), F32),
        scratch_shapes=[pltpu.VMEM((lp, qb), jnp.int32), pltpu.VMEM((lp, qb), F32),
                        pltpu.VMEM((kt, 4 * qb), F32), pltpu.VMEM((kt, 4 * qb), F32),
                        pltpu.VMEM((kt, 4 * qb), BF16), pltpu.VMEM((kt, 4 * qb), BF16),
                        pltpu.VMEM((KV_HEADS, LANES, 4 * qb), F32)],
        compiler_params=_cparams(("arbitrary", "arbitrary")),
        name="dsa_t",
    )(q, qi, kw, k_g, vxt, ki_bf, tri)


def _split_bf16(x):
    hi = x.astype(BF16)
    return hi, (x - hi.astype(F32)).astype(BF16)


_NN = (((1,), (0,)), ((), ()))


def _dots(a_sp, b_sp, dims=_NN):
    (ah, al), (bh, bl) = a_sp, b_sp
    d = functools.partial(lax.dot_general, dimension_numbers=dims, preferred_element_type=F32)
    return d(ah, bh) + (d(ah, bl) + d(al, bh))


def _dot3(a, b, dims=_NN):
    return _dots(_split_bf16(a), _split_bf16(b), dims)


def _dot2(a, b_exact):
    ah, al = _split_bf16(a)
    return jnp.dot(ah, b_exact, preferred_element_type=F32) + jnp.dot(al, b_exact, preferred_element_type=F32)


_NT = (((1,), (1,)), ((), ()))
_TN = (((0,), (0,)), ((), ()))


def _rwkv_kernel(p_ref, prev_ref, s0_ref, mu_ref, w0_ref, w2_ref, a0_ref, a2_ref, g2_ref, kk_ref, ka_ref,
                 rk_ref, lnw_ref, lnb_ref, hsum_ref, tri_ref, o_ref, st_ref, s_s, prev_s, o_s, *, c, nb):
    j = pl.program_id(1)

    @pl.when(j == 0)
    def _init():
        s_s[...] = s0_ref[...]
        prev_s[...] = prev_ref[...]

    hsum = hsum_ref[...]
    tri = tri_ref[...]
    row = lax.broadcasted_iota(jnp.int32, (c, 1), 0)
    o1, o2, o3 = RWKV_WIDTH, 2 * RWKV_WIDTH, 3 * RWKV_WIDTH

    def prepare(bi):
        p = p_ref[bi]
        p_prev = jnp.where(row == 0, prev_s[bi], pltpu.roll(p, 1, 0))
        prev_s[bi] = p[c - 1:c, :]
        xs = p + (p_prev - p) * mu_ref[...]
        r, k, v = xs[:, :o1], xs[:, o1:o2], xs[:, o2:o3]
        dw = xs[:, o3:o3 + W_LORA]
        da = xs[:, o3 + W_LORA:o3 + W_LORA + A_LORA]
        dg = xs[:, o3 + W_LORA + A_LORA:]
        w_log = -jax.nn.softplus(-(w0_ref[...] + _dot3(jnp.tanh(dw), w2_ref[...]))) - 0.5
        lw = -jnp.exp(w_log)
        a = jax.nn.sigmoid(a0_ref[...] + _dot3(da, a2_ref[...]))
        g = _dot3(jax.nn.sigmoid(dg), g2_ref[...])
        kk = k * kk_ref[...]
        kk = kk / jnp.maximum(jnp.sqrt(_dot2(kk * kk, hsum)), 1e-12)
        km = k * (1.0 + (a - 1.0) * ka_ref[...])
        bm = kk * a
        lw_hi, lw_lo = _split_bf16(lw)
        cum = (jnp.dot(tri, lw_hi, preferred_element_type=F32)
               + jnp.dot(tri, lw_lo, preferred_element_type=F32))
        tot = cum[c - 1:c, :]
        e_out = jnp.exp(-cum)
        e_end = jnp.exp(tot - cum)
        return dict(a_t=-kk * jnp.exp(cum - lw), r_t=r * jnp.exp(cum), b_t=bm * e_out, k_t=km * e_out,
                    b_h=bm * e_end, k_h=km * e_end, gam=jnp.exp(tot), r=r, v=v, km=km, g=g)

    pre = [prepare(bi) for bi in range(nb)]

    ri = lax.broadcasted_iota(jnp.int32, (c, c), 0)
    ci = lax.broadcasted_iota(jnp.int32, (c, c), 1)
    strict = ri > ci
    incl = ri >= ci
    eye = jnp.where(ri == ci, 1.0, 0.0)
    n_double = int(np.log2(c)) - 1
    chains = [(bi, h) for bi in range(nb) for h in range(RWKV_HEADS)]
    ids = range(len(chains))
    col = lambda name, i: pre[chains[i][0]][name][:, chains[i][1] * HEAD_DIM:(chains[i][1] + 1) * HEAD_DIM]
    s0 = [s_s[bi, h] for bi, h in chains]
    s0_sp = [_split_bf16(s) for s in s0]
    ar_sp = [_split_bf16(jnp.concatenate([col("a_t", i), col("r_t", i)], axis=0)) for i in ids]
    bk_sp = [_split_bf16(jnp.concatenate([col("b_t", i), col("k_t", i)], axis=0)) for i in ids]
    v_sp = [_split_bf16(col("v", i)) for i in ids]
    m = [_dots(ar_sp[i], bk_sp[i], _NT) for i in ids]
    as0 = [_dots(ar_sp[i], s0_sp[i], _NT) for i in ids]
    x_sp = [_split_bf16(jnp.where(strict, m[i][:c, :c], 0.0)) for i in ids]
    lak_sp = [_split_bf16(jnp.where(strict, m[i][:c, c:], 0.0)) for i in ids]
    tinv = [eye + jnp.where(strict, m[i][:c, :c], 0.0) for i in ids]
    rhs = [as0[i][:c] + _dots(lak_sp[i], v_sp[i]) for i in ids]
    for _ in range(n_double):
        x_sp = [_split_bf16(_dots(x_sp[i], x_sp[i])) for i in ids]
        tinv = [tinv[i] + _dots(_split_bf16(tinv[i]), x_sp[i]) for i in ids]
    u_sp = [_split_bf16(_dot3(tinv[i], rhs[i])) for i in ids]
    for i in ids:
        bi, h = chains[i]
        m_rb = jnp.where(incl, m[i][c:, :c], 0.0)
        m_rk = jnp.where(incl, m[i][c:, c:], 0.0)
        o_s[bi, :, h * HEAD_DIM:(h + 1) * HEAD_DIM] = (as0[i][c:] + _dots(_split_bf16(m_rb), u_sp[i])
                                                       + _dots(_split_bf16(m_rk), v_sp[i]))
    s_new = [s0[i] * col("gam", i) + _dots(u_sp[i], _split_bf16(col("b_h", i)), _TN)
             + _dots(v_sp[i], _split_bf16(col("k_h", i)), _TN) for i in ids]
    for i in ids:
        s_s[chains[i][0], chains[i][1]] = s_new[i]

    inv_d = 1.0 / HEAD_DIM
    for bi in range(nb):
        o = o_s[bi]
        mean = _dot2(o, hsum) * inv_d
        cen = o - mean
        var = _dot2(cen * cen, hsum) * inv_d
        on = cen * lax.rsqrt(var + GN_EPS) * lnw_ref[...] + lnb_ref[...]
        bonus = _dot2(pre[bi]["r"] * pre[bi]["km"] * rk_ref[...], hsum) * pre[bi]["v"]
        o_ref[bi] = (on + bonus) * pre[bi]["g"]

    @pl.when(j == pl.num_programs(1) - 1)
    def _fin():
        st_ref[...] = s_s[...]


def _rwkv(p, prev, s0, rw, c, nb):
    mu, w0, w2, a0, a2, g2, k_k, k_a, r_k, lnx_w, lnx_b = rw
    b, t, _ = p.shape
    nc = t // c
    hid = lax.broadcasted_iota(jnp.int32, (RWKV_WIDTH, RWKV_WIDTH), 0) // HEAD_DIM
    hsum = (hid == hid.T).astype(BF16)
    tri = (lax.broadcasted_iota(jnp.int32, (c, c), 0) >= lax.broadcasted_iota(jnp.int32, (c, c), 1)).astype(BF16)
    vec = lambda a: a.reshape(1, -1)
    full = lambda a: pl.BlockSpec(a.shape, lambda bi, j: (0,) * a.ndim)
    small = [vec(mu), vec(w0), w2, vec(a0), a2, g2, vec(k_k), vec(k_a), vec(r_k), vec(lnx_w), vec(lnx_b), hsum, tri]
    state = pl.BlockSpec((nb, RWKV_HEADS, HEAD_DIM, HEAD_DIM), lambda bi, j: (bi, 0, 0, 0))
    return pl.pallas_call(
        functools.partial(_rwkv_kernel, c=c, nb=nb),
        grid=(b // nb, nc),
        in_specs=[pl.BlockSpec((nb, c, RWKV_COLS), lambda bi, j: (bi, j, 0)),
                  pl.BlockSpec((nb, 1, RWKV_COLS), lambda bi, j: (bi, 0, 0)), state] + [full(a) for a in small],
        out_specs=[pl.BlockSpec((nb, c, RWKV_WIDTH), lambda bi, j: (bi, j, 0)), state],
        out_shape=[jax.ShapeDtypeStruct((b, t, RWKV_WIDTH), F32),
                   jax.ShapeDtypeStruct((b, RWKV_HEADS, HEAD_DIM, HEAD_DIM), F32)],
        scratch_shapes=[pltpu.VMEM((nb, RWKV_HEADS, HEAD_DIM, HEAD_DIM), F32), pltpu.VMEM((nb, 1, RWKV_COLS), F32),
                        pltpu.VMEM((nb, c, RWKV_WIDTH), F32)],
        compiler_params=_cparams(("arbitrary", "arbitrary")),
        name="rwkv",
    )(p, prev, s0, *small)


def _branch(x, mods, pos, prev, s0, caches, weights, tm, tm_moe):
    (g_mix, g_ffn, g_final, w_r, w_a, rw, w_out, rwt, rb, wup, bup, wdn, bdn) = weights
    sh1, sc1, gt1, sh2, sc2, gt2 = mods
    b, t, d = x.shape
    tab, tabk = _rope_tables(pos)
    p_r, q, k, v, qi, kw = _inproj(x, g_mix, sh1, sc1, w_r, w_a, tab, tabk, tm)
    p_r3 = p_r.reshape(b, t, RWKV_COLS)
    o_r, wkv = _rwkv(p_r3, prev, s0, rw, min(CHUNK, t), RWKV_BATCH)
    shift = p_r3[:, -1:]
    k4 = k.reshape(b, t, KV_HEADS, HEAD_DIM)
    v4 = v.reshape(b, t, KV_HEADS, HEAD_DIM)
    ki = kw.reshape(b, t, KW_WIDTH)[..., :IDX_DIM]
    kt = 1024 if caches is None else 512
    if caches is None:
        k_all, v_all, ki_all, q0, l_valid = k4, v4, ki, 0, t
    else:
        ck, cv, cki = caches
        q0 = ck.shape[1]
        l_valid = q0 + t
        padn = (-l_valid) % kt
        zpad = lambda a: jnp.concatenate([a, jnp.zeros((b, padn) + a.shape[2:], a.dtype)], axis=1)
        k_all = zpad(jnp.concatenate([ck, k4], axis=1))
        v_all = zpad(jnp.concatenate([cv, v4], axis=1))
        ki_all = zpad(jnp.concatenate([cki, ki], axis=1))
    qb = min(Q_BLOCK, t)
    dsa = _dsa_t if qb == LANES else _dsa
    o_a = dsa(q, qi, kw, k_all.astype(BF16), v_all.astype(BF16), ki_all.astype(BF16),
              qb=qb, kt=kt, q0=q0, l_valid=l_valid, n_sel=min(TOPK_MAX, l_valid // 4))
    x1, h2 = _outproj(o_r, o_a, x, w_out, gt1, g_ffn, sh2, sc2, tm)
    y_moe = _moe(h2, rwt, rb, wup, bup, wdn, bdn, *tm_moe)
    y = _final(x1, y_moe, gt2, g_final, b, t, tm)
    return y, k4, v4, ki, wkv, shift


def kernel(x_prompt, x_sample, c_prompt, c_sample, cache_k, cache_v, cache_kidx, state_wkv, state_shift,
           w_ada, b_ada, g_mix, g_ffn, g_final, w_in, mu_shift, w0, w_lora_w, a0, w_lora_a, w_lora_g,
           k_k, k_a, r_k, lnx_w, lnx_b, w_out, router_w, router_b, w_up, b_up, w_down, b_down):
    B, T, D = x_prompt.shape
    DB, DT, _ = x_sample.shape
    P = cache_k.shape[2]
    l = 0
    assert w_ada.shape[0] == 1

    rows = B + DB
    pad = (-rows) % 8
    c_all = jnp.concatenate([c_prompt, c_sample, jnp.zeros((pad, D), F32)], axis=0)
    m = _ada(c_all, w_ada[l], b_ada[l])
    mods_p = tuple(t.reshape(B, 1, D) for t in jnp.split(m[:B], 6, axis=-1))
    mods_s = tuple(t.reshape(DB, 1, D) for t in jnp.split(m[B:rows], 6, axis=-1))

    w_r = w_in[l][:, :RWKV_COLS].astype(BF16)
    w_att = w_in[l][:, RWKV_COLS:]
    att_pad = KW_WIDTH - IDX_DIM - IDX_HEADS
    w_a = jnp.concatenate([w_att, jnp.zeros((D, att_pad), F32)], axis=1).astype(BF16)
    rw = (mu_shift[l], w0[l], w_lora_w[l], a0[l], w_lora_a[l], w_lora_g[l], k_k[l], k_a[l], r_k[l],
          lnx_w[l], lnx_b[l])
    wup = _deinterleave_cast(w_up[l])
    bup = jnp.concatenate([b_up[l][..., 0::2], b_up[l][..., 1::2]], axis=-1).reshape(N_EXPERTS, 1, 2 * D_FF)
    wdn = w_down[l].astype(BF16)
    bdn = b_down[l].reshape(N_EXPERTS, 1, D)
    weights = (g_mix[l], g_ffn[l], g_final, w_r, w_a, rw, w_out[l].astype(BF16), router_w[l].T,
               router_b[l].reshape(N_EXPERTS, 1), wup, bup, wdn, bdn)

    pos_p = jnp.arange(T, dtype=jnp.int32)
    pos_s = P + jnp.arange(DT, dtype=jnp.int32)
    yp, kp, vp, kip, wkvp, shp = _branch(
        x_prompt, mods_p, pos_p, jnp.zeros((B, 1, RWKV_COLS), F32),
        jnp.zeros((B, RWKV_HEADS, HEAD_DIM, HEAD_DIM), F32), None, weights, 512, (MOE_SUB, MOE_NSUB))
    ys, ks, vs, kis, wkvs, shs = _branch(
        x_sample, mods_s, pos_s, state_shift[l], state_wkv[l],
        (cache_k[l], cache_v[l], cache_kidx[l]), weights, DT, (DB * DT, 1))
    return (yp, ys, kp[None], vp[None], kip[None], wkvp[None], shp[None],
            ks[None], vs[None], kis[None], wkvs[None], shs[None])
```

```python
import functools

import numpy as np
import jax
import jax.numpy as jnp
from jax import lax
from jax.experimental import pallas as pl
from jax.experimental.pallas import tpu as pltpu

F32 = jnp.float32
BF16 = jnp.bfloat16
HIGHEST = lax.Precision.HIGHEST

D_MODEL = 1024
CHUNK = 64
HEAD_DIM = 64
RWKV_HEADS = 8
RWKV_WIDTH = RWKV_HEADS * HEAD_DIM
W_LORA = 64
A_LORA = 64
G_LORA = 128
RWKV_COLS = 3 * RWKV_WIDTH + W_LORA + A_LORA + G_LORA
GN_EPS = 64e-5
ATT_HEADS = 8
ATT_WIDTH = ATT_HEADS * HEAD_DIM
KV_HEADS = 2
KV_WIDTH = KV_HEADS * HEAD_DIM
IDX_HEADS = 4
IDX_DIM = 64
TOPK_MAX = 256
Q_BLOCK = 128
ROPE_THETA = 500000.0
ROT_DIM = HEAD_DIM // 4
N_EXPERTS = 32
TOP_K = 4
D_FF = 1024
SWIGLU_ALPHA = 1.702
SWIGLU_LIMIT = 7.0
NORM_EPS = 1e-5

LANES = 128
MOE_ROWS = 128
MOE_SUB = 768
MOE_NSUB = 2
VMEM_LIMIT = 56 * 1024 * 1024

QI_WIDTH = IDX_HEADS * IDX_DIM
KW_WIDTH = LANES


def _cparams(sem):
    return pltpu.CompilerParams(dimension_semantics=sem, vmem_limit_bytes=VMEM_LIMIT)


def _ada_kernel(c_ref, w_ref, b_ref, o_ref):
    c = c_ref[...]
    s = c * jax.nn.sigmoid(c)
    o_ref[...] = jnp.dot(s, w_ref[...], precision=HIGHEST, preferred_element_type=F32) + b_ref[...]


def _ada(c, w, b):
    rows, d = c.shape
    n = w.shape[1]
    tn = 1536
    return pl.pallas_call(
        _ada_kernel,
        grid=(n // tn,),
        in_specs=[pl.BlockSpec((rows, d), lambda j: (0, 0)),
                  pl.BlockSpec((d, tn), lambda j: (0, j)),
                  pl.BlockSpec((1, tn), lambda j: (0, j))],
        out_specs=pl.BlockSpec((rows, tn), lambda j: (0, j)),
        out_shape=jax.ShapeDtypeStruct((rows, n), F32),
        compiler_params=_cparams(("arbitrary",)),
        name="ada",
    )(c, w, b.reshape(1, n))


def _rope_slab(y, tab_ref):
    return (y * tab_ref[0] + pltpu.roll(y, LANES - ROT_DIM // 2, 1) * tab_ref[1]
            + pltpu.roll(y, ROT_DIM // 2, 1) * tab_ref[2])


def _norm_mod(x, g, sh, sc):
    var = jnp.mean(x * x, axis=-1, keepdims=True)
    return (x * lax.rsqrt(var + NORM_EPS) * g) * (1.0 + sc) + sh


def _inproj_kernel(x_ref, g_ref, sh_ref, sc_ref, wr_ref, wa_ref, tab_ref, tabk_ref,
                   rw_ref, q_ref, k_ref, v_ref, qi_ref, kw_ref):
    h = _norm_mod(x_ref[...], g_ref[...], sh_ref[0], sc_ref[0]).astype(BF16)
    rw_ref[...] = jnp.dot(h, wr_ref[...], preferred_element_type=F32)
    pa = jnp.dot(h, wa_ref[...], preferred_element_type=F32)
    off = 0
    for ref, width, rot in ((q_ref, ATT_WIDTH, True), (k_ref, KV_WIDTH, True), (v_ref, KV_WIDTH, False),
                            (qi_ref, QI_WIDTH, True)):
        for s in range(width // LANES):
            slab = pa[:, off + s * LANES: off + (s + 1) * LANES]
            ref[:, s * LANES:(s + 1) * LANES] = _rope_slab(slab, tab_ref) if rot else slab
        off += width
    kw_ref[...] = _rope_slab(pa[:, off:off + LANES], tabk_ref)


def _rope_tables(pos):
    half = ROT_DIM // 2
    inv = ROPE_THETA ** (-jnp.arange(0, ROT_DIM, 2, dtype=F32) / ROT_DIM)
    ang = pos.astype(F32)[:, None] * inv[None, :]
    cos, sin = jnp.cos(ang), jnp.sin(ang)
    t = pos.shape[0]
    one = jnp.ones((t, HEAD_DIM - ROT_DIM), F32)
    zero_r = jnp.zeros((t, HEAD_DIM - ROT_DIM), F32)
    zero_h = jnp.zeros((t, half), F32)
    c_head = jnp.concatenate([cos, cos, one], axis=1)
    up_head = jnp.concatenate([-sin, zero_h, zero_r], axis=1)
    dn_head = jnp.concatenate([zero_h, sin, zero_r], axis=1)
    tab = jnp.stack([jnp.tile(c_head, (1, 2)), jnp.tile(up_head, (1, 2)), jnp.tile(dn_head, (1, 2))])
    wscale = jnp.concatenate([jnp.full((t, IDX_HEADS), IDX_HEADS ** -0.5, F32),
                              jnp.ones((t, HEAD_DIM - IDX_HEADS), F32)], axis=1)
    zero64 = jnp.zeros((t, HEAD_DIM), F32)
    tabk = jnp.stack([jnp.concatenate([c_head, wscale], axis=1),
                      jnp.concatenate([up_head, zero64], axis=1),
                      jnp.concatenate([dn_head, zero64], axis=1)])
    return tab, tabk


def _inproj(x, g, sh, sc, w_r, w_a, tab, tabk, tm):
    b, t, d = x.shape
    nt = t // tm
    n = b * t
    x2 = x.reshape(n, d)
    widths = (RWKV_COLS, ATT_WIDTH, KV_WIDTH, KV_WIDTH, QI_WIDTH, KW_WIDTH)
    row = lambda w: pl.BlockSpec((tm, w), lambda i: (i, 0))
    mod = pl.BlockSpec((1, 1, d), lambda i: (i // nt, 0, 0))
    tabspec = pl.BlockSpec((3, tm, LANES), lambda i: (0, i % nt, 0))
    return pl.pallas_call(
        _inproj_kernel,
        grid=(n // tm,),
        in_specs=[row(d), pl.BlockSpec((1, d), lambda i: (0, 0)), mod, mod,
                  pl.BlockSpec(w_r.shape, lambda i: (0, 0)), pl.BlockSpec(w_a.shape, lambda i: (0, 0)),
                  tabspec, tabspec],
        out_specs=[row(w) for w in widths],
        out_shape=[jax.ShapeDtypeStruct((n, w), F32) for w in widths],
        compiler_params=_cparams(("arbitrary",)),
        name="inproj",
    )(x2, g.reshape(1, d), sh, sc, w_r, w_a, tab, tabk)


def _outproj_kernel(or_ref, oa_ref, x_ref, w_ref, gt_ref, g_ref, sh_ref, sc_ref, x1_ref, h2_ref):
    m = jnp.dot(or_ref[...].astype(BF16), w_ref[:RWKV_WIDTH, :], preferred_element_type=F32)
    m = m + jnp.dot(oa_ref[...].astype(BF16), w_ref[RWKV_WIDTH:, :], preferred_element_type=F32)
    x1 = x_ref[...] + gt_ref[0] * m
    x1_ref[...] = x1
    h2_ref[...] = _norm_mod(x1, g_ref[...], sh_ref[0], sc_ref[0])


def _outproj(o_r, o_a, x, w_out, gt, g, sh, sc, tm):
    b, t, d = x.shape
    nt = t // tm
    n = b * t
    row = lambda w: pl.BlockSpec((tm, w), lambda i: (i, 0))
    mod = pl.BlockSpec((1, 1, d), lambda i: (i // nt, 0, 0))
    return pl.pallas_call(
        _outproj_kernel,
        grid=(n // tm,),
        in_specs=[row(RWKV_WIDTH), row(ATT_WIDTH), row(d), pl.BlockSpec(w_out.shape, lambda i: (0, 0)),
                  mod, pl.BlockSpec((1, d), lambda i: (0, 0)), mod, mod],
        out_specs=[row(d), row(d)],
        out_shape=[jax.ShapeDtypeStruct((n, d), F32)] * 2,
        compiler_params=_cparams(("arbitrary",)),
        name="outproj",
    )(o_r.reshape(n, RWKV_WIDTH), o_a.reshape(n, ATT_WIDTH), x.reshape(n, d), w_out, gt, g.reshape(1, d), sh, sc)


def _moe_kernel(h_ref, rwt_ref, rb_ref, tri_ref, wup_ref, bup_ref, wdn_ref, bdn_ref, y_ref,
                hb_s, rank_s, gate_s, count_s, *, n_tokens, sub):
    e = pl.program_id(1)
    n_sub = h_ref.shape[0] // sub

    @pl.when(e == 0)
    def _route():
        rwt = rwt_ref[...].astype(BF16)
        for s in range(n_sub):
            first_row = pl.program_id(0) * (n_sub * sub) + s * sub
            in_rows = first_row + lax.broadcasted_iota(jnp.int32, (sub, 1), 0) < n_tokens
            in_cols = first_row + lax.broadcasted_iota(jnp.int32, (1, sub), 1) < n_tokens
            hb = jnp.where(in_rows, h_ref[s * sub:(s + 1) * sub, :], 0.0).astype(BF16)
            hb_s[s * sub:(s + 1) * sub, :] = hb
            logits = lax.dot_general(rwt, hb, (((1,), (1,)), ((), ())), preferred_element_type=F32) + rb_ref[...]
            eidx = lax.broadcasted_iota(jnp.int32, logits.shape, 0)
            work = logits
            top = None
            for _ in range(TOP_K):
                m = jnp.max(work, axis=0, keepdims=True)
                if top is None:
                    top = m
                first = jnp.min(jnp.where(work == m, eidx, N_EXPERTS), axis=0, keepdims=True)
                work = jnp.where(eidx == first, -jnp.inf, work)
            ex = jnp.where(work != logits, jnp.exp(logits - top), 0.0)
            sel = jnp.logical_and(work != logits, in_cols)
            gate_s[s] = ex / jnp.sum(ex, axis=0, keepdims=True)
            before = jnp.dot(jnp.where(sel, 1.0, 0.0).astype(BF16), tri_ref[...], preferred_element_type=F32)
            rank_s[s] = jnp.where(sel, before, -1.0)
            for ex_id in range(N_EXPERTS):
                count_s[s, ex_id] = jnp.sum(jnp.where(sel[ex_id:ex_id + 1, :], 1, 0))
        y_ref[...] = jnp.zeros(y_ref.shape, F32)

    for s in range(n_sub):
        r_row = rank_s[s, pl.ds(e, 1), :]
        g_row = gate_s[s, pl.ds(e, 1), :]
        n_blocks = (count_s[s, e] + MOE_ROWS - 1) // MOE_ROWS

        def block(j, carry, s=s, r_row=r_row, g_row=g_row):
            rows = (lax.broadcasted_iota(jnp.int32, (MOE_ROWS, sub), 0) + j * MOE_ROWS).astype(F32)
            hit = r_row == rows
            p = jnp.where(hit, 1.0, 0.0).astype(BF16)
            xe = jnp.dot(p, hb_s[s * sub:(s + 1) * sub, :], preferred_element_type=F32).astype(BF16)
            u = jnp.dot(xe, wup_ref[0], preferred_element_type=F32) + bup_ref[0]
            glu = jnp.minimum(u[:, :D_FF], SWIGLU_LIMIT)
            lin = jnp.clip(u[:, D_FF:], -SWIGLU_LIMIT, SWIGLU_LIMIT)
            act = glu * jax.nn.sigmoid(SWIGLU_ALPHA * glu) * (lin + 1.0)
            yb = jnp.dot(act.astype(BF16), wdn_ref[0], preferred_element_type=F32) + bdn_ref[0]
            g_rows = jnp.sum(jnp.where(hit, g_row, 0.0), axis=1, keepdims=True)
            ys = (yb * g_rows).astype(BF16)
            y_ref[s * sub:(s + 1) * sub, :] += lax.dot_general(p, ys, (((0,), (0,)), ((), ())),
                                                               preferred_element_type=F32)
            return carry

        lax.fori_loop(0, n_blocks, block, 0)


def _moe(h, rwt, rb, wup, bup, wdn, bdn, sub, n_sub):
    n, d = h.shape
    tm = sub * n_sub
    tri = (lax.broadcasted_iota(jnp.int32, (sub, sub), 0) < lax.broadcasted_iota(jnp.int32, (sub, sub), 1)).astype(BF16)
    return pl.pallas_call(
        functools.partial(_moe_kernel, n_tokens=n, sub=sub),
        grid=(pl.cdiv(n, tm), N_EXPERTS),
        in_specs=[pl.BlockSpec((tm, d), lambda i, e: (i, 0)),
                  pl.BlockSpec((N_EXPERTS, d), lambda i, e: (0, 0)),
                  pl.BlockSpec((N_EXPERTS, 1), lambda i, e: (0, 0)),
                  pl.BlockSpec((sub, sub), lambda i, e: (0, 0)),
                  pl.BlockSpec((1, d, 2 * D_FF), lambda i, e: (e, 0, 0)),
                  pl.BlockSpec((1, 1, 2 * D_FF), lambda i, e: (e, 0, 0)),
                  pl.BlockSpec((1, D_FF, d), lambda i, e: (e, 0, 0)),
                  pl.BlockSpec((1, 1, d), lambda i, e: (e, 0, 0))],
        out_specs=pl.BlockSpec((tm, d), lambda i, e: (i, 0)),
        out_shape=jax.ShapeDtypeStruct((n, d), F32),
        scratch_shapes=[pltpu.VMEM((tm, d), BF16), pltpu.VMEM((n_sub, N_EXPERTS, sub), F32),
                        pltpu.VMEM((n_sub, N_EXPERTS, sub), F32), pltpu.SMEM((n_sub, N_EXPERTS), jnp.int32)],
        compiler_params=_cparams(("arbitrary", "arbitrary")),
        name="moe",
    )(h, rwt, rb, tri, wup, bup, wdn, bdn)


def _deinterleave_kernel(w_ref, perm_ref, o_ref):
    n = w_ref.shape[2]
    blk = 2 * LANES
    for j in range(n // blk):
        r = jnp.dot(w_ref[0, :, j * blk:(j + 1) * blk].astype(BF16), perm_ref[...],
                    preferred_element_type=F32).astype(BF16)
        o_ref[0, :, j * LANES:(j + 1) * LANES] = r[:, :LANES]
        o_ref[0, :, n // 2 + j * LANES:n // 2 + (j + 1) * LANES] = r[:, LANES:]


def _deinterleave_cast(w):
    e, d, n = w.shape
    tk = 512
    blk = 2 * LANES
    src = lax.broadcasted_iota(jnp.int32, (blk, blk), 0)
    dst = lax.broadcasted_iota(jnp.int32, (blk, blk), 1)
    perm = (src == jnp.where(dst < LANES, 2 * dst, 2 * (dst - LANES) + 1)).astype(BF16)
    return pl.pallas_call(
        _deinterleave_kernel,
        grid=(e, d // tk),
        in_specs=[pl.BlockSpec((1, tk, n), lambda i, j: (i, j, 0)), pl.BlockSpec((blk, blk), lambda i, j: (0, 0))],
        out_specs=pl.BlockSpec((1, tk, n), lambda i, j: (i, j, 0)),
        out_shape=jax.ShapeDtypeStruct((e, d, n), BF16),
        compiler_params=_cparams(("arbitrary", "arbitrary")),
        name="deinterleave",
    )(w, perm)


def _final_kernel(x1_ref, y_ref, gt_ref, g_ref, o_ref):
    x = x1_ref[...] + gt_ref[0] * y_ref[...]
    var = jnp.mean(x * x, axis=-1, keepdims=True)
    o_ref[...] = x * lax.rsqrt(var + NORM_EPS) * g_ref[...]


def _final(x1, y, gt, g, b, t, tm):
    n, d = x1.shape
    nt = t // tm
    row = pl.BlockSpec((tm, d), lambda i: (i, 0))
    return pl.pallas_call(
        _final_kernel,
        grid=(n // tm,),
        in_specs=[row, row, pl.BlockSpec((1, 1, d), lambda i: (i // nt, 0, 0)),
                  pl.BlockSpec((1, d), lambda i: (0, 0))],
        out_specs=row,
        out_shape=jax.ShapeDtypeStruct((n, d), F32),
        compiler_params=_cparams(("arbitrary",)),
        name="final",
    )(x1, y, gt, g.reshape(1, d)).reshape(b, t, d)


MASKED = -1e30
INT_MIN = -2 ** 31
COUNT_ROWS = 64
ATT_ROWS = 128
BIT_GROUP = 4
RWKV_BATCH = 2


def _dsa_kernel(q_ref, qi_ref, kw_ref, k_ref, vx_ref, ki_ref, tri_ref, o_ref, key_s, bias_s, wib_s,
                *, qb, kt, q0, l_valid, n_sel):
    i = pl.program_id(1)
    ns = kt // LANES
    row = lax.broadcasted_iota(jnp.int32, (qb, 1), 0)
    qpos = q0 + i * qb + row
    lim = jnp.minimum((qpos // CHUNK + 1) * CHUNK, l_valid)
    last_lim = jnp.minimum(((q0 + i * qb + qb - 1) // CHUNK + 1) * CHUNK, l_valid)
    n_kt = (last_lim + kt - 1) // kt
    lane = lax.broadcasted_iota(jnp.int32, (1, LANES), 1)

    for h in range(IDX_HEADS):
        w = kw_ref[:, IDX_DIM + h:IDX_DIM + h + 1] * (IDX_DIM ** -0.5)
        wib_s[h] = jnp.broadcast_to(w, (qb, LANES))
    qi = qi_ref[...].astype(BF16)

    def f32_key(x):
        bits = pltpu.bitcast(x, jnp.int32)
        return bits ^ ((bits >> 31) & 0x7FFFFFFF)

    def score_tile(t, c):
        ks = pl.multiple_of(t * kt, kt)
        kit = ki_ref[0, pl.ds(ks, kt), :]
        lg = [lax.dot_general(qi[:, h * IDX_DIM:(h + 1) * IDX_DIM], kit, (((1,), (1,)), ((), ())),
                              preferred_element_type=F32) for h in range(IDX_HEADS)]
        for s in range(ns):
            sc = jnp.zeros((qb, LANES), F32)
            for h in range(IDX_HEADS):
                sc = sc + wib_s[h] * jnp.maximum(lg[h][:, s * LANES:(s + 1) * LANES], 0.0)
            kpos = ks + s * LANES + lane
            sc = jnp.where(kpos < lim, sc + 0.0, -jnp.inf)
            key_s[t, :, s * LANES:(s + 1) * LANES] = f32_key(sc)
        return c

    lax.fori_loop(0, n_kt, score_tile, 0)

    def count(pred_fn):
        def tile(t, acc):
            for s in range(ns):
                acc = acc + jnp.where(pred_fn(key_s[t, :, s * LANES:(s + 1) * LANES]), 1.0, 0.0)
            return acc
        acc = lax.fori_loop(0, n_kt, tile, jnp.zeros((qb, LANES), F32))
        return jnp.sum(acc, axis=1, keepdims=True)

    def bit_step(b, lo):
        cand = lo + jnp.left_shift(jnp.int32(1), 31 - b)
        candb = jnp.broadcast_to(cand, (qb, LANES))
        return jnp.where(count(lambda k: k >= candb) >= n_sel, cand, lo)

    thr = lax.fori_loop(0, 32, bit_step, jnp.full((qb, 1), INT_MIN, jnp.int32))
    thrb = jnp.broadcast_to(thr, (qb, LANES))
    need = jnp.broadcast_to(n_sel - count(lambda k: k > thrb), (qb, LANES))

    def sel_tile(t, off):
        for s in range(ns):
            key = key_s[t, :, s * LANES:(s + 1) * LANES]
            eq = key == thrb
            pre = jnp.dot(jnp.where(eq, 1.0, 0.0).astype(BF16), tri_ref[...], preferred_element_type=F32)
            kpos = t * kt + s * LANES + lane
            keep = jnp.logical_or(key > thrb, jnp.logical_and(eq, pre[:, :LANES] + off < need))
            keep = jnp.logical_and(keep, kpos < lim)
            bias_s[t, :, s * LANES:(s + 1) * LANES] = jnp.where(keep, 0.0, MASKED)
            off = off + pre[:, LANES:]
        return off

    lax.fori_loop(0, n_kt, sel_tile, jnp.zeros((qb, LANES), F32))

    rep = ATT_HEADS // KV_HEADS
    qgs = []
    for g in range(KV_HEADS):
        qg = jnp.concatenate([q_ref[:, (g * rep + r) * HEAD_DIM:(g * rep + r + 1) * HEAD_DIM] for r in range(rep)],
                             axis=0)
        qgs.append((qg * (HEAD_DIM ** -0.5)).astype(BF16))

    def att_tile(t, carry):
        ks = pl.multiple_of(t * kt, kt)
        out = []
        for g in range(KV_HEADS):
            m, acc = carry[g]
            s = lax.dot_general(qgs[g], k_ref[0, g, pl.ds(ks, kt), :], (((1,), (1,)), ((), ())),
                                preferred_element_type=F32)
            s = (s.reshape(rep, qb, kt) + bias_s[t][None]).reshape(rep * qb, kt)
            m_new = jnp.maximum(m, jnp.max(s, axis=1, keepdims=True))
            p = jnp.exp(s - m_new)
            pv = jnp.dot(p.astype(BF16), vx_ref[0, pl.ds(ks, kt), g * LANES:(g + 1) * LANES],
                         preferred_element_type=F32)
            out.append((m_new, acc * jnp.exp(m - m_new) + pv))
        return tuple(out)

    init = (jnp.full((rep * qb, 1), MASKED, F32), jnp.zeros((rep * qb, LANES), F32))
    res = lax.fori_loop(0, n_kt, att_tile, (init,) * KV_HEADS)
    for g in range(KV_HEADS):
        acc = res[g][1]
        out = acc * pltpu.roll(1.0 / acc, HEAD_DIM, 1)
        for r in range(rep):
            h = g * rep + r
            o_ref[:, h * HEAD_DIM:(h + 1) * HEAD_DIM] = out[r * qb:(r + 1) * qb, :HEAD_DIM]


def _dsa(q, qi, kw, k_bf, v_bf, ki_bf, *, qb, kt, q0, l_valid, n_sel):
    b, lp = ki_bf.shape[:2]
    nq = q.shape[0] // (b * qb)
    k_g = jnp.moveaxis(k_bf, 2, 1)
    ones = jnp.ones((b, lp, KV_HEADS, HEAD_DIM), BF16)
    vx = jnp.concatenate([v_bf, ones], axis=-1).reshape(b, lp, KV_HEADS * LANES)
    tr = lax.broadcasted_iota(jnp.int32, (LANES, 2 * LANES), 0)
    tc = lax.broadcasted_iota(jnp.int32, (LANES, 2 * LANES), 1)
    tri = jnp.logical_or(tr < tc, tc >= LANES).astype(BF16)
    row = lambda w: pl.BlockSpec((qb, w), lambda bi, i: (bi * nq + i, 0))
    kern = functools.partial(_dsa_kernel, qb=qb, kt=kt, q0=q0, l_valid=l_valid, n_sel=n_sel)
    return pl.pallas_call(
        kern,
        grid=(b, nq),
        in_specs=[row(ATT_WIDTH), row(QI_WIDTH), row(KW_WIDTH),
                  pl.BlockSpec((1, KV_HEADS, lp, HEAD_DIM), lambda bi, i: (bi, 0, 0, 0)),
                  pl.BlockSpec((1, lp, KV_HEADS * LANES), lambda bi, i: (bi, 0, 0)),
                  pl.BlockSpec((1, lp, IDX_DIM), lambda bi, i: (bi, 0, 0)),
                  pl.BlockSpec((LANES, 2 * LANES), lambda bi, i: (0, 0))],
        out_specs=row(ATT_WIDTH),
        out_shape=jax.ShapeDtypeStruct((q.shape[0], ATT_WIDTH), F32),
        scratch_shapes=[pltpu.VMEM((lp // kt, qb, kt), jnp.int32), pltpu.VMEM((lp // kt, qb, kt), F32),
                        pltpu.VMEM((IDX_HEADS, qb, LANES), F32)],
        compiler_params=_cparams(("arbitrary", "arbitrary")),
        name="dsa",
    )(q, qi, kw, k_g, vx, ki_bf, tri)


def _dsa_t_kernel(q_ref, qi_ref, kw_ref, k_ref, vxt_ref, ki_ref, tri_ref, o_ref, key_s, bias_s, s0_s, s1_s,
                  p0_s, p1_s, acc_s, *, qb, kt, q0, l_valid, n_sel):
    i = pl.program_id(1)
    ns = kt // LANES
    qpos = q0 + i * qb + lax.broadcasted_iota(jnp.int32, (1, qb), 1)
    lim = jnp.minimum((qpos // CHUNK + 1) * CHUNK, l_valid)
    last_lim = jnp.minimum(((q0 + i * qb + qb - 1) // CHUNK + 1) * CHUNK, l_valid)
    n_kt = (last_lim + kt - 1) // kt
    rep = ATT_HEADS // KV_HEADS

    def by_head(x_t, n):
        return jnp.concatenate([x_t[h * HEAD_DIM:(h + 1) * HEAD_DIM, :] for h in range(n)], axis=1)

    w_qi = by_head(qi_ref[...].T, IDX_HEADS).astype(BF16)
    q_t = q_ref[...].T * (HEAD_DIM ** -0.5)
    w_q = [by_head(q_t[g * rep * HEAD_DIM:(g + 1) * rep * HEAD_DIM, :], rep).astype(BF16) for g in range(KV_HEADS)]
    kw_t = kw_ref[...].T
    wi = [kw_t[IDX_DIM + h:IDX_DIM + h + 1, :] * (IDX_DIM ** -0.5) for h in range(IDX_HEADS)]

    def f32_key(x):
        bits = pltpu.bitcast(x, jnp.int32)
        return bits ^ ((bits >> 31) & 0x7FFFFFFF)

    def score_tile(t, c):
        ks = pl.multiple_of(t * kt, kt)
        lg = jnp.dot(ki_ref[0, pl.ds(ks, kt), :], w_qi, preferred_element_type=F32)
        sc = jnp.zeros((kt, qb), F32)
        for h in range(IDX_HEADS):
            sc = sc + wi[h] * jnp.maximum(lg[:, h * qb:(h + 1) * qb], 0.0)
        kpos = ks + lax.broadcasted_iota(jnp.int32, (kt, qb), 0)
        sc = jnp.where(kpos < lim, sc + 0.0, -jnp.inf)
        key_s[pl.ds(ks, kt), :] = f32_key(sc)
        return c

    lax.fori_loop(0, n_kt, score_tile, 0)

    def count(pred_fn):
        def tile(t, acc):
            ks = pl.multiple_of(t * kt, kt)
            for c in range(kt // COUNT_ROWS):
                chunk = key_s[pl.ds(ks + c * COUNT_ROWS, COUNT_ROWS), :]
                acc = acc + jnp.where(pred_fn(chunk), 1.0, 0.0)
            return acc
        acc = lax.fori_loop(0, n_kt, tile, jnp.zeros((COUNT_ROWS, qb), F32))
        return jnp.sum(acc, axis=0, keepdims=True)

    c_zero = count(lambda k: k >= 0)
    c_pos = count(lambda k: k >= 1)
    zero_tie = jnp.logical_and(c_zero >= n_sel, c_pos < n_sel)
    lo0 = jnp.where(c_zero >= n_sel, 0, INT_MIN).astype(jnp.int32)
    c_lo0 = jnp.where(c_zero >= n_sel, c_zero, (n_kt * kt).astype(F32))

    def unsettled(c_lo):
        return jnp.max(jnp.where(jnp.logical_or(c_lo == n_sel, zero_tie), 0.0, 1.0))

    def group_cond(st):
        g, lo, c_lo = st
        return jnp.logical_and(g * BIT_GROUP < 31, unsettled(c_lo) > 0.0)

    def group_step(st):
        g, lo, c_lo = st
        for j in range(BIT_GROUP):
            shift = 30 - (g * BIT_GROUP + j)
            bit = jnp.where(shift >= 0, jnp.left_shift(jnp.int32(1), jnp.maximum(shift, 0)), 0)
            cand = lo + bit
            cnt = count(lambda k: k >= cand)
            up = cnt >= n_sel
            lo = jnp.where(up, cand, lo)
            c_lo = jnp.where(up, cnt, c_lo)
        return g + 1, lo, c_lo

    _, thr, _ = lax.while_loop(group_cond, group_step, (jnp.int32(0), lo0, c_lo0))
    need = n_sel - count(lambda k: k > thr)

    def sel_tile(t, off):
        for s in range(ns):
            ks = pl.multiple_of(t * kt + s * LANES, LANES)
            key = key_s[pl.ds(ks, LANES), :]
            eq = key == thr
            pre = jnp.dot(tri_ref[...], jnp.where(eq, 1.0, 0.0).astype(BF16), preferred_element_type=F32)
            kpos = ks + lax.broadcasted_iota(jnp.int32, (LANES, qb), 0)
            keep = jnp.logical_or(key > thr, jnp.logical_and(eq, pre[:LANES] + off < need))
            keep = jnp.logical_and(keep, kpos < lim)
            bias_s[pl.ds(ks, LANES), :] = jnp.where(keep, 0.0, MASKED)
            off = off + pre[LANES:]
        return off

    lax.fori_loop(0, n_kt, sel_tile, jnp.zeros((LANES, qb), F32))

    nc = kt // ATT_ROWS
    width = rep * qb
    s_refs = (s0_s, s1_s)
    p_refs = (p0_s, p1_s)

    def score_chunk(t, g, c, mx):
        ks = pl.multiple_of(t * kt + c * ATT_ROWS, ATT_ROWS)
        sc = jnp.dot(k_ref[0, g, pl.ds(ks, ATT_ROWS), :], w_q[g], preferred_element_type=F32)
        bias = bias_s[pl.ds(ks, ATT_ROWS), :]
        sc = jnp.concatenate([sc[:, r * qb:(r + 1) * qb] + bias for r in range(rep)], axis=1)
        s_refs[g][c * ATT_ROWS:(c + 1) * ATT_ROWS, :] = sc
        return jnp.maximum(mx, jnp.max(sc.reshape(ATT_ROWS // 8, 8, width), axis=0))

    def prob_chunk(g, c, m_new):
        rows = slice(c * ATT_ROWS, (c + 1) * ATT_ROWS)
        p_refs[g][rows, :] = jnp.exp(s_refs[g][rows, :] - m_new).astype(BF16)

    def finish(t, g, m, m_new):
        pv = jnp.dot(vxt_ref[0, t, g * LANES:(g + 1) * LANES, :], p_refs[g][...], preferred_element_type=F32)
        acc_s[g] = acc_s[g] * jnp.exp(m - m_new) + pv

    mx_init = jnp.full((8, width), MASKED, F32)

    def att_tile(t, carry):
        m0, m1, mx0 = carry
        m0_new = jnp.maximum(m0, jnp.max(mx0, axis=0, keepdims=True))
        mx1 = mx_init
        for c in range(nc):
            mx1 = score_chunk(t, 1, c, mx1)
            prob_chunk(0, c, m0_new)
        m1_new = jnp.maximum(m1, jnp.max(mx1, axis=0, keepdims=True))
        finish(t, 0, m0, m0_new)
        t_next = jnp.minimum(t + 1, n_kt - 1)
        mx0 = mx_init
        for c in range(nc):
            mx0 = score_chunk(t_next, 0, c, mx0)
            prob_chunk(1, c, m1_new)
        finish(t, 1, m1, m1_new)
        return m0_new, m1_new, mx0

    acc_s[...] = jnp.zeros(acc_s.shape, F32)
    mx0 = mx_init
    for c in range(nc):
        mx0 = score_chunk(0, 0, c, mx0)
    m_init = jnp.full((1, width), MASKED, F32)
    lax.fori_loop(0, n_kt, att_tile, (m_init, m_init, mx0))
    for g in range(KV_HEADS):
        acc = acc_s[g]
        out = acc * (1.0 / acc[HEAD_DIM:HEAD_DIM + 1, :])
        for r in range(rep):
            h = g * rep + r
            o_ref[:, h * HEAD_DIM:(h + 1) * HEAD_DIM] = out[:, r * qb:(r + 1) * qb].T[:, :HEAD_DIM]


def _dsa_t(q, qi, kw, k_bf, v_bf, ki_bf, *, qb, kt, q0, l_valid, n_sel):
    assert qb == LANES
    b, lp = ki_bf.shape[:2]
    nq = q.shape[0] // (b * qb)
    k_g = jnp.moveaxis(k_bf, 2, 1)
    ones = jnp.ones((b, lp, KV_HEADS, HEAD_DIM), BF16)
    vx = jnp.concatenate([v_bf, ones], axis=-1).reshape(b, lp // kt, kt, KV_HEADS * LANES)
    vxt = jnp.swapaxes(vx, 2, 3)
    tr = lax.broadcasted_iota(jnp.int32, (2 * LANES, LANES), 0)
    tc = lax.broadcasted_iota(jnp.int32, (2 * LANES, LANES), 1)
    tri = jnp.logical_or(tc < tr, tr >= LANES).astype(BF16)
    row = lambda w: pl.BlockSpec((qb, w), lambda bi, i: (bi * nq + i, 0))
    kern = functools.partial(_dsa_t_kernel, qb=qb, kt=kt, q0=q0, l_valid=l_valid, n_sel=n_sel)
    return pl.pallas_call(
        kern,
        grid=(b, nq),
        in_specs=[row(ATT_WIDTH), row(QI_WIDTH), row(KW_WIDTH),
                  pl.BlockSpec((1, KV_HEADS, lp, HEAD_DIM), lambda bi, i: (bi, 0, 0, 0)),
                  pl.BlockSpec((1, lp // kt, KV_HEADS * LANES, kt), lambda bi, i: (bi, 0, 0, 0)),
                  pl.BlockSpec((1, lp, IDX_DIM), lambda bi, i: (bi, 0, 0)),
                  pl.BlockSpec((2 * LANES, LANES), lambda bi, i: (0, 0))],
        out_specs=row(ATT_WIDTH),
        out_shape=jax.ShapeDtypeStruct((q.shape[0], ATT_WIDTH), F32),
        scratch_shapes=[pltpu.VMEM((lp, qb), jnp.int32), pltpu.VMEM((lp, qb), F32),
                        pltpu.VMEM((kt, 4 * qb), F32), pltpu.VMEM((kt, 4 * qb), F32),
                        pltpu.VMEM((kt, 4 * qb), BF16), pltpu.VMEM((kt, 4 * qb), BF16),
                        pltpu.VMEM((KV_HEADS, LANES, 4 * qb), F32)],
        compiler_params=_cparams(("arbitrary", "arbitrary")),
        name="dsa_t",
    )(q, qi, kw, k_g, vxt, ki_bf, tri)


def _split_bf16(x):
    hi = x.astype(BF16)
    return hi, (x - hi.astype(F32)).astype(BF16)


_NN = (((1,), (0,)), ((), ()))


def _dots(a_sp, b_sp, dims=_NN):
    (ah, al), (bh, bl) = a_sp, b_sp
    d = functools.partial(lax.dot_general, dimension_numbers=dims, preferred_element_type=F32)
    return d(ah, bh) + (d(ah, bl) + d(al, bh))


def _dot3(a, b, dims=_NN):
    return _dots(_split_bf16(a), _split_bf16(b), dims)


def _dot2(a, b_exact):
    ah, al = _split_bf16(a)
    return jnp.dot(ah, b_exact, preferred_element_type=F32) + jnp.dot(al, b_exact, preferred_element_type=F32)


_NT = (((1,), (1,)), ((), ()))
_TN = (((0,), (0,)), ((), ()))


def _rwkv_kernel(p_ref, prev_ref, s0_ref, mu_ref, w0_ref, w2_ref, a0_ref, a2_ref, g2_ref, kk_ref, ka_ref,
                 rk_ref, lnw_ref, lnb_ref, hsum_ref, tri_ref, o_ref, st_ref, s_s, prev_s, o_s, *, c, nb):
    j = pl.program_id(1)

    @pl.when(j == 0)
    def _init():
        s_s[...] = s0_ref[...]
        prev_s[...] = prev_ref[...]

    hsum = hsum_ref[...]
    tri = tri_ref[...]
    row = lax.broadcasted_iota(jnp.int32, (c, 1), 0)
    o1, o2, o3 = RWKV_WIDTH, 2 * RWKV_WIDTH, 3 * RWKV_WIDTH

    def prepare(bi):
        p = p_ref[bi]
        p_prev = jnp.where(row == 0, prev_s[bi], pltpu.roll(p, 1, 0))
        prev_s[bi] = p[c - 1:c, :]
        xs = p + (p_prev - p) * mu_ref[...]
        r, k, v = xs[:, :o1], xs[:, o1:o2], xs[:, o2:o3]
        dw = xs[:, o3:o3 + W_LORA]
        da = xs[:, o3 + W_LORA:o3 + W_LORA + A_LORA]
        dg = xs[:, o3 + W_LORA + A_LORA:]
        w_log = -jax.nn.softplus(-(w0_ref[...] + _dot3(jnp.tanh(dw), w2_ref[...]))) - 0.5
        lw = -jnp.exp(w_log)
        a = jax.nn.sigmoid(a0_ref[...] + _dot3(da, a2_ref[...]))
        g = _dot3(jax.nn.sigmoid(dg), g2_ref[...])
        kk = k * kk_ref[...]
        kk = kk / jnp.maximum(jnp.sqrt(_dot2(kk * kk, hsum)), 1e-12)
        km = k * (1.0 + (a - 1.0) * ka_ref[...])
        bm = kk * a
        lw_hi, lw_lo = _split_bf16(lw)
        cum = (jnp.dot(tri, lw_hi, preferred_element_type=F32)
               + jnp.dot(tri, lw_lo, preferred_element_type=F32))
        tot = cum[c - 1:c, :]
        e_out = jnp.exp(-cum)
        e_end = jnp.exp(tot - cum)
        return dict(a_t=-kk * jnp.exp(cum - lw), r_t=r * jnp.exp(cum), b_t=bm * e_out, k_t=km * e_out,
                    b_h=bm * e_end, k_h=km * e_end, gam=jnp.exp(tot), r=r, v=v, km=km, g=g)

    pre = [prepare(bi) for bi in range(nb)]

    ri = lax.broadcasted_iota(jnp.int32, (c, c), 0)
    ci = lax.broadcasted_iota(jnp.int32, (c, c), 1)
    strict = ri > ci
    incl = ri >= ci
    eye = jnp.where(ri == ci, 1.0, 0.0)
    n_double = int(np.log2(c)) - 1
    chains = [(bi, h) for bi in range(nb) for h in range(RWKV_HEADS)]
    ids = range(len(chains))
    col = lambda name, i: pre[chains[i][0]][name][:, chains[i][1] * HEAD_DIM:(chains[i][1] + 1) * HEAD_DIM]
    s0 = [s_s[bi, h] for bi, h in chains]
    s0_sp = [_split_bf16(s) for s in s0]
    ar_sp = [_split_bf16(jnp.concatenate([col("a_t", i), col("r_t", i)], axis=0)) for i in ids]
    bk_sp = [_split_bf16(jnp.concatenate([col("b_t", i), col("k_t", i)], axis=0)) for i in ids]
    v_sp = [_split_bf16(col("v", i)) for i in ids]
    m = [_dots(ar_sp[i], bk_sp[i], _NT) for i in ids]
    as0 = [_dots(ar_sp[i], s0_sp[i], _NT) for i in ids]
    x_sp = [_split_bf16(jnp.where(strict, m[i][:c, :c], 0.0)) for i in ids]
    lak_sp = [_split_bf16(jnp.where(strict, m[i][:c, c:], 0.0)) for i in ids]
    tinv = [eye + jnp.where(strict, m[i][:c, :c], 0.0) for i in ids]
    rhs = [as0[i][:c] + _dots(lak_sp[i], v_sp[i]) for i in ids]
    for _ in range(n_double):
        x_sp = [_split_bf16(_dots(x_sp[i], x_sp[i])) for i in ids]
        tinv = [tinv[i] + _dots(_split_bf16(tinv[i]), x_sp[i]) for i in ids]
    u_sp = [_split_bf16(_dot3(tinv[i], rhs[i])) for i in ids]
    for i in ids:
        bi, h = chains[i]
        m_rb = jnp.where(incl, m[i][c:, :c], 0.0)
        m_rk = jnp.where(incl, m[i][c:, c:], 0.0)
        o_s[bi, :, h * HEAD_DIM:(h + 1) * HEAD_DIM] = (as0[i][c:] + _dots(_split_bf16(m_rb), u_sp[i])
                                                       + _dots(_split_bf16(m_rk), v_sp[i]))
    s_new = [s0[i] * col("gam", i) + _dots(u_sp[i], _split_bf16(col("b_h", i)), _TN)
             + _dots(v_sp[i], _split_bf16(col("k_h", i)), _TN) for i in ids]
    for i in ids:
        s_s[chains[i][0], chains[i][1]] = s_new[i]

    inv_d = 1.0 / HEAD_DIM
    for bi in range(nb):
        o = o_s[bi]
        mean = _dot2(o, hsum) * inv_d
        cen = o - mean
        var = _dot2(cen * cen, hsum) * inv_d
        on = cen * lax.rsqrt(var + GN_EPS) * lnw_ref[...] + lnb_ref[...]
        bonus = _dot2(pre[bi]["r"] * pre[bi]["km"] * rk_ref[...], hsum) * pre[bi]["v"]
        o_ref[bi] = (on + bonus) * pre[bi]["g"]

    @pl.when(j == pl.num_programs(1) - 1)
    def _fin():
        st_ref[...] = s_s[...]


def _rwkv(p, prev, s0, rw, c, nb):
    mu, w0, w2, a0, a2, g2, k_k, k_a, r_k, lnx_w, lnx_b = rw
    b, t, _ = p.shape
    nc = t // c
    hid = lax.broadcasted_iota(jnp.int32, (RWKV_WIDTH, RWKV_WIDTH), 0) // HEAD_DIM
    hsum = (hid == hid.T).astype(BF16)
    tri = (lax.broadcasted_iota(jnp.int32, (c, c), 0) >= lax.broadcasted_iota(jnp.int32, (c, c), 1)).astype(BF16)
    vec = lambda a: a.reshape(1, -1)
    full = lambda a: pl.BlockSpec(a.shape, lambda bi, j: (0,) * a.ndim)
    small = [vec(mu), vec(w0), w2, vec(a0), a2, g2, vec(k_k), vec(k_a), vec(r_k), vec(lnx_w), vec(lnx_b), hsum, tri]
    state = pl.BlockSpec((nb, RWKV_HEADS, HEAD_DIM, HEAD_DIM), lambda bi, j: (bi, 0, 0, 0))
    return pl.pallas_call(
        functools.partial(_rwkv_kernel, c=c, nb=nb),
        grid=(b // nb, nc),
        in_specs=[pl.BlockSpec((nb, c, RWKV_COLS), lambda bi, j: (bi, j, 0)),
                  pl.BlockSpec((nb, 1, RWKV_COLS), lambda bi, j: (bi, 0, 0)), state] + [full(a) for a in small],
        out_specs=[pl.BlockSpec((nb, c, RWKV_WIDTH), lambda bi, j: (bi, j, 0)), state],
        out_shape=[jax.ShapeDtypeStruct((b, t, RWKV_WIDTH), F32),
                   jax.ShapeDtypeStruct((b, RWKV_HEADS, HEAD_DIM, HEAD_DIM), F32)],
        scratch_shapes=[pltpu.VMEM((nb, RWKV_HEADS, HEAD_DIM, HEAD_DIM), F32), pltpu.VMEM((nb, 1, RWKV_COLS), F32),
                        pltpu.VMEM((nb, c, RWKV_WIDTH), F32)],
        compiler_params=_cparams(("arbitrary", "arbitrary")),
        name="rwkv",
    )(p, prev, s0, *small)


def _branch(x, mods, pos, prev, s0, caches, weights, tm, tm_moe):
    (g_mix, g_ffn, g_final, w_r, w_a, rw, w_out, rwt, rb, wup, bup, wdn, bdn) = weights
    sh1, sc1, gt1, sh2, sc2, gt2 = mods
    b, t, d = x.shape
    tab, tabk = _rope_tables(pos)
    p_r, q, k, v, qi, kw = _inproj(x, g_mix, sh1, sc1, w_r, w_a, tab, tabk, tm)
    p_r3 = p_r.reshape(b, t, RWKV_COLS)
    o_r, wkv = _rwkv(p_r3, prev, s0, rw, min(CHUNK, t), RWKV_BATCH)
    shift = p_r3[:, -1:]
    k4 = k.reshape(b, t, KV_HEADS, HEAD_DIM)
    v4 = v.reshape(b, t, KV_HEADS, HEAD_DIM)
    ki = kw.reshape(b, t, KW_WIDTH)[..., :IDX_DIM]
    kt = 1024 if caches is None else 512
    if caches is None:
        k_all, v_all, ki_all, q0, l_valid = k4, v4, ki, 0, t
    else:
        ck, cv, cki = caches
        q0 = ck.shape[1]
        l_valid = q0 + t
        padn = (-l_valid) % kt
        zpad = lambda a: jnp.concatenate([a, jnp.zeros((b, padn) + a.shape[2:], a.dtype)], axis=1)
        k_all = zpad(jnp.concatenate([ck, k4], axis=1))
        v_all = zpad(jnp.concatenate([cv, v4], axis=1))
        ki_all = zpad(jnp.concatenate([cki, ki], axis=1))
    qb = min(Q_BLOCK, t)
    dsa = _dsa_t if qb == LANES else _dsa
    o_a = dsa(q, qi, kw, k_all.astype(BF16), v_all.astype(BF16), ki_all.astype(BF16),
              qb=qb, kt=kt, q0=q0, l_valid=l_valid, n_sel=min(TOPK_MAX, l_valid // 4))
    x1, h2 = _outproj(o_r, o_a, x, w_out, gt1, g_ffn, sh2, sc2, tm)
    y_moe = _moe(h2, rwt, rb, wup, bup, wdn, bdn, *tm_moe)
    y = _final(x1, y_moe, gt2, g_final, b, t, tm)
    return y, k4, v4, ki, wkv, shift


def kernel(x_prompt, x_sample, c_prompt, c_sample, cache_k, cache_v, cache_kidx, state_wkv, state_shift,
           w_ada, b_ada, g_mix, g_ffn, g_final, w_in, mu_shift, w0, w_lora_w, a0, w_lora_a, w_lora_g,
           k_k, k_a, r_k, lnx_w, lnx_b, w_out, router_w, router_b, w_up, b_up, w_down, b_down):
    B, T, D = x_prompt.shape
    DB, DT, _ = x_sample.shape
    P = cache_k.shape[2]
    l = 0
    assert w_ada.shape[0] == 1

    rows = B + DB
    pad = (-rows) % 8
    c_all = jnp.concatenate([c_prompt, c_sample, jnp.zeros((pad, D), F32)], axis=0)
    m = _ada(c_all, w_ada[l], b_ada[l])
    mods_p = tuple(t.reshape(B, 1, D) for t in jnp.split(m[:B], 6, axis=-1))
    mods_s = tuple(t.reshape(DB, 1, D) for t in jnp.split(m[B:rows], 6, axis=-1))

    w_r = w_in[l][:, :RWKV_COLS].astype(BF16)
    w_att = w_in[l][:, RWKV_COLS:]
    att_pad = KW_WIDTH - IDX_DIM - IDX_HEADS
    w_a = jnp.concatenate([w_att, jnp.zeros((D, att_pad), F32)], axis=1).astype(BF16)
    rw = (mu_shift[l], w0[l], w_lora_w[l], a0[l], w_lora_a[l], w_lora_g[l], k_k[l], k_a[l], r_k[l],
          lnx_w[l], lnx_b[l])
    wup = _deinterleave_cast(w_up[l])
    bup = jnp.concatenate([b_up[l][..., 0::2], b_up[l][..., 1::2]], axis=-1).reshape(N_EXPERTS, 1, 2 * D_FF)
    wdn = w_down[l].astype(BF16)
    bdn = b_down[l].reshape(N_EXPERTS, 1, D)
    weights = (g_mix[l], g_ffn[l], g_final, w_r, w_a, rw, w_out[l].astype(BF16), router_w[l].T,
               router_b[l].reshape(N_EXPERTS, 1), wup, bup, wdn, bdn)

    pos_p = jnp.arange(T, dtype=jnp.int32)
    pos_s = P + jnp.arange(DT, dtype=jnp.int32)
    yp, kp, vp, kip, wkvp, shp = _branch(
        x_prompt, mods_p, pos_p, jnp.zeros((B, 1, RWKV_COLS), F32),
        jnp.zeros((B, RWKV_HEADS, HEAD_DIM, HEAD_DIM), F32), None, weights, 512, (MOE_SUB, MOE_NSUB))
    ys, ks, vs, kis, wkvs, shs = _branch(
        x_sample, mods_s, pos_s, state_shift[l], state_wkv[l],
        (cache_k[l], cache_v[l], cache_kidx[l]), weights, DT, (DB * DT, 1))
    return (yp, ys, kp[None], vp[None], kip[None], wkvp[None], shp[None],
            ks[None], vs[None], kis[None], wkvs[None], shs[None])
```

```python
import functools

import numpy as np
import jax
import jax.numpy as jnp
from jax import lax
from jax.experimental import pallas as pl
from jax.experimental.pallas import tpu as pltpu

F32 = jnp.float32
BF16 = jnp.bfloat16
HIGHEST = lax.Precision.HIGHEST

D_MODEL = 1024
CHUNK = 64
HEAD_DIM = 64
RWKV_HEADS = 8
RWKV_WIDTH = RWKV_HEADS * HEAD_DIM
W_LORA = 64
A_LORA = 64
G_LORA = 128
RWKV_COLS = 3 * RWKV_WIDTH + W_LORA + A_LORA + G_LORA
GN_EPS = 64e-5
ATT_HEADS = 8
ATT_WIDTH = ATT_HEADS * HEAD_DIM
KV_HEADS = 2
KV_WIDTH = KV_HEADS * HEAD_DIM
IDX_HEADS = 4
IDX_DIM = 64
TOPK_MAX = 256
Q_BLOCK = 128
ROPE_THETA = 500000.0
ROT_DIM = HEAD_DIM // 4
N_EXPERTS = 32
TOP_K = 4
D_FF = 1024
SWIGLU_ALPHA = 1.702
SWIGLU_LIMIT = 7.0
NORM_EPS = 1e-5

LANES = 128
MOE_ROWS = 128
MOE_SUB = 768
MOE_NSUB = 3
VMEM_LIMIT = 56 * 1024 * 1024

QI_WIDTH = IDX_HEADS * IDX_DIM
KW_WIDTH = LANES


def _cparams(sem):
    return pltpu.CompilerParams(dimension_semantics=sem, vmem_limit_bytes=VMEM_LIMIT)


def _ada_kernel(c_ref, w_ref, b_ref, o_ref):
    c = c_ref[...]
    s = c * jax.nn.sigmoid(c)
    o_ref[...] = jnp.dot(s, w_ref[...], precision=HIGHEST, preferred_element_type=F32) + b_ref[...]


def _ada(c, w, b):
    rows, d = c.shape
    n = w.shape[1]
    tn = 1536
    return pl.pallas_call(
        _ada_kernel,
        grid=(n // tn,),
        in_specs=[pl.BlockSpec((rows, d), lambda j: (0, 0)),
                  pl.BlockSpec((d, tn), lambda j: (0, j)),
                  pl.BlockSpec((1, tn), lambda j: (0, j))],
        out_specs=pl.BlockSpec((rows, tn), lambda j: (0, j)),
        out_shape=jax.ShapeDtypeStruct((rows, n), F32),
        compiler_params=_cparams(("arbitrary",)),
        name="ada",
    )(c, w, b.reshape(1, n))


def _rope_slab(y, tab_ref):
    return (y * tab_ref[0] + pltpu.roll(y, LANES - ROT_DIM // 2, 1) * tab_ref[1]
            + pltpu.roll(y, ROT_DIM // 2, 1) * tab_ref[2])


def _norm_mod(x, g, sh, sc):
    var = jnp.mean(x * x, axis=-1, keepdims=True)
    return (x * lax.rsqrt(var + NORM_EPS) * g) * (1.0 + sc) + sh


def _inproj_kernel(x_ref, g_ref, sh_ref, sc_ref, wr_ref, wa_ref, tab_ref, tabk_ref,
                   rw_ref, q_ref, k_ref, v_ref, qi_ref, kw_ref):
    h = _norm_mod(x_ref[...], g_ref[...], sh_ref[0], sc_ref[0]).astype(BF16)
    rw_ref[...] = jnp.dot(h, wr_ref[...], preferred_element_type=F32)
    pa = jnp.dot(h, wa_ref[...], preferred_element_type=F32)
    off = 0
    for ref, width, rot in ((q_ref, ATT_WIDTH, True), (k_ref, KV_WIDTH, True), (v_ref, KV_WIDTH, False),
                            (qi_ref, QI_WIDTH, True)):
        for s in range(width // LANES):
            slab = pa[:, off + s * LANES: off + (s + 1) * LANES]
            ref[:, s * LANES:(s + 1) * LANES] = _rope_slab(slab, tab_ref) if rot else slab
        off += width
    kw_ref[...] = _rope_slab(pa[:, off:off + LANES], tabk_ref)


def _rope_tables(pos):
    half = ROT_DIM // 2
    inv = ROPE_THETA ** (-jnp.arange(0, ROT_DIM, 2, dtype=F32) / ROT_DIM)
    ang = pos.astype(F32)[:, None] * inv[None, :]
    cos, sin = jnp.cos(ang), jnp.sin(ang)
    t = pos.shape[0]
    one = jnp.ones((t, HEAD_DIM - ROT_DIM), F32)
    zero_r = jnp.zeros((t, HEAD_DIM - ROT_DIM), F32)
    zero_h = jnp.zeros((t, half), F32)
    c_head = jnp.concatenate([cos, cos, one], axis=1)
    up_head = jnp.concatenate([-sin, zero_h, zero_r], axis=1)
    dn_head = jnp.concatenate([zero_h, sin, zero_r], axis=1)
    tab = jnp.stack([jnp.tile(c_head, (1, 2)), jnp.tile(up_head, (1, 2)), jnp.tile(dn_head, (1, 2))])
    wscale = jnp.concatenate([jnp.full((t, IDX_HEADS), IDX_HEADS ** -0.5, F32),
                              jnp.ones((t, HEAD_DIM - IDX_HEADS), F32)], axis=1)
    zero64 = jnp.zeros((t, HEAD_DIM), F32)
    tabk = jnp.stack([jnp.concatenate([c_head, wscale], axis=1),
                      jnp.concatenate([up_head, zero64], axis=1),
                      jnp.concatenate([dn_head, zero64], axis=1)])
    return tab, tabk


def _inproj(x, g, sh, sc, w_r, w_a, tab, tabk, tm):
    b, t, d = x.shape
    nt = t // tm
    n = b * t
    x2 = x.reshape(n, d)
    widths = (RWKV_COLS, ATT_WIDTH, KV_WIDTH, KV_WIDTH, QI_WIDTH, KW_WIDTH)
    row = lambda w: pl.BlockSpec((tm, w), lambda i: (i, 0))
    mod = pl.BlockSpec((1, 1, d), lambda i: (i // nt, 0, 0))
    tabspec = pl.BlockSpec((3, tm, LANES), lambda i: (0, i % nt, 0))
    return pl.pallas_call(
        _inproj_kernel,
        grid=(n // tm,),
        in_specs=[row(d), pl.BlockSpec((1, d), lambda i: (0, 0)), mod, mod,
                  pl.BlockSpec(w_r.shape, lambda i: (0, 0)), pl.BlockSpec(w_a.shape, lambda i: (0, 0)),
                  tabspec, tabspec],
        out_specs=[row(w) for w in widths],
        out_shape=[jax.ShapeDtypeStruct((n, w), F32) for w in widths],
        compiler_params=_cparams(("arbitrary",)),
        name="inproj",
    )(x2, g.reshape(1, d), sh, sc, w_r, w_a, tab, tabk)


def _outproj_kernel(or_ref, oa_ref, x_ref, w_ref, gt_ref, g_ref, sh_ref, sc_ref, x1_ref, h2_ref):
    m = jnp.dot(or_ref[...].astype(BF16), w_ref[:RWKV_WIDTH, :], preferred_element_type=F32)
    m = m + jnp.dot(oa_ref[...].astype(BF16), w_ref[RWKV_WIDTH:, :], preferred_element_type=F32)
    x1 = x_ref[...] + gt_ref[0] * m
    x1_ref[...] = x1
    h2_ref[...] = _norm_mod(x1, g_ref[...], sh_ref[0], sc_ref[0]).astype(BF16)


def _outproj(o_r, o_a, x, w_out, gt, g, sh, sc, tm):
    b, t, d = x.shape
    nt = t // tm
    n = b * t
    row = lambda w: pl.BlockSpec((tm, w), lambda i: (i, 0))
    mod = pl.BlockSpec((1, 1, d), lambda i: (i // nt, 0, 0))
    return pl.pallas_call(
        _outproj_kernel,
        grid=(n // tm,),
        in_specs=[row(RWKV_WIDTH), row(ATT_WIDTH), row(d), pl.BlockSpec(w_out.shape, lambda i: (0, 0)),
                  mod, pl.BlockSpec((1, d), lambda i: (0, 0)), mod, mod],
        out_specs=[row(d), row(d)],
        out_shape=[jax.ShapeDtypeStruct((n, d), F32), jax.ShapeDtypeStruct((n, d), BF16)],
        compiler_params=_cparams(("arbitrary",)),
        name="outproj",
    )(o_r.reshape(n, RWKV_WIDTH), o_a.reshape(n, ATT_WIDTH), x.reshape(n, d), w_out, gt, g.reshape(1, d), sh, sc)


def _moe_kernel(h_ref, rwt_ref, rb_ref, tri_ref, wup_ref, bup_ref, wdn_ref, bdn_ref, y_ref,
                hb_s, rank_s, gate_s, count_s, *, n_tokens, sub):
    e = pl.program_id(1)
    n_sub = h_ref.shape[0] // sub

    @pl.when(e == 0)
    def _route():
        rwt = rwt_ref[...].astype(BF16)
        for s in range(n_sub):
            first_row = pl.program_id(0) * (n_sub * sub) + s * sub
            in_rows = first_row + lax.broadcasted_iota(jnp.int32, (sub, 1), 0) < n_tokens
            in_cols = first_row + lax.broadcasted_iota(jnp.int32, (1, sub), 1) < n_tokens
            hb = jnp.where(in_rows, h_ref[s * sub:(s + 1) * sub, :], 0.0).astype(BF16)
            hb_s[s * sub:(s + 1) * sub, :] = hb
            logits = lax.dot_general(rwt, hb, (((1,), (1,)), ((), ())), preferred_element_type=F32) + rb_ref[...]
            eidx = lax.broadcasted_iota(jnp.int32, logits.shape, 0)
            work = logits
            top = None
            for _ in range(TOP_K):
                m = jnp.max(work, axis=0, keepdims=True)
                if top is None:
                    top = m
                first = jnp.min(jnp.where(work == m, eidx, N_EXPERTS), axis=0, keepdims=True)
                work = jnp.where(eidx == first, -jnp.inf, work)
            ex = jnp.where(work != logits, jnp.exp(logits - top), 0.0)
            sel = jnp.logical_and(work != logits, in_cols)
            gate_s[s] = ex / jnp.sum(ex, axis=0, keepdims=True)
            before = jnp.dot(jnp.where(sel, 1.0, 0.0).astype(BF16), tri_ref[...], preferred_element_type=F32)
            rank_s[s] = jnp.where(sel, before, -1.0)
            for ex_id in range(N_EXPERTS):
                count_s[s, ex_id] = jnp.sum(jnp.where(sel[ex_id:ex_id + 1, :], 1, 0))
        y_ref[...] = jnp.zeros(y_ref.shape, F32)

    for s in range(n_sub):
        r_row = rank_s[s, pl.ds(e, 1), :]
        g_row = gate_s[s, pl.ds(e, 1), :]
        n_blocks = (count_s[s, e] + MOE_ROWS - 1) // MOE_ROWS

        def block(j, carry, s=s, r_row=r_row, g_row=g_row):
            rows = (lax.broadcasted_iota(jnp.int32, (MOE_ROWS, sub), 0) + j * MOE_ROWS).astype(F32)
            hit = r_row == rows
            p = jnp.where(hit, 1.0, 0.0).astype(BF16)
            xe = jnp.dot(p, hb_s[s * sub:(s + 1) * sub, :], preferred_element_type=F32).astype(BF16)
            u = jnp.dot(xe, wup_ref[0], preferred_element_type=F32) + bup_ref[0]
            glu = jnp.minimum(u[:, :D_FF], SWIGLU_LIMIT)
            lin = jnp.clip(u[:, D_FF:], -SWIGLU_LIMIT, SWIGLU_LIMIT)
            act = glu * jax.nn.sigmoid(SWIGLU_ALPHA * glu) * (lin + 1.0)
            yb = jnp.dot(act.astype(BF16), wdn_ref[0], preferred_element_type=F32) + bdn_ref[0]
            g_rows = jnp.sum(jnp.where(hit, g_row, 0.0), axis=1, keepdims=True)
            ys = (yb * g_rows).astype(BF16)
            y_ref[s * sub:(s + 1) * sub, :] += lax.dot_general(p, ys, (((0,), (0,)), ((), ())),
                                                               preferred_element_type=F32)
            return carry

        lax.fori_loop(0, n_blocks, block, 0)


def _moe(h, rwt, rb, wup, bup, wdn, bdn, sub, n_sub):
    n, d = h.shape
    tm = sub * n_sub
    tri = (lax.broadcasted_iota(jnp.int32, (sub, sub), 0) < lax.broadcasted_iota(jnp.int32, (sub, sub), 1)).astype(BF16)
    return pl.pallas_call(
        functools.partial(_moe_kernel, n_tokens=n, sub=sub),
        grid=(pl.cdiv(n, tm), N_EXPERTS),
        in_specs=[pl.BlockSpec((tm, d), lambda i, e: (i, 0)),
                  pl.BlockSpec((N_EXPERTS, d), lambda i, e: (0, 0)),
                  pl.BlockSpec((N_EXPERTS, 1), lambda i, e: (0, 0)),
                  pl.BlockSpec((sub, sub), lambda i, e: (0, 0)),
                  pl.BlockSpec((1, d, 2 * D_FF), lambda i, e: (e, 0, 0)),
                  pl.BlockSpec((1, 1, 2 * D_FF), lambda i, e: (e, 0, 0)),
                  pl.BlockSpec((1, D_FF, d), lambda i, e: (e, 0, 0)),
                  pl.BlockSpec((1, 1, d), lambda i, e: (e, 0, 0))],
        out_specs=pl.BlockSpec((tm, d), lambda i, e: (i, 0)),
        out_shape=jax.ShapeDtypeStruct((n, d), F32),
        scratch_shapes=[pltpu.VMEM((tm, d), BF16), pltpu.VMEM((n_sub, N_EXPERTS, sub), F32),
                        pltpu.VMEM((n_sub, N_EXPERTS, sub), F32), pltpu.SMEM((n_sub, N_EXPERTS), jnp.int32)],
        compiler_params=_cparams(("arbitrary", "arbitrary")),
        name="moe",
    )(h, rwt, rb, tri, wup, bup, wdn, bdn)


def _deinterleave_kernel(w_ref, perm_ref, o_ref):
    n = w_ref.shape[2]
    blk = 2 * LANES
    for j in range(n // blk):
        r = jnp.dot(w_ref[0, :, j * blk:(j + 1) * blk].astype(BF16), perm_ref[...],
                    preferred_element_type=F32).astype(BF16)
        o_ref[0, :, j * LANES:(j + 1) * LANES] = r[:, :LANES]
        o_ref[0, :, n // 2 + j * LANES:n // 2 + (j + 1) * LANES] = r[:, LANES:]


def _deinterleave_cast(w):
    e, d, n = w.shape
    tk = 512
    blk = 2 * LANES
    src = lax.broadcasted_iota(jnp.int32, (blk, blk), 0)
    dst = lax.broadcasted_iota(jnp.int32, (blk, blk), 1)
    perm = (src == jnp.where(dst < LANES, 2 * dst, 2 * (dst - LANES) + 1)).astype(BF16)
    return pl.pallas_call(
        _deinterleave_kernel,
        grid=(e, d // tk),
        in_specs=[pl.BlockSpec((1, tk, n), lambda i, j: (i, j, 0)), pl.BlockSpec((blk, blk), lambda i, j: (0, 0))],
        out_specs=pl.BlockSpec((1, tk, n), lambda i, j: (i, j, 0)),
        out_shape=jax.ShapeDtypeStruct((e, d, n), BF16),
        compiler_params=_cparams(("arbitrary", "arbitrary")),
        name="deinterleave",
    )(w, perm)


def _final_kernel(x1_ref, y_ref, gt_ref, g_ref, o_ref):
    x = x1_ref[...] + gt_ref[0] * y_ref[...]
    var = jnp.mean(x * x, axis=-1, keepdims=True)
    o_ref[...] = x * lax.rsqrt(var + NORM_EPS) * g_ref[...]


def _final(x1, y, gt, g, b, t, tm):
    n, d = x1.shape
    nt = t // tm
    row = pl.BlockSpec((tm, d), lambda i: (i, 0))
    return pl.pallas_call(
        _final_kernel,
        grid=(n // tm,),
        in_specs=[row, row, pl.BlockSpec((1, 1, d), lambda i: (i // nt, 0, 0)),
                  pl.BlockSpec((1, d), lambda i: (0, 0))],
        out_specs=row,
        out_shape=jax.ShapeDtypeStruct((n, d), F32),
        compiler_params=_cparams(("arbitrary",)),
        name="final",
    )(x1, y, gt, g.reshape(1, d)).reshape(b, t, d)


MASKED = -1e30
INT_MIN = -2 ** 31
COUNT_ROWS = 64
ATT_ROWS = 128
RWKV_BATCH = 2


def _dsa_kernel(q_ref, qi_ref, kw_ref, k_ref, vx_ref, ki_ref, tri_ref, o_ref, key_s, bias_s, wib_s,
                *, qb, kt, q0, l_valid, n_sel):
    i = pl.program_id(1)
    ns = kt // LANES
    row = lax.broadcasted_iota(jnp.int32, (qb, 1), 0)
    qpos = q0 + i * qb + row
    lim = jnp.minimum((qpos // CHUNK + 1) * CHUNK, l_valid)
    last_lim = jnp.minimum(((q0 + i * qb + qb - 1) // CHUNK + 1) * CHUNK, l_valid)
    n_kt = (last_lim + kt - 1) // kt
    lane = lax.broadcasted_iota(jnp.int32, (1, LANES), 1)

    for h in range(IDX_HEADS):
        w = kw_ref[:, IDX_DIM + h:IDX_DIM + h + 1] * (IDX_DIM ** -0.5)
        wib_s[h] = jnp.broadcast_to(w, (qb, LANES))
    qi = qi_ref[...].astype(BF16)

    def f32_key(x):
        bits = pltpu.bitcast(x, jnp.int32)
        return bits ^ ((bits >> 31) & 0x7FFFFFFF)

    def score_tile(t, c):
        ks = pl.multiple_of(t * kt, kt)
        kit = ki_ref[0, pl.ds(ks, kt), :]
        lg = [lax.dot_general(qi[:, h * IDX_DIM:(h + 1) * IDX_DIM], kit, (((1,), (1,)), ((), ())),
                              preferred_element_type=F32) for h in range(IDX_HEADS)]
        for s in range(ns):
            sc = jnp.zeros((qb, LANES), F32)
            for h in range(IDX_HEADS):
                sc = sc + wib_s[h] * jnp.maximum(lg[h][:, s * LANES:(s + 1) * LANES], 0.0)
            kpos = ks + s * LANES + lane
            sc = jnp.where(kpos < lim, sc + 0.0, -jnp.inf)
            key_s[t, :, s * LANES:(s + 1) * LANES] = f32_key(sc)
        return c

    lax.fori_loop(0, n_kt, score_tile, 0)

    def count(pred_fn):
        def tile(t, acc):
            for s in range(ns):
                acc = acc + jnp.where(pred_fn(key_s[t, :, s * LANES:(s + 1) * LANES]), 1.0, 0.0)
            return acc
        acc = lax.fori_loop(0, n_kt, tile, jnp.zeros((qb, LANES), F32))
        return jnp.sum(acc, axis=1, keepdims=True)

    def bit_step(b, lo):
        cand = lo + jnp.left_shift(jnp.int32(1), 31 - b)
        candb = jnp.broadcast_to(cand, (qb, LANES))
        return jnp.where(count(lambda k: k >= candb) >= n_sel, cand, lo)

    thr = lax.fori_loop(0, 32, bit_step, jnp.full((qb, 1), INT_MIN, jnp.int32))
    thrb = jnp.broadcast_to(thr, (qb, LANES))
    need = jnp.broadcast_to(n_sel - count(lambda k: k > thrb), (qb, LANES))

    def sel_tile(t, off):
        for s in range(ns):
            key = key_s[t, :, s * LANES:(s + 1) * LANES]
            eq = key == thrb
            pre = jnp.dot(jnp.where(eq, 1.0, 0.0).astype(BF16), tri_ref[...], preferred_element_type=F32)
            kpos = t * kt + s * LANES + lane
            keep = jnp.logical_or(key > thrb, jnp.logical_and(eq, pre[:, :LANES] + off < need))
            keep = jnp.logical_and(keep, kpos < lim)
            bias_s[t, :, s * LANES:(s + 1) * LANES] = jnp.where(keep, 0.0, MASKED)
            off = off + pre[:, LANES:]
        return off

    lax.fori_loop(0, n_kt, sel_tile, jnp.zeros((qb, LANES), F32))

    rep = ATT_HEADS // KV_HEADS
    qgs = []
    for g in range(KV_HEADS):
        qg = jnp.concatenate([q_ref[:, (g * rep + r) * HEAD_DIM:(g * rep + r + 1) * HEAD_DIM] for r in range(rep)],
                             axis=0)
        qgs.append((qg * (HEAD_DIM ** -0.5)).astype(BF16))

    def att_tile(t, carry):
        ks = pl.multiple_of(t * kt, kt)
        out = []
        for g in range(KV_HEADS):
            m, acc = carry[g]
            s = lax.dot_general(qgs[g], k_ref[0, g, pl.ds(ks, kt), :], (((1,), (1,)), ((), ())),
                                preferred_element_type=F32)
            s = (s.reshape(rep, qb, kt) + bias_s[t][None]).reshape(rep * qb, kt)
            m_new = jnp.maximum(m, jnp.max(s, axis=1, keepdims=True))
            p = jnp.exp(s - m_new)
            pv = jnp.dot(p.astype(BF16), vx_ref[0, pl.ds(ks, kt), g * LANES:(g + 1) * LANES],
                         preferred_element_type=F32)
            out.append((m_new, acc * jnp.exp(m - m_new) + pv))
        return tuple(out)

    init = (jnp.full((rep * qb, 1), MASKED, F32), jnp.zeros((rep * qb, LANES), F32))
    res = lax.fori_loop(0, n_kt, att_tile, (init,) * KV_HEADS)
    for g in range(KV_HEADS):
        acc = res[g][1]
        out = acc * pltpu.roll(1.0 / acc, HEAD_DIM, 1)
        for r in range(rep):
            h = g * rep + r
            o_ref[:, h * HEAD_DIM:(h + 1) * HEAD_DIM] = out[r * qb:(r + 1) * qb, :HEAD_DIM]


def _dsa(q, qi, kw, k_bf, v_bf, ki_bf, *, qb, kt, q0, l_valid, n_sel):
    b, lp = ki_bf.shape[:2]
    nq = q.shape[0] // (b * qb)
    k_g = jnp.moveaxis(k_bf, 2, 1)
    ones = jnp.ones((b, lp, KV_HEADS, HEAD_DIM), BF16)
    vx = jnp.concatenate([v_bf, ones], axis=-1).reshape(b, lp, KV_HEADS * LANES)
    tr = lax.broadcasted_iota(jnp.int32, (LANES, 2 * LANES), 0)
    tc = lax.broadcasted_iota(jnp.int32, (LANES, 2 * LANES), 1)
    tri = jnp.logical_or(tr < tc, tc >= LANES).astype(BF16)
    row = lambda w: pl.BlockSpec((qb, w), lambda bi, i: (bi * nq + i, 0))
    kern = functools.partial(_dsa_kernel, qb=qb, kt=kt, q0=q0, l_valid=l_valid, n_sel=n_sel)
    return pl.pallas_call(
        kern,
        grid=(b, nq),
        in_specs=[row(ATT_WIDTH), row(QI_WIDTH), row(KW_WIDTH),
                  pl.BlockSpec((1, KV_HEADS, lp, HEAD_DIM), lambda bi, i: (bi, 0, 0, 0)),
                  pl.BlockSpec((1, lp, KV_HEADS * LANES), lambda bi, i: (bi, 0, 0)),
                  pl.BlockSpec((1, lp, IDX_DIM), lambda bi, i: (bi, 0, 0)),
                  pl.BlockSpec((LANES, 2 * LANES), lambda bi, i: (0, 0))],
        out_specs=row(ATT_WIDTH),
        out_shape=jax.ShapeDtypeStruct((q.shape[0], ATT_WIDTH), F32),
        scratch_shapes=[pltpu.VMEM((lp // kt, qb, kt), jnp.int32), pltpu.VMEM((lp // kt, qb, kt), F32),
                        pltpu.VMEM((IDX_HEADS, qb, LANES), F32)],
        compiler_params=_cparams(("arbitrary", "arbitrary")),
        name="dsa",
    )(q, qi, kw, k_g, vx, ki_bf, tri)


def _dsa_t_kernel(q_ref, qi_ref, kw_ref, k_ref, vxt_ref, ki_ref, tri_ref, o_ref, key_s, bias_s, s0_s, s1_s,
                  p0_s, p1_s, acc_s, *, qb, kt, q0, l_valid, n_sel):
    i = pl.program_id(1)
    ns = kt // LANES
    qpos = q0 + i * qb + lax.broadcasted_iota(jnp.int32, (1, qb), 1)
    lim = jnp.minimum((qpos // CHUNK + 1) * CHUNK, l_valid)
    last_lim = jnp.minimum(((q0 + i * qb + qb - 1) // CHUNK + 1) * CHUNK, l_valid)
    n_kt = (last_lim + kt - 1) // kt
    rep = ATT_HEADS // KV_HEADS

    def by_head(x_t, n):
        return jnp.concatenate([x_t[h * HEAD_DIM:(h + 1) * HEAD_DIM, :] for h in range(n)], axis=1)

    w_qi = by_head(qi_ref[...].T, IDX_HEADS).astype(BF16)
    q_t = q_ref[...].T * (HEAD_DIM ** -0.5)
    w_q = [by_head(q_t[g * rep * HEAD_DIM:(g + 1) * rep * HEAD_DIM, :], rep).astype(BF16) for g in range(KV_HEADS)]
    kw_t = kw_ref[...].T
    wi = [kw_t[IDX_DIM + h:IDX_DIM + h + 1, :] * (IDX_DIM ** -0.5) for h in range(IDX_HEADS)]

    def f32_key(x):
        bits = pltpu.bitcast(x, jnp.int32)
        return bits ^ ((bits >> 31) & 0x7FFFFFFF)

    def score_tile(t, c):
        ks = pl.multiple_of(t * kt, kt)
        lg = jnp.dot(ki_ref[0, pl.ds(ks, kt), :], w_qi, preferred_element_type=F32)
        sc = jnp.zeros((kt, qb), F32)
        for h in range(IDX_HEADS):
            sc = sc + wi[h] * jnp.maximum(lg[:, h * qb:(h + 1) * qb], 0.0)
        kpos = ks + lax.broadcasted_iota(jnp.int32, (kt, qb), 0)
        sc = jnp.where(kpos < lim, sc + 0.0, -jnp.inf)
        key_s[pl.ds(ks, kt), :] = f32_key(sc)
        return c

    lax.fori_loop(0, n_kt, score_tile, 0)

    def count(pred_fn):
        def tile(t, acc):
            ks = pl.multiple_of(t * kt, kt)
            for c in range(kt // COUNT_ROWS):
                chunk = key_s[pl.ds(ks + c * COUNT_ROWS, COUNT_ROWS), :]
                acc = acc + jnp.where(pred_fn(chunk), 1.0, 0.0)
            return acc
        acc = lax.fori_loop(0, n_kt, tile, jnp.zeros((COUNT_ROWS, qb), F32))
        return jnp.sum(acc, axis=0, keepdims=True)

    def bit_step(b, lo):
        cand = lo + jnp.left_shift(jnp.int32(1), 31 - b)
        return jnp.where(count(lambda k: k >= cand) >= n_sel, cand, lo)

    thr = lax.fori_loop(0, 32, bit_step, jnp.full((1, qb), INT_MIN, jnp.int32))
    need = n_sel - count(lambda k: k > thr)

    def sel_tile(t, off):
        for s in range(ns):
            ks = pl.multiple_of(t * kt + s * LANES, LANES)
            key = key_s[pl.ds(ks, LANES), :]
            eq = key == thr
            pre = jnp.dot(tri_ref[...], jnp.where(eq, 1.0, 0.0).astype(BF16), preferred_element_type=F32)
            kpos = ks + lax.broadcasted_iota(jnp.int32, (LANES, qb), 0)
            keep = jnp.logical_or(key > thr, jnp.logical_and(eq, pre[:LANES] + off < need))
            keep = jnp.logical_and(keep, kpos < lim)
            bias_s[pl.ds(ks, LANES), :] = jnp.where(keep, 0.0, MASKED)
            off = off + pre[LANES:]
        return off

    lax.fori_loop(0, n_kt, sel_tile, jnp.zeros((LANES, qb), F32))

    nc = kt // ATT_ROWS
    width = rep * qb
    s_refs = (s0_s, s1_s)
    p_refs = (p0_s, p1_s)

    def score_chunk(t, g, c, mx):
        ks = pl.multiple_of(t * kt + c * ATT_ROWS, ATT_ROWS)
        sc = jnp.dot(k_ref[0, g, pl.ds(ks, ATT_ROWS), :], w_q[g], preferred_element_type=F32)
        bias = bias_s[pl.ds(ks, ATT_ROWS), :]
        sc = jnp.concatenate([sc[:, r * qb:(r + 1) * qb] + bias for r in range(rep)], axis=1)
        s_refs[g][c * ATT_ROWS:(c + 1) * ATT_ROWS, :] = sc
        return jnp.maximum(mx, jnp.max(sc.reshape(ATT_ROWS // 8, 8, width), axis=0))

    def prob_chunk(g, c, m_new):
        rows = slice(c * ATT_ROWS, (c + 1) * ATT_ROWS)
        p_refs[g][rows, :] = jnp.exp(s_refs[g][rows, :] - m_new).astype(BF16)

    def finish(t, g, m, m_new):
        pv = jnp.dot(vxt_ref[0, t, g * LANES:(g + 1) * LANES, :], p_refs[g][...], preferred_element_type=F32)
        acc_s[g] = acc_s[g] * jnp.exp(m - m_new) + pv

    mx_init = jnp.full((8, width), MASKED, F32)

    def att_tile(t, carry):
        m0, m1, mx0 = carry
        m0_new = jnp.maximum(m0, jnp.max(mx0, axis=0, keepdims=True))
        mx1 = mx_init
        for c in range(nc):
            mx1 = score_chunk(t, 1, c, mx1)
            prob_chunk(0, c, m0_new)
        m1_new = jnp.maximum(m1, jnp.max(mx1, axis=0, keepdims=True))
        finish(t, 0, m0, m0_new)
        t_next = jnp.minimum(t + 1, n_kt - 1)
        mx0 = mx_init
        for c in range(nc):
            mx0 = score_chunk(t_next, 0, c, mx0)
            prob_chunk(1, c, m1_new)
        finish(t, 1, m1, m1_new)
        return m0_new, m1_new, mx0

    acc_s[...] = jnp.zeros(acc_s.shape, F32)
    mx0 = mx_init
    for c in range(nc):
        mx0 = score_chunk(0, 0, c, mx0)
    m_init = jnp.full((1, width), MASKED, F32)
    lax.fori_loop(0, n_kt, att_tile, (m_init, m_init, mx0))
    for g in range(KV_HEADS):
        acc = acc_s[g]
        out = acc * (1.0 / acc[HEAD_DIM:HEAD_DIM + 1, :])
        for r in range(rep):
            h = g * rep + r
            o_ref[:, h * HEAD_DIM:(h + 1) * HEAD_DIM] = out[:, r * qb:(r + 1) * qb].T[:, :HEAD_DIM]


def _dsa_t(q, qi, kw, k_bf, v_bf, ki_bf, *, qb, kt, q0, l_valid, n_sel):
    assert qb == LANES
    b, lp = ki_bf.shape[:2]
    nq = q.shape[0] // (b * qb)
    k_g = jnp.moveaxis(k_bf, 2, 1)
    ones = jnp.ones((b, lp, KV_HEADS, HEAD_DIM), BF16)
    vx = jnp.concatenate([v_bf, ones], axis=-1).reshape(b, lp // kt, kt, KV_HEADS * LANES)
    vxt = jnp.swapaxes(vx, 2, 3)
    tr = lax.broadcasted_iota(jnp.int32, (2 * LANES, LANES), 0)
    tc = lax.broadcasted_iota(jnp.int32, (2 * LANES, LANES), 1)
    tri = jnp.logical_or(tc < tr, tr >= LANES).astype(BF16)
    row = lambda w: pl.BlockSpec((qb, w), lambda bi, i: (bi * nq + i, 0))
    kern = functools.partial(_dsa_t_kernel, qb=qb, kt=kt, q0=q0, l_valid=l_valid, n_sel=n_sel)
    return pl.pallas_call(
        kern,
        grid=(b, nq),
        in_specs=[row(ATT_WIDTH), row(QI_WIDTH), row(KW_WIDTH),
                  pl.BlockSpec((1, KV_HEADS, lp, HEAD_DIM), lambda bi, i: (bi, 0, 0, 0)),
                  pl.BlockSpec((1, lp // kt, KV_HEADS * LANES, kt), lambda bi, i: (bi, 0, 0, 0)),
                  pl.BlockSpec((1, lp, IDX_DIM), lambda bi, i: (bi, 0, 0)),
                  pl.BlockSpec((2 * LANES, LANES), lambda bi, i: (0, 0))],
        out_specs=row(ATT_WIDTH),
        out_shape=jax.ShapeDtypeStruct((q.shape[0], ATT_WIDTH), F32),
        scratch_shapes=[pltpu.VMEM((lp, qb), jnp.int32), pltpu.VMEM((lp, qb), F32),
                        pltpu.VMEM((kt, 4 * qb), F32), pltpu.VMEM((kt, 4 * qb), F32),
                        pltpu.VMEM((kt, 4 * qb), BF16), pltpu.VMEM((kt, 4 * qb), BF16),
                        pltpu.VMEM((KV_HEADS, LANES, 4 * qb), F32)],
        compiler_params=_cparams(("arbitrary", "arbitrary")),
        name="dsa_t",
    )(q, qi, kw, k_g, vxt, ki_bf, tri)


def _split_bf16(x):
    hi = x.astype(BF16)
    return hi, (x - hi.astype(F32)).astype(BF16)


_NN = (((1,), (0,)), ((), ()))


def _dots(a_sp, b_sp, dims=_NN):
    (ah, al), (bh, bl) = a_sp, b_sp
    d = functools.partial(lax.dot_general, dimension_numbers=dims, preferred_element_type=F32)
    return d(ah, bh) + (d(ah, bl) + d(al, bh))


def _dot3(a, b, dims=_NN):
    return _dots(_split_bf16(a), _split_bf16(b), dims)


def _dot2(a, b_exact):
    ah, al = _split_bf16(a)
    return jnp.dot(ah, b_exact, preferred_element_type=F32) + jnp.dot(al, b_exact, preferred_element_type=F32)


_NT = (((1,), (1,)), ((), ()))
_TN = (((0,), (0,)), ((), ()))


def _rwkv_kernel(p_ref, prev_ref, s0_ref, mu_ref, w0_ref, w2_ref, a0_ref, a2_ref, g2_ref, kk_ref, ka_ref,
                 rk_ref, lnw_ref, lnb_ref, hsum_ref, tri_ref, o_ref, st_ref, s_s, prev_s, o_s, *, c, nb):
    j = pl.program_id(1)

    @pl.when(j == 0)
    def _init():
        s_s[...] = s0_ref[...]
        prev_s[...] = prev_ref[...]

    hsum = hsum_ref[...]
    tri = tri_ref[...]
    row = lax.broadcasted_iota(jnp.int32, (c, 1), 0)
    o1, o2, o3 = RWKV_WIDTH, 2 * RWKV_WIDTH, 3 * RWKV_WIDTH

    def prepare(bi):
        p = p_ref[bi]
        p_prev = jnp.where(row == 0, prev_s[bi], pltpu.roll(p, 1, 0))
        prev_s[bi] = p[c - 1:c, :]
        xs = p + (p_prev - p) * mu_ref[...]
        r, k, v = xs[:, :o1], xs[:, o1:o2], xs[:, o2:o3]
        dw = xs[:, o3:o3 + W_LORA]
        da = xs[:, o3 + W_LORA:o3 + W_LORA + A_LORA]
        dg = xs[:, o3 + W_LORA + A_LORA:]
        w_log = -jax.nn.softplus(-(w0_ref[...] + _dot3(jnp.tanh(dw), w2_ref[...]))) - 0.5
        lw = -jnp.exp(w_log)
        a = jax.nn.sigmoid(a0_ref[...] + _dot3(da, a2_ref[...]))
        g = _dot3(jax.nn.sigmoid(dg), g2_ref[...])
        kk = k * kk_ref[...]
        kk = kk / jnp.maximum(jnp.sqrt(_dot2(kk * kk, hsum)), 1e-12)
        km = k * (1.0 + (a - 1.0) * ka_ref[...])
        bm = kk * a
        lw_hi, lw_lo = _split_bf16(lw)
        cum = (jnp.dot(tri, lw_hi, preferred_element_type=F32)
               + jnp.dot(tri, lw_lo, preferred_element_type=F32))
        tot = cum[c - 1:c, :]
        e_out = jnp.exp(-cum)
        e_end = jnp.exp(tot - cum)
        return dict(a_t=-kk * jnp.exp(cum - lw), r_t=r * jnp.exp(cum), b_t=bm * e_out, k_t=km * e_out,
                    b_h=bm * e_end, k_h=km * e_end, gam=jnp.exp(tot), r=r, v=v, km=km, g=g)

    pre = [prepare(bi) for bi in range(nb)]

    ri = lax.broadcasted_iota(jnp.int32, (c, c), 0)
    ci = lax.broadcasted_iota(jnp.int32, (c, c), 1)
    strict = ri > ci
    incl = ri >= ci
    eye = jnp.where(ri == ci, 1.0, 0.0)
    n_double = int(np.log2(c)) - 1
    chains = [(bi, h) for bi in range(nb) for h in range(RWKV_HEADS)]
    ids = range(len(chains))
    col = lambda name, i: pre[chains[i][0]][name][:, chains[i][1] * HEAD_DIM:(chains[i][1] + 1) * HEAD_DIM]
    s0 = [s_s[bi, h] for bi, h in chains]
    s0_sp = [_split_bf16(s) for s in s0]
    ar_sp = [_split_bf16(jnp.concatenate([col("a_t", i), col("r_t", i)], axis=0)) for i in ids]
    bk_sp = [_split_bf16(jnp.concatenate([col("b_t", i), col("k_t", i)], axis=0)) for i in ids]
    v_sp = [_split_bf16(col("v", i)) for i in ids]
    m = [_dots(ar_sp[i], bk_sp[i], _NT) for i in ids]
    as0 = [_dots(ar_sp[i], s0_sp[i], _NT) for i in ids]
    x_sp = [_split_bf16(jnp.where(strict, m[i][:c, :c], 0.0)) for i in ids]
    lak_sp = [_split_bf16(jnp.where(strict, m[i][:c, c:], 0.0)) for i in ids]
    tinv = [eye + jnp.where(strict, m[i][:c, :c], 0.0) for i in ids]
    rhs = [as0[i][:c] + _dots(lak_sp[i], v_sp[i]) for i in ids]
    for _ in range(n_double):
        x_sp = [_split_bf16(_dots(x_sp[i], x_sp[i])) for i in ids]
        tinv = [tinv[i] + _dots(_split_bf16(tinv[i]), x_sp[i]) for i in ids]
    u_sp = [_split_bf16(_dot3(tinv[i], rhs[i])) for i in ids]
    for i in ids:
        bi, h = chains[i]
        m_rb = jnp.where(incl, m[i][c:, :c], 0.0)
        m_rk = jnp.where(incl, m[i][c:, c:], 0.0)
        o_s[bi, :, h * HEAD_DIM:(h + 1) * HEAD_DIM] = (as0[i][c:] + _dots(_split_bf16(m_rb), u_sp[i])
                                                       + _dots(_split_bf16(m_rk), v_sp[i]))
    s_new = [s0[i] * col("gam", i) + _dots(u_sp[i], _split_bf16(col("b_h", i)), _TN)
             + _dots(v_sp[i], _split_bf16(col("k_h", i)), _TN) for i in ids]
    for i in ids:
        s_s[chains[i][0], chains[i][1]] = s_new[i]

    inv_d = 1.0 / HEAD_DIM
    for bi in range(nb):
        o = o_s[bi]
        mean = _dot2(o, hsum) * inv_d
        cen = o - mean
        var = _dot2(cen * cen, hsum) * inv_d
        on = cen * lax.rsqrt(var + GN_EPS) * lnw_ref[...] + lnb_ref[...]
        bonus = _dot2(pre[bi]["r"] * pre[bi]["km"] * rk_ref[...], hsum) * pre[bi]["v"]
        o_ref[bi] = (on + bonus) * pre[bi]["g"]

    @pl.when(j == pl.num_programs(1) - 1)
    def _fin():
        st_ref[...] = s_s[...]


def _rwkv(p, prev, s0, rw, c, nb):
    mu, w0, w2, a0, a2, g2, k_k, k_a, r_k, lnx_w, lnx_b = rw
    b, t, _ = p.shape
    nc = t // c
    hid = lax.broadcasted_iota(jnp.int32, (RWKV_WIDTH, RWKV_WIDTH), 0) // HEAD_DIM
    hsum = (hid == hid.T).astype(BF16)
    tri = (lax.broadcasted_iota(jnp.int32, (c, c), 0) >= lax.broadcasted_iota(jnp.int32, (c, c), 1)).astype(BF16)
    vec = lambda a: a.reshape(1, -1)
    full = lambda a: pl.BlockSpec(a.shape, lambda bi, j: (0,) * a.ndim)
    small = [vec(mu), vec(w0), w2, vec(a0), a2, g2, vec(k_k), vec(k_a), vec(r_k), vec(lnx_w), vec(lnx_b), hsum, tri]
    state = pl.BlockSpec((nb, RWKV_HEADS, HEAD_DIM, HEAD_DIM), lambda bi, j: (bi, 0, 0, 0))
    return pl.pallas_call(
        functools.partial(_rwkv_kernel, c=c, nb=nb),
        grid=(b // nb, nc),
        in_specs=[pl.BlockSpec((nb, c, RWKV_COLS), lambda bi, j: (bi, j, 0)),
                  pl.BlockSpec((nb, 1, RWKV_COLS), lambda bi, j: (bi, 0, 0)), state] + [full(a) for a in small],
        out_specs=[pl.BlockSpec((nb, c, RWKV_WIDTH), lambda bi, j: (bi, j, 0)), state],
        out_shape=[jax.ShapeDtypeStruct((b, t, RWKV_WIDTH), F32),
                   jax.ShapeDtypeStruct((b, RWKV_HEADS, HEAD_DIM, HEAD_DIM), F32)],
        scratch_shapes=[pltpu.VMEM((nb, RWKV_HEADS, HEAD_DIM, HEAD_DIM), F32), pltpu.VMEM((nb, 1, RWKV_COLS), F32),
                        pltpu.VMEM((nb, c, RWKV_WIDTH), F32)],
        compiler_params=_cparams(("arbitrary", "arbitrary")),
        name="rwkv",
    )(p, prev, s0, *small)


def _branch(x, mods, pos, prev, s0, caches, weights, tm, tm_moe):
    (g_mix, g_ffn, g_final, w_r, w_a, rw, w_out, rwt, rb, wup, bup, wdn, bdn) = weights
    sh1, sc1, gt1, sh2, sc2, gt2 = mods
    b, t, d = x.shape
    tab, tabk = _rope_tables(pos)
    p_r, q, k, v, qi, kw = _inproj(x, g_mix, sh1, sc1, w_r, w_a, tab, tabk, tm)
    p_r3 = p_r.reshape(b, t, RWKV_COLS)
    o_r, wkv = _rwkv(p_r3, prev, s0, rw, min(CHUNK, t), RWKV_BATCH)
    shift = p_r3[:, -1:]
    k4 = k.reshape(b, t, KV_HEADS, HEAD_DIM)
    v4 = v.reshape(b, t, KV_HEADS, HEAD_DIM)
    ki = kw.reshape(b, t, KW_WIDTH)[..., :IDX_DIM]
    kt = 1024 if caches is None else 512
    if caches is None:
        k_all, v_all, ki_all, q0, l_valid = k4, v4, ki, 0, t
    else:
        ck, cv, cki = caches
        q0 = ck.shape[1]
        l_valid = q0 + t
        padn = (-l_valid) % kt
        zpad = lambda a: jnp.concatenate([a, jnp.zeros((b, padn) + a.shape[2:], a.dtype)], axis=1)
        k_all = zpad(jnp.concatenate([ck, k4], axis=1))
        v_all = zpad(jnp.concatenate([cv, v4], axis=1))
        ki_all = zpad(jnp.concatenate([cki, ki], axis=1))
    qb = min(Q_BLOCK, t)
    dsa = _dsa_t if qb == LANES else _dsa
    o_a = dsa(q, qi, kw, k_all.astype(BF16), v_all.astype(BF16), ki_all.astype(BF16),
              qb=qb, kt=kt, q0=q0, l_valid=l_valid, n_sel=min(TOPK_MAX, l_valid // 4))
    x1, h2 = _outproj(o_r, o_a, x, w_out, gt1, g_ffn, sh2, sc2, tm)
    y_moe = _moe(h2, rwt, rb, wup, bup, wdn, bdn, *tm_moe)
    y = _final(x1, y_moe, gt2, g_final, b, t, tm)
    return y, k4, v4, ki, wkv, shift


def kernel(x_prompt, x_sample, c_prompt, c_sample, cache_k, cache_v, cache_kidx, state_wkv, state_shift,
           w_ada, b_ada, g_mix, g_ffn, g_final, w_in, mu_shift, w0, w_lora_w, a0, w_lora_a, w_lora_g,
           k_k, k_a, r_k, lnx_w, lnx_b, w_out, router_w, router_b, w_up, b_up, w_down, b_down):
    B, T, D = x_prompt.shape
    DB, DT, _ = x_sample.shape
    P = cache_k.shape[2]
    l = 0
    assert w_ada.shape[0] == 1

    rows = B + DB
    pad = (-rows) % 8
    c_all = jnp.concatenate([c_prompt, c_sample, jnp.zeros((pad, D), F32)], axis=0)
    m = _ada(c_all, w_ada[l], b_ada[l])
    mods_p = tuple(t.reshape(B, 1, D) for t in jnp.split(m[:B], 6, axis=-1))
    mods_s = tuple(t.reshape(DB, 1, D) for t in jnp.split(m[B:rows], 6, axis=-1))

    w_r = w_in[l][:, :RWKV_COLS].astype(BF16)
    w_att = w_in[l][:, RWKV_COLS:]
    att_pad = KW_WIDTH - IDX_DIM - IDX_HEADS
    w_a = jnp.concatenate([w_att, jnp.zeros((D, att_pad), F32)], axis=1).astype(BF16)
    rw = (mu_shift[l], w0[l], w_lora_w[l], a0[l], w_lora_a[l], w_lora_g[l], k_k[l], k_a[l], r_k[l],
          lnx_w[l], lnx_b[l])
    wup = _deinterleave_cast(w_up[l])
    bup = jnp.concatenate([b_up[l][..., 0::2], b_up[l][..., 1::2]], axis=-1).reshape(N_EXPERTS, 1, 2 * D_FF)
    wdn = w_down[l].astype(BF16)
    bdn = b_down[l].reshape(N_EXPERTS, 1, D)
    weights = (g_mix[l], g_ffn[l], g_final, w_r, w_a, rw, w_out[l].astype(BF16), router_w[l].T,
               router_b[l].reshape(N_EXPERTS, 1), wup, bup, wdn, bdn)

    pos_p = jnp.arange(T, dtype=jnp.int32)
    pos_s = P + jnp.arange(DT, dtype=jnp.int32)
    yp, kp, vp, kip, wkvp, shp = _branch(
        x_prompt, mods_p, pos_p, jnp.zeros((B, 1, RWKV_COLS), F32),
        jnp.zeros((B, RWKV_HEADS, HEAD_DIM, HEAD_DIM), F32), None, weights, 512, (MOE_SUB, MOE_NSUB))
    ys, ks, vs, kis, wkvs, shs = _branch(
        x_sample, mods_s, pos_s, state_shift[l], state_wkv[l],
        (cache_k[l], cache_v[l], cache_kidx[l]), weights, DT, (DB * DT, 1))
    return (yp, ys, kp[None], vp[None], kip[None], wkvp[None], shp[None],
            ks[None], vs[None], kis[None], wkvs[None], shs[None])
```

```python
import functools

import numpy as np
import jax
import jax.numpy as jnp
from jax import lax
from jax.experimental import pallas as pl
from jax.experimental.pallas import tpu as pltpu

F32 = jnp.float32
BF16 = jnp.bfloat16
HIGHEST = lax.Precision.HIGHEST

D_MODEL = 1024
CHUNK = 64
HEAD_DIM = 64
RWKV_HEADS = 8
RWKV_WIDTH = RWKV_HEADS * HEAD_DIM
W_LORA = 64
A_LORA = 64
G_LORA = 128
RWKV_COLS = 3 * RWKV_WIDTH + W_LORA + A_LORA + G_LORA
GN_EPS = 64e-5
ATT_HEADS = 8
ATT_WIDTH = ATT_HEADS * HEAD_DIM
KV_HEADS = 2
KV_WIDTH = KV_HEADS * HEAD_DIM
IDX_HEADS = 4
IDX_DIM = 64
TOPK_MAX = 256
Q_BLOCK = 128
ROPE_THETA = 500000.0
ROT_DIM = HEAD_DIM // 4
N_EXPERTS = 32
TOP_K = 4
D_FF = 1024
SWIGLU_ALPHA = 1.702
SWIGLU_LIMIT = 7.0
NORM_EPS = 1e-5

LANES = 128
MOE_ROWS = 128
MOE_SUB = 768
MOE_NSUB = 3
VMEM_LIMIT = 56 * 1024 * 1024

QI_WIDTH = IDX_HEADS * IDX_DIM
KW_WIDTH = LANES


def _cparams(sem):
    return pltpu.CompilerParams(dimension_semantics=sem, vmem_limit_bytes=VMEM_LIMIT)


def _ada_kernel(c_ref, w_ref, b_ref, o_ref):
    c = c_ref[...]
    s = c * jax.nn.sigmoid(c)
    o_ref[...] = jnp.dot(s, w_ref[...], precision=HIGHEST, preferred_element_type=F32) + b_ref[...]


def _ada(c, w, b):
    rows, d = c.shape
    n = w.shape[1]
    tn = 1536
    return pl.pallas_call(
        _ada_kernel,
        grid=(n // tn,),
        in_specs=[pl.BlockSpec((rows, d), lambda j: (0, 0)),
                  pl.BlockSpec((d, tn), lambda j: (0, j)),
                  pl.BlockSpec((1, tn), lambda j: (0, j))],
        out_specs=pl.BlockSpec((rows, tn), lambda j: (0, j)),
        out_shape=jax.ShapeDtypeStruct((rows, n), F32),
        compiler_params=_cparams(("arbitrary",)),
        name="ada",
    )(c, w, b.reshape(1, n))


def _rope_slab(y, tab_ref):
    return (y * tab_ref[0] + pltpu.roll(y, LANES - ROT_DIM // 2, 1) * tab_ref[1]
            + pltpu.roll(y, ROT_DIM // 2, 1) * tab_ref[2])


def _norm_mod(x, g, sh, sc):
    var = jnp.mean(x * x, axis=-1, keepdims=True)
    return (x * lax.rsqrt(var + NORM_EPS) * g) * (1.0 + sc) + sh


def _inproj_kernel(x_ref, g_ref, sh_ref, sc_ref, wr_ref, wa_ref, tab_ref, tabk_ref,
                   rw_ref, q_ref, k_ref, v_ref, qi_ref, kw_ref):
    h = _norm_mod(x_ref[...], g_ref[...], sh_ref[0], sc_ref[0]).astype(BF16)
    rw_ref[...] = jnp.dot(h, wr_ref[...], preferred_element_type=F32)
    pa = jnp.dot(h, wa_ref[...], preferred_element_type=F32)
    off = 0
    for ref, width, rot in ((q_ref, ATT_WIDTH, True), (k_ref, KV_WIDTH, True), (v_ref, KV_WIDTH, False),
                            (qi_ref, QI_WIDTH, True)):
        for s in range(width // LANES):
            slab = pa[:, off + s * LANES: off + (s + 1) * LANES]
            ref[:, s * LANES:(s + 1) * LANES] = _rope_slab(slab, tab_ref) if rot else slab
        off += width
    kw_ref[...] = _rope_slab(pa[:, off:off + LANES], tabk_ref)


def _rope_tables(pos):
    half = ROT_DIM // 2
    inv = ROPE_THETA ** (-jnp.arange(0, ROT_DIM, 2, dtype=F32) / ROT_DIM)
    ang = pos.astype(F32)[:, None] * inv[None, :]
    cos, sin = jnp.cos(ang), jnp.sin(ang)
    t = pos.shape[0]
    one = jnp.ones((t, HEAD_DIM - ROT_DIM), F32)
    zero_r = jnp.zeros((t, HEAD_DIM - ROT_DIM), F32)
    zero_h = jnp.zeros((t, half), F32)
    c_head = jnp.concatenate([cos, cos, one], axis=1)
    up_head = jnp.concatenate([-sin, zero_h, zero_r], axis=1)
    dn_head = jnp.concatenate([zero_h, sin, zero_r], axis=1)
    tab = jnp.stack([jnp.tile(c_head, (1, 2)), jnp.tile(up_head, (1, 2)), jnp.tile(dn_head, (1, 2))])
    wscale = jnp.concatenate([jnp.full((t, IDX_HEADS), IDX_HEADS ** -0.5, F32),
                              jnp.ones((t, HEAD_DIM - IDX_HEADS), F32)], axis=1)
    zero64 = jnp.zeros((t, HEAD_DIM), F32)
    tabk = jnp.stack([jnp.concatenate([c_head, wscale], axis=1),
                      jnp.concatenate([up_head, zero64], axis=1),
                      jnp.concatenate([dn_head, zero64], axis=1)])
    return tab, tabk


def _inproj(x, g, sh, sc, w_r, w_a, tab, tabk, tm):
    b, t, d = x.shape
    nt = t // tm
    n = b * t
    x2 = x.reshape(n, d)
    widths = (RWKV_COLS, ATT_WIDTH, KV_WIDTH, KV_WIDTH, QI_WIDTH, KW_WIDTH)
    row = lambda w: pl.BlockSpec((tm, w), lambda i: (i, 0))
    mod = pl.BlockSpec((1, 1, d), lambda i: (i // nt, 0, 0))
    tabspec = pl.BlockSpec((3, tm, LANES), lambda i: (0, i % nt, 0))
    return pl.pallas_call(
        _inproj_kernel,
        grid=(n // tm,),
        in_specs=[row(d), pl.BlockSpec((1, d), lambda i: (0, 0)), mod, mod,
                  pl.BlockSpec(w_r.shape, lambda i: (0, 0)), pl.BlockSpec(w_a.shape, lambda i: (0, 0)),
                  tabspec, tabspec],
        out_specs=[row(w) for w in widths],
        out_shape=[jax.ShapeDtypeStruct((n, w), F32) for w in widths],
        compiler_params=_cparams(("arbitrary",)),
        name="inproj",
    )(x2, g.reshape(1, d), sh, sc, w_r, w_a, tab, tabk)


def _outproj_kernel(or_ref, oa_ref, x_ref, w_ref, gt_ref, g_ref, sh_ref, sc_ref, x1_ref, h2_ref):
    m = jnp.dot(or_ref[...].astype(BF16), w_ref[:RWKV_WIDTH, :], preferred_element_type=F32)
    m = m + jnp.dot(oa_ref[...].astype(BF16), w_ref[RWKV_WIDTH:, :], preferred_element_type=F32)
    x1 = x_ref[...] + gt_ref[0] * m
    x1_ref[...] = x1
    h2_ref[...] = _norm_mod(x1, g_ref[...], sh_ref[0], sc_ref[0]).astype(BF16)


def _outproj(o_r, o_a, x, w_out, gt, g, sh, sc, tm):
    b, t, d = x.shape
    nt = t // tm
    n = b * t
    row = lambda w: pl.BlockSpec((tm, w), lambda i: (i, 0))
    mod = pl.BlockSpec((1, 1, d), lambda i: (i // nt, 0, 0))
    return pl.pallas_call(
        _outproj_kernel,
        grid=(n // tm,),
        in_specs=[row(RWKV_WIDTH), row(ATT_WIDTH), row(d), pl.BlockSpec(w_out.shape, lambda i: (0, 0)),
                  mod, pl.BlockSpec((1, d), lambda i: (0, 0)), mod, mod],
        out_specs=[row(d), row(d)],
        out_shape=[jax.ShapeDtypeStruct((n, d), F32), jax.ShapeDtypeStruct((n, d), BF16)],
        compiler_params=_cparams(("arbitrary",)),
        name="outproj",
    )(o_r.reshape(n, RWKV_WIDTH), o_a.reshape(n, ATT_WIDTH), x.reshape(n, d), w_out, gt, g.reshape(1, d), sh, sc)


def _moe_kernel(h_ref, rwt_ref, rb_ref, tri_ref, wup_ref, bup_ref, wdn_ref, bdn_ref, y_ref,
                hb_s, rank_s, gate_s, count_s, *, n_tokens, sub):
    e = pl.program_id(1)
    n_sub = h_ref.shape[0] // sub

    @pl.when(e == 0)
    def _route():
        rwt = rwt_ref[...].astype(BF16)
        for s in range(n_sub):
            first_row = pl.program_id(0) * (n_sub * sub) + s * sub
            in_rows = first_row + lax.broadcasted_iota(jnp.int32, (sub, 1), 0) < n_tokens
            in_cols = first_row + lax.broadcasted_iota(jnp.int32, (1, sub), 1) < n_tokens
            hb = jnp.where(in_rows, h_ref[s * sub:(s + 1) * sub, :], 0.0).astype(BF16)
            hb_s[s * sub:(s + 1) * sub, :] = hb
            logits = lax.dot_general(rwt, hb, (((1,), (1,)), ((), ())), preferred_element_type=F32) + rb_ref[...]
            eidx = lax.broadcasted_iota(jnp.int32, logits.shape, 0)
            work = logits
            top = None
            for _ in range(TOP_K):
                m = jnp.max(work, axis=0, keepdims=True)
                if top is None:
                    top = m
                first = jnp.min(jnp.where(work == m, eidx, N_EXPERTS), axis=0, keepdims=True)
                work = jnp.where(eidx == first, -jnp.inf, work)
            ex = jnp.where(work != logits, jnp.exp(logits - top), 0.0)
            sel = jnp.logical_and(work != logits, in_cols)
            gate_s[s] = ex / jnp.sum(ex, axis=0, keepdims=True)
            before = jnp.dot(jnp.where(sel, 1.0, 0.0).astype(BF16), tri_ref[...], preferred_element_type=F32)
            rank_s[s] = jnp.where(sel, before, -1.0)
            for ex_id in range(N_EXPERTS):
                count_s[s, ex_id] = jnp.sum(jnp.where(sel[ex_id:ex_id + 1, :], 1, 0))
        y_ref[...] = jnp.zeros(y_ref.shape, F32)

    for s in range(n_sub):
        r_row = rank_s[s, pl.ds(e, 1), :]
        g_row = gate_s[s, pl.ds(e, 1), :]
        n_blocks = (count_s[s, e] + MOE_ROWS - 1) // MOE_ROWS

        def block(j, carry, s=s, r_row=r_row, g_row=g_row):
            rows = (lax.broadcasted_iota(jnp.int32, (MOE_ROWS, sub), 0) + j * MOE_ROWS).astype(F32)
            hit = r_row == rows
            p = jnp.where(hit, 1.0, 0.0).astype(BF16)
            xe = jnp.dot(p, hb_s[s * sub:(s + 1) * sub, :], preferred_element_type=F32).astype(BF16)
            u = jnp.dot(xe, wup_ref[0], preferred_element_type=F32) + bup_ref[0]
            glu = jnp.minimum(u[:, :D_FF], SWIGLU_LIMIT)
            lin = jnp.clip(u[:, D_FF:], -SWIGLU_LIMIT, SWIGLU_LIMIT)
            act = glu * jax.nn.sigmoid(SWIGLU_ALPHA * glu) * (lin + 1.0)
            yb = jnp.dot(act.astype(BF16), wdn_ref[0], preferred_element_type=F32) + bdn_ref[0]
            g_rows = jnp.sum(jnp.where(hit, g_row, 0.0), axis=1, keepdims=True)
            ys = (yb * g_rows).astype(BF16)
            y_ref[s * sub:(s + 1) * sub, :] += lax.dot_general(p, ys, (((0,), (0,)), ((), ())),
                                                               preferred_element_type=F32)
            return carry

        lax.fori_loop(0, n_blocks, block, 0)


def _moe(h, rwt, rb, wup, bup, wdn, bdn, sub, n_sub):
    n, d = h.shape
    tm = sub * n_sub
    tri = (lax.broadcasted_iota(jnp.int32, (sub, sub), 0) < lax.broadcasted_iota(jnp.int32, (sub, sub), 1)).astype(BF16)
    return pl.pallas_call(
        functools.partial(_moe_kernel, n_tokens=n, sub=sub),
        grid=(pl.cdiv(n, tm), N_EXPERTS),
        in_specs=[pl.BlockSpec((tm, d), lambda i, e: (i, 0)),
                  pl.BlockSpec((N_EXPERTS, d), lambda i, e: (0, 0)),
                  pl.BlockSpec((N_EXPERTS, 1), lambda i, e: (0, 0)),
                  pl.BlockSpec((sub, sub), lambda i, e: (0, 0)),
                  pl.BlockSpec((1, d, 2 * D_FF), lambda i, e: (e, 0, 0)),
                  pl.BlockSpec((1, 1, 2 * D_FF), lambda i, e: (e, 0, 0)),
                  pl.BlockSpec((1, D_FF, d), lambda i, e: (e, 0, 0)),
                  pl.BlockSpec((1, 1, d), lambda i, e: (e, 0, 0))],
        out_specs=pl.BlockSpec((tm, d), lambda i, e: (i, 0)),
        out_shape=jax.ShapeDtypeStruct((n, d), F32),
        scratch_shapes=[pltpu.VMEM((tm, d), BF16), pltpu.VMEM((n_sub, N_EXPERTS, sub), F32),
                        pltpu.VMEM((n_sub, N_EXPERTS, sub), F32), pltpu.SMEM((n_sub, N_EXPERTS), jnp.int32)],
        compiler_params=_cparams(("arbitrary", "arbitrary")),
        name="moe",
    )(h, rwt, rb, tri, wup, bup, wdn, bdn)


def _deinterleave_kernel(w_ref, perm_ref, o_ref):
    n = w_ref.shape[2]
    blk = 2 * LANES
    for j in range(n // blk):
        r = jnp.dot(w_ref[0, :, j * blk:(j + 1) * blk].astype(BF16), perm_ref[...],
                    preferred_element_type=F32).astype(BF16)
        o_ref[0, :, j * LANES:(j + 1) * LANES] = r[:, :LANES]
        o_ref[0, :, n // 2 + j * LANES:n // 2 + (j + 1) * LANES] = r[:, LANES:]


def _deinterleave_cast(w):
    e, d, n = w.shape
    tk = 512
    blk = 2 * LANES
    src = lax.broadcasted_iota(jnp.int32, (blk, blk), 0)
    dst = lax.broadcasted_iota(jnp.int32, (blk, blk), 1)
    perm = (src == jnp.where(dst < LANES, 2 * dst, 2 * (dst - LANES) + 1)).astype(BF16)
    return pl.pallas_call(
        _deinterleave_kernel,
        grid=(e, d // tk),
        in_specs=[pl.BlockSpec((1, tk, n), lambda i, j: (i, j, 0)), pl.BlockSpec((blk, blk), lambda i, j: (0, 0))],
        out_specs=pl.BlockSpec((1, tk, n), lambda i, j: (i, j, 0)),
        out_shape=jax.ShapeDtypeStruct((e, d, n), BF16),
        compiler_params=_cparams(("arbitrary", "arbitrary")),
        name="deinterleave",
    )(w, perm)


def _final_kernel(x1_ref, y_ref, gt_ref, g_ref, o_ref):
    x = x1_ref[...] + gt_ref[0] * y_ref[...]
    var = jnp.mean(x * x, axis=-1, keepdims=True)
    o_ref[...] = x * lax.rsqrt(var + NORM_EPS) * g_ref[...]


def _final(x1, y, gt, g, b, t, tm):
    n, d = x1.shape
    nt = t // tm
    row = pl.BlockSpec((tm, d), lambda i: (i, 0))
    return pl.pallas_call(
        _final_kernel,
        grid=(n // tm,),
        in_specs=[row, row, pl.BlockSpec((1, 1, d), lambda i: (i // nt, 0, 0)),
                  pl.BlockSpec((1, d), lambda i: (0, 0))],
        out_specs=row,
        out_shape=jax.ShapeDtypeStruct((n, d), F32),
        compiler_params=_cparams(("arbitrary",)),
        name="final",
    )(x1, y, gt, g.reshape(1, d)).reshape(b, t, d)


MASKED = -1e30
INT_MIN = -2 ** 31
COUNT_ROWS = 64
ATT_ROWS = 128
RWKV_BATCH = 2


def _dsa_kernel(q_ref, qi_ref, kw_ref, k_ref, vx_ref, ki_ref, tri_ref, o_ref, key_s, bias_s, wib_s,
                *, qb, kt, q0, l_valid, n_sel):
    i = pl.program_id(1)
    ns = kt // LANES
    row = lax.broadcasted_iota(jnp.int32, (qb, 1), 0)
    qpos = q0 + i * qb + row
    lim = jnp.minimum((qpos // CHUNK + 1) * CHUNK, l_valid)
    last_lim = jnp.minimum(((q0 + i * qb + qb - 1) // CHUNK + 1) * CHUNK, l_valid)
    n_kt = (last_lim + kt - 1) // kt
    lane = lax.broadcasted_iota(jnp.int32, (1, LANES), 1)

    for h in range(IDX_HEADS):
        w = kw_ref[:, IDX_DIM + h:IDX_DIM + h + 1] * (IDX_DIM ** -0.5)
        wib_s[h] = jnp.broadcast_to(w, (qb, LANES))
    qi = qi_ref[...].astype(BF16)

    def f32_key(x):
        bits = pltpu.bitcast(x, jnp.int32)
        return bits ^ ((bits >> 31) & 0x7FFFFFFF)

    def score_tile(t, c):
        ks = pl.multiple_of(t * kt, kt)
        kit = ki_ref[0, pl.ds(ks, kt), :]
        lg = [lax.dot_general(qi[:, h * IDX_DIM:(h + 1) * IDX_DIM], kit, (((1,), (1,)), ((), ())),
                              preferred_element_type=F32) for h in range(IDX_HEADS)]
        for s in range(ns):
            sc = jnp.zeros((qb, LANES), F32)
            for h in range(IDX_HEADS):
                sc = sc + wib_s[h] * jnp.maximum(lg[h][:, s * LANES:(s + 1) * LANES], 0.0)
            kpos = ks + s * LANES + lane
            sc = jnp.where(kpos < lim, sc + 0.0, -jnp.inf)
            key_s[t, :, s * LANES:(s + 1) * LANES] = f32_key(sc)
        return c

    lax.fori_loop(0, n_kt, score_tile, 0)

    def count(pred_fn):
        def tile(t, acc):
            for s in range(ns):
                acc = acc + jnp.where(pred_fn(key_s[t, :, s * LANES:(s + 1) * LANES]), 1.0, 0.0)
            return acc
        acc = lax.fori_loop(0, n_kt, tile, jnp.zeros((qb, LANES), F32))
        return jnp.sum(acc, axis=1, keepdims=True)

    def bit_step(b, lo):
        cand = lo + jnp.left_shift(jnp.int32(1), 31 - b)
        candb = jnp.broadcast_to(cand, (qb, LANES))
        return jnp.where(count(lambda k: k >= candb) >= n_sel, cand, lo)

    thr = lax.fori_loop(0, 32, bit_step, jnp.full((qb, 1), INT_MIN, jnp.int32))
    thrb = jnp.broadcast_to(thr, (qb, LANES))
    need = jnp.broadcast_to(n_sel - count(lambda k: k > thrb), (qb, LANES))

    def sel_tile(t, off):
        for s in range(ns):
            key = key_s[t, :, s * LANES:(s + 1) * LANES]
            eq = key == thrb
            pre = jnp.dot(jnp.where(eq, 1.0, 0.0).astype(BF16), tri_ref[...], preferred_element_type=F32)
            kpos = t * kt + s * LANES + lane
            keep = jnp.logical_or(key > thrb, jnp.logical_and(eq, pre[:, :LANES] + off < need))
            keep = jnp.logical_and(keep, kpos < lim)
            bias_s[t, :, s * LANES:(s + 1) * LANES] = jnp.where(keep, 0.0, MASKED)
            off = off + pre[:, LANES:]
        return off

    lax.fori_loop(0, n_kt, sel_tile, jnp.zeros((qb, LANES), F32))

    rep = ATT_HEADS // KV_HEADS
    qgs = []
    for g in range(KV_HEADS):
        qg = jnp.concatenate([q_ref[:, (g * rep + r) * HEAD_DIM:(g * rep + r + 1) * HEAD_DIM] for r in range(rep)],
                             axis=0)
        qgs.append((qg * (HEAD_DIM ** -0.5)).astype(BF16))

    def att_tile(t, carry):
        ks = pl.multiple_of(t * kt, kt)
        out = []
        for g in range(KV_HEADS):
            m, acc = carry[g]
            s = lax.dot_general(qgs[g], k_ref[0, g, pl.ds(ks, kt), :], (((1,), (1,)), ((), ())),
                                preferred_element_type=F32)
            s = (s.reshape(rep, qb, kt) + bias_s[t][None]).reshape(rep * qb, kt)
            m_new = jnp.maximum(m, jnp.max(s, axis=1, keepdims=True))
            p = jnp.exp(s - m_new)
            pv = jnp.dot(p.astype(BF16), vx_ref[0, pl.ds(ks, kt), g * LANES:(g + 1) * LANES],
                         preferred_element_type=F32)
            out.append((m_new, acc * jnp.exp(m - m_new) + pv))
        return tuple(out)

    init = (jnp.full((rep * qb, 1), MASKED, F32), jnp.zeros((rep * qb, LANES), F32))
    res = lax.fori_loop(0, n_kt, att_tile, (init,) * KV_HEADS)
    for g in range(KV_HEADS):
        acc = res[g][1]
        out = acc * pltpu.roll(1.0 / acc, HEAD_DIM, 1)
        for r in range(rep):
            h = g * rep + r
            o_ref[:, h * HEAD_DIM:(h + 1) * HEAD_DIM] = out[r * qb:(r + 1) * qb, :HEAD_DIM]


def _dsa(q, qi, kw, k_bf, v_bf, ki_bf, *, qb, kt, q0, l_valid, n_sel):
    b, lp = ki_bf.shape[:2]
    nq = q.shape[0] // (b * qb)
    k_g = jnp.moveaxis(k_bf, 2, 1)
    ones = jnp.ones((b, lp, KV_HEADS, HEAD_DIM), BF16)
    vx = jnp.concatenate([v_bf, ones], axis=-1).reshape(b, lp, KV_HEADS * LANES)
    tr = lax.broadcasted_iota(jnp.int32, (LANES, 2 * LANES), 0)
    tc = lax.broadcasted_iota(jnp.int32, (LANES, 2 * LANES), 1)
    tri = jnp.logical_or(tr < tc, tc >= LANES).astype(BF16)
    row = lambda w: pl.BlockSpec((qb, w), lambda bi, i: (bi * nq + i, 0))
    kern = functools.partial(_dsa_kernel, qb=qb, kt=kt, q0=q0, l_valid=l_valid, n_sel=n_sel)
    return pl.pallas_call(
        kern,
        grid=(b, nq),
        in_specs=[row(ATT_WIDTH), row(QI_WIDTH), row(KW_WIDTH),
                  pl.BlockSpec((1, KV_HEADS, lp, HEAD_DIM), lambda bi, i: (bi, 0, 0, 0)),
                  pl.BlockSpec((1, lp, KV_HEADS * LANES), lambda bi, i: (bi, 0, 0)),
                  pl.BlockSpec((1, lp, IDX_DIM), lambda bi, i: (bi, 0, 0)),
                  pl.BlockSpec((LANES, 2 * LANES), lambda bi, i: (0, 0))],
        out_specs=row(ATT_WIDTH),
        out_shape=jax.ShapeDtypeStruct((q.shape[0], ATT_WIDTH), F32),
        scratch_shapes=[pltpu.VMEM((lp // kt, qb, kt), jnp.int32), pltpu.VMEM((lp // kt, qb, kt), F32),
                        pltpu.VMEM((IDX_HEADS, qb, LANES), F32)],
        compiler_params=_cparams(("arbitrary", "arbitrary")),
        name="dsa",
    )(q, qi, kw, k_g, vx, ki_bf, tri)


def _dsa_t_kernel(q_ref, qi_ref, kw_ref, k_ref, vxt_ref, ki_ref, tri_ref, o_ref, key_s, bias_s, s0_s, s1_s,
                  p0_s, p1_s, acc_s, *, qb, kt, q0, l_valid, n_sel):
    i = pl.program_id(1)
    ns = kt // LANES
    qpos = q0 + i * qb + lax.broadcasted_iota(jnp.int32, (1, qb), 1)
    lim = jnp.minimum((qpos // CHUNK + 1) * CHUNK, l_valid)
    last_lim = jnp.minimum(((q0 + i * qb + qb - 1) // CHUNK + 1) * CHUNK, l_valid)
    n_kt = (last_lim + kt - 1) // kt
    rep = ATT_HEADS // KV_HEADS

    def by_head(x_t, n):
        return jnp.concatenate([x_t[h * HEAD_DIM:(h + 1) * HEAD_DIM, :] for h in range(n)], axis=1)

    w_qi = by_head(qi_ref[...].T, IDX_HEADS).astype(BF16)
    q_t = q_ref[...].T * (HEAD_DIM ** -0.5)
    w_q = [by_head(q_t[g * rep * HEAD_DIM:(g + 1) * rep * HEAD_DIM, :], rep).astype(BF16) for g in range(KV_HEADS)]
    kw_t = kw_ref[...].T
    wi = [kw_t[IDX_DIM + h:IDX_DIM + h + 1, :] * (IDX_DIM ** -0.5) for h in range(IDX_HEADS)]

    def f32_key(x):
        bits = pltpu.bitcast(x, jnp.int32)
        return bits ^ ((bits >> 31) & 0x7FFFFFFF)

    def score_tile(t, c):
        ks = pl.multiple_of(t * kt, kt)
        lg = jnp.dot(ki_ref[0, pl.ds(ks, kt), :], w_qi, preferred_element_type=F32)
        sc = jnp.zeros((kt, qb), F32)
        for h in range(IDX_HEADS):
            sc = sc + wi[h] * jnp.maximum(lg[:, h * qb:(h + 1) * qb], 0.0)
        kpos = ks + lax.broadcasted_iota(jnp.int32, (kt, qb), 0)
        sc = jnp.where(kpos < lim, sc + 0.0, -jnp.inf)
        key_s[pl.ds(ks, kt), :] = f32_key(sc)
        return c

    lax.fori_loop(0, n_kt, score_tile, 0)

    def count(pred_fn):
        def tile(t, acc):
            ks = pl.multiple_of(t * kt, kt)
            for c in range(kt // COUNT_ROWS):
                chunk = key_s[pl.ds(ks + c * COUNT_ROWS, COUNT_ROWS), :]
                acc = acc + jnp.where(pred_fn(chunk), 1.0, 0.0)
            return acc
        acc = lax.fori_loop(0, n_kt, tile, jnp.zeros((COUNT_ROWS, qb), F32))
        return jnp.sum(acc, axis=0, keepdims=True)

    def bit_step(b, lo):
        cand = lo + jnp.left_shift(jnp.int32(1), 31 - b)
        return jnp.where(count(lambda k: k >= cand) >= n_sel, cand, lo)

    thr = lax.fori_loop(0, 32, bit_step, jnp.full((1, qb), INT_MIN, jnp.int32))
    need = n_sel - count(lambda k: k > thr)

    def sel_tile(t, off):
        for s in range(ns):
            ks = pl.multiple_of(t * kt + s * LANES, LANES)
            key = key_s[pl.ds(ks, LANES), :]
            eq = key == thr
            pre = jnp.dot(tri_ref[...], jnp.where(eq, 1.0, 0.0).astype(BF16), preferred_element_type=F32)
            kpos = ks + lax.broadcasted_iota(jnp.int32, (LANES, qb), 0)
            keep = jnp.logical_or(key > thr, jnp.logical_and(eq, pre[:LANES] + off < need))
            keep = jnp.logical_and(keep, kpos < lim)
            bias_s[pl.ds(ks, LANES), :] = jnp.where(keep, 0.0, MASKED)
            off = off + pre[LANES:]
        return off

    lax.fori_loop(0, n_kt, sel_tile, jnp.zeros((LANES, qb), F32))

    nc = kt // ATT_ROWS
    width = rep * qb
    s_refs = (s0_s, s1_s)
    p_refs = (p0_s, p1_s)

    def score_chunk(t, g, c, mx):
        ks = pl.multiple_of(t * kt + c * ATT_ROWS, ATT_ROWS)
        sc = jnp.dot(k_ref[0, g, pl.ds(ks, ATT_ROWS), :], w_q[g], preferred_element_type=F32)
        bias = bias_s[pl.ds(ks, ATT_ROWS), :]
        sc = jnp.concatenate([sc[:, r * qb:(r + 1) * qb] + bias for r in range(rep)], axis=1)
        s_refs[g][c * ATT_ROWS:(c + 1) * ATT_ROWS, :] = sc
        return jnp.maximum(mx, jnp.max(sc.reshape(ATT_ROWS // 8, 8, width), axis=0))

    def prob_chunk(g, c, m_new):
        rows = slice(c * ATT_ROWS, (c + 1) * ATT_ROWS)
        p_refs[g][rows, :] = jnp.exp(s_refs[g][rows, :] - m_new).astype(BF16)

    def finish(t, g, m, m_new):
        pv = jnp.dot(vxt_ref[0, t, g * LANES:(g + 1) * LANES, :], p_refs[g][...], preferred_element_type=F32)
        acc_s[g] = acc_s[g] * jnp.exp(m - m_new) + pv

    mx_init = jnp.full((8, width), MASKED, F32)

    def att_tile(t, carry):
        m0, m1, mx0 = carry
        m0_new = jnp.maximum(m0, jnp.max(mx0, axis=0, keepdims=True))
        mx1 = mx_init
        for c in range(nc):
            mx1 = score_chunk(t, 1, c, mx1)
            prob_chunk(0, c, m0_new)
        m1_new = jnp.maximum(m1, jnp.max(mx1, axis=0, keepdims=True))
        finish(t, 0, m0, m0_new)
        t_next = jnp.minimum(t + 1, n_kt - 1)
        mx0 = mx_init
        for c in range(nc):
            mx0 = score_chunk(t_next, 0, c, mx0)
            prob_chunk(1, c, m1_new)
        finish(t, 1, m1, m1_new)
        return m0_new, m1_new, mx0

    acc_s[...] = jnp.zeros(acc_s.shape, F32)
    mx0 = mx_init
    for c in range(nc):
        mx0 = score_chunk(0, 0, c, mx0)
    m_init = jnp.full((1, width), MASKED, F32)
    lax.fori_loop(0, n_kt, att_tile, (m_init, m_init, mx0))
    for g in range(KV_HEADS):
        acc = acc_s[g]
        out = acc * (1.0 / acc[HEAD_DIM:HEAD_DIM + 1, :])
        for r in range(rep):
            h = g * rep + r
            o_ref[:, h * HEAD_DIM:(h + 1) * HEAD_DIM] = out[:, r * qb:(r + 1) * qb].T[:, :HEAD_DIM]


def _dsa_t(q, qi, kw, k_bf, v_bf, ki_bf, *, qb, kt, q0, l_valid, n_sel):
    assert qb == LANES
    b, lp = ki_bf.shape[:2]
    nq = q.shape[0] // (b * qb)
    k_g = jnp.moveaxis(k_bf, 2, 1)
    ones = jnp.ones((b, lp, KV_HEADS, HEAD_DIM), BF16)
    vx = jnp.concatenate([v_bf, ones], axis=-1).reshape(b, lp // kt, kt, KV_HEADS * LANES)
    vxt = jnp.swapaxes(vx, 2, 3)
    tr = lax.broadcasted_iota(jnp.int32, (2 * LANES, LANES), 0)
    tc = lax.broadcasted_iota(jnp.int32, (2 * LANES, LANES), 1)
    tri = jnp.logical_or(tc < tr, tr >= LANES).astype(BF16)
    row = lambda w: pl.BlockSpec((qb, w), lambda bi, i: (bi * nq + i, 0))
    kern = functools.partial(_dsa_t_kernel, qb=qb, kt=kt, q0=q0, l_valid=l_valid, n_sel=n_sel)
    return pl.pallas_call(
        kern,
        grid=(b, nq),
        in_specs=[row(ATT_WIDTH), row(QI_WIDTH), row(KW_WIDTH),
                  pl.BlockSpec((1, KV_HEADS, lp, HEAD_DIM), lambda bi, i: (bi, 0, 0, 0)),
                  pl.BlockSpec((1, lp // kt, KV_HEADS * LANES, kt), lambda bi, i: (bi, 0, 0, 0)),
                  pl.BlockSpec((1, lp, IDX_DIM), lambda bi, i: (bi, 0, 0)),
                  pl.BlockSpec((2 * LANES, LANES), lambda bi, i: (0, 0))],
        out_specs=row(ATT_WIDTH),
        out_shape=jax.ShapeDtypeStruct((q.shape[0], ATT_WIDTH), F32),
        scratch_shapes=[pltpu.VMEM((lp, qb), jnp.int32), pltpu.VMEM((lp, qb), F32),
                        pltpu.VMEM((kt, 4 * qb), F32), pltpu.VMEM((kt, 4 * qb), F32),
                        pltpu.VMEM((kt, 4 * qb), BF16), pltpu.VMEM((kt, 4 * qb), BF16),
                        pltpu.VMEM((KV_HEADS, LANES, 4 * qb), F32)],
        compiler_params=_cparams(("arbitrary", "arbitrary")),
        name="dsa_t",
    )(q, qi, kw, k_g, vxt, ki_bf, tri)


def _split_bf16(x):
    hi = x.astype(BF16)
    return hi, (x - hi.astype(F32)).astype(BF16)


_NN = (((1,), (0,)), ((), ()))


def _dots(a_sp, b_sp, dims=_NN):
    (ah, al), (bh, bl) = a_sp, b_sp
    d = functools.partial(lax.dot_general, dimension_numbers=dims, preferred_element_type=F32)
    return d(ah, bh) + (d(ah, bl) + d(al, bh))


def _dot3(a, b, dims=_NN):
    return _dots(_split_bf16(a), _split_bf16(b), dims)


def _dot2(a, b_exact):
    ah, al = _split_bf16(a)
    return jnp.dot(ah, b_exact, preferred_element_type=F32) + jnp.dot(al, b_exact, preferred_element_type=F32)


_NT = (((1,), (1,)), ((), ()))
_TN = (((0,), (0,)), ((), ()))


def _rwkv_kernel(p_ref, prev_ref, s0_ref, mu_ref, w0_ref, w2_ref, a0_ref, a2_ref, g2_ref, kk_ref, ka_ref,
                 rk_ref, lnw_ref, lnb_ref, hsum_ref, tri_ref, o_ref, st_ref, s_s, prev_s, o_s, *, c, nb):
    j = pl.program_id(1)

    @pl.when(j == 0)
    def _init():
        s_s[...] = s0_ref[...]
        prev_s[...] = prev_ref[...]

    hsum = hsum_ref[...]
    rows = nb * c
    row = lax.broadcasted_iota(jnp.int32, (rows, 1), 0)
    o1, o2, o3 = RWKV_WIDTH, 2 * RWKV_WIDTH, 3 * RWKV_WIDTH

    p = p_ref[...].reshape(rows, RWKV_COLS)
    p_prev = pltpu.roll(p, 1, 0)
    for bi in range(nb):
        p_prev = jnp.where(row == bi * c, prev_s[bi], p_prev)
        prev_s[bi] = p[(bi + 1) * c - 1:(bi + 1) * c, :]
    xs = p + (p_prev - p) * mu_ref[...]
    r, k, v = xs[:, :o1], xs[:, o1:o2], xs[:, o2:o3]
    dw = xs[:, o3:o3 + W_LORA]
    da = xs[:, o3 + W_LORA:o3 + W_LORA + A_LORA]
    dg = xs[:, o3 + W_LORA + A_LORA:]
    w_log = -jax.nn.softplus(-(w0_ref[...] + _dot3(jnp.tanh(dw), w2_ref[...]))) - 0.5
    lw = -jnp.exp(w_log)
    a = jax.nn.sigmoid(a0_ref[...] + _dot3(da, a2_ref[...]))
    g = _dot3(jax.nn.sigmoid(dg), g2_ref[...])
    kk = k * kk_ref[...]
    kk = kk / jnp.maximum(jnp.sqrt(_dot2(kk * kk, hsum)), 1e-12)
    km = k * (1.0 + (a - 1.0) * ka_ref[...])
    bm = kk * a
    lw_hi, lw_lo = _split_bf16(lw)
    tri = tri_ref[...]
    cum = (jnp.dot(tri, lw_hi, preferred_element_type=F32)
           + jnp.dot(tri, lw_lo, preferred_element_type=F32))
    tot = cum[rows - 1:rows, :]
    for bi in range(nb - 1):
        in_chunk = jnp.logical_and(row >= bi * c, row < (bi + 1) * c)
        tot = jnp.where(in_chunk, cum[(bi + 1) * c - 1:(bi + 1) * c, :], tot)
    e_out = jnp.exp(-cum)
    e_end = jnp.exp(tot - cum)
    stacked = dict(a_t=-kk * jnp.exp(cum - lw), r_t=r * jnp.exp(cum), b_t=bm * e_out, k_t=km * e_out,
                   b_h=bm * e_end, k_h=km * e_end, gam=jnp.exp(tot), v=v)
    pre = [{name: val[bi * c:(bi + 1) * c] for name, val in stacked.items()} for bi in range(nb)]

    ri = lax.broadcasted_iota(jnp.int32, (c, c), 0)
    ci = lax.broadcasted_iota(jnp.int32, (c, c), 1)
    strict = ri > ci
    incl = ri >= ci
    eye = jnp.where(ri == ci, 1.0, 0.0)
    n_double = int(np.log2(c)) - 1
    chains = [(bi, h) for bi in range(nb) for h in range(RWKV_HEADS)]
    ids = range(len(chains))
    col = lambda name, i: pre[chains[i][0]][name][:, chains[i][1] * HEAD_DIM:(chains[i][1] + 1) * HEAD_DIM]
    s0 = [s_s[bi, h] for bi, h in chains]
    s0_sp = [_split_bf16(s) for s in s0]
    ar_sp = [_split_bf16(jnp.concatenate([col("a_t", i), col("r_t", i)], axis=0)) for i in ids]
    bk_sp = [_split_bf16(jnp.concatenate([col("b_t", i), col("k_t", i)], axis=0)) for i in ids]
    v_sp = [_split_bf16(col("v", i)) for i in ids]
    m = [_dots(ar_sp[i], bk_sp[i], _NT) for i in ids]
    as0 = [_dots(ar_sp[i], s0_sp[i], _NT) for i in ids]
    x_sp = [_split_bf16(jnp.where(strict, m[i][:c, :c], 0.0)) for i in ids]
    lak_sp = [_split_bf16(jnp.where(strict, m[i][:c, c:], 0.0)) for i in ids]
    tinv = [eye + jnp.where(strict, m[i][:c, :c], 0.0) for i in ids]
    rhs = [as0[i][:c] + _dots(lak_sp[i], v_sp[i]) for i in ids]
    for _ in range(n_double):
        x_sp = [_split_bf16(_dots(x_sp[i], x_sp[i])) for i in ids]
        tinv = [tinv[i] + _dots(_split_bf16(tinv[i]), x_sp[i]) for i in ids]
    u_sp = [_split_bf16(_dot3(tinv[i], rhs[i])) for i in ids]
    for i in ids:
        bi, h = chains[i]
        m_rb = jnp.where(incl, m[i][c:, :c], 0.0)
        m_rk = jnp.where(incl, m[i][c:, c:], 0.0)
        o_s[bi, :, h * HEAD_DIM:(h + 1) * HEAD_DIM] = (as0[i][c:] + _dots(_split_bf16(m_rb), u_sp[i])
                                                       + _dots(_split_bf16(m_rk), v_sp[i]))
    s_new = [s0[i] * col("gam", i)[:1] +_dots(u_sp[i], _split_bf16(col("b_h", i)), _TN)
             + _dots(v_sp[i], _split_bf16(col("k_h", i)), _TN) for i in ids]
    for i in ids:
        s_s[chains[i][0], chains[i][1]] = s_new[i]

    inv_d = 1.0 / HEAD_DIM
    o = o_s[...].reshape(rows, RWKV_WIDTH)
    mean = _dot2(o, hsum) * inv_d
    cen = o - mean
    var = _dot2(cen * cen, hsum) * inv_d
    on = cen * lax.rsqrt(var + GN_EPS) * lnw_ref[...] + lnb_ref[...]
    bonus = _dot2(r * km * rk_ref[...], hsum) * v
    o_ref[...] = ((on + bonus) * g).reshape(nb, c, RWKV_WIDTH)

    @pl.when(j == pl.num_programs(1) - 1)
    def _fin():
        st_ref[...] = s_s[...]


def _rwkv(p, prev, s0, rw, c, nb):
    mu, w0, w2, a0, a2, g2, k_k, k_a, r_k, lnx_w, lnx_b = rw
    b, t, _ = p.shape
    nc = t // c
    hid = lax.broadcasted_iota(jnp.int32, (RWKV_WIDTH, RWKV_WIDTH), 0) // HEAD_DIM
    hsum = (hid == hid.T).astype(BF16)
    tr = lax.broadcasted_iota(jnp.int32, (nb * c, nb * c), 0)
    tc = lax.broadcasted_iota(jnp.int32, (nb * c, nb * c), 1)
    tri = jnp.logical_and(tr >= tc, tr // c == tc // c).astype(BF16)
    vec = lambda a: a.reshape(1, -1)
    full = lambda a: pl.BlockSpec(a.shape, lambda bi, j: (0,) * a.ndim)
    small = [vec(mu), vec(w0), w2, vec(a0), a2, g2, vec(k_k), vec(k_a), vec(r_k), vec(lnx_w), vec(lnx_b), hsum, tri]
    state = pl.BlockSpec((nb, RWKV_HEADS, HEAD_DIM, HEAD_DIM), lambda bi, j: (bi, 0, 0, 0))
    return pl.pallas_call(
        functools.partial(_rwkv_kernel, c=c, nb=nb),
        grid=(b // nb, nc),
        in_specs=[pl.BlockSpec((nb, c, RWKV_COLS), lambda bi, j: (bi, j, 0)),
                  pl.BlockSpec((nb, 1, RWKV_COLS), lambda bi, j: (bi, 0, 0)), state] + [full(a) for a in small],
        out_specs=[pl.BlockSpec((nb, c, RWKV_WIDTH), lambda bi, j: (bi, j, 0)), state],
        out_shape=[jax.ShapeDtypeStruct((b, t, RWKV_WIDTH), F32),
                   jax.ShapeDtypeStruct((b, RWKV_HEADS, HEAD_DIM, HEAD_DIM), F32)],
        scratch_shapes=[pltpu.VMEM((nb, RWKV_HEADS, HEAD_DIM, HEAD_DIM), F32), pltpu.VMEM((nb, 1, RWKV_COLS), F32),
                        pltpu.VMEM((nb, c, RWKV_WIDTH), F32)],
        compiler_params=_cparams(("arbitrary", "arbitrary")),
        name="rwkv",
    )(p, prev, s0, *small)


def _branch(x, mods, pos, prev, s0, caches, weights, tm, tm_moe):
    (g_mix, g_ffn, g_final, w_r, w_a, rw, w_out, rwt, rb, wup, bup, wdn, bdn) = weights
    sh1, sc1, gt1, sh2, sc2, gt2 = mods
    b, t, d = x.shape
    tab, tabk = _rope_tables(pos)
    p_r, q, k, v, qi, kw = _inproj(x, g_mix, sh1, sc1, w_r, w_a, tab, tabk, tm)
    p_r3 = p_r.reshape(b, t, RWKV_COLS)
    o_r, wkv = _rwkv(p_r3, prev, s0, rw, min(CHUNK, t), RWKV_BATCH)
    shift = p_r3[:, -1:]
    k4 = k.reshape(b, t, KV_HEADS, HEAD_DIM)
    v4 = v.reshape(b, t, KV_HEADS, HEAD_DIM)
    ki = kw.reshape(b, t, KW_WIDTH)[..., :IDX_DIM]
    kt = 1024 if caches is None else 512
    if caches is None:
        k_all, v_all, ki_all, q0, l_valid = k4, v4, ki, 0, t
    else:
        ck, cv, cki = caches
        q0 = ck.shape[1]
        l_valid = q0 + t
        padn = (-l_valid) % kt
        zpad = lambda a: jnp.concatenate([a, jnp.zeros((b, padn) + a.shape[2:], a.dtype)], axis=1)
        k_all = zpad(jnp.concatenate([ck, k4], axis=1))
        v_all = zpad(jnp.concatenate([cv, v4], axis=1))
        ki_all = zpad(jnp.concatenate([cki, ki], axis=1))
    qb = min(Q_BLOCK, t)
    dsa = _dsa_t if qb == LANES else _dsa
    o_a = dsa(q, qi, kw, k_all.astype(BF16), v_all.astype(BF16), ki_all.astype(BF16),
              qb=qb, kt=kt, q0=q0, l_valid=l_valid, n_sel=min(TOPK_MAX, l_valid // 4))
    x1, h2 = _outproj(o_r, o_a, x, w_out, gt1, g_ffn, sh2, sc2, tm)
    y_moe = _moe(h2, rwt, rb, wup, bup, wdn, bdn, *tm_moe)
    y = _final(x1, y_moe, gt2, g_final, b, t, tm)
    return y, k4, v4, ki, wkv, shift


def kernel(x_prompt, x_sample, c_prompt, c_sample, cache_k, cache_v, cache_kidx, state_wkv, state_shift,
           w_ada, b_ada, g_mix, g_ffn, g_final, w_in, mu_shift, w0, w_lora_w, a0, w_lora_a, w_lora_g,
           k_k, k_a, r_k, lnx_w, lnx_b, w_out, router_w, router_b, w_up, b_up, w_down, b_down):
    B, T, D = x_prompt.shape
    DB, DT, _ = x_sample.shape
    P = cache_k.shape[2]
    l = 0
    assert w_ada.shape[0] == 1

    rows = B + DB
    pad = (-rows) % 8
    c_all = jnp.concatenate([c_prompt, c_sample, jnp.zeros((pad, D), F32)], axis=0)
    m = _ada(c_all, w_ada[l], b_ada[l])
    mods_p = tuple(t.reshape(B, 1, D) for t in jnp.split(m[:B], 6, axis=-1))
    mods_s = tuple(t.reshape(DB, 1, D) for t in jnp.split(m[B:rows], 6, axis=-1))

    w_r = w_in[l][:, :RWKV_COLS].astype(BF16)
    w_att = w_in[l][:, RWKV_COLS:]
    att_pad = KW_WIDTH - IDX_DIM - IDX_HEADS
    w_a = jnp.concatenate([w_att, jnp.zeros((D, att_pad), F32)], axis=1).astype(BF16)
    rw = (mu_shift[l], w0[l], w_lora_w[l], a0[l], w_lora_a[l], w_lora_g[l], k_k[l], k_a[l], r_k[l],
          lnx_w[l], lnx_b[l])
    wup = _deinterleave_cast(w_up[l])
    bup = jnp.concatenate([b_up[l][..., 0::2], b_up[l][..., 1::2]], axis=-1).reshape(N_EXPERTS, 1, 2 * D_FF)
    wdn = w_down[l].astype(BF16)
    bdn = b_down[l].reshape(N_EXPERTS, 1, D)
    weights = (g_mix[l], g_ffn[l], g_final, w_r, w_a, rw, w_out[l].astype(BF16), router_w[l].T,
               router_b[l].reshape(N_EXPERTS, 1), wup, bup, wdn, bdn)

    pos_p = jnp.arange(T, dtype=jnp.int32)
    pos_s = P + jnp.arange(DT, dtype=jnp.int32)
    yp, kp, vp, kip, wkvp, shp = _branch(
        x_prompt, mods_p, pos_p, jnp.zeros((B, 1, RWKV_COLS), F32),
        jnp.zeros((B, RWKV_HEADS, HEAD_DIM, HEAD_DIM), F32), None, weights, 512, (MOE_SUB, MOE_NSUB))
    ys, ks, vs, kis, wkvs, shs = _branch(
        x_sample, mods_s, pos_s, state_shift[l], state_wkv[l],
        (cache_k[l], cache_v[l], cache_kidx[l]), weights, DT, (DB * DT, 1))
    return (yp, ys, kp[None], vp[None], kip[None], wkvp[None], shp[None],
            ks[None], vs[None], kis[None], wkvs[None], shs[None])
```

```python
import functools

import numpy as np
import jax
import jax.numpy as jnp
from jax import lax
from jax.experimental import pallas as pl
from jax.experimental.pallas import tpu as pltpu

F32 = jnp.float32
BF16 = jnp.bfloat16
HIGHEST = lax.Precision.HIGHEST

D_MODEL = 1024
CHUNK = 64
HEAD_DIM = 64
RWKV_HEADS = 8
RWKV_WIDTH = RWKV_HEADS * HEAD_DIM
W_LORA = 64
A_LORA = 64
G_LORA = 128
RWKV_COLS = 3 * RWKV_WIDTH + W_LORA + A_LORA + G_LORA
GN_EPS = 64e-5
ATT_HEADS = 8
ATT_WIDTH = ATT_HEADS * HEAD_DIM
KV_HEADS = 2
KV_WIDTH = KV_HEADS * HEAD_DIM
IDX_HEADS = 4
IDX_DIM = 64
TOPK_MAX = 256
Q_BLOCK = 128
ROPE_THETA = 500000.0
ROT_DIM = HEAD_DIM // 4
N_EXPERTS = 32
TOP_K = 4
D_FF = 1024
SWIGLU_ALPHA = 1.702
SWIGLU_LIMIT = 7.0
NORM_EPS = 1e-5

LANES = 128
MOE_ROWS = 128
MOE_SUB = 768
MOE_NSUB = 3
VMEM_LIMIT = 56 * 1024 * 1024

QI_WIDTH = IDX_HEADS * IDX_DIM
KW_WIDTH = LANES


def _cparams(sem):
    return pltpu.CompilerParams(dimension_semantics=sem, vmem_limit_bytes=VMEM_LIMIT)


def _ada_kernel(c_ref, w_ref, b_ref, o_ref):
    c = c_ref[...]
    s = c * jax.nn.sigmoid(c)
    o_ref[...] = jnp.dot(s, w_ref[...], precision=HIGHEST, preferred_element_type=F32) + b_ref[...]


def _ada(c, w, b):
    rows, d = c.shape
    n = w.shape[1]
    tn = 1536
    return pl.pallas_call(
        _ada_kernel,
        grid=(n // tn,),
        in_specs=[pl.BlockSpec((rows, d), lambda j: (0, 0)),
                  pl.BlockSpec((d, tn), lambda j: (0, j)),
                  pl.BlockSpec((1, tn), lambda j: (0, j))],
        out_specs=pl.BlockSpec((rows, tn), lambda j: (0, j)),
        out_shape=jax.ShapeDtypeStruct((rows, n), F32),
        compiler_params=_cparams(("arbitrary",)),
        name="ada",
    )(c, w, b.reshape(1, n))


def _rope_slab(y, tab_ref):
    return (y * tab_ref[0] + pltpu.roll(y, LANES - ROT_DIM // 2, 1) * tab_ref[1]
            + pltpu.roll(y, ROT_DIM // 2, 1) * tab_ref[2])


def _norm_mod(x, g, sh, sc):
    var = jnp.mean(x * x, axis=-1, keepdims=True)
    return (x * lax.rsqrt(var + NORM_EPS) * g) * (1.0 + sc) + sh


def _inproj_kernel(x_ref, g_ref, sh_ref, sc_ref, wr_ref, wa_ref, tab_ref, tabk_ref,
                   rw_ref, q_ref, k_ref, v_ref, qi_ref, kw_ref):
    h = _norm_mod(x_ref[...], g_ref[...], sh_ref[0], sc_ref[0]).astype(BF16)
    rw_ref[...] = jnp.dot(h, wr_ref[...], preferred_element_type=F32)
    pa = jnp.dot(h, wa_ref[...], preferred_element_type=F32)
    off = 0
    for ref, width, rot in ((q_ref, ATT_WIDTH, True), (k_ref, KV_WIDTH, True), (v_ref, KV_WIDTH, False),
                            (qi_ref, QI_WIDTH, True)):
        for s in range(width // LANES):
            slab = pa[:, off + s * LANES: off + (s + 1) * LANES]
            ref[:, s * LANES:(s + 1) * LANES] = _rope_slab(slab, tab_ref) if rot else slab
        off += width
    kw_ref[...] = _rope_slab(pa[:, off:off + LANES], tabk_ref)


def _rope_tables(pos):
    half = ROT_DIM // 2
    inv = ROPE_THETA ** (-jnp.arange(0, ROT_DIM, 2, dtype=F32) / ROT_DIM)
    ang = pos.astype(F32)[:, None] * inv[None, :]
    cos, sin = jnp.cos(ang), jnp.sin(ang)
    t = pos.shape[0]
    one = jnp.ones((t, HEAD_DIM - ROT_DIM), F32)
    zero_r = jnp.zeros((t, HEAD_DIM - ROT_DIM), F32)
    zero_h = jnp.zeros((t, half), F32)
    c_head = jnp.concatenate([cos, cos, one], axis=1)
    up_head = jnp.concatenate([-sin, zero_h, zero_r], axis=1)
    dn_head = jnp.concatenate([zero_h, sin, zero_r], axis=1)
    tab = jnp.stack([jnp.tile(c_head, (1, 2)), jnp.tile(up_head, (1, 2)), jnp.tile(dn_head, (1, 2))])
    wscale = jnp.concatenate([jnp.full((t, IDX_HEADS), IDX_HEADS ** -0.5, F32),
                              jnp.ones((t, HEAD_DIM - IDX_HEADS), F32)], axis=1)
    zero64 = jnp.zeros((t, HEAD_DIM), F32)
    tabk = jnp.stack([jnp.concatenate([c_head, wscale], axis=1),
                      jnp.concatenate([up_head, zero64], axis=1),
                      jnp.concatenate([dn_head, zero64], axis=1)])
    return tab, tabk


def _inproj(x, g, sh, sc, w_r, w_a, tab, tabk, tm):
    b, t, d = x.shape
    nt = t // tm
    n = b * t
    x2 = x.reshape(n, d)
    widths = (RWKV_COLS, ATT_WIDTH, KV_WIDTH, KV_WIDTH, QI_WIDTH, KW_WIDTH)
    row = lambda w: pl.BlockSpec((tm, w), lambda i: (i, 0))
    mod = pl.BlockSpec((1, 1, d), lambda i: (i // nt, 0, 0))
    tabspec = pl.BlockSpec((3, tm, LANES), lambda i: (0, i % nt, 0))
    return pl.pallas_call(
        _inproj_kernel,
        grid=(n // tm,),
        in_specs=[row(d), pl.BlockSpec((1, d), lambda i: (0, 0)), mod, mod,
                  pl.BlockSpec(w_r.shape, lambda i: (0, 0)), pl.BlockSpec(w_a.shape, lambda i: (0, 0)),
                  tabspec, tabspec],
        out_specs=[row(w) for w in widths],
        out_shape=[jax.ShapeDtypeStruct((n, w), F32) for w in widths],
        compiler_params=_cparams(("arbitrary",)),
        name="inproj",
    )(x2, g.reshape(1, d), sh, sc, w_r, w_a, tab, tabk)


def _outproj_kernel(or_ref, oa_ref, x_ref, w_ref, gt_ref, g_ref, sh_ref, sc_ref, x1_ref, h2_ref):
    m = jnp.dot(or_ref[...].astype(BF16), w_ref[:RWKV_WIDTH, :], preferred_element_type=F32)
    m = m + jnp.dot(oa_ref[...].astype(BF16), w_ref[RWKV_WIDTH:, :], preferred_element_type=F32)
    x1 = x_ref[...] + gt_ref[0] * m
    x1_ref[...] = x1
    h2_ref[...] = _norm_mod(x1, g_ref[...], sh_ref[0], sc_ref[0]).astype(BF16)


def _outproj(o_r, o_a, x, w_out, gt, g, sh, sc, tm):
    b, t, d = x.shape
    nt = t // tm
    n = b * t
    row = lambda w: pl.BlockSpec((tm, w), lambda i: (i, 0))
    mod = pl.BlockSpec((1, 1, d), lambda i: (i // nt, 0, 0))
    return pl.pallas_call(
        _outproj_kernel,
        grid=(n // tm,),
        in_specs=[row(RWKV_WIDTH), row(ATT_WIDTH), row(d), pl.BlockSpec(w_out.shape, lambda i: (0, 0)),
                  mod, pl.BlockSpec((1, d), lambda i: (0, 0)), mod, mod],
        out_specs=[row(d), row(d)],
        out_shape=[jax.ShapeDtypeStruct((n, d), F32), jax.ShapeDtypeStruct((n, d), BF16)],
        compiler_params=_cparams(("arbitrary",)),
        name="outproj",
    )(o_r.reshape(n, RWKV_WIDTH), o_a.reshape(n, ATT_WIDTH), x.reshape(n, d), w_out, gt, g.reshape(1, d), sh, sc)


def _moe_kernel(h_ref, rwt_ref, rb_ref, tri_ref, wup_ref, bup_ref, wdn_ref, bdn_ref, y_ref,
                hb_s, rank_s, gate_s, count_s, *, n_tokens, sub):
    e = pl.program_id(1)
    n_sub = h_ref.shape[0] // sub

    @pl.when(e == 0)
    def _route():
        rwt = rwt_ref[...].astype(BF16)
        for s in range(n_sub):
            first_row = pl.program_id(0) * (n_sub * sub) + s * sub
            in_rows = first_row + lax.broadcasted_iota(jnp.int32, (sub, 1), 0) < n_tokens
            in_cols = first_row + lax.broadcasted_iota(jnp.int32, (1, sub), 1) < n_tokens
            hb = jnp.where(in_rows, h_ref[s * sub:(s + 1) * sub, :], 0.0).astype(BF16)
            hb_s[s * sub:(s + 1) * sub, :] = hb
            logits = lax.dot_general(rwt, hb, (((1,), (1,)), ((), ())), preferred_element_type=F32) + rb_ref[...]
            eidx = lax.broadcasted_iota(jnp.int32, logits.shape, 0)
            work = logits
            top = None
            for _ in range(TOP_K):
                m = jnp.max(work, axis=0, keepdims=True)
                if top is None:
                    top = m
                first = jnp.min(jnp.where(work == m, eidx, N_EXPERTS), axis=0, keepdims=True)
                work = jnp.where(eidx == first, -jnp.inf, work)
            ex = jnp.where(work != logits, jnp.exp(logits - top), 0.0)
            sel = jnp.logical_and(work != logits, in_cols)
            gate_s[s] = ex / jnp.sum(ex, axis=0, keepdims=True)
            before = jnp.dot(jnp.where(sel, 1.0, 0.0).astype(BF16), tri_ref[...], preferred_element_type=F32)
            rank_s[s] = jnp.where(sel, before, -1.0)
            for ex_id in range(N_EXPERTS):
                count_s[s, ex_id] = jnp.sum(jnp.where(sel[ex_id:ex_id + 1, :], 1, 0))
        y_ref[...] = jnp.zeros(y_ref.shape, F32)

    for s in range(n_sub):
        r_row = rank_s[s, pl.ds(e, 1), :]
        g_row = gate_s[s, pl.ds(e, 1), :]
        n_blocks = (count_s[s, e] + MOE_ROWS - 1) // MOE_ROWS

        def block(j, carry, s=s, r_row=r_row, g_row=g_row):
            rows = (lax.broadcasted_iota(jnp.int32, (MOE_ROWS, sub), 0) + j * MOE_ROWS).astype(F32)
            hit = r_row == rows
            p = jnp.where(hit, 1.0, 0.0).astype(BF16)
            xe = jnp.dot(p, hb_s[s * sub:(s + 1) * sub, :], preferred_element_type=F32).astype(BF16)
            u = jnp.dot(xe, wup_ref[0], preferred_element_type=F32) + bup_ref[0]
            glu = jnp.minimum(u[:, :D_FF], SWIGLU_LIMIT)
            lin = jnp.clip(u[:, D_FF:], -SWIGLU_LIMIT, SWIGLU_LIMIT)
            act = glu * jax.nn.sigmoid(SWIGLU_ALPHA * glu) * (lin + 1.0)
            yb = jnp.dot(act.astype(BF16), wdn_ref[0], preferred_element_type=F32) + bdn_ref[0]
            g_rows = jnp.sum(jnp.where(hit, g_row, 0.0), axis=1, keepdims=True)
            ys = (yb * g_rows).astype(BF16)
            y_ref[s * sub:(s + 1) * sub, :] += lax.dot_general(p, ys, (((0,), (0,)), ((), ())),
                                                               preferred_element_type=F32)
            return carry

        lax.fori_loop(0, n_blocks, block, 0)


def _moe(h, rwt, rb, wup, bup, wdn, bdn, sub, n_sub):
    n, d = h.shape
    tm = sub * n_sub
    tri = (lax.broadcasted_iota(jnp.int32, (sub, sub), 0) < lax.broadcasted_iota(jnp.int32, (sub, sub), 1)).astype(BF16)
    return pl.pallas_call(
        functools.partial(_moe_kernel, n_tokens=n, sub=sub),
        grid=(pl.cdiv(n, tm), N_EXPERTS),
        in_specs=[pl.BlockSpec((tm, d), lambda i, e: (i, 0)),
                  pl.BlockSpec((N_EXPERTS, d), lambda i, e: (0, 0)),
                  pl.BlockSpec((N_EXPERTS, 1), lambda i, e: (0, 0)),
                  pl.BlockSpec((sub, sub), lambda i, e: (0, 0)),
                  pl.BlockSpec((1, d, 2 * D_FF), lambda i, e: (e, 0, 0)),
                  pl.BlockSpec((1, 1, 2 * D_FF), lambda i, e: (e, 0, 0)),
                  pl.BlockSpec((1, D_FF, d), lambda i, e: (e, 0, 0)),
                  pl.BlockSpec((1, 1, d), lambda i, e: (e, 0, 0))],
        out_specs=pl.BlockSpec((tm, d), lambda i, e: (i, 0)),
        out_shape=jax.ShapeDtypeStruct((n, d), F32),
        scratch_shapes=[pltpu.VMEM((tm, d), BF16), pltpu.VMEM((n_sub, N_EXPERTS, sub), F32),
                        pltpu.VMEM((n_sub, N_EXPERTS, sub), F32), pltpu.SMEM((n_sub, N_EXPERTS), jnp.int32)],
        compiler_params=_cparams(("arbitrary", "arbitrary")),
        name="moe",
    )(h, rwt, rb, tri, wup, bup, wdn, bdn)


def _deinterleave_kernel(w_ref, perm_ref, o_ref):
    n = w_ref.shape[2]
    blk = 2 * LANES
    for j in range(n // blk):
        r = jnp.dot(w_ref[0, :, j * blk:(j + 1) * blk].astype(BF16), perm_ref[...],
                    preferred_element_type=F32).astype(BF16)
        o_ref[0, :, j * LANES:(j + 1) * LANES] = r[:, :LANES]
        o_ref[0, :, n // 2 + j * LANES:n // 2 + (j + 1) * LANES] = r[:, LANES:]


def _deinterleave_cast(w):
    e, d, n = w.shape
    tk = 512
    blk = 2 * LANES
    src = lax.broadcasted_iota(jnp.int32, (blk, blk), 0)
    dst = lax.broadcasted_iota(jnp.int32, (blk, blk), 1)
    perm = (src == jnp.where(dst < LANES, 2 * dst, 2 * (dst - LANES) + 1)).astype(BF16)
    return pl.pallas_call(
        _deinterleave_kernel,
        grid=(e, d // tk),
        in_specs=[pl.BlockSpec((1, tk, n), lambda i, j: (i, j, 0)), pl.BlockSpec((blk, blk), lambda i, j: (0, 0))],
        out_specs=pl.BlockSpec((1, tk, n), lambda i, j: (i, j, 0)),
        out_shape=jax.ShapeDtypeStruct((e, d, n), BF16),
        compiler_params=_cparams(("arbitrary", "arbitrary")),
        name="deinterleave",
    )(w, perm)


def _final_kernel(x1_ref, y_ref, gt_ref, g_ref, o_ref):
    x = x1_ref[...] + gt_ref[0] * y_ref[...]
    var = jnp.mean(x * x, axis=-1, keepdims=True)
    o_ref[...] = x * lax.rsqrt(var + NORM_EPS) * g_ref[...]


def _final(x1, y, gt, g, b, t, tm):
    n, d = x1.shape
    nt = t // tm
    row = pl.BlockSpec((tm, d), lambda i: (i, 0))
    return pl.pallas_call(
        _final_kernel,
        grid=(n // tm,),
        in_specs=[row, row, pl.BlockSpec((1, 1, d), lambda i: (i // nt, 0, 0)),
                  pl.BlockSpec((1, d), lambda i: (0, 0))],
        out_specs=row,
        out_shape=jax.ShapeDtypeStruct((n, d), F32),
        compiler_params=_cparams(("arbitrary",)),
        name="final",
    )(x1, y, gt, g.reshape(1, d)).reshape(b, t, d)


MASKED = -1e30
INT_MIN = -2 ** 31
COUNT_ROWS = 64
ATT_ROWS = 128
RWKV_BATCH = 2


def _dsa_kernel(q_ref, qi_ref, kw_ref, k_ref, vx_ref, ki_ref, tri_ref, o_ref, key_s, bias_s, wib_s,
                *, qb, kt, q0, l_valid, n_sel):
    i = pl.program_id(1)
    ns = kt // LANES
    row = lax.broadcasted_iota(jnp.int32, (qb, 1), 0)
    qpos = q0 + i * qb + row
    lim = jnp.minimum((qpos // CHUNK + 1) * CHUNK, l_valid)
    last_lim = jnp.minimum(((q0 + i * qb + qb - 1) // CHUNK + 1) * CHUNK, l_valid)
    n_kt = (last_lim + kt - 1) // kt
    lane = lax.broadcasted_iota(jnp.int32, (1, LANES), 1)

    for h in range(IDX_HEADS):
        w = kw_ref[:, IDX_DIM + h:IDX_DIM + h + 1] * (IDX_DIM ** -0.5)
        wib_s[h] = jnp.broadcast_to(w, (qb, LANES))
    qi = qi_ref[...].astype(BF16)

    def f32_key(x):
        bits = pltpu.bitcast(x, jnp.int32)
        return bits ^ ((bits >> 31) & 0x7FFFFFFF)

    def score_tile(t, c):
        ks = pl.multiple_of(t * kt, kt)
        kit = ki_ref[0, pl.ds(ks, kt), :]
        lg = [lax.dot_general(qi[:, h * IDX_DIM:(h + 1) * IDX_DIM], kit, (((1,), (1,)), ((), ())),
                              preferred_element_type=F32) for h in range(IDX_HEADS)]
        for s in range(ns):
            sc = jnp.zeros((qb, LANES), F32)
            for h in range(IDX_HEADS):
                sc = sc + wib_s[h] * jnp.maximum(lg[h][:, s * LANES:(s + 1) * LANES], 0.0)
            kpos = ks + s * LANES + lane
            sc = jnp.where(kpos < lim, sc + 0.0, -jnp.inf)
            key_s[t, :, s * LANES:(s + 1) * LANES] = f32_key(sc)
        return c

    lax.fori_loop(0, n_kt, score_tile, 0)

    def count(pred_fn):
        def tile(t, acc):
            for s in range(ns):
                acc = acc + jnp.where(pred_fn(key_s[t, :, s * LANES:(s + 1) * LANES]), 1.0, 0.0)
            return acc
        acc = lax.fori_loop(0, n_kt, tile, jnp.zeros((qb, LANES), F32))
        return jnp.sum(acc, axis=1, keepdims=True)

    def bit_step(b, lo):
        cand = lo + jnp.left_shift(jnp.int32(1), 31 - b)
        candb = jnp.broadcast_to(cand, (qb, LANES))
        return jnp.where(count(lambda k: k >= candb) >= n_sel, cand, lo)

    thr = lax.fori_loop(0, 32, bit_step, jnp.full((qb, 1), INT_MIN, jnp.int32))
    thrb = jnp.broadcast_to(thr, (qb, LANES))
    need = jnp.broadcast_to(n_sel - count(lambda k: k > thrb), (qb, LANES))

    def sel_tile(t, off):
        for s in range(ns):
            key = key_s[t, :, s * LANES:(s + 1) * LANES]
            eq = key == thrb
            pre = jnp.dot(jnp.where(eq, 1.0, 0.0).astype(BF16), tri_ref[...], preferred_element_type=F32)
            kpos = t * kt + s * LANES + lane
            keep = jnp.logical_or(key > thrb, jnp.logical_and(eq, pre[:, :LANES] + off < need))
            keep = jnp.logical_and(keep, kpos < lim)
            bias_s[t, :, s * LANES:(s + 1) * LANES] = jnp.where(keep, 0.0, MASKED)
            off = off + pre[:, LANES:]
        return off

    lax.fori_loop(0, n_kt, sel_tile, jnp.zeros((qb, LANES), F32))

    rep = ATT_HEADS // KV_HEADS
    qgs = []
    for g in range(KV_HEADS):
        qg = jnp.concatenate([q_ref[:, (g * rep + r) * HEAD_DIM:(g * rep + r + 1) * HEAD_DIM] for r in range(rep)],
                             axis=0)
        qgs.append((qg * (HEAD_DIM ** -0.5)).astype(BF16))

    def att_tile(t, carry):
        ks = pl.multiple_of(t * kt, kt)
        out = []
        for g in range(KV_HEADS):
            m, acc = carry[g]
            s = lax.dot_general(qgs[g], k_ref[0, g, pl.ds(ks, kt), :], (((1,), (1,)), ((), ())),
                                preferred_element_type=F32)
            s = (s.reshape(rep, qb, kt) + bias_s[t][None]).reshape(rep * qb, kt)
            m_new = jnp.maximum(m, jnp.max(s, axis=1, keepdims=True))
            p = jnp.exp(s - m_new)
            pv = jnp.dot(p.astype(BF16), vx_ref[0, pl.ds(ks, kt), g * LANES:(g + 1) * LANES],
                         preferred_element_type=F32)
            out.append((m_new, acc * jnp.exp(m - m_new) + pv))
        return tuple(out)

    init = (jnp.full((rep * qb, 1), MASKED, F32), jnp.zeros((rep * qb, LANES), F32))
    res = lax.fori_loop(0, n_kt, att_tile, (init,) * KV_HEADS)
    for g in range(KV_HEADS):
        acc = res[g][1]
        out = acc * pltpu.roll(1.0 / acc, HEAD_DIM, 1)
        for r in range(rep):
            h = g * rep + r
            o_ref[:, h * HEAD_DIM:(h + 1) * HEAD_DIM] = out[r * qb:(r + 1) * qb, :HEAD_DIM]


def _dsa(q, qi, kw, k_bf, v_bf, ki_bf, *, qb, kt, q0, l_valid, n_sel):
    b, lp = ki_bf.shape[:2]
    nq = q.shape[0] // (b * qb)
    k_g = jnp.moveaxis(k_bf, 2, 1)
    ones = jnp.ones((b, lp, KV_HEADS, HEAD_DIM), BF16)
    vx = jnp.concatenate([v_bf, ones], axis=-1).reshape(b, lp, KV_HEADS * LANES)
    tr = lax.broadcasted_iota(jnp.int32, (LANES, 2 * LANES), 0)
    tc = lax.broadcasted_iota(jnp.int32, (LANES, 2 * LANES), 1)
    tri = jnp.logical_or(tr < tc, tc >= LANES).astype(BF16)
    row = lambda w: pl.BlockSpec((qb, w), lambda bi, i: (bi * nq + i, 0))
    kern = functools.partial(_dsa_kernel, qb=qb, kt=kt, q0=q0, l_valid=l_valid, n_sel=n_sel)
    return pl.pallas_call(
        kern,
        grid=(b, nq),
        in_specs=[row(ATT_WIDTH), row(QI_WIDTH), row(KW_WIDTH),
                  pl.BlockSpec((1, KV_HEADS, lp, HEAD_DIM), lambda bi, i: (bi, 0, 0, 0)),
                  pl.BlockSpec((1, lp, KV_HEADS * LANES), lambda bi, i: (bi, 0, 0)),
                  pl.BlockSpec((1, lp, IDX_DIM), lambda bi, i: (bi, 0, 0)),
                  pl.BlockSpec((LANES, 2 * LANES), lambda bi, i: (0, 0))],
        out_specs=row(ATT_WIDTH),
        out_shape=jax.ShapeDtypeStruct((q.shape[0], ATT_WIDTH), F32),
        scratch_shapes=[pltpu.VMEM((lp // kt, qb, kt), jnp.int32), pltpu.VMEM((lp // kt, qb, kt), F32),
                        pltpu.VMEM((IDX_HEADS, qb, LANES), F32)],
        compiler_params=_cparams(("arbitrary", "arbitrary")),
        name="dsa",
    )(q, qi, kw, k_g, vx, ki_bf, tri)


def _dsa_t_kernel(q_ref, qi_ref, kw_ref, k_ref, vxt_ref, ki_ref, tri_ref, o_ref, key_s, bias_s, s0_s, s1_s,
                  p0_s, p1_s, acc_s, *, qb, kt, q0, l_valid, n_sel):
    i = pl.program_id(1)
    ns = kt // LANES
    qpos = q0 + i * qb + lax.broadcasted_iota(jnp.int32, (1, qb), 1)
    lim = jnp.minimum((qpos // CHUNK + 1) * CHUNK, l_valid)
    last_lim = jnp.minimum(((q0 + i * qb + qb - 1) // CHUNK + 1) * CHUNK, l_valid)
    n_kt = (last_lim + kt - 1) // kt
    rep = ATT_HEADS // KV_HEADS

    def by_head(x_t, n):
        return jnp.concatenate([x_t[h * HEAD_DIM:(h + 1) * HEAD_DIM, :] for h in range(n)], axis=1)

    w_qi = by_head(qi_ref[...].T, IDX_HEADS).astype(BF16)
    q_t = q_ref[...].T * (HEAD_DIM ** -0.5)
    w_q = [by_head(q_t[g * rep * HEAD_DIM:(g + 1) * rep * HEAD_DIM, :], rep).astype(BF16) for g in range(KV_HEADS)]
    kw_t = kw_ref[...].T
    wi = [kw_t[IDX_DIM + h:IDX_DIM + h + 1, :] * (IDX_DIM ** -0.5) for h in range(IDX_HEADS)]

    def f32_key(x):
        bits = pltpu.bitcast(x, jnp.int32)
        return bits ^ ((bits >> 31) & 0x7FFFFFFF)

    def score_tile(t, c):
        ks = pl.multiple_of(t * kt, kt)
        lg = jnp.dot(ki_ref[0, pl.ds(ks, kt), :], w_qi, preferred_element_type=F32)
        sc = jnp.zeros((kt, qb), F32)
        for h in range(IDX_HEADS):
            sc = sc + wi[h] * jnp.maximum(lg[:, h * qb:(h + 1) * qb], 0.0)
        kpos = ks + lax.broadcasted_iota(jnp.int32, (kt, qb), 0)
        sc = jnp.where(kpos < lim, sc + 0.0, -jnp.inf)
        key_s[pl.ds(ks, kt), :] = f32_key(sc)
        return c

    lax.fori_loop(0, n_kt, score_tile, 0)

    def count(pred_fn):
        def tile(t, acc):
            ks = pl.multiple_of(t * kt, kt)
            for c in range(kt // COUNT_ROWS):
                chunk = key_s[pl.ds(ks + c * COUNT_ROWS, COUNT_ROWS), :]
                acc = acc + jnp.where(pred_fn(chunk), 1.0, 0.0)
            return acc
        acc = lax.fori_loop(0, n_kt, tile, jnp.zeros((COUNT_ROWS, qb), F32))
        return jnp.sum(acc, axis=0, keepdims=True)

    def bit_step(b, lo):
        cand = lo + jnp.left_shift(jnp.int32(1), 31 - b)
        return jnp.where(count(lambda k: k >= cand) >= n_sel, cand, lo)

    thr = lax.fori_loop(0, 32, bit_step, jnp.full((1, qb), INT_MIN, jnp.int32))
    need = n_sel - count(lambda k: k > thr)

    def sel_tile(t, off):
        for s in range(ns):
            ks = pl.multiple_of(t * kt + s * LANES, LANES)
            key = key_s[pl.ds(ks, LANES), :]
            eq = key == thr
            pre = jnp.dot(tri_ref[...], jnp.where(eq, 1.0, 0.0).astype(BF16), preferred_element_type=F32)
            kpos = ks + lax.broadcasted_iota(jnp.int32, (LANES, qb), 0)
            keep = jnp.logical_or(key > thr, jnp.logical_and(eq, pre[:LANES] + off < need))
            keep = jnp.logical_and(keep, kpos < lim)
            bias_s[pl.ds(ks, LANES), :] = jnp.where(keep, 0.0, MASKED)
            off = off + pre[LANES:]
        return off

    lax.fori_loop(0, n_kt, sel_tile, jnp.zeros((LANES, qb), F32))

    nc = kt // ATT_ROWS
    width = rep * qb
    s_refs = (s0_s, s1_s)
    p_refs = (p0_s, p1_s)

    def score_chunk(t, g, c, mx):
        ks = pl.multiple_of(t * kt + c * ATT_ROWS, ATT_ROWS)
        sc = jnp.dot(k_ref[0, g, pl.ds(ks, ATT_ROWS), :], w_q[g], preferred_element_type=F32)
        bias = bias_s[pl.ds(ks, ATT_ROWS), :]
        sc = jnp.concatenate([sc[:, r * qb:(r + 1) * qb] + bias for r in range(rep)], axis=1)
        s_refs[g][c * ATT_ROWS:(c + 1) * ATT_ROWS, :] = sc
        return jnp.maximum(mx, jnp.max(sc.reshape(ATT_ROWS // 8, 8, width), axis=0))

    def prob_chunk(g, c, m_new):
        rows = slice(c * ATT_ROWS, (c + 1) * ATT_ROWS)
        p_refs[g][rows, :] = jnp.exp(s_refs[g][rows, :] - m_new).astype(BF16)

    def finish(t, g, m, m_new):
        pv = jnp.dot(vxt_ref[0, t, g * LANES:(g + 1) * LANES, :], p_refs[g][...], preferred_element_type=F32)
        acc_s[g] = acc_s[g] * jnp.exp(m - m_new) + pv

    mx_init = jnp.full((8, width), MASKED, F32)

    def att_tile(t, carry):
        m0, m1, mx0 = carry
        m0_new = jnp.maximum(m0, jnp.max(mx0, axis=0, keepdims=True))
        mx1 = mx_init
        for c in range(nc):
            mx1 = score_chunk(t, 1, c, mx1)
            prob_chunk(0, c, m0_new)
        m1_new = jnp.maximum(m1, jnp.max(mx1, axis=0, keepdims=True))
        finish(t, 0, m0, m0_new)
        t_next = jnp.minimum(t + 1, n_kt - 1)
        mx0 = mx_init
        for c in range(nc):
            mx0 = score_chunk(t_next, 0, c, mx0)
            prob_chunk(1, c, m1_new)
        finish(t, 1, m1, m1_new)
        return m0_new, m1_new, mx0

    acc_s[...] = jnp.zeros(acc_s.shape, F32)
    mx0 = mx_init
    for c in range(nc):
        mx0 = score_chunk(0, 0, c, mx0)
    m_init = jnp.full((1, width), MASKED, F32)
    lax.fori_loop(0, n_kt, att_tile, (m_init, m_init, mx0))
    for g in range(KV_HEADS):
        acc = acc_s[g]
        out = acc * (1.0 / acc[HEAD_DIM:HEAD_DIM + 1, :])
        for r in range(rep):
            h = g * rep + r
            o_ref[:, h * HEAD_DIM:(h + 1) * HEAD_DIM] = out[:, r * qb:(r + 1) * qb].T[:, :HEAD_DIM]


def _dsa_t(q, qi, kw, k_bf, v_bf, ki_bf, *, qb, kt, q0, l_valid, n_sel):
    assert qb == LANES
    b, lp = ki_bf.shape[:2]
    nq = q.shape[0] // (b * qb)
    k_g = jnp.moveaxis(k_bf, 2, 1)
    ones = jnp.ones((b, lp, KV_HEADS, HEAD_DIM), BF16)
    vx = jnp.concatenate([v_bf, ones], axis=-1).reshape(b, lp // kt, kt, KV_HEADS * LANES)
    vxt = jnp.swapaxes(vx, 2, 3)
    tr = lax.broadcasted_iota(jnp.int32, (2 * LANES, LANES), 0)
    tc = lax.broadcasted_iota(jnp.int32, (2 * LANES, LANES), 1)
    tri = jnp.logical_or(tc < tr, tr >= LANES).astype(BF16)
    row = lambda w: pl.BlockSpec((qb, w), lambda bi, i: (bi * nq + i, 0))
    kern = functools.partial(_dsa_t_kernel, qb=qb, kt=kt, q0=q0, l_valid=l_valid, n_sel=n_sel)
    return pl.pallas_call(
        kern,
        grid=(b, nq),
        in_specs=[row(ATT_WIDTH), row(QI_WIDTH), row(KW_WIDTH),
                  pl.BlockSpec((1, KV_HEADS, lp, HEAD_DIM), lambda bi, i: (bi, 0, 0, 0)),
                  pl.BlockSpec((1, lp // kt, KV_HEADS * LANES, kt), lambda bi, i: (bi, 0, 0, 0)),
                  pl.BlockSpec((1, lp, IDX_DIM), lambda bi, i: (bi, 0, 0)),
                  pl.BlockSpec((2 * LANES, LANES), lambda bi, i: (0, 0))],
        out_specs=row(ATT_WIDTH),
        out_shape=jax.ShapeDtypeStruct((q.shape[0], ATT_WIDTH), F32),
        scratch_shapes=[pltpu.VMEM((lp, qb), jnp.int32), pltpu.VMEM((lp, qb), F32),
                        pltpu.VMEM((kt, 4 * qb), F32), pltpu.VMEM((kt, 4 * qb), F32),
                        pltpu.VMEM((kt, 4 * qb), BF16), pltpu.VMEM((kt, 4 * qb), BF16),
                        pltpu.VMEM((KV_HEADS, LANES, 4 * qb), F32)],
        compiler_params=_cparams(("arbitrary", "arbitrary")),
        name="dsa_t",
    )(q, qi, kw, k_g, vxt, ki_bf, tri)


def _split_bf16(x):
    hi = x.astype(BF16)
    return hi, (x - hi.astype(F32)).astype(BF16)


_NN = (((1,), (0,)), ((), ()))


def _dots(a_sp, b_sp, dims=_NN):
    (ah, al), (bh, bl) = a_sp, b_sp
    d = functools.partial(lax.dot_general, dimension_numbers=dims, preferred_element_type=F32)
    return d(ah, bh) + (d(ah, bl) + d(al, bh))


def _dot3(a, b, dims=_NN):
    return _dots(_split_bf16(a), _split_bf16(b), dims)


def _dot2(a, b_exact):
    ah, al = _split_bf16(a)
    return jnp.dot(ah, b_exact, preferred_element_type=F32) + jnp.dot(al, b_exact, preferred_element_type=F32)


def _head_sums(x, hsum):
    w = hsum.shape[0]
    return jnp.concatenate([_dot2(x[:, j * w:(j + 1) * w], hsum) for j in range(x.shape[1] // w)], axis=1)


_NT = (((1,), (1,)), ((), ()))
_TN = (((0,), (0,)), ((), ()))


def _rwkv_kernel(p_ref, prev_ref, s0_ref, mu_ref, w0_ref, w2_ref, a0_ref, a2_ref, g2_ref, kk_ref, ka_ref,
                 rk_ref, lnw_ref, lnb_ref, hsum_ref, tri_ref, o_ref, st_ref, s_s, prev_s, o_s, *, c, nb):
    j = pl.program_id(1)

    @pl.when(j == 0)
    def _init():
        s_s[...] = s0_ref[...]
        prev_s[...] = prev_ref[...]

    hsum = hsum_ref[...]
    rows = nb * c
    row = lax.broadcasted_iota(jnp.int32, (rows, 1), 0)
    o1, o2, o3 = RWKV_WIDTH, 2 * RWKV_WIDTH, 3 * RWKV_WIDTH

    p = p_ref[...].reshape(rows, RWKV_COLS)
    p_prev = pltpu.roll(p, 1, 0)
    for bi in range(nb):
        p_prev = jnp.where(row == bi * c, prev_s[bi], p_prev)
        prev_s[bi] = p[(bi + 1) * c - 1:(bi + 1) * c, :]
    xs = p + (p_prev - p) * mu_ref[...]
    r, k, v = xs[:, :o1], xs[:, o1:o2], xs[:, o2:o3]
    dw = xs[:, o3:o3 + W_LORA]
    da = xs[:, o3 + W_LORA:o3 + W_LORA + A_LORA]
    dg = xs[:, o3 + W_LORA + A_LORA:]
    w_log = -jax.nn.softplus(-(w0_ref[...] + _dot3(jnp.tanh(dw), w2_ref[...]))) - 0.5
    lw = -jnp.exp(w_log)
    a = jax.nn.sigmoid(a0_ref[...] + _dot3(da, a2_ref[...]))
    g = _dot3(jax.nn.sigmoid(dg), g2_ref[...])
    kk = k * kk_ref[...]
    kk = kk / jnp.maximum(jnp.sqrt(_head_sums(kk * kk, hsum)), 1e-12)
    km = k * (1.0 + (a - 1.0) * ka_ref[...])
    bm = kk * a
    lw_hi, lw_lo = _split_bf16(lw)
    tri = tri_ref[...]
    cum = (jnp.dot(tri, lw_hi, preferred_element_type=F32)
           + jnp.dot(tri, lw_lo, preferred_element_type=F32))
    tot = cum[rows - 1:rows, :]
    for bi in range(nb - 1):
        in_chunk = jnp.logical_and(row >= bi * c, row < (bi + 1) * c)
        tot = jnp.where(in_chunk, cum[(bi + 1) * c - 1:(bi + 1) * c, :], tot)
    e_out = jnp.exp(-cum)
    e_end = jnp.exp(tot - cum)
    stacked = dict(a_t=-kk * jnp.exp(cum - lw), r_t=r * jnp.exp(cum), b_t=bm * e_out, k_t=km * e_out,
                   b_h=bm * e_end, k_h=km * e_end, gam=jnp.exp(tot), v=v)
    pre = [{name: val[bi * c:(bi + 1) * c] for name, val in stacked.items()} for bi in range(nb)]

    ri = lax.broadcasted_iota(jnp.int32, (c, c), 0)
    ci = lax.broadcasted_iota(jnp.int32, (c, c), 1)
    strict = ri > ci
    incl = ri >= ci
    eye = jnp.where(ri == ci, 1.0, 0.0)
    n_double = int(np.log2(c)) - 1
    chains = [(bi, h) for bi in range(nb) for h in range(RWKV_HEADS)]
    ids = range(len(chains))
    col = lambda name, i: pre[chains[i][0]][name][:, chains[i][1] * HEAD_DIM:(chains[i][1] + 1) * HEAD_DIM]
    s0 = [s_s[bi, h] for bi, h in chains]
    s0_sp = [_split_bf16(s) for s in s0]
    ar_sp = [_split_bf16(jnp.concatenate([col("a_t", i), col("r_t", i)], axis=0)) for i in ids]
    bk_sp = [_split_bf16(jnp.concatenate([col("b_t", i), col("k_t", i)], axis=0)) for i in ids]
    v_sp = [_split_bf16(col("v", i)) for i in ids]
    m = [_dots(ar_sp[i], bk_sp[i], _NT) for i in ids]
    as0 = [_dots(ar_sp[i], s0_sp[i], _NT) for i in ids]
    x_sp = [_split_bf16(jnp.where(strict, m[i][:c, :c], 0.0)) for i in ids]
    lak_sp = [_split_bf16(jnp.where(strict, m[i][:c, c:], 0.0)) for i in ids]
    tinv = [eye + jnp.where(strict, m[i][:c, :c], 0.0) for i in ids]
    rhs = [as0[i][:c] + _dots(lak_sp[i], v_sp[i]) for i in ids]
    for _ in range(n_double):
        x_sp = [_split_bf16(_dots(x_sp[i], x_sp[i])) for i in ids]
        tinv = [tinv[i] + _dots(_split_bf16(tinv[i]), x_sp[i]) for i in ids]
    u_sp = [_split_bf16(_dot3(tinv[i], rhs[i])) for i in ids]
    for i in ids:
        bi, h = chains[i]
        m_rb = jnp.where(incl, m[i][c:, :c], 0.0)
        m_rk = jnp.where(incl, m[i][c:, c:], 0.0)
        o_s[bi, :, h * HEAD_DIM:(h + 1) * HEAD_DIM] = (as0[i][c:] + _dots(_split_bf16(m_rb), u_sp[i])
                                                       + _dots(_split_bf16(m_rk), v_sp[i]))
    s_new = [s0[i] * col("gam", i)[:1] +_dots(u_sp[i], _split_bf16(col("b_h", i)), _TN)
             + _dots(v_sp[i], _split_bf16(col("k_h", i)), _TN) for i in ids]
    for i in ids:
        s_s[chains[i][0], chains[i][1]] = s_new[i]

    inv_d = 1.0 / HEAD_DIM
    o = o_s[...].reshape(rows, RWKV_WIDTH)
    mean = _head_sums(o, hsum) * inv_d
    cen = o - mean
    var = _head_sums(cen * cen, hsum) * inv_d
    on = cen * lax.rsqrt(var + GN_EPS) * lnw_ref[...] + lnb_ref[...]
    bonus = _head_sums(r * km * rk_ref[...], hsum) * v
    o_ref[...] = ((on + bonus) * g).reshape(nb, c, RWKV_WIDTH)

    @pl.when(j == pl.num_programs(1) - 1)
    def _fin():
        st_ref[...] = s_s[...]


def _rwkv(p, prev, s0, rw, c, nb):
    mu, w0, w2, a0, a2, g2, k_k, k_a, r_k, lnx_w, lnx_b = rw
    b, t, _ = p.shape
    nc = t // c
    hid = lax.broadcasted_iota(jnp.int32, (2 * LANES, 2 * LANES), 0) // HEAD_DIM
    hsum = (hid == hid.T).astype(BF16)
    tr = lax.broadcasted_iota(jnp.int32, (nb * c, nb * c), 0)
    tc = lax.broadcasted_iota(jnp.int32, (nb * c, nb * c), 1)
    tri = jnp.logical_and(tr >= tc, tr // c == tc // c).astype(BF16)
    vec = lambda a: a.reshape(1, -1)
    full = lambda a: pl.BlockSpec(a.shape, lambda bi, j: (0,) * a.ndim)
    small = [vec(mu), vec(w0), w2, vec(a0), a2, g2, vec(k_k), vec(k_a), vec(r_k), vec(lnx_w), vec(lnx_b), hsum, tri]
    state = pl.BlockSpec((nb, RWKV_HEADS, HEAD_DIM, HEAD_DIM), lambda bi, j: (bi, 0, 0, 0))
    return pl.pallas_call(
        functools.partial(_rwkv_kernel, c=c, nb=nb),
        grid=(b // nb, nc),
        in_specs=[pl.BlockSpec((nb, c, RWKV_COLS), lambda bi, j: (bi, j, 0)),
                  pl.BlockSpec((nb, 1, RWKV_COLS), lambda bi, j: (bi, 0, 0)), state] + [full(a) for a in small],
        out_specs=[pl.BlockSpec((nb, c, RWKV_WIDTH), lambda bi, j: (bi, j, 0)), state],
        out_shape=[jax.ShapeDtypeStruct((b, t, RWKV_WIDTH), F32),
                   jax.ShapeDtypeStruct((b, RWKV_HEADS, HEAD_DIM, HEAD_DIM), F32)],
        scratch_shapes=[pltpu.VMEM((nb, RWKV_HEADS, HEAD_DIM, HEAD_DIM), F32), pltpu.VMEM((nb, 1, RWKV_COLS), F32),
                        pltpu.VMEM((nb, c, RWKV_WIDTH), F32)],
        compiler_params=_cparams(("arbitrary", "arbitrary")),
        name="rwkv",
    )(p, prev, s0, *small)


def _branch(x, mods, pos, prev, s0, caches, weights, tm, tm_moe):
    (g_mix, g_ffn, g_final, w_r, w_a, rw, w_out, rwt, rb, wup, bup, wdn, bdn) = weights
    sh1, sc1, gt1, sh2, sc2, gt2 = mods
    b, t, d = x.shape
    tab, tabk = _rope_tables(pos)
    p_r, q, k, v, qi, kw = _inproj(x, g_mix, sh1, sc1, w_r, w_a, tab, tabk, tm)
    p_r3 = p_r.reshape(b, t, RWKV_COLS)
    o_r, wkv = _rwkv(p_r3, prev, s0, rw, min(CHUNK, t), RWKV_BATCH)
    shift = p_r3[:, -1:]
    k4 = k.reshape(b, t, KV_HEADS, HEAD_DIM)
    v4 = v.reshape(b, t, KV_HEADS, HEAD_DIM)
    ki = kw.reshape(b, t, KW_WIDTH)[..., :IDX_DIM]
    kt = 1024 if caches is None else 512
    if caches is None:
        k_all, v_all, ki_all, q0, l_valid = k4, v4, ki, 0, t
    else:
        ck, cv, cki = caches
        q0 = ck.shape[1]
        l_valid = q0 + t
        padn = (-l_valid) % kt
        zpad = lambda a: jnp.concatenate([a, jnp.zeros((b, padn) + a.shape[2:], a.dtype)], axis=1)
        k_all = zpad(jnp.concatenate([ck, k4], axis=1))
        v_all = zpad(jnp.concatenate([cv, v4], axis=1))
        ki_all = zpad(jnp.concatenate([cki, ki], axis=1))
    qb = min(Q_BLOCK, t)
    dsa = _dsa_t if qb == LANES else _dsa
    o_a = dsa(q, qi, kw, k_all.astype(BF16), v_all.astype(BF16), ki_all.astype(BF16),
              qb=qb, kt=kt, q0=q0, l_valid=l_valid, n_sel=min(TOPK_MAX, l_valid // 4))
    x1, h2 = _outproj(o_r, o_a, x, w_out, gt1, g_ffn, sh2, sc2, tm)
    y_moe = _moe(h2, rwt, rb, wup, bup, wdn, bdn, *tm_moe)
    y = _final(x1, y_moe, gt2, g_final, b, t, tm)
    return y, k4, v4, ki, wkv, shift


def kernel(x_prompt, x_sample, c_prompt, c_sample, cache_k, cache_v, cache_kidx, state_wkv, state_shift,
           w_ada, b_ada, g_mix, g_ffn, g_final, w_in, mu_shift, w0, w_lora_w, a0, w_lora_a, w_lora_g,
           k_k, k_a, r_k, lnx_w, lnx_b, w_out, router_w, router_b, w_up, b_up, w_down, b_down):
    B, T, D = x_prompt.shape
    DB, DT, _ = x_sample.shape
    P = cache_k.shape[2]
    l = 0
    assert w_ada.shape[0] == 1

    rows = B + DB
    pad = (-rows) % 8
    c_all = jnp.concatenate([c_prompt, c_sample, jnp.zeros((pad, D), F32)], axis=0)
    m = _ada(c_all, w_ada[l], b_ada[l])
    mods_p = tuple(t.reshape(B, 1, D) for t in jnp.split(m[:B], 6, axis=-1))
    mods_s = tuple(t.reshape(DB, 1, D) for t in jnp.split(m[B:rows], 6, axis=-1))

    w_r = w_in[l][:, :RWKV_COLS].astype(BF16)
    w_att = w_in[l][:, RWKV_COLS:]
    att_pad = KW_WIDTH - IDX_DIM - IDX_HEADS
    w_a = jnp.concatenate([w_att, jnp.zeros((D, att_pad), F32)], axis=1).astype(BF16)
    rw = (mu_shift[l], w0[l], w_lora_w[l], a0[l], w_lora_a[l], w_lora_g[l], k_k[l], k_a[l], r_k[l],
          lnx_w[l], lnx_b[l])
    wup = _deinterleave_cast(w_up[l])
    bup = jnp.concatenate([b_up[l][..., 0::2], b_up[l][..., 1::2]], axis=-1).reshape(N_EXPERTS, 1, 2 * D_FF)
    wdn = w_down[l].astype(BF16)
    bdn = b_down[l].reshape(N_EXPERTS, 1, D)
    weights = (g_mix[l], g_ffn[l], g_final, w_r, w_a, rw, w_out[l].astype(BF16), router_w[l].T,
               router_b[l].reshape(N_EXPERTS, 1), wup, bup, wdn, bdn)

    pos_p = jnp.arange(T, dtype=jnp.int32)
    pos_s = P + jnp.arange(DT, dtype=jnp.int32)
    yp, kp, vp, kip, wkvp, shp = _branch(
        x_prompt, mods_p, pos_p, jnp.zeros((B, 1, RWKV_COLS), F32),
        jnp.zeros((B, RWKV_HEADS, HEAD_DIM, HEAD_DIM), F32), None, weights, 512, (MOE_SUB, MOE_NSUB))
    ys, ks, vs, kis, wkvs, shs = _branch(
        x_sample, mods_s, pos_s, state_shift[l], state_wkv[l],
        (cache_k[l], cache_v[l], cache_kidx[l]), weights, DT, (DB * DT, 1))
    return (yp, ys, kp[None], vp[None], kip[None], wkvp[None], shp[None],
            ks[None], vs[None], kis[None], wkvs[None], shs[None])
```

```python
import functools

import numpy as np
import jax
import jax.numpy as jnp
from jax import lax
from jax.experimental import pallas as pl
from jax.experimental.pallas import tpu as pltpu

F32 = jnp.float32
BF16 = jnp.bfloat16
HIGHEST = lax.Precision.HIGHEST

D_MODEL = 1024
CHUNK = 64
HEAD_DIM = 64
RWKV_HEADS = 8
RWKV_WIDTH = RWKV_HEADS * HEAD_DIM
W_LORA = 64
A_LORA = 64
G_LORA = 128
RWKV_COLS = 3 * RWKV_WIDTH + W_LORA + A_LORA + G_LORA
GN_EPS = 64e-5
ATT_HEADS = 8
ATT_WIDTH = ATT_HEADS * HEAD_DIM
KV_HEADS = 2
KV_WIDTH = KV_HEADS * HEAD_DIM
IDX_HEADS = 4
IDX_DIM = 64
TOPK_MAX = 256
Q_BLOCK = 128
ROPE_THETA = 500000.0
ROT_DIM = HEAD_DIM // 4
N_EXPERTS = 32
TOP_K = 4
D_FF = 1024
SWIGLU_ALPHA = 1.702
SWIGLU_LIMIT = 7.0
NORM_EPS = 1e-5

LANES = 128
MOE_ROWS = 128
MOE_SUB = 768
MOE_NSUB = 3
VMEM_LIMIT = 56 * 1024 * 1024

QI_WIDTH = IDX_HEADS * IDX_DIM
KW_WIDTH = LANES


def _cparams(sem):
    return pltpu.CompilerParams(dimension_semantics=sem, vmem_limit_bytes=VMEM_LIMIT)


def _ada_kernel(c_ref, w_ref, b_ref, o_ref):
    c = c_ref[...]
    s = c * jax.nn.sigmoid(c)
    o_ref[...] = jnp.dot(s, w_ref[...], precision=HIGHEST, preferred_element_type=F32) + b_ref[...]


def _ada(c, w, b):
    rows, d = c.shape
    n = w.shape[1]
    tn = 1536
    return pl.pallas_call(
        _ada_kernel,
        grid=(n // tn,),
        in_specs=[pl.BlockSpec((rows, d), lambda j: (0, 0)),
                  pl.BlockSpec((d, tn), lambda j: (0, j)),
                  pl.BlockSpec((1, tn), lambda j: (0, j))],
        out_specs=pl.BlockSpec((rows, tn), lambda j: (0, j)),
        out_shape=jax.ShapeDtypeStruct((rows, n), F32),
        compiler_params=_cparams(("arbitrary",)),
        name="ada",
    )(c, w, b.reshape(1, n))


def _rope_slab(y, tab_ref):
    return (y * tab_ref[0] + pltpu.roll(y, LANES - ROT_DIM // 2, 1) * tab_ref[1]
            + pltpu.roll(y, ROT_DIM // 2, 1) * tab_ref[2])


def _norm_mod(x, g, sh, sc):
    var = jnp.mean(x * x, axis=-1, keepdims=True)
    return (x * lax.rsqrt(var + NORM_EPS) * g) * (1.0 + sc) + sh


def _inproj_kernel(x_ref, g_ref, sh_ref, sc_ref, wr_ref, wa_ref, tab_ref, tabk_ref,
                   rw_ref, q_ref, k_ref, v_ref, qi_ref, kw_ref):
    h = _norm_mod(x_ref[...], g_ref[...], sh_ref[0], sc_ref[0]).astype(BF16)
    rw_ref[...] = jnp.dot(h, wr_ref[...], preferred_element_type=F32)
    pa = jnp.dot(h, wa_ref[...], preferred_element_type=F32)
    off = 0
    for ref, width, rot in ((q_ref, ATT_WIDTH, True), (k_ref, KV_WIDTH, True), (v_ref, KV_WIDTH, False),
                            (qi_ref, QI_WIDTH, True)):
        for s in range(width // LANES):
            slab = pa[:, off + s * LANES: off + (s + 1) * LANES]
            ref[:, s * LANES:(s + 1) * LANES] = _rope_slab(slab, tab_ref) if rot else slab
        off += width
    kw_ref[...] = _rope_slab(pa[:, off:off + LANES], tabk_ref)


def _rope_tables(pos):
    half = ROT_DIM // 2
    inv = ROPE_THETA ** (-jnp.arange(0, ROT_DIM, 2, dtype=F32) / ROT_DIM)
    ang = pos.astype(F32)[:, None] * inv[None, :]
    cos, sin = jnp.cos(ang), jnp.sin(ang)
    t = pos.shape[0]
    one = jnp.ones((t, HEAD_DIM - ROT_DIM), F32)
    zero_r = jnp.zeros((t, HEAD_DIM - ROT_DIM), F32)
    zero_h = jnp.zeros((t, half), F32)
    c_head = jnp.concatenate([cos, cos, one], axis=1)
    up_head = jnp.concatenate([-sin, zero_h, zero_r], axis=1)
    dn_head = jnp.concatenate([zero_h, sin, zero_r], axis=1)
    tab = jnp.stack([jnp.tile(c_head, (1, 2)), jnp.tile(up_head, (1, 2)), jnp.tile(dn_head, (1, 2))])
    wscale = jnp.concatenate([jnp.full((t, IDX_HEADS), IDX_HEADS ** -0.5, F32),
                              jnp.ones((t, HEAD_DIM - IDX_HEADS), F32)], axis=1)
    zero64 = jnp.zeros((t, HEAD_DIM), F32)
    tabk = jnp.stack([jnp.concatenate([c_head, wscale], axis=1),
                      jnp.concatenate([up_head, zero64], axis=1),
                      jnp.concatenate([dn_head, zero64], axis=1)])
    return tab, tabk


def _inproj(x, g, sh, sc, w_r, w_a, tab, tabk, tm):
    b, t, d = x.shape
    nt = t // tm
    n = b * t
    x2 = x.reshape(n, d)
    widths = (RWKV_COLS, ATT_WIDTH, KV_WIDTH, KV_WIDTH, QI_WIDTH, KW_WIDTH)
    row = lambda w: pl.BlockSpec((tm, w), lambda i: (i, 0))
    mod = pl.BlockSpec((1, 1, d), lambda i: (i // nt, 0, 0))
    tabspec = pl.BlockSpec((3, tm, LANES), lambda i: (0, i % nt, 0))
    return pl.pallas_call(
        _inproj_kernel,
        grid=(n // tm,),
        in_specs=[row(d), pl.BlockSpec((1, d), lambda i: (0, 0)), mod, mod,
                  pl.BlockSpec(w_r.shape, lambda i: (0, 0)), pl.BlockSpec(w_a.shape, lambda i: (0, 0)),
                  tabspec, tabspec],
        out_specs=[row(w) for w in widths],
        out_shape=[jax.ShapeDtypeStruct((n, w), F32) for w in widths],
        compiler_params=_cparams(("arbitrary",)),
        name="inproj",
    )(x2, g.reshape(1, d), sh, sc, w_r, w_a, tab, tabk)


def _outproj_kernel(or_ref, oa_ref, x_ref, w_ref, gt_ref, g_ref, sh_ref, sc_ref, x1_ref, h2_ref):
    m = jnp.dot(or_ref[...].astype(BF16), w_ref[:RWKV_WIDTH, :], preferred_element_type=F32)
    m = m + jnp.dot(oa_ref[...].astype(BF16), w_ref[RWKV_WIDTH:, :], preferred_element_type=F32)
    x1 = x_ref[...] + gt_ref[0] * m
    x1_ref[...] = x1
    h2_ref[...] = _norm_mod(x1, g_ref[...], sh_ref[0], sc_ref[0]).astype(BF16)


def _outproj(o_r, o_a, x, w_out, gt, g, sh, sc, tm):
    b, t, d = x.shape
    nt = t // tm
    n = b * t
    row = lambda w: pl.BlockSpec((tm, w), lambda i: (i, 0))
    mod = pl.BlockSpec((1, 1, d), lambda i: (i // nt, 0, 0))
    return pl.pallas_call(
        _outproj_kernel,
        grid=(n // tm,),
        in_specs=[row(RWKV_WIDTH), row(ATT_WIDTH), row(d), pl.BlockSpec(w_out.shape, lambda i: (0, 0)),
                  mod, pl.BlockSpec((1, d), lambda i: (0, 0)), mod, mod],
        out_specs=[row(d), row(d)],
        out_shape=[jax.ShapeDtypeStruct((n, d), F32), jax.ShapeDtypeStruct((n, d), BF16)],
        compiler_params=_cparams(("arbitrary",)),
        name="outproj",
    )(o_r.reshape(n, RWKV_WIDTH), o_a.reshape(n, ATT_WIDTH), x.reshape(n, d), w_out, gt, g.reshape(1, d), sh, sc)


def _moe_kernel(h_ref, rwt_ref, rb_ref, tri_ref, wup_ref, bup_ref, wdn_ref, bdn_ref, y_ref,
                hb_s, rank_s, gate_s, count_s, *, n_tokens, sub):
    e = pl.program_id(1)
    n_sub = h_ref.shape[0] // sub

    @pl.when(e == 0)
    def _route():
        rwt = rwt_ref[...].astype(BF16)
        for s in range(n_sub):
            first_row = pl.program_id(0) * (n_sub * sub) + s * sub
            in_rows = first_row + lax.broadcasted_iota(jnp.int32, (sub, 1), 0) < n_tokens
            in_cols = first_row + lax.broadcasted_iota(jnp.int32, (1, sub), 1) < n_tokens
            hb = jnp.where(in_rows, h_ref[s * sub:(s + 1) * sub, :], 0.0).astype(BF16)
            hb_s[s * sub:(s + 1) * sub, :] = hb
            logits = lax.dot_general(rwt, hb, (((1,), (1,)), ((), ())), preferred_element_type=F32) + rb_ref[...]
            eidx = lax.broadcasted_iota(jnp.int32, logits.shape, 0)
            work = logits
            top = None
            for _ in range(TOP_K):
                m = jnp.max(work, axis=0, keepdims=True)
                if top is None:
                    top = m
                first = jnp.min(jnp.where(work == m, eidx, N_EXPERTS), axis=0, keepdims=True)
                work = jnp.where(eidx == first, -jnp.inf, work)
            ex = jnp.where(work != logits, jnp.exp(logits - top), 0.0)
            sel = jnp.logical_and(work != logits, in_cols)
            gate_s[s] = ex / jnp.sum(ex, axis=0, keepdims=True)
            before = jnp.dot(jnp.where(sel, 1.0, 0.0).astype(BF16), tri_ref[...], preferred_element_type=F32)
            rank_s[s] = jnp.where(sel, before, -1.0)
            for ex_id in range(N_EXPERTS):
                count_s[s, ex_id] = jnp.sum(jnp.where(sel[ex_id:ex_id + 1, :], 1, 0))
        y_ref[...] = jnp.zeros(y_ref.shape, F32)

    for s in range(n_sub):
        r_row = rank_s[s, pl.ds(e, 1), :]
        g_row = gate_s[s, pl.ds(e, 1), :]
        n_blocks = (count_s[s, e] + MOE_ROWS - 1) // MOE_ROWS

        def block(j, carry, s=s, r_row=r_row, g_row=g_row):
            rows = (lax.broadcasted_iota(jnp.int32, (MOE_ROWS, sub), 0) + j * MOE_ROWS).astype(F32)
            hit = r_row == rows
            p = jnp.where(hit, 1.0, 0.0).astype(BF16)
            xe = jnp.dot(p, hb_s[s * sub:(s + 1) * sub, :], preferred_element_type=F32).astype(BF16)
            u = jnp.dot(xe, wup_ref[0], preferred_element_type=F32) + bup_ref[0]
            glu = jnp.minimum(u[:, :D_FF], SWIGLU_LIMIT)
            lin = jnp.clip(u[:, D_FF:], -SWIGLU_LIMIT, SWIGLU_LIMIT)
            act = glu * jax.nn.sigmoid(SWIGLU_ALPHA * glu) * (lin + 1.0)
            yb = jnp.dot(act.astype(BF16), wdn_ref[0], preferred_element_type=F32) + bdn_ref[0]
            g_rows = jnp.sum(jnp.where(hit, g_row, 0.0), axis=1, keepdims=True)
            ys = (yb * g_rows).astype(BF16)
            y_ref[s * sub:(s + 1) * sub, :] += lax.dot_general(p, ys, (((0,), (0,)), ((), ())),
                                                               preferred_element_type=F32)
            return carry

        lax.fori_loop(0, n_blocks, block, 0)


def _moe(h, rwt, rb, wup, bup, wdn, bdn, sub, n_sub):
    n, d = h.shape
    tm = sub * n_sub
    tri = (lax.broadcasted_iota(jnp.int32, (sub, sub), 0) < lax.broadcasted_iota(jnp.int32, (sub, sub), 1)).astype(BF16)
    return pl.pallas_call(
        functools.partial(_moe_kernel, n_tokens=n, sub=sub),
        grid=(pl.cdiv(n, tm), N_EXPERTS),
        in_specs=[pl.BlockSpec((tm, d), lambda i, e: (i, 0)),
                  pl.BlockSpec((N_EXPERTS, d), lambda i, e: (0, 0)),
                  pl.BlockSpec((N_EXPERTS, 1), lambda i, e: (0, 0)),
                  pl.BlockSpec((sub, sub), lambda i, e: (0, 0)),
                  pl.BlockSpec((1, d, 2 * D_FF), lambda i, e: (e, 0, 0)),
                  pl.BlockSpec((1, 1, 2 * D_FF), lambda i, e: (e, 0, 0)),
                  pl.BlockSpec((1, D_FF, d), lambda i, e: (e, 0, 0)),
                  pl.BlockSpec((1, 1, d), lambda i, e: (e, 0, 0))],
        out_specs=pl.BlockSpec((tm, d), lambda i, e: (i, 0)),
        out_shape=jax.ShapeDtypeStruct((n, d), F32),
        scratch_shapes=[pltpu.VMEM((tm, d), BF16), pltpu.VMEM((n_sub, N_EXPERTS, sub), F32),
                        pltpu.VMEM((n_sub, N_EXPERTS, sub), F32), pltpu.SMEM((n_sub, N_EXPERTS), jnp.int32)],
        compiler_params=_cparams(("arbitrary", "arbitrary")),
        name="moe",
    )(h, rwt, rb, tri, wup, bup, wdn, bdn)


def _deinterleave_kernel(w_ref, perm_ref, o_ref):
    n = w_ref.shape[2]
    blk = 2 * LANES
    for j in range(n // blk):
        r = jnp.dot(w_ref[0, :, j * blk:(j + 1) * blk].astype(BF16), perm_ref[...],
                    preferred_element_type=F32).astype(BF16)
        o_ref[0, :, j * LANES:(j + 1) * LANES] = r[:, :LANES]
        o_ref[0, :, n // 2 + j * LANES:n // 2 + (j + 1) * LANES] = r[:, LANES:]


def _deinterleave_cast(w):
    e, d, n = w.shape
    tk = 512
    blk = 2 * LANES
    src = lax.broadcasted_iota(jnp.int32, (blk, blk), 0)
    dst = lax.broadcasted_iota(jnp.int32, (blk, blk), 1)
    perm = (src == jnp.where(dst < LANES, 2 * dst, 2 * (dst - LANES) + 1)).astype(BF16)
    return pl.pallas_call(
        _deinterleave_kernel,
        grid=(e, d // tk),
        in_specs=[pl.BlockSpec((1, tk, n), lambda i, j: (i, j, 0)), pl.BlockSpec((blk, blk), lambda i, j: (0, 0))],
        out_specs=pl.BlockSpec((1, tk, n), lambda i, j: (i, j, 0)),
        out_shape=jax.ShapeDtypeStruct((e, d, n), BF16),
        compiler_params=_cparams(("arbitrary", "arbitrary")),
        name="deinterleave",
    )(w, perm)


def _final_kernel(x1_ref, y_ref, gt_ref, g_ref, o_ref):
    x = x1_ref[...] + gt_ref[0] * y_ref[...]
    var = jnp.mean(x * x, axis=-1, keepdims=True)
    o_ref[...] = x * lax.rsqrt(var + NORM_EPS) * g_ref[...]


def _final(x1, y, gt, g, b, t, tm):
    n, d = x1.shape
    nt = t // tm
    row = pl.BlockSpec((tm, d), lambda i: (i, 0))
    return pl.pallas_call(
        _final_kernel,
        grid=(n // tm,),
        in_specs=[row, row, pl.BlockSpec((1, 1, d), lambda i: (i // nt, 0, 0)),
                  pl.BlockSpec((1, d), lambda i: (0, 0))],
        out_specs=row,
        out_shape=jax.ShapeDtypeStruct((n, d), F32),
        compiler_params=_cparams(("arbitrary",)),
        name="final",
    )(x1, y, gt, g.reshape(1, d)).reshape(b, t, d)


MASKED = -1e30
INT_MIN = -2 ** 31
COUNT_ROWS = 64
ATT_ROWS = 128
RWKV_BATCH = 2


def _dsa_kernel(q_ref, qi_ref, kw_ref, k_ref, vx_ref, ki_ref, tri_ref, o_ref, key_s, bias_s, wib_s,
                *, qb, kt, q0, l_valid, n_sel):
    i = pl.program_id(1)
    ns = kt // LANES
    row = lax.broadcasted_iota(jnp.int32, (qb, 1), 0)
    qpos = q0 + i * qb + row
    lim = jnp.minimum((qpos // CHUNK + 1) * CHUNK, l_valid)
    last_lim = jnp.minimum(((q0 + i * qb + qb - 1) // CHUNK + 1) * CHUNK, l_valid)
    n_kt = (last_lim + kt - 1) // kt
    lane = lax.broadcasted_iota(jnp.int32, (1, LANES), 1)

    for h in range(IDX_HEADS):
        w = kw_ref[:, IDX_DIM + h:IDX_DIM + h + 1] * (IDX_DIM ** -0.5)
        wib_s[h] = jnp.broadcast_to(w, (qb, LANES))
    qi = qi_ref[...].astype(BF16)

    def f32_key(x):
        bits = pltpu.bitcast(x, jnp.int32)
        return bits ^ ((bits >> 31) & 0x7FFFFFFF)

    def score_tile(t, c):
        ks = pl.multiple_of(t * kt, kt)
        kit = ki_ref[0, pl.ds(ks, kt), :]
        lg = [lax.dot_general(qi[:, h * IDX_DIM:(h + 1) * IDX_DIM], kit, (((1,), (1,)), ((), ())),
                              preferred_element_type=F32) for h in range(IDX_HEADS)]
        for s in range(ns):
            sc = jnp.zeros((qb, LANES), F32)
            for h in range(IDX_HEADS):
                sc = sc + wib_s[h] * jnp.maximum(lg[h][:, s * LANES:(s + 1) * LANES], 0.0)
            kpos = ks + s * LANES + lane
            sc = jnp.where(kpos < lim, sc + 0.0, -jnp.inf)
            key_s[t, :, s * LANES:(s + 1) * LANES] = f32_key(sc)
        return c

    lax.fori_loop(0, n_kt, score_tile, 0)

    def count(pred_fn):
        def tile(t, acc):
            for s in range(ns):
                acc = acc + jnp.where(pred_fn(key_s[t, :, s * LANES:(s + 1) * LANES]), 1.0, 0.0)
            return acc
        acc = lax.fori_loop(0, n_kt, tile, jnp.zeros((qb, LANES), F32))
        return jnp.sum(acc, axis=1, keepdims=True)

    def bit_step(b, lo):
        cand = lo + jnp.left_shift(jnp.int32(1), 31 - b)
        candb = jnp.broadcast_to(cand, (qb, LANES))
        return jnp.where(count(lambda k: k >= candb) >= n_sel, cand, lo)

    thr = lax.fori_loop(0, 32, bit_step, jnp.full((qb, 1), INT_MIN, jnp.int32))
    thrb = jnp.broadcast_to(thr, (qb, LANES))
    need = jnp.broadcast_to(n_sel - count(lambda k: k > thrb), (qb, LANES))

    def sel_tile(t, off):
        for s in range(ns):
            key = key_s[t, :, s * LANES:(s + 1) * LANES]
            eq = key == thrb
            pre = jnp.dot(jnp.where(eq, 1.0, 0.0).astype(BF16), tri_ref[...], preferred_element_type=F32)
            kpos = t * kt + s * LANES + lane
            keep = jnp.logical_or(key > thrb, jnp.logical_and(eq, pre[:, :LANES] + off < need))
            keep = jnp.logical_and(keep, kpos < lim)
            bias_s[t, :, s * LANES:(s + 1) * LANES] = jnp.where(keep, 0.0, MASKED)
            off = off + pre[:, LANES:]
        return off

    lax.fori_loop(0, n_kt, sel_tile, jnp.zeros((qb, LANES), F32))

    rep = ATT_HEADS // KV_HEADS
    qgs = []
    for g in range(KV_HEADS):
        qg = jnp.concatenate([q_ref[:, (g * rep + r) * HEAD_DIM:(g * rep + r + 1) * HEAD_DIM] for r in range(rep)],
                             axis=0)
        qgs.append((qg * (HEAD_DIM ** -0.5)).astype(BF16))

    def att_tile(t, carry):
        ks = pl.multiple_of(t * kt, kt)
        out = []
        for g in range(KV_HEADS):
            m, acc = carry[g]
            s = lax.dot_general(qgs[g], k_ref[0, g, pl.ds(ks, kt), :], (((1,), (1,)), ((), ())),
                                preferred_element_type=F32)
            s = (s.reshape(rep, qb, kt) + bias_s[t][None]).reshape(rep * qb, kt)
            m_new = jnp.maximum(m, jnp.max(s, axis=1, keepdims=True))
            p = jnp.exp(s - m_new)
            pv = jnp.dot(p.astype(BF16), vx_ref[0, pl.ds(ks, kt), g * LANES:(g + 1) * LANES],
                         preferred_element_type=F32)
            out.append((m_new, acc * jnp.exp(m - m_new) + pv))
        return tuple(out)

    init = (jnp.full((rep * qb, 1), MASKED, F32), jnp.zeros((rep * qb, LANES), F32))
    res = lax.fori_loop(0, n_kt, att_tile, (init,) * KV_HEADS)
    for g in range(KV_HEADS):
        acc = res[g][1]
        out = acc * pltpu.roll(1.0 / acc, HEAD_DIM, 1)
        for r in range(rep):
            h = g * rep + r
            o_ref[:, h * HEAD_DIM:(h + 1) * HEAD_DIM] = out[r * qb:(r + 1) * qb, :HEAD_DIM]


def _dsa(q, qi, kw, k_bf, v_bf, ki_bf, *, qb, kt, q0, l_valid, n_sel):
    b, lp = ki_bf.shape[:2]
    nq = q.shape[0] // (b * qb)
    k_g = jnp.moveaxis(k_bf, 2, 1)
    ones = jnp.ones((b, lp, KV_HEADS, HEAD_DIM), BF16)
    vx = jnp.concatenate([v_bf, ones], axis=-1).reshape(b, lp, KV_HEADS * LANES)
    tr = lax.broadcasted_iota(jnp.int32, (LANES, 2 * LANES), 0)
    tc = lax.broadcasted_iota(jnp.int32, (LANES, 2 * LANES), 1)
    tri = jnp.logical_or(tr < tc, tc >= LANES).astype(BF16)
    row = lambda w: pl.BlockSpec((qb, w), lambda bi, i: (bi * nq + i, 0))
    kern = functools.partial(_dsa_kernel, qb=qb, kt=kt, q0=q0, l_valid=l_valid, n_sel=n_sel)
    return pl.pallas_call(
        kern,
        grid=(b, nq),
        in_specs=[row(ATT_WIDTH), row(QI_WIDTH), row(KW_WIDTH),
                  pl.BlockSpec((1, KV_HEADS, lp, HEAD_DIM), lambda bi, i: (bi, 0, 0, 0)),
                  pl.BlockSpec((1, lp, KV_HEADS * LANES), lambda bi, i: (bi, 0, 0)),
                  pl.BlockSpec((1, lp, IDX_DIM), lambda bi, i: (bi, 0, 0)),
                  pl.BlockSpec((LANES, 2 * LANES), lambda bi, i: (0, 0))],
        out_specs=row(ATT_WIDTH),
        out_shape=jax.ShapeDtypeStruct((q.shape[0], ATT_WIDTH), F32),
        scratch_shapes=[pltpu.VMEM((lp // kt, qb, kt), jnp.int32), pltpu.VMEM((lp // kt, qb, kt), F32),
                        pltpu.VMEM((IDX_HEADS, qb, LANES), F32)],
        compiler_params=_cparams(("arbitrary", "arbitrary")),
        name="dsa",
    )(q, qi, kw, k_g, vx, ki_bf, tri)


def _dsa_t_kernel(q_ref, qi_ref, kw_ref, k_ref, vxt_ref, ki_ref, tri_ref, o_ref, key_s, bias_s, s0_s, s1_s,
                  p0_s, p1_s, acc_s, *, qb, kt, q0, l_valid, n_sel):
    i = pl.program_id(1)
    ns = kt // LANES
    qpos = q0 + i * qb + lax.broadcasted_iota(jnp.int32, (1, qb), 1)
    lim = jnp.minimum((qpos // CHUNK + 1) * CHUNK, l_valid)
    last_lim = jnp.minimum(((q0 + i * qb + qb - 1) // CHUNK + 1) * CHUNK, l_valid)
    n_kt = (last_lim + kt - 1) // kt
    rep = ATT_HEADS // KV_HEADS

    def by_head(x_t, n):
        return jnp.concatenate([x_t[h * HEAD_DIM:(h + 1) * HEAD_DIM, :] for h in range(n)], axis=1)

    w_qi = by_head(qi_ref[...].T, IDX_HEADS).astype(BF16)
    q_t = q_ref[...].T * (HEAD_DIM ** -0.5)
    w_q = [by_head(q_t[g * rep * HEAD_DIM:(g + 1) * rep * HEAD_DIM, :], rep).astype(BF16) for g in range(KV_HEADS)]
    kw_t = kw_ref[...].T
    wi = [kw_t[IDX_DIM + h:IDX_DIM + h + 1, :] * (IDX_DIM ** -0.5) for h in range(IDX_HEADS)]

    def f32_key(x):
        bits = pltpu.bitcast(x, jnp.int32)
        return bits ^ ((bits >> 31) & 0x7FFFFFFF)

    def score_tile(t, c):
        for u in range(kt // ATT_ROWS):
            ks = pl.multiple_of(t * kt + u * ATT_ROWS, ATT_ROWS)
            lg = jnp.dot(ki_ref[0, pl.ds(ks, ATT_ROWS), :], w_qi, preferred_element_type=F32)
            sc = jnp.zeros((ATT_ROWS, qb), F32)
            for h in range(IDX_HEADS):
                sc = sc + wi[h] * jnp.maximum(lg[:, h * qb:(h + 1) * qb], 0.0)
            kpos = ks + lax.broadcasted_iota(jnp.int32, (ATT_ROWS, qb), 0)
            sc = jnp.where(kpos < lim, sc + 0.0, -jnp.inf)
            key_s[pl.ds(ks, ATT_ROWS), :] = f32_key(sc)
        return c

    lax.fori_loop(0, n_kt, score_tile, 0)

    def count(pred_fn):
        def tile(t, acc):
            ks = pl.multiple_of(t * kt, kt)
            for c in range(kt // COUNT_ROWS):
                chunk = key_s[pl.ds(ks + c * COUNT_ROWS, COUNT_ROWS), :]
                acc = acc + jnp.where(pred_fn(chunk), 1.0, 0.0)
            return acc
        acc = lax.fori_loop(0, n_kt, tile, jnp.zeros((COUNT_ROWS, qb), F32))
        return jnp.sum(acc, axis=0, keepdims=True)

    def bit_step(b, lo):
        cand = lo + jnp.left_shift(jnp.int32(1), 31 - b)
        return jnp.where(count(lambda k: k >= cand) >= n_sel, cand, lo)

    thr = lax.fori_loop(0, 32, bit_step, jnp.full((1, qb), INT_MIN, jnp.int32))
    need = n_sel - count(lambda k: k > thr)

    def sel_tile(t, off):
        for s in range(ns):
            ks = pl.multiple_of(t * kt + s * LANES, LANES)
            key = key_s[pl.ds(ks, LANES), :]
            eq = key == thr
            pre = jnp.dot(tri_ref[...], jnp.where(eq, 1.0, 0.0).astype(BF16), preferred_element_type=F32)
            kpos = ks + lax.broadcasted_iota(jnp.int32, (LANES, qb), 0)
            keep = jnp.logical_or(key > thr, jnp.logical_and(eq, pre[:LANES] + off < need))
            keep = jnp.logical_and(keep, kpos < lim)
            bias_s[pl.ds(ks, LANES), :] = jnp.where(keep, 0.0, MASKED)
            off = off + pre[LANES:]
        return off

    lax.fori_loop(0, n_kt, sel_tile, jnp.zeros((LANES, qb), F32))

    nc = kt // ATT_ROWS
    width = rep * qb
    s_refs = (s0_s, s1_s)
    p_refs = (p0_s, p1_s)

    def score_chunk(t, g, c, mx):
        ks = pl.multiple_of(t * kt + c * ATT_ROWS, ATT_ROWS)
        sc = jnp.dot(k_ref[0, g, pl.ds(ks, ATT_ROWS), :], w_q[g], preferred_element_type=F32)
        bias = bias_s[pl.ds(ks, ATT_ROWS), :]
        sc = jnp.concatenate([sc[:, r * qb:(r + 1) * qb] + bias for r in range(rep)], axis=1)
        s_refs[g][c * ATT_ROWS:(c + 1) * ATT_ROWS, :] = sc
        return jnp.maximum(mx, jnp.max(sc.reshape(ATT_ROWS // 8, 8, width), axis=0))

    def prob_chunk(g, c, m_new):
        rows = slice(c * ATT_ROWS, (c + 1) * ATT_ROWS)
        p_refs[g][rows, :] = jnp.exp(s_refs[g][rows, :] - m_new).astype(BF16)

    def finish(t, g, m, m_new):
        pv = jnp.dot(vxt_ref[0, t, g * LANES:(g + 1) * LANES, :], p_refs[g][...], preferred_element_type=F32)
        acc_s[g] = acc_s[g] * jnp.exp(m - m_new) + pv

    mx_init = jnp.full((8, width), MASKED, F32)

    def att_tile(t, carry):
        m0, m1, mx0 = carry
        m0_new = jnp.maximum(m0, jnp.max(mx0, axis=0, keepdims=True))
        mx1 = mx_init
        for c in range(nc):
            mx1 = score_chunk(t, 1, c, mx1)
            prob_chunk(0, c, m0_new)
        m1_new = jnp.maximum(m1, jnp.max(mx1, axis=0, keepdims=True))
        finish(t, 0, m0, m0_new)
        t_next = jnp.minimum(t + 1, n_kt - 1)
        mx0 = mx_init
        for c in range(nc):
            mx0 = score_chunk(t_next, 0, c, mx0)
            prob_chunk(1, c, m1_new)
        finish(t, 1, m1, m1_new)
        return m0_new, m1_new, mx0

    acc_s[...] = jnp.zeros(acc_s.shape, F32)
    mx0 = mx_init
    for c in range(nc):
        mx0 = score_chunk(0, 0, c, mx0)
    m_init = jnp.full((1, width), MASKED, F32)
    lax.fori_loop(0, n_kt, att_tile, (m_init, m_init, mx0))
    for g in range(KV_HEADS):
        acc = acc_s[g]
        out = acc * (1.0 / acc[HEAD_DIM:HEAD_DIM + 1, :])
        for r in range(rep):
            h = g * rep + r
            o_ref[:, h * HEAD_DIM:(h + 1) * HEAD_DIM] = out[:, r * qb:(r + 1) * qb].T[:, :HEAD_DIM]


def _dsa_t(q, qi, kw, k_bf, v_bf, ki_bf, *, qb, kt, q0, l_valid, n_sel):
    assert qb == LANES
    b, lp = ki_bf.shape[:2]
    nq = q.shape[0] // (b * qb)
    k_g = jnp.moveaxis(k_bf, 2, 1)
    ones = jnp.ones((b, lp, KV_HEADS, HEAD_DIM), BF16)
    vx = jnp.concatenate([v_bf, ones], axis=-1).reshape(b, lp // kt, kt, KV_HEADS * LANES)
    vxt = jnp.swapaxes(vx, 2, 3)
    tr = lax.broadcasted_iota(jnp.int32, (2 * LANES, LANES), 0)
    tc = lax.broadcasted_iota(jnp.int32, (2 * LANES, LANES), 1)
    tri = jnp.logical_or(tc < tr, tr >= LANES).astype(BF16)
    row = lambda w: pl.BlockSpec((qb, w), lambda bi, i: (bi * nq + i, 0))
    kern = functools.partial(_dsa_t_kernel, qb=qb, kt=kt, q0=q0, l_valid=l_valid, n_sel=n_sel)
    return pl.pallas_call(
        kern,
        grid=(b, nq),
        in_specs=[row(ATT_WIDTH), row(QI_WIDTH), row(KW_WIDTH),
                  pl.BlockSpec((1, KV_HEADS, lp, HEAD_DIM), lambda bi, i: (bi, 0, 0, 0)),
                  pl.BlockSpec((1, lp // kt, KV_HEADS * LANES, kt), lambda bi, i: (bi, 0, 0, 0)),
                  pl.BlockSpec((1, lp, IDX_DIM), lambda bi, i: (bi, 0, 0)),
                  pl.BlockSpec((2 * LANES, LANES), lambda bi, i: (0, 0))],
        out_specs=row(ATT_WIDTH),
        out_shape=jax.ShapeDtypeStruct((q.shape[0], ATT_WIDTH), F32),
        scratch_shapes=[pltpu.VMEM((lp, qb), jnp.int32), pltpu.VMEM((lp, qb), F32),
                        pltpu.VMEM((kt, 4 * qb), F32), pltpu.VMEM((kt, 4 * qb), F32),
                        pltpu.VMEM((kt, 4 * qb), BF16), pltpu.VMEM((kt, 4 * qb), BF16),
                        pltpu.VMEM((KV_HEADS, LANES, 4 * qb), F32)],
        compiler_params=_cparams(("arbitrary", "arbitrary")),
        name="dsa_t",
    )(q, qi, kw, k_g, vxt, ki_bf, tri)


def _split_bf16(x):
    hi = x.astype(BF16)
    return hi, (x - hi.astype(F32)).astype(BF16)


_NN = (((1,), (0,)), ((), ()))


def _dots(a_sp, b_sp, dims=_NN):
    (ah, al), (bh, bl) = a_sp, b_sp
    d = functools.partial(lax.dot_general, dimension_numbers=dims, preferred_element_type=F32)
    return d(ah, bh) + (d(ah, bl) + d(al, bh))


def _dot3(a, b, dims=_NN):
    return _dots(_split_bf16(a), _split_bf16(b), dims)


def _dot2(a, b_exact):
    ah, al = _split_bf16(a)
    return jnp.dot(ah, b_exact, preferred_element_type=F32) + jnp.dot(al, b_exact, preferred_element_type=F32)


def _head_sums(x, hsum):
    w = hsum.shape[0]
    return jnp.concatenate([_dot2(x[:, j * w:(j + 1) * w], hsum) for j in range(x.shape[1] // w)], axis=1)


_NT = (((1,), (1,)), ((), ()))
_TN = (((0,), (0,)), ((), ()))


def _rwkv_kernel(p_ref, prev_ref, s0_ref, mu_ref, w0_ref, w2_ref, a0_ref, a2_ref, g2_ref, kk_ref, ka_ref,
                 rk_ref, lnw_ref, lnb_ref, hsum_ref, tri_ref, o_ref, st_ref, s_s, prev_s, o_s, *, c, nb):
    j = pl.program_id(1)

    @pl.when(j == 0)
    def _init():
        s_s[...] = s0_ref[...]
        prev_s[...] = prev_ref[...]

    hsum = hsum_ref[...]
    rows = nb * c
    row = lax.broadcasted_iota(jnp.int32, (rows, 1), 0)
    o1, o2, o3 = RWKV_WIDTH, 2 * RWKV_WIDTH, 3 * RWKV_WIDTH

    p = p_ref[...].reshape(rows, RWKV_COLS)
    p_prev = pltpu.roll(p, 1, 0)
    for bi in range(nb):
        p_prev = jnp.where(row == bi * c, prev_s[bi], p_prev)
        prev_s[bi] = p[(bi + 1) * c - 1:(bi + 1) * c, :]
    xs = p + (p_prev - p) * mu_ref[...]
    r, k, v = xs[:, :o1], xs[:, o1:o2], xs[:, o2:o3]
    dw = xs[:, o3:o3 + W_LORA]
    da = xs[:, o3 + W_LORA:o3 + W_LORA + A_LORA]
    dg = xs[:, o3 + W_LORA + A_LORA:]
    w_log = -jax.nn.softplus(-(w0_ref[...] + _dot3(jnp.tanh(dw), w2_ref[...]))) - 0.5
    lw = -jnp.exp(w_log)
    a = jax.nn.sigmoid(a0_ref[...] + _dot3(da, a2_ref[...]))
    g = _dot3(jax.nn.sigmoid(dg), g2_ref[...])
    kk = k * kk_ref[...]
    kk = kk / jnp.maximum(jnp.sqrt(_head_sums(kk * kk, hsum)), 1e-12)
    km = k * (1.0 + (a - 1.0) * ka_ref[...])
    bm = kk * a
    lw_hi, lw_lo = _split_bf16(lw)
    tri = tri_ref[...]
    cum = (jnp.dot(tri, lw_hi, preferred_element_type=F32)
           + jnp.dot(tri, lw_lo, preferred_element_type=F32))
    tot = cum[rows - 1:rows, :]
    for bi in range(nb - 1):
        in_chunk = jnp.logical_and(row >= bi * c, row < (bi + 1) * c)
        tot = jnp.where(in_chunk, cum[(bi + 1) * c - 1:(bi + 1) * c, :], tot)
    e_out = jnp.exp(-cum)
    e_end = jnp.exp(tot - cum)
    stacked = dict(a_t=-kk * jnp.exp(cum - lw), r_t=r * jnp.exp(cum), b_t=bm * e_out, k_t=km * e_out,
                   b_h=bm * e_end, k_h=km * e_end, gam=jnp.exp(tot), v=v)
    pre = [{name: val[bi * c:(bi + 1) * c] for name, val in stacked.items()} for bi in range(nb)]

    ri = lax.broadcasted_iota(jnp.int32, (c, c), 0)
    ci = lax.broadcasted_iota(jnp.int32, (c, c), 1)
    strict = ri > ci
    incl = ri >= ci
    eye = jnp.where(ri == ci, 1.0, 0.0)
    n_double = int(np.log2(c)) - 1
    chains = [(bi, h) for bi in range(nb) for h in range(RWKV_HEADS)]
    ids = range(len(chains))
    col = lambda name, i: pre[chains[i][0]][name][:, chains[i][1] * HEAD_DIM:(chains[i][1] + 1) * HEAD_DIM]
    s0 = [s_s[bi, h] for bi, h in chains]
    s0_sp = [_split_bf16(s) for s in s0]
    ar_sp = [_split_bf16(jnp.concatenate([col("a_t", i), col("r_t", i)], axis=0)) for i in ids]
    bk_sp = [_split_bf16(jnp.concatenate([col("b_t", i), col("k_t", i)], axis=0)) for i in ids]
    v_sp = [_split_bf16(col("v", i)) for i in ids]
    m = [_dots(ar_sp[i], bk_sp[i], _NT) for i in ids]
    as0 = [_dots(ar_sp[i], s0_sp[i], _NT) for i in ids]
    x_sp = [_split_bf16(jnp.where(strict, m[i][:c, :c], 0.0)) for i in ids]
    lak_sp = [_split_bf16(jnp.where(strict, m[i][:c, c:], 0.0)) for i in ids]
    tinv = [eye + jnp.where(strict, m[i][:c, :c], 0.0) for i in ids]
    rhs = [as0[i][:c] + _dots(lak_sp[i], v_sp[i]) for i in ids]
    for _ in range(n_double):
        x_sp = [_split_bf16(_dots(x_sp[i], x_sp[i])) for i in ids]
        tinv = [tinv[i] + _dots(_split_bf16(tinv[i]), x_sp[i]) for i in ids]
    u_sp = [_split_bf16(_dot3(tinv[i], rhs[i])) for i in ids]
    for i in ids:
        bi, h = chains[i]
        m_rb = jnp.where(incl, m[i][c:, :c], 0.0)
        m_rk = jnp.where(incl, m[i][c:, c:], 0.0)
        o_s[bi, :, h * HEAD_DIM:(h + 1) * HEAD_DIM] = (as0[i][c:] + _dots(_split_bf16(m_rb), u_sp[i])
                                                       + _dots(_split_bf16(m_rk), v_sp[i]))
    s_new = [s0[i] * col("gam", i)[:1] +_dots(u_sp[i], _split_bf16(col("b_h", i)), _TN)
             + _dots(v_sp[i], _split_bf16(col("k_h", i)), _TN) for i in ids]
    for i in ids:
        s_s[chains[i][0], chains[i][1]] = s_new[i]

    inv_d = 1.0 / HEAD_DIM
    o = o_s[...].reshape(rows, RWKV_WIDTH)
    mean = _head_sums(o, hsum) * inv_d
    cen = o - mean
    var = _head_sums(cen * cen, hsum) * inv_d
    on = cen * lax.rsqrt(var + GN_EPS) * lnw_ref[...] + lnb_ref[...]
    bonus = _head_sums(r * km * rk_ref[...], hsum) * v
    o_ref[...] = ((on + bonus) * g).reshape(nb, c, RWKV_WIDTH)

    @pl.when(j == pl.num_programs(1) - 1)
    def _fin():
        st_ref[...] = s_s[...]


def _rwkv(p, prev, s0, rw, c, nb):
    mu, w0, w2, a0, a2, g2, k_k, k_a, r_k, lnx_w, lnx_b = rw
    b, t, _ = p.shape
    nc = t // c
    hid = lax.broadcasted_iota(jnp.int32, (2 * LANES, 2 * LANES), 0) // HEAD_DIM
    hsum = (hid == hid.T).astype(BF16)
    tr = lax.broadcasted_iota(jnp.int32, (nb * c, nb * c), 0)
    tc = lax.broadcasted_iota(jnp.int32, (nb * c, nb * c), 1)
    tri = jnp.logical_and(tr >= tc, tr // c == tc // c).astype(BF16)
    vec = lambda a: a.reshape(1, -1)
    full = lambda a: pl.BlockSpec(a.shape, lambda bi, j: (0,) * a.ndim)
    small = [vec(mu), vec(w0), w2, vec(a0), a2, g2, vec(k_k), vec(k_a), vec(r_k), vec(lnx_w), vec(lnx_b), hsum, tri]
    state = pl.BlockSpec((nb, RWKV_HEADS, HEAD_DIM, HEAD_DIM), lambda bi, j: (bi, 0, 0, 0))
    return pl.pallas_call(
        functools.partial(_rwkv_kernel, c=c, nb=nb),
        grid=(b // nb, nc),
        in_specs=[pl.BlockSpec((nb, c, RWKV_COLS), lambda bi, j: (bi, j, 0)),
                  pl.BlockSpec((nb, 1, RWKV_COLS), lambda bi, j: (bi, 0, 0)), state] + [full(a) for a in small],
        out_specs=[pl.BlockSpec((nb, c, RWKV_WIDTH), lambda bi, j: (bi, j, 0)), state],
        out_shape=[jax.ShapeDtypeStruct((b, t, RWKV_WIDTH), F32),
                   jax.ShapeDtypeStruct((b, RWKV_HEADS, HEAD_DIM, HEAD_DIM), F32)],
        scratch_shapes=[pltpu.VMEM((nb, RWKV_HEADS, HEAD_DIM, HEAD_DIM), F32), pltpu.VMEM((nb, 1, RWKV_COLS), F32),
                        pltpu.VMEM((nb, c, RWKV_WIDTH), F32)],
        compiler_params=_cparams(("arbitrary", "arbitrary")),
        name="rwkv",
    )(p, prev, s0, *small)


def _branch(x, mods, pos, prev, s0, caches, weights, tm, tm_moe):
    (g_mix, g_ffn, g_final, w_r, w_a, rw, w_out, rwt, rb, wup, bup, wdn, bdn) = weights
    sh1, sc1, gt1, sh2, sc2, gt2 = mods
    b, t, d = x.shape
    tab, tabk = _rope_tables(pos)
    p_r, q, k, v, qi, kw = _inproj(x, g_mix, sh1, sc1, w_r, w_a, tab, tabk, tm)
    p_r3 = p_r.reshape(b, t, RWKV_COLS)
    o_r, wkv = _rwkv(p_r3, prev, s0, rw, min(CHUNK, t), RWKV_BATCH)
    shift = p_r3[:, -1:]
    k4 = k.reshape(b, t, KV_HEADS, HEAD_DIM)
    v4 = v.reshape(b, t, KV_HEADS, HEAD_DIM)
    ki = kw.reshape(b, t, KW_WIDTH)[..., :IDX_DIM]
    kt = 1024 if caches is None else 512
    if caches is None:
        k_all, v_all, ki_all, q0, l_valid = k4, v4, ki, 0, t
    else:
        ck, cv, cki = caches
        q0 = ck.shape[1]
        l_valid = q0 + t
        padn = (-l_valid) % kt
        zpad = lambda a: jnp.concatenate([a, jnp.zeros((b, padn) + a.shape[2:], a.dtype)], axis=1)
        k_all = zpad(jnp.concatenate([ck, k4], axis=1))
        v_all = zpad(jnp.concatenate([cv, v4], axis=1))
        ki_all = zpad(jnp.concatenate([cki, ki], axis=1))
    qb = min(Q_BLOCK, t)
    dsa = _dsa_t if qb == LANES else _dsa
    o_a = dsa(q, qi, kw, k_all.astype(BF16), v_all.astype(BF16), ki_all.astype(BF16),
              qb=qb, kt=kt, q0=q0, l_valid=l_valid, n_sel=min(TOPK_MAX, l_valid // 4))
    x1, h2 = _outproj(o_r, o_a, x, w_out, gt1, g_ffn, sh2, sc2, tm)
    y_moe = _moe(h2, rwt, rb, wup, bup, wdn, bdn, *tm_moe)
    y = _final(x1, y_moe, gt2, g_final, b, t, tm)
    return y, k4, v4, ki, wkv, shift


def kernel(x_prompt, x_sample, c_prompt, c_sample, cache_k, cache_v, cache_kidx, state_wkv, state_shift,
           w_ada, b_ada, g_mix, g_ffn, g_final, w_in, mu_shift, w0, w_lora_w, a0, w_lora_a, w_lora_g,
           k_k, k_a, r_k, lnx_w, lnx_b, w_out, router_w, router_b, w_up, b_up, w_down, b_down):
    B, T, D = x_prompt.shape
    DB, DT, _ = x_sample.shape
    P = cache_k.shape[2]
    l = 0
    assert w_ada.shape[0] == 1

    rows = B + DB
    pad = (-rows) % 8
    c_all = jnp.concatenate([c_prompt, c_sample, jnp.zeros((pad, D), F32)], axis=0)
    m = _ada(c_all, w_ada[l], b_ada[l])
    mods_p = tuple(t.reshape(B, 1, D) for t in jnp.split(m[:B], 6, axis=-1))
    mods_s = tuple(t.reshape(DB, 1, D) for t in jnp.split(m[B:rows], 6, axis=-1))

    w_r = w_in[l][:, :RWKV_COLS].astype(BF16)
    w_att = w_in[l][:, RWKV_COLS:]
    att_pad = KW_WIDTH - IDX_DIM - IDX_HEADS
    w_a = jnp.concatenate([w_att, jnp.zeros((D, att_pad), F32)], axis=1).astype(BF16)
    rw = (mu_shift[l], w0[l], w_lora_w[l], a0[l], w_lora_a[l], w_lora_g[l], k_k[l], k_a[l], r_k[l],
          lnx_w[l], lnx_b[l])
    wup = _deinterleave_cast(w_up[l])
    bup = jnp.concatenate([b_up[l][..., 0::2], b_up[l][..., 1::2]], axis=-1).reshape(N_EXPERTS, 1, 2 * D_FF)
    wdn = w_down[l].astype(BF16)
    bdn = b_down[l].reshape(N_EXPERTS, 1, D)
    weights = (g_mix[l], g_ffn[l], g_final, w_r, w_a, rw, w_out[l].astype(BF16), router_w[l].T,
               router_b[l].reshape(N_EXPERTS, 1), wup, bup, wdn, bdn)

    pos_p = jnp.arange(T, dtype=jnp.int32)
    pos_s = P + jnp.arange(DT, dtype=jnp.int32)
    yp, kp, vp, kip, wkvp, shp = _branch(
        x_prompt, mods_p, pos_p, jnp.zeros((B, 1, RWKV_COLS), F32),
        jnp.zeros((B, RWKV_HEADS, HEAD_DIM, HEAD_DIM), F32), None, weights, 512, (MOE_SUB, MOE_NSUB))
    ys, ks, vs, kis, wkvs, shs = _branch(
        x_sample, mods_s, pos_s, state_shift[l], state_wkv[l],
        (cache_k[l], cache_v[l], cache_kidx[l]), weights, DT, (DB * DT, 1))
    return (yp, ys, kp[None], vp[None], kip[None], wkvp[None], shp[None],
            ks[None], vs[None], kis[None], wkvs[None], shs[None])
```
